```python
import jax, jax.numpy as jnp
from jax import lax
import numpy as np

D_MODEL = 1024
BATCH = 8
SEQ = 8192
DEPTH = 4

N_MIXERS = 2
N_A_LAYERS = (DEPTH + 1) // 2
N_B_LAYERS = DEPTH // 2
HGRN_EXPAND = 128
HGRN_HEADS = D_MODEL // HGRN_EXPAND
HGRN_HEAD_K = HGRN_EXPAND
HGRN_HEAD_V = D_MODEL // HGRN_HEADS
HGRN_CHUNK = 32
CONV_WIDTH = 31
D_FF = 4 * D_MODEL
DEEPNORM_ALPHA = (2.0 * DEPTH) ** 0.25
DEEPNORM_BETA = (8.0 * DEPTH) ** -0.25
LN_EPS = 1e-5
RMS_EPS = 1e-6
GATE_EPS = 1e-6

kernel_name = "hgrn2_conformer_interleaved_deepnorm"


def layer_norm(x, g, b):
    x32 = x.astype(jnp.float32)
    mu = jnp.mean(x32, axis=-1, keepdims=True)
    var = jnp.mean(jnp.square(x32 - mu), axis=-1, keepdims=True)
    y = (x32 - mu) * lax.rsqrt(var + LN_EPS) * g.astype(jnp.float32) + b.astype(jnp.float32)
    return y.astype(x.dtype)


def chunkwise_gated_recurrence(q, k, v, b):
    C = q.shape[-2]
    causal = jnp.tril(jnp.ones((C, C), dtype=bool))[:, :, None]

    def step(S, inp):
        qc, kc, vc, bc = inp
        diff = bc[..., :, None, :] - bc[..., None, :, :]
        decay = jnp.where(causal, jnp.exp(jnp.where(causal, diff, 0.0)), 0.0)
        scores = jnp.einsum('bhtd,bhsd,bhtsd->bhts', qc, kc, decay)
        o = (jnp.einsum('bhts,bhsv->bhtv', scores, vc)
             + jnp.einsum('bhtd,bhdv->bhtv', qc * jnp.exp(bc), S))
        b_last = bc[..., -1:, :]
        S = (jnp.exp(b_last)[..., 0, :, None] * S
             + jnp.einsum('bhsd,bhsv->bhdv', kc * jnp.exp(b_last - bc), vc))
        return S, o

    S0 = jnp.zeros(q.shape[1:3] + (q.shape[-1], v.shape[-1]), jnp.float32)
    _, o = lax.scan(step, S0, (q, k, v, b))
    return o


def hgrn2_mixer(h, w_in, lb, norm_g, w_out):
    B_, S_, D = h.shape
    H, dk, dv, C = HGRN_HEADS, HGRN_HEAD_K, HGRN_HEAD_V, HGRN_CHUNK
    nC = S_ // C
    proj = h @ w_in
    q, fz, v, g = jnp.split(proj, 4, axis=-1)
    q = jax.nn.silu(q.astype(jnp.float32))
    lb32 = lb.astype(jnp.float32)
    f = lb32 + (1.0 - lb32) * jax.nn.sigmoid(fz.astype(jnp.float32))
    log_f = jnp.log(jnp.maximum(f, GATE_EPS))
    k = 1.0 - f

    def to_chunks(t, d):
        return t.astype(jnp.float32).reshape(B_, nC, C, H, d).transpose(1, 0, 3, 2, 4)

    qc, kc, vc = to_chunks(q, dk), to_chunks(k, dk), to_chunks(v, dv)
    bc = jnp.cumsum(to_chunks(log_f, dk), axis=-2)
    o = chunkwise_gated_recurrence(qc, kc, vc, bc)
    o = o.transpose(1, 0, 3, 2, 4).reshape(B_, S_, H, dv)
    o = o * lax.rsqrt(jnp.mean(jnp.square(o), axis=-1, keepdims=True) + RMS_EPS)
    o = o * norm_g.astype(jnp.float32).reshape(H, dv)
    o = o.reshape(B_, S_, D) * jax.nn.silu(g.astype(jnp.float32))
    return o.astype(h.dtype) @ w_out


def conformer_conv_mixer(h, w_pw1, b_pw1, w_dw, b_dw, ln_g, ln_b, w_pw2, b_pw2):
    u = h @ w_pw1 + b_pw1
    a, gate = jnp.split(u, 2, axis=-1)
    u = a * jax.nn.sigmoid(gate)
    u = lax.conv_general_dilated(
        u, w_dw[:, None, :].astype(u.dtype), window_strides=(1,), padding=[(CONV_WIDTH - 1, 0)],
        dimension_numbers=('NWC', 'WIO', 'NWC'), feature_group_count=D_MODEL) + b_dw
    u = jax.nn.silu(layer_norm(u, ln_g, ln_b))
    return u @ w_pw2 + b_pw2


def _fwd_setup_inputs(seed: int = 0) -> dict:
    key = jax.random.key(seed)
    ks = jax.random.split(key, 20)
    D, F, K = D_MODEL, D_FF, CONV_WIDTH
    beta = DEEPNORM_BETA

    def nrm(k, shape, scale):
        return jax.random.normal(k, shape, jnp.float32) * scale

    x = nrm(ks[0], (BATCH, SEQ, D), 1.0)
    ln_mix_g = 1.0 + nrm(ks[1], (DEPTH, D), 0.02)
    ln_mix_b = nrm(ks[2], (DEPTH, D), 0.02)
    ln_ffn_g = 1.0 + nrm(ks[3], (DEPTH, D), 0.02)
    ln_ffn_b = nrm(ks[4], (DEPTH, D), 0.02)
    ffn_w1 = nrm(ks[5], (DEPTH, D, F), D ** -0.5 * beta)
    ffn_w2 = nrm(ks[6], (DEPTH, F, D), F ** -0.5 * beta)
    col_scale = jnp.concatenate([jnp.ones((2 * D,), jnp.float32), jnp.full((D,), beta, jnp.float32),
                                 jnp.ones((D,), jnp.float32)])
    a_w_in = nrm(ks[7], (N_A_LAYERS, D, 4 * D), D ** -0.5) * col_scale
    a_lb_logits = nrm(ks[8], (N_A_LAYERS, D), 0.5)
    a_norm_g = 1.0 + nrm(ks[9], (N_A_LAYERS, D), 0.02)
    a_w_out = nrm(ks[10], (N_A_LAYERS, D, D), D ** -0.5 * beta)
    b_w_pw1 = nrm(ks[11], (N_B_LAYERS, D, 2 * D), D ** -0.5)
    b_b_pw1 = nrm(ks[12], (N_B_LAYERS, 2 * D), 0.02)
    b_w_dw = nrm(ks[13], (N_B_LAYERS, K, D), K ** -0.5)
    b_b_dw = nrm(ks[14], (N_B_LAYERS, D), 0.02)
    b_ln_g = 1.0 + nrm(ks[15], (N_B_LAYERS, D), 0.02)
    b_ln_b = nrm(ks[16], (N_B_LAYERS, D), 0.02)
    b_w_pw2 = nrm(ks[17], (N_B_LAYERS, D, D), D ** -0.5 * beta)
    b_b_pw2 = nrm(ks[18], (N_B_LAYERS, D), 0.02)
    return {"x": x, "ln_mix_g": ln_mix_g, "ln_mix_b": ln_mix_b, "ln_ffn_g": ln_ffn_g, "ln_ffn_b": ln_ffn_b,
            "ffn_w1": ffn_w1, "ffn_w2": ffn_w2,
            "a_w_in": a_w_in, "a_lb_logits": a_lb_logits, "a_norm_g": a_norm_g, "a_w_out": a_w_out,
            "b_w_pw1": b_w_pw1, "b_b_pw1": b_b_pw1, "b_w_dw": b_w_dw, "b_b_dw": b_b_dw,
            "b_ln_g": b_ln_g, "b_ln_b": b_ln_b, "b_w_pw2": b_w_pw2, "b_b_pw2": b_b_pw2}


def _fwd_reference(x, ln_mix_g, ln_mix_b, ln_ffn_g, ln_ffn_b, ffn_w1, ffn_w2,
              a_w_in, a_lb_logits, a_norm_g, a_w_out,
              b_w_pw1, b_b_pw1, b_w_dw, b_b_dw, b_ln_g, b_ln_b, b_w_pw2, b_b_pw2):
    lb_soft = jax.nn.softmax(a_lb_logits.astype(jnp.float32), axis=0)
    lb_all = jnp.cumsum(lb_soft, axis=0) - lb_soft[0]
    for i in range(DEPTH):
        j = i // N_MIXERS
        if i % N_MIXERS == 0:
            mix = hgrn2_mixer(x, a_w_in[j], lb_all[j], a_norm_g[j], a_w_out[j])
        else:
            mix = conformer_conv_mixer(x, b_w_pw1[j], b_b_pw1[j], b_w_dw[j], b_b_dw[j],
                                       b_ln_g[j], b_ln_b[j], b_w_pw2[j], b_b_pw2[j])
        x = layer_norm(DEEPNORM_ALPHA * x + mix, ln_mix_g[i], ln_mix_b[i])
        ff = jnp.square(jax.nn.relu(x @ ffn_w1[i])) @ ffn_w2[i]
        x = layer_norm(DEEPNORM_ALPHA * x + ff, ln_ffn_g[i], ln_ffn_b[i])
    return x


import jax as _jax
import jax.numpy as _jnp

TWIN_FORMAT = 'train_step'
FWD_PARAMS = ['x', 'ln_mix_g', 'ln_mix_b', 'ln_ffn_g', 'ln_ffn_b', 'ffn_w1', 'ffn_w2', 'a_w_in', 'a_lb_logits', 'a_norm_g', 'a_w_out', 'b_w_pw1', 'b_b_pw1', 'b_w_dw', 'b_b_dw', 'b_ln_g', 'b_ln_b', 'b_w_pw2', 'b_b_pw2']
TWIN_WEIGHTS = ['ln_mix_g', 'ln_mix_b', 'ln_ffn_g', 'ln_ffn_b', 'ffn_w1', 'ffn_w2', 'a_w_in', 'a_lb_logits', 'a_norm_g', 'a_w_out', 'b_w_pw1', 'b_b_pw1', 'b_w_dw', 'b_b_dw', 'b_ln_g', 'b_ln_b', 'b_w_pw2', 'b_b_pw2']
TWIN_DIFF_INPUT = 'x'
TWIN_INPUTS = ['x', 'ln_mix_g', 'ln_mix_b', 'ln_ffn_g', 'ln_ffn_b', 'ffn_w1', 'ffn_w2', 'a_w_in', 'a_lb_logits', 'a_norm_g', 'a_w_out', 'b_w_pw1', 'b_b_pw1', 'b_w_dw', 'b_b_dw', 'b_ln_g', 'b_ln_b', 'b_w_pw2', 'b_b_pw2', 'loss_target', 'm_ln_mix_g', 'm_ln_mix_b', 'm_ln_ffn_g', 'm_ln_ffn_b', 'm_ffn_w1', 'm_ffn_w2', 'm_a_w_in', 'm_a_lb_logits', 'm_a_norm_g', 'm_a_w_out', 'm_b_w_pw1', 'm_b_b_pw1', 'm_b_w_dw', 'm_b_b_dw', 'm_b_ln_g', 'm_b_ln_b', 'm_b_w_pw2', 'm_b_b_pw2', 'v_ln_mix_g', 'v_ln_mix_b', 'v_ln_ffn_g', 'v_ln_ffn_b', 'v_ffn_w1', 'v_ffn_w2', 'v_a_w_in', 'v_a_lb_logits', 'v_a_norm_g', 'v_a_w_out', 'v_b_w_pw1', 'v_b_b_pw1', 'v_b_w_dw', 'v_b_b_dw', 'v_b_ln_g', 'v_b_ln_b', 'v_b_w_pw2', 'v_b_b_pw2']
TWIN_OUTPUTS = ['loss', 'grad_x', 'grad_ln_mix_g', 'grad_ln_mix_b', 'grad_ln_ffn_g', 'grad_ln_ffn_b', 'grad_ffn_w1', 'grad_ffn_w2', 'grad_a_w_in', 'grad_a_lb_logits', 'grad_a_norm_g', 'grad_a_w_out', 'grad_b_w_pw1', 'grad_b_b_pw1', 'grad_b_w_dw', 'grad_b_b_dw', 'grad_b_ln_g', 'grad_b_ln_b', 'grad_b_w_pw2', 'grad_b_b_pw2', 'delta_ln_mix_g', 'delta_ln_mix_b', 'delta_ln_ffn_g', 'delta_ln_ffn_b', 'delta_ffn_w1', 'delta_ffn_w2', 'delta_a_w_in', 'delta_a_lb_logits', 'delta_a_norm_g', 'delta_a_w_out', 'delta_b_w_pw1', 'delta_b_b_pw1', 'delta_b_w_dw', 'delta_b_b_dw', 'delta_b_ln_g', 'delta_b_ln_b', 'delta_b_w_pw2', 'delta_b_b_pw2', 'new_m_ln_mix_g', 'new_m_ln_mix_b', 'new_m_ln_ffn_g', 'new_m_ln_ffn_b', 'new_m_ffn_w1', 'new_m_ffn_w2', 'new_m_a_w_in', 'new_m_a_lb_logits', 'new_m_a_norm_g', 'new_m_a_w_out', 'new_m_b_w_pw1', 'new_m_b_b_pw1', 'new_m_b_w_dw', 'new_m_b_b_dw', 'new_m_b_ln_g', 'new_m_b_ln_b', 'new_m_b_w_pw2', 'new_m_b_b_pw2', 'new_v_ln_mix_g', 'new_v_ln_mix_b', 'new_v_ln_ffn_g', 'new_v_ln_ffn_b', 'new_v_ffn_w1', 'new_v_ffn_w2', 'new_v_a_w_in', 'new_v_a_lb_logits', 'new_v_a_norm_g', 'new_v_a_w_out', 'new_v_b_w_pw1', 'new_v_b_b_pw1', 'new_v_b_w_dw', 'new_v_b_b_dw', 'new_v_b_ln_g', 'new_v_b_ln_b', 'new_v_b_w_pw2', 'new_v_b_b_pw2']
TWIN_LEAF_KINDS = {'loss': 'loss', 'grad_x': 'grad_x', 'grad_ln_mix_g': 'grad_w', 'grad_ln_mix_b': 'grad_w', 'grad_ln_ffn_g': 'grad_w', 'grad_ln_ffn_b': 'grad_w', 'grad_ffn_w1': 'grad_w', 'grad_ffn_w2': 'grad_w', 'grad_a_w_in': 'grad_w', 'grad_a_lb_logits': 'grad_w', 'grad_a_norm_g': 'grad_w', 'grad_a_w_out': 'grad_w', 'grad_b_w_pw1': 'grad_w', 'grad_b_b_pw1': 'grad_w', 'grad_b_w_dw': 'grad_w', 'grad_b_b_dw': 'grad_w', 'grad_b_ln_g': 'grad_w', 'grad_b_ln_b': 'grad_w', 'grad_b_w_pw2': 'grad_w', 'grad_b_b_pw2': 'grad_w', 'delta_ln_mix_g': 'delta_w', 'delta_ln_mix_b': 'delta_w', 'delta_ln_ffn_g': 'delta_w', 'delta_ln_ffn_b': 'delta_w', 'delta_ffn_w1': 'delta_w', 'delta_ffn_w2': 'delta_w', 'delta_a_w_in': 'delta_w', 'delta_a_lb_logits': 'delta_w', 'delta_a_norm_g': 'delta_w', 'delta_a_w_out': 'delta_w', 'delta_b_w_pw1': 'delta_w', 'delta_b_b_pw1': 'delta_w', 'delta_b_w_dw': 'delta_w', 'delta_b_b_dw': 'delta_w', 'delta_b_ln_g': 'delta_w', 'delta_b_ln_b': 'delta_w', 'delta_b_w_pw2': 'delta_w', 'delta_b_b_pw2': 'delta_w', 'new_m_ln_mix_g': 'new_m', 'new_m_ln_mix_b': 'new_m', 'new_m_ln_ffn_g': 'new_m', 'new_m_ln_ffn_b': 'new_m', 'new_m_ffn_w1': 'new_m', 'new_m_ffn_w2': 'new_m', 'new_m_a_w_in': 'new_m', 'new_m_a_lb_logits': 'new_m', 'new_m_a_norm_g': 'new_m', 'new_m_a_w_out': 'new_m', 'new_m_b_w_pw1': 'new_m', 'new_m_b_b_pw1': 'new_m', 'new_m_b_w_dw': 'new_m', 'new_m_b_b_dw': 'new_m', 'new_m_b_ln_g': 'new_m', 'new_m_b_ln_b': 'new_m', 'new_m_b_w_pw2': 'new_m', 'new_m_b_b_pw2': 'new_m', 'new_v_ln_mix_g': 'new_v', 'new_v_ln_mix_b': 'new_v', 'new_v_ln_ffn_g': 'new_v', 'new_v_ln_ffn_b': 'new_v', 'new_v_ffn_w1': 'new_v', 'new_v_ffn_w2': 'new_v', 'new_v_a_w_in': 'new_v', 'new_v_a_lb_logits': 'new_v', 'new_v_a_norm_g': 'new_v', 'new_v_a_w_out': 'new_v', 'new_v_b_w_pw1': 'new_v', 'new_v_b_b_pw1': 'new_v', 'new_v_b_w_dw': 'new_v', 'new_v_b_b_dw': 'new_v', 'new_v_b_ln_g': 'new_v', 'new_v_b_ln_b': 'new_v', 'new_v_b_w_pw2': 'new_v', 'new_v_b_b_pw2': 'new_v'}


def _forward(args):
    return _fwd_reference(*[args[k] for k in FWD_PARAMS])


def _output_shape():
    def fwd():
        inp = _fwd_setup_inputs(0)
        return _fwd_reference(*[inp[k] for k in FWD_PARAMS])
    out = _jax.eval_shape(fwd)
    return out.shape, out.dtype

N_MICROBATCH = 1
ADAM_LR = 0.001
ADAM_B1 = 0.9
ADAM_B2 = 0.999
ADAM_EPS = 1e-08
ADAM_WD = 0.01
ADAM_STEP = 10
PER_EXAMPLE_BATCH_AXIS = {'x': 0, 'loss_target': 0}
SHARED_INPUTS = []
_WEIGHT_DTYPES = {'ln_mix_g': _jnp.float32, 'ln_mix_b': _jnp.float32, 'ln_ffn_g': _jnp.float32, 'ln_ffn_b': _jnp.float32, 'ffn_w1': _jnp.float32, 'ffn_w2': _jnp.float32, 'a_w_in': _jnp.float32, 'a_lb_logits': _jnp.float32, 'a_norm_g': _jnp.float32, 'a_w_out': _jnp.float32, 'b_w_pw1': _jnp.float32, 'b_b_pw1': _jnp.float32, 'b_w_dw': _jnp.float32, 'b_b_dw': _jnp.float32, 'b_ln_g': _jnp.float32, 'b_ln_b': _jnp.float32, 'b_w_pw2': _jnp.float32, 'b_b_pw2': _jnp.float32}
MOMENT_SCALE = {'ln_mix_g': 2.203859e+00, 'ln_mix_b': 1.008523e+00, 'ln_ffn_g': 3.223039e+01, 'ln_ffn_b': 3.214326e+00, 'ffn_w1': 1.895017e-02, 'ffn_w2': 5.136133e-02, 'a_w_in': 4.896255e-02, 'a_lb_logits': 3.227378e-03, 'a_norm_g': 3.794561e-02, 'a_w_out': 8.974542e-02, 'b_w_pw1': 2.951369e-02, 'b_b_pw1': 7.772095e-02, 'b_w_dw': 3.948733e-02, 'b_b_dw': 1.911850e-01, 'b_ln_g': 7.605341e-02, 'b_ln_b': 1.182273e-01, 'b_w_pw2': 1.271688e-01, 'b_b_pw2': 5.967439e-01}


def _to_microbatches(a, axis):
    t = _jnp.moveaxis(a, axis, 0)
    t = t.reshape((N_MICROBATCH, t.shape[0] // N_MICROBATCH) + t.shape[1:])
    return _jnp.moveaxis(t, 1, axis + 1)


def setup_inputs(seed: int = 0) -> dict:
    inp = _fwd_setup_inputs(seed)
    key = _jax.random.fold_in(_jax.random.key(seed), 7919)
    shape, _ = _output_shape()
    out = dict(inp)
    out["loss_target"] = _jax.random.normal(_jax.random.fold_in(key, 0), shape, _jnp.float32)
    for i, name in enumerate(TWIN_WEIGHTS):
        w = inp[name].astype(_jnp.float32)
        if MOMENT_SCALE is None:
            s = _jnp.sqrt(_jnp.mean(_jnp.square(w)) + 1e-30)
        else:
            s = MOMENT_SCALE[name]
        km, kv = _jax.random.split(_jax.random.fold_in(key, i + 1))
        out[name] = w
        out["m_" + name] = s * _jax.random.normal(km, w.shape, _jnp.float32)
        out["v_" + name] = (s * s) * _jax.random.uniform(kv, w.shape, _jnp.float32, 0.5, 1.5)
    if N_MICROBATCH > 1:
        for name, axis in PER_EXAMPLE_BATCH_AXIS.items():
            out[name] = _to_microbatches(out[name], axis)
    return {'x': out['x'], 'ln_mix_g': out['ln_mix_g'], 'ln_mix_b': out['ln_mix_b'], 'ln_ffn_g': out['ln_ffn_g'], 'ln_ffn_b': out['ln_ffn_b'], 'ffn_w1': out['ffn_w1'], 'ffn_w2': out['ffn_w2'], 'a_w_in': out['a_w_in'], 'a_lb_logits': out['a_lb_logits'], 'a_norm_g': out['a_norm_g'], 'a_w_out': out['a_w_out'], 'b_w_pw1': out['b_w_pw1'], 'b_b_pw1': out['b_b_pw1'], 'b_w_dw': out['b_w_dw'], 'b_b_dw': out['b_b_dw'], 'b_ln_g': out['b_ln_g'], 'b_ln_b': out['b_ln_b'], 'b_w_pw2': out['b_w_pw2'], 'b_b_pw2': out['b_b_pw2'], 'loss_target': out['loss_target'], 'm_ln_mix_g': out['m_ln_mix_g'], 'm_ln_mix_b': out['m_ln_mix_b'], 'm_ln_ffn_g': out['m_ln_ffn_g'], 'm_ln_ffn_b': out['m_ln_ffn_b'], 'm_ffn_w1': out['m_ffn_w1'], 'm_ffn_w2': out['m_ffn_w2'], 'm_a_w_in': out['m_a_w_in'], 'm_a_lb_logits': out['m_a_lb_logits'], 'm_a_norm_g': out['m_a_norm_g'], 'm_a_w_out': out['m_a_w_out'], 'm_b_w_pw1': out['m_b_w_pw1'], 'm_b_b_pw1': out['m_b_b_pw1'], 'm_b_w_dw': out['m_b_w_dw'], 'm_b_b_dw': out['m_b_b_dw'], 'm_b_ln_g': out['m_b_ln_g'], 'm_b_ln_b': out['m_b_ln_b'], 'm_b_w_pw2': out['m_b_w_pw2'], 'm_b_b_pw2': out['m_b_b_pw2'], 'v_ln_mix_g': out['v_ln_mix_g'], 'v_ln_mix_b': out['v_ln_mix_b'], 'v_ln_ffn_g': out['v_ln_ffn_g'], 'v_ln_ffn_b': out['v_ln_ffn_b'], 'v_ffn_w1': out['v_ffn_w1'], 'v_ffn_w2': out['v_ffn_w2'], 'v_a_w_in': out['v_a_w_in'], 'v_a_lb_logits': out['v_a_lb_logits'], 'v_a_norm_g': out['v_a_norm_g'], 'v_a_w_out': out['v_a_w_out'], 'v_b_w_pw1': out['v_b_w_pw1'], 'v_b_b_pw1': out['v_b_b_pw1'], 'v_b_w_dw': out['v_b_w_dw'], 'v_b_b_dw': out['v_b_b_dw'], 'v_b_ln_g': out['v_b_ln_g'], 'v_b_ln_b': out['v_b_ln_b'], 'v_b_w_pw2': out['v_b_w_pw2'], 'v_b_b_pw2': out['v_b_b_pw2']}


def _loss(weights, diff, rest, loss_target):
    with _jax.named_scope("forward"):
        args = {**rest, TWIN_DIFF_INPUT: diff, **{k: w.astype(_WEIGHT_DTYPES[k]) for k, w in weights.items()}}
        y = _forward(args)
    with _jax.named_scope("loss_head"):
        err = _jnp.square(y.astype(_jnp.float32) - loss_target)
        return 0.5 * _jnp.sum(_jnp.mean(err, axis=-1)) if err.ndim else 0.5 * err


def _adamw(w, g, m, v):
    m = ADAM_B1 * m + (1.0 - ADAM_B1) * g
    v = ADAM_B2 * v + (1.0 - ADAM_B2) * _jnp.square(g)
    m_hat = m / (1.0 - ADAM_B1 ** ADAM_STEP)
    v_hat = v / (1.0 - ADAM_B2 ** ADAM_STEP)
    delta = -ADAM_LR * (m_hat / (_jnp.sqrt(v_hat) + ADAM_EPS) + ADAM_WD * w)
    return delta, m, v


def reference(x, ln_mix_g, ln_mix_b, ln_ffn_g, ln_ffn_b, ffn_w1, ffn_w2, a_w_in, a_lb_logits, a_norm_g, a_w_out, b_w_pw1, b_b_pw1, b_w_dw, b_b_dw, b_ln_g, b_ln_b, b_w_pw2, b_b_pw2, loss_target, m_ln_mix_g, m_ln_mix_b, m_ln_ffn_g, m_ln_ffn_b, m_ffn_w1, m_ffn_w2, m_a_w_in, m_a_lb_logits, m_a_norm_g, m_a_w_out, m_b_w_pw1, m_b_b_pw1, m_b_w_dw, m_b_b_dw, m_b_ln_g, m_b_ln_b, m_b_w_pw2, m_b_b_pw2, v_ln_mix_g, v_ln_mix_b, v_ln_ffn_g, v_ln_ffn_b, v_ffn_w1, v_ffn_w2, v_a_w_in, v_a_lb_logits, v_a_norm_g, v_a_w_out, v_b_w_pw1, v_b_b_pw1, v_b_w_dw, v_b_b_dw, v_b_ln_g, v_b_ln_b, v_b_w_pw2, v_b_b_pw2):
    given = dict(x=x, ln_mix_g=ln_mix_g, ln_mix_b=ln_mix_b, ln_ffn_g=ln_ffn_g, ln_ffn_b=ln_ffn_b, ffn_w1=ffn_w1, ffn_w2=ffn_w2, a_w_in=a_w_in, a_lb_logits=a_lb_logits, a_norm_g=a_norm_g, a_w_out=a_w_out, b_w_pw1=b_w_pw1, b_b_pw1=b_b_pw1, b_w_dw=b_w_dw, b_b_dw=b_b_dw, b_ln_g=b_ln_g, b_ln_b=b_ln_b, b_w_pw2=b_w_pw2, b_b_pw2=b_b_pw2, loss_target=loss_target, m_ln_mix_g=m_ln_mix_g, m_ln_mix_b=m_ln_mix_b, m_ln_ffn_g=m_ln_ffn_g, m_ln_ffn_b=m_ln_ffn_b, m_ffn_w1=m_ffn_w1, m_ffn_w2=m_ffn_w2, m_a_w_in=m_a_w_in, m_a_lb_logits=m_a_lb_logits, m_a_norm_g=m_a_norm_g, m_a_w_out=m_a_w_out, m_b_w_pw1=m_b_w_pw1, m_b_b_pw1=m_b_b_pw1, m_b_w_dw=m_b_w_dw, m_b_b_dw=m_b_b_dw, m_b_ln_g=m_b_ln_g, m_b_ln_b=m_b_ln_b, m_b_w_pw2=m_b_w_pw2, m_b_b_pw2=m_b_b_pw2, v_ln_mix_g=v_ln_mix_g, v_ln_mix_b=v_ln_mix_b, v_ln_ffn_g=v_ln_ffn_g, v_ln_ffn_b=v_ln_ffn_b, v_ffn_w1=v_ffn_w1, v_ffn_w2=v_ffn_w2, v_a_w_in=v_a_w_in, v_a_lb_logits=v_a_lb_logits, v_a_norm_g=v_a_norm_g, v_a_w_out=v_a_w_out, v_b_w_pw1=v_b_w_pw1, v_b_b_pw1=v_b_b_pw1, v_b_w_dw=v_b_w_dw, v_b_b_dw=v_b_b_dw, v_b_ln_g=v_b_ln_g, v_b_ln_b=v_b_ln_b, v_b_w_pw2=v_b_w_pw2, v_b_b_pw2=v_b_b_pw2)
    weights = {n: given[n] for n in TWIN_WEIGHTS}
    shared = {n: given[n] for n in SHARED_INPUTS}
    per_example = {n: given[n] for n in ['x']}
    grad_fn = _jax.value_and_grad(_loss, argnums=(0, 1))

    def one_microbatch(ex, loss_target):
        ex = dict(ex)
        diff = ex.pop(TWIN_DIFF_INPUT)
        return grad_fn(weights, diff, {**shared, **ex}, loss_target)

    if N_MICROBATCH == 1:
        loss, (grad_w, grad_x) = one_microbatch(per_example, given["loss_target"])
    else:
        def body(carry, xs):
            loss_sum, grad_sum = carry
            l_k, (gw_k, gx_k) = one_microbatch(xs[0], xs[1])
            with _jax.named_scope("update"):
                return (loss_sum + l_k, _jax.tree.map(_jnp.add, grad_sum, gw_k)), gx_k

        init = (_jnp.zeros((), _jnp.float32), _jax.tree.map(_jnp.zeros_like, weights))
        (loss, grad_w), grad_x = _jax.lax.scan(body, init, (per_example, given["loss_target"]))
    with _jax.named_scope("update"):
        delta_w, new_m, new_v = {}, {}, {}
        for n in TWIN_WEIGHTS:
            delta_w[n], new_m[n], new_v[n] = _adamw(weights[n], grad_w[n], given["m_" + n], given["v_" + n])
    return (loss, grad_x, *[grad_w[n] for n in TWIN_WEIGHTS], *[delta_w[n] for n in TWIN_WEIGHTS],
            *[new_m[n] for n in TWIN_WEIGHTS], *[new_v[n] for n in TWIN_WEIGHTS])
```

```python
import functools

import numpy as np
import jax
import jax.numpy as jnp
from jax import lax
from jax.experimental import pallas as pl
from jax.experimental.pallas import tpu as pltpu

F32 = jnp.float32
BF = jnp.bfloat16

N_DEV = 8
DEPTH = 4
HEAD = 128
CHUNK = 32
SUB = 8
CONV_W = 31
HALO = 32
ALPHA = (2.0 * DEPTH) ** 0.25
LN_EPS = 1e-5
RMS_EPS = 1e-6
GATE_EPS = 1e-6
ADAM_LR = 0.001
ADAM_B1 = 0.9
ADAM_B2 = 0.999
ADAM_EPS = 1e-08
ADAM_WD = 0.01
ADAM_STEP = 10
VMEM_LIMIT = 56 * 1024 * 1024
MESH = pl.DeviceIdType.MESH


def _cp(sem=None):
    return pltpu.CompilerParams(vmem_limit_bytes=VMEM_LIMIT, dimension_semantics=sem)


def _dot(a, b):
    return jnp.dot(a, b, preferred_element_type=F32)


def _dot_nt(a, b):
    return lax.dot_general(a, b, (((1,), (1,)), ((), ())), preferred_element_type=F32)


def _dot_tn(a, b):
    return lax.dot_general(a, b, (((0,), (0,)), ((), ())), preferred_element_type=F32)


def _sig(x):
    return jax.nn.sigmoid(x)


def _exchange(mode, arrays, name):
    n = len(arrays)
    if mode == "gather":
        out_shape = [jax.ShapeDtypeStruct((N_DEV,) + a.shape, a.dtype) for a in arrays]
    else:
        out_shape = [jax.ShapeDtypeStruct(a.shape, a.dtype) for a in arrays]

    def body(*refs):
        ins, outs = refs[:n], refs[n:2 * n]
        send_sems, recv_sems, loc_sems = refs[2 * n:]
        x, y, c = lax.axis_index("x"), lax.axis_index("y"), lax.axis_index("c")
        me = 4 * x + 2 * y + c
        started = []
        for t in range(n):
            src = ins[t] if mode == "gather" else ins[t].at[me]
            lc = pltpu.make_async_copy(src, outs[t].at[me], loc_sems.at[t])
            lc.start()
            started.append(lc)
        remote = []
        for r in range(1, N_DEV):
            px = (1 - x) if (r >> 2) & 1 else x
            py = (1 - y) if (r >> 1) & 1 else y
            pc = (1 - c) if r & 1 else c
            pid = 4 * px + 2 * py + pc
            for t in range(n):
                src = ins[t] if mode == "gather" else ins[t].at[pid]
                cp = pltpu.make_async_remote_copy(
                    src_ref=src, dst_ref=outs[t].at[me],
                    send_sem=send_sems.at[t, r - 1], recv_sem=recv_sems.at[t, r - 1],
                    device_id=(px, py, pc), device_id_type=MESH)
                cp.start()
                remote.append(cp)
        for cp in remote:
            cp.wait_send()
        for cp in remote:
            cp.wait_recv()
        for lc in started:
            lc.wait()

    anyspec = pl.BlockSpec(memory_space=pl.ANY)
    return pl.pallas_call(
        body, name=name, out_shape=out_shape,
        in_specs=[anyspec] * n, out_specs=[anyspec] * n,
        scratch_shapes=[pltpu.SemaphoreType.DMA((n, N_DEV - 1)),
                        pltpu.SemaphoreType.DMA((n, N_DEV - 1)),
                        pltpu.SemaphoreType.DMA((n,))],
        compiler_params=pltpu.CompilerParams(has_side_effects=True),
    )(*arrays)


def _mm(a, w, l, kind, *, out_dtype, name, tm=256, a_fn=None, epi=None, extras=()):
    T, Ka = a.shape
    _, _, d2, d3 = w.shape
    n_out = {"nn_col": N_DEV * d3, "nn_row": d3, "nt_col": d2, "nt_row": N_DEV * d2}[kind]
    tm = min(tm, T)
    ne = len(extras)

    def body(*refs):
        a_ref, w_ref = refs[0], refs[1]
        e_refs = refs[2:2 + ne]
        o_ref = refs[2 + ne]
        av = a_ref[...]
        if a_fn is not None:
            av = a_fn(av)
        av = av.astype(BF)
        if kind == "nn_col":
            acc = jnp.concatenate([_dot(av, w_ref[j]) for j in range(N_DEV)], axis=1)
        elif kind == "nn_row":
            acc = _dot(av, w_ref[...].reshape(N_DEV * d2, d3))
        elif kind == "nt_col":
            acc = _dot_nt(av[:, 0:d3], w_ref[0])
            for j in range(1, N_DEV):
                acc = acc + _dot_nt(av[:, j * d3:(j + 1) * d3], w_ref[j])
        else:
            acc = jnp.concatenate([_dot_nt(av, w_ref[j]) for j in range(N_DEV)], axis=1)
        if epi is not None:
            acc = epi(acc, *[e[...] for e in e_refs])
        o_ref[...] = acc.astype(out_dtype)

    in_specs = [pl.BlockSpec((tm, Ka), lambda i: (i, 0)),
                pl.BlockSpec((N_DEV, None, d2, d3), lambda i: (0, l, 0, 0))]
    for e in extras:
        if e.shape[0] == 1:
            in_specs.append(pl.BlockSpec((1, n_out), lambda i: (0, 0)))
        else:
            in_specs.append(pl.BlockSpec((tm, n_out), lambda i: (i, 0)))
    return pl.pallas_call(
        body, name=name, grid=(T // tm,),
        out_shape=jax.ShapeDtypeStruct((T, n_out), out_dtype),
        in_specs=in_specs, out_specs=pl.BlockSpec((tm, n_out), lambda i: (i, 0)),
        compiler_params=_cp(("parallel",)),
    )(a, w, *extras)


def _mm_tn(g, a, dy, l, kind, *, name, tt=512, a_fn=None):
    T = a.shape[0]
    _, _, d2, d3 = g.shape
    tt = min(tt, T)
    if kind == "col":
        a_spec = pl.BlockSpec((tt, d2), lambda j, t: (t, 0))
        d_spec = pl.BlockSpec((tt, d3), lambda j, t: (t, j))
    else:
        a_spec = pl.BlockSpec((tt, d2), lambda j, t: (t, j))
        d_spec = pl.BlockSpec((tt, d3), lambda j, t: (t, 0))

    def body(g_in, a_ref, d_ref, o_ref):
        del g_in
        av = a_ref[...]
        if a_fn is not None:
            av = a_fn(av)
        part = _dot_tn(av.astype(BF), d_ref[...].astype(BF))

        @pl.when(pl.program_id(1) == 0)
        def _():
            o_ref[...] = part

        @pl.when(pl.program_id(1) > 0)
        def _():
            o_ref[...] += part

    return pl.pallas_call(
        body, name=name, grid=(N_DEV, T // tt),
        out_shape=jax.ShapeDtypeStruct(g.shape, F32),
        in_specs=[pl.BlockSpec(memory_space=pl.ANY), a_spec, d_spec],
        out_specs=pl.BlockSpec((None, None, d2, d3), lambda j, t: (j, l, 0, 0)),
        input_output_aliases={0: 0},
        compiler_params=_cp(("parallel", "arbitrary")),
    )(g, a, dy)


def _rw(fn, tiles, rows, outs, sums, *, name, tr=256):
    T = tiles[0][0].shape[0]
    tr = min(tr, T)
    nt, nr, no, ns = len(tiles), len(rows), len(outs), len(sums)

    def body(*refs):
        t_refs, r_refs = refs[:nt], refs[nt:nt + nr]
        o_refs = refs[nt + nr:nt + nr + no]
        s_refs = refs[nt + nr + no:]
        res = fn(*[t[...] for t in t_refs], *[r[...] for r in r_refs])
        for o_ref, val in zip(o_refs, res[:no]):
            o_ref[...] = val.astype(o_ref.dtype)
        if ns:
            first = pl.program_id(0) == 0

            @pl.when(first)
            def _():
                for s_ref, val in zip(s_refs, res[no:]):
                    s_ref[...] = val

            @pl.when(jnp.logical_not(first))
            def _():
                for s_ref, val in zip(s_refs, res[no:]):
                    s_ref[...] += val

    in_specs = [pl.BlockSpec((tr, w), functools.partial(lambda i, cb: (i, cb), cb=cb)) for (_, w, cb) in tiles]
    in_specs += [pl.BlockSpec(r.shape, lambda i: (0, 0)) for r in rows]
    out_shape = [jax.ShapeDtypeStruct((T, w), dt) for (w, dt) in outs]
    out_shape += [jax.ShapeDtypeStruct((1, w), F32) for w in sums]
    out_specs = [pl.BlockSpec((tr, w), lambda i: (i, 0)) for (w, _) in outs]
    out_specs += [pl.BlockSpec((1, w), lambda i: (0, 0)) for w in sums]
    return pl.pallas_call(
        body, name=name, grid=(T // tr,), out_shape=out_shape,
        in_specs=in_specs, out_specs=out_specs,
        compiler_params=_cp(("arbitrary",)),
    )(*[t[0] for t in tiles], *rows)


def _ln_stats(z):
    mu = jnp.mean(z, axis=-1, keepdims=True)
    zc = z - mu
    var = jnp.mean(zc * zc, axis=-1, keepdims=True)
    rstd = lax.rsqrt(var + LN_EPS)
    return zc * rstd, rstd


def _ln_bwd_core(dy, zh, rstd, g):
    dzh = dy * g
    m1 = jnp.mean(dzh, axis=-1, keepdims=True)
    m2 = jnp.mean(dzh * zh, axis=-1, keepdims=True)
    return rstd * (dzh - m1 - zh * m2)


def _colsum(v):
    return jnp.sum(v, axis=0, keepdims=True)


def _ln_fwd(z, g, b, name):
    D = z.shape[1]

    def fn(zt, gr, br):
        zh, _ = _ln_stats(zt)
        return (zh * gr + br,)

    return _rw(fn, [(z, D, 0)], [g, b], [(D, F32)], [], name=name)[0]


def _ln_bwd(dy, z, g, name):
    D = z.shape[1]

    def fn(dyt, zt, gr):
        zh, rstd = _ln_stats(zt)
        dz = _ln_bwd_core(dyt, zh, rstd, gr)
        return dz, _colsum(dyt * zh), _colsum(dyt), _colsum(dz)

    return _rw(fn, [(dy, D, 0), (z, D, 0)], [g], [(D, F32)], [D, D, D], name=name)


def _heads(D):
    return D // HEAD


def _rms_parts(o, D):
    xs, rs = [], []
    for h in range(_heads(D)):
        oh = o[:, h * HEAD:(h + 1) * HEAD]
        r = lax.rsqrt(jnp.mean(oh * oh, axis=-1, keepdims=True) + RMS_EPS)
        xs.append(oh * r)
        rs.append(r)
    return xs, rs


def _rms_gate_fwd(o, proj, ng, name):
    D = o.shape[1]

    def fn(ot, grt, ngr):
        xs, _ = _rms_parts(ot, D)
        xh = jnp.concatenate(xs, axis=1)
        return (xh * ngr * (grt * _sig(grt)),)

    return _rw(fn, [(o, D, 0), (proj, D, 3)], [ng], [(D, BF)], [], name=name)[0]


def _rms_gate_bwd(dy, o, proj, ng, name):
    D = o.shape[1]

    def fn(dyt, ot, grt, ngr):
        xs, rs = _rms_parts(ot, D)
        xh = jnp.concatenate(xs, axis=1)
        sg = _sig(grt)
        on = xh * ngr
        dgr = dyt * on * (sg * (1.0 + grt * (1.0 - sg)))
        don = dyt * (grt * sg)
        dxh = don * ngr
        dos = []
        for h in range(_heads(D)):
            sl = slice(h * HEAD, (h + 1) * HEAD)
            m = jnp.mean(dxh[:, sl] * xs[h], axis=-1, keepdims=True)
            dos.append(rs[h] * (dxh[:, sl] - xs[h] * m))
        return jnp.concatenate(dos, axis=1), dgr, _colsum(don * xh)

    return _rw(fn, [(dy, D, 0), (o, D, 0), (proj, D, 3)], [ng], [(D, F32), (D, BF)], [D], name=name)


def _glu_fwd(u, name):
    D = u.shape[1] // 2

    def fn(at, gt):
        return (at * _sig(gt),)

    return _rw(fn, [(u, D, 0), (u, D, 1)], [], [(D, F32)], [], name=name)[0]


def _glu_bwd(dglu, u, name):
    D = u.shape[1] // 2

    def fn(dg, at, gt):
        sg = _sig(gt)
        du = jnp.concatenate([dg * sg, dg * at * sg * (1.0 - sg)], axis=1)
        return du, _colsum(du)

    return _rw(fn, [(dglu, D, 0), (u, D, 0), (u, D, 1)], [], [(2 * D, BF)], [2 * D], name=name)


def _silu_ln_bwd(ds, c, g, b, name):
    D = c.shape[1]

    def fn(dst, ct, gr, br):
        zh, rstd = _ln_stats(ct)
        ln = zh * gr + br
        sg = _sig(ln)
        dln = dst * (sg * (1.0 + ln * (1.0 - sg)))
        dc = _ln_bwd_core(dln, zh, rstd, gr)
        return dc, _colsum(dln * zh), _colsum(dln), _colsum(dc)

    return _rw(fn, [(ds, D, 0), (c, D, 0)], [g, b], [(D, F32)], [D, D, D], name=name)


def _loss_k(y, tgt, name):
    D = y.shape[1]

    def fn(yt, tt):
        e = yt - tt
        return e * (1.0 / D), _colsum(e * e) * (0.5 / D)

    return _rw(fn, [(y, D, 0), (tgt, D, 0)], [], [(D, F32)], [D], name=name)


def _shifted(ext, n, tr):
    for b in range(8):
        rb = ext if b == 0 else pltpu.roll(ext, n - b, 0)
        for a in range(HALO // 8 + 1):
            o = 8 * a + b
            if o <= HALO:
                yield o, rb[8 * a:8 * a + tr, :]


def _conv_fwd(glu, w, bdw, g, b, name, tr=256):
    T, D = glu.shape
    tr = min(tr, T // 2)
    n = tr + HALO
    hb = tr // HALO

    def body(cur_ref, halo_ref, w_ref, bdw_ref, g_ref, b_ref, c_ref, s_ref):
        i = pl.program_id(0)
        halo = jnp.where(i > 0, halo_ref[...], 0.0)
        ext = jnp.concatenate([halo, cur_ref[...]], axis=0)
        acc = jnp.zeros((tr, D), F32)
        for o, sh in _shifted(ext, n, tr):
            k = o - (HALO - CONV_W + 1)
            if 0 <= k < CONV_W:
                acc = acc + w_ref[k:k + 1, :] * sh
        cv = acc + bdw_ref[...]
        c_ref[...] = cv
        zh, _ = _ln_stats(cv)
        ln = zh * g_ref[...] + b_ref[...]
        s_ref[...] = (ln * _sig(ln)).astype(BF)

    row = pl.BlockSpec((1, D), lambda i: (0, 0))
    return pl.pallas_call(
        body, name=name, grid=(T // tr,),
        out_shape=[jax.ShapeDtypeStruct((T, D), F32), jax.ShapeDtypeStruct((T, D), BF)],
        in_specs=[pl.BlockSpec((tr, D), lambda i: (i, 0)),
                  pl.BlockSpec((HALO, D), lambda i: (jnp.maximum(i * hb - 1, 0), 0)),
                  pl.BlockSpec((HALO, D), lambda i: (0, 0)), row, row, row],
        out_specs=[pl.BlockSpec((tr, D), lambda i: (i, 0))] * 2,
        compiler_params=_cp(("parallel",)),
    )(glu, glu, w, bdw, g, b)


def _conv_bwd(dc, glu, w, name, tr=256):
    T, D = glu.shape
    tr = min(tr, T // 2)
    n = tr + HALO
    hb = tr // HALO
    nsteps = T // tr
    last_hb = T // HALO - 1

    def body(dc_ref, nxt_ref, glu_ref, prv_ref, w_ref, dg_ref, dw_ref):
        i = pl.program_id(0)
        dcv = dc_ref[...]
        nxt = jnp.where(i < nsteps - 1, nxt_ref[...], 0.0)
        ext2 = jnp.concatenate([dcv, nxt], axis=0)
        acc = jnp.zeros((tr, D), F32)
        for o, sh in _shifted(ext2, n, tr):
            k = CONV_W - 1 - o
            if 0 <= k < CONV_W:
                acc = acc + w_ref[k:k + 1, :] * sh
        dg_ref[...] = acc

        @pl.when(i == 0)
        def _():
            dw_ref[...] = jnp.zeros((HALO, D), F32)

        prv = jnp.where(i > 0, prv_ref[...], 0.0)
        ext = jnp.concatenate([prv, glu_ref[...]], axis=0)
        for o, sh in _shifted(ext, n, tr):
            k = o - (HALO - CONV_W + 1)
            if 0 <= k < CONV_W:
                dw_ref[k:k + 1, :] += _colsum(dcv * sh)

    return pl.pallas_call(
        body, name=name, grid=(nsteps,),
        out_shape=[jax.ShapeDtypeStruct((T, D), F32), jax.ShapeDtypeStruct((HALO, D), F32)],
        in_specs=[pl.BlockSpec((tr, D), lambda i: (i, 0)),
                  pl.BlockSpec((HALO, D), lambda i: (jnp.minimum((i + 1) * hb, last_hb), 0)),
                  pl.BlockSpec((tr, D), lambda i: (i, 0)),
                  pl.BlockSpec((HALO, D), lambda i: (jnp.maximum(i * hb - 1, 0), 0)),
                  pl.BlockSpec((HALO, D), lambda i: (0, 0))],
        out_specs=[pl.BlockSpec((tr, D), lambda i: (i, 0)),
                   pl.BlockSpec((HALO, D), lambda i: (0, 0))],
        compiler_params=_cp(("arbitrary",)),
    )(dc, dc, glu, glu, w)


def _rec_consts(C):
    t = np.arange(C)
    lb = (t[None, :] <= t[:, None]).astype(np.float32)

    def cum_at(idx):
        return (t[None, :] <= idx[:, None]).astype(np.float32)

    blocks, masks = [], []
    mid = SUB * (t // SUB) + SUB // 2 - 1
    eq = lb - cum_at(mid)
    blocks += [eq, -eq]
    masks.append(((t[:, None] // SUB) == (t[None, :] // SUB)) & (t[None, :] <= t[:, None]))
    nb = SUB
    while nb < C:
        odd = (t // nb) % 2 == 1
        e_t = nb * (t // nb) - 1
        e_s = nb * (t // nb) + nb - 1
        blocks.append(np.where(odd[:, None], lb - cum_at(e_t), 0.0))
        blocks.append(np.where(~odd[:, None], cum_at(e_s) - lb, 0.0))
        masks.append(((t[:, None] // (2 * nb)) == (t[None, :] // (2 * nb))) & odd[:, None] & ~odd[None, :])
        nb *= 2
    blocks += [lb, 1.0 - lb, np.ones((HEAD, C), np.float32)]
    L = np.concatenate(blocks, axis=0).astype(np.float32)
    L3 = np.concatenate([L, L, L], axis=1)
    LT3 = np.concatenate([L.T, L.T, L.T], axis=1)
    m = np.stack(masks).astype(np.float32)
    mT = np.transpose(m, (0, 2, 1)).copy()
    return (jnp.asarray(L3, BF), jnp.asarray(LT3, BF), jnp.asarray(m), jnp.asarray(mT), len(masks))


def _split3(x):
    h = x.astype(BF)
    r = x - h.astype(F32)
    m = r.astype(BF)
    lo = (r - m.astype(F32)).astype(BF)
    return h, m, lo


def _gates(qr, fz, lbr):
    sq = _sig(qr)
    q = qr * sq
    sg = _sig(fz)
    f = lbr + (1.0 - lbr) * sg
    fc = jnp.maximum(f, GATE_EPS)
    return q, 1.0 - f, jnp.log(fc), sq, sg, f, fc


def _rec_fwd(proj, lbr, name):
    T, D4 = proj.shape
    D = D4 // 4
    H = _heads(D)
    C = CHUNK
    nC = T // C
    L3, _, m, _, nl = _rec_consts(C)
    R = L3.shape[0]

    def body(q_ref, f_ref, v_ref, lb_ref, l_ref, m_ref, o_ref, s_ref, st):
        @pl.when(pl.program_id(0) == 0)
        def _():
            st[...] = jnp.zeros((H, HEAD, HEAD), F32)

        q, k, logf = _gates(q_ref[...], f_ref[...], lb_ref[...])[:3]
        ex = jnp.exp(_dot(l_ref[...], jnp.concatenate(_split3(logf), axis=0)))
        vb = v_ref[...].astype(BF)
        s_ref[0] = st[...]
        outs = []
        for h in range(H):
            sl = slice(h * HEAD, (h + 1) * HEAD)
            qh, kh = q[:, sl], k[:, sl]
            p = jnp.zeros((C, C), F32)
            for lv in range(nl):
                qt = (qh * ex[2 * lv * C:(2 * lv + 1) * C, sl]).astype(BF)
                kt = (kh * ex[(2 * lv + 1) * C:(2 * lv + 2) * C, sl]).astype(BF)
                p = p + jnp.where(m_ref[lv] > 0.0, _dot_nt(qt, kt), 0.0)
            base = 2 * nl * C
            qhat = (qh * ex[base:base + C, sl]).astype(BF)
            khat = (kh * ex[base + C:base + 2 * C, sl]).astype(BF)
            elast = ex[base + 2 * C:base + 2 * C + HEAD, sl]
            sth = st[h]
            outs.append(_dot(p.astype(BF), vb[:, sl]) + _dot_nt(qhat, sth.astype(BF)))
            st[h] = elast * sth + _dot_tn(vb[:, sl], khat)
        o_ref[...] = jnp.concatenate(outs, axis=1)

    def cblk(cb):
        return pl.BlockSpec((C, D), lambda i: (i, cb))

    return pl.pallas_call(
        body, name=name, grid=(nC,),
        out_shape=[jax.ShapeDtypeStruct((T, D), F32), jax.ShapeDtypeStruct((nC, H, HEAD, HEAD), F32)],
        in_specs=[cblk(0), cblk(1), cblk(2), pl.BlockSpec((1, D), lambda i: (0, 0)),
                  pl.BlockSpec(L3.shape, lambda i: (0, 0)), pl.BlockSpec(m.shape, lambda i: (0, 0, 0))],
        out_specs=[pl.BlockSpec((C, D), lambda i: (i, 0)),
                   pl.BlockSpec((1, H, HEAD, HEAD), lambda i: (i, 0, 0, 0))],
        scratch_shapes=[pltpu.VMEM((H, HEAD, HEAD), F32)],
        compiler_params=_cp(("arbitrary",)),
    )(proj, proj, proj, lbr, L3, m)


def _rec_bwd(proj, lbr, do, dgr, ssave, name):
    T, D4 = proj.shape
    D = D4 // 4
    H = _heads(D)
    C = CHUNK
    nC = T // C
    L3, LT3, m, mT, nl = _rec_consts(C)

    def body(q_ref, f_ref, v_ref, lb_ref, do_ref, dgr_ref, s_ref, l_ref, lt_ref, m_ref, mt_ref,
             dp_ref, dlb_ref, dst):
        @pl.when(pl.program_id(0) == 0)
        def _():
            dst[...] = jnp.zeros((H, HEAD, HEAD), F32)
            dlb_ref[...] = jnp.zeros((1, D), F32)

        qr = q_ref[...]
        lbv = lb_ref[...]
        q, k, logf, sq, sg, f, fc = _gates(qr, f_ref[...], lbv)
        ex = jnp.exp(_dot(l_ref[...], jnp.concatenate(_split3(logf), axis=0)))
        vb = v_ref[...].astype(BF)
        dob = do_ref[...].astype(BF)
        base = 2 * nl * C
        de = [[] for _ in range(2 * nl + 3)]
        dqs, dks, dvs = [], [], []
        for h in range(H):
            sl = slice(h * HEAD, (h + 1) * HEAD)
            qh, kh, vh, doh = q[:, sl], k[:, sl], vb[:, sl], dob[:, sl]
            dp = _dot_nt(doh, vh)
            dpt = _dot_nt(vh, doh)
            sth = s_ref[0, h]
            dsth = dst[h]
            dsb = dsth.astype(BF)
            pt = jnp.zeros((C, C), F32)
            dq = jnp.zeros((C, HEAD), F32)
            dk = jnp.zeros((C, HEAD), F32)
            for lv in range(nl):
                exq = ex[2 * lv * C:(2 * lv + 1) * C, sl]
                exk = ex[(2 * lv + 1) * C:(2 * lv + 2) * C, sl]
                qt = qh * exq
                kt = kh * exk
                qtb, ktb = qt.astype(BF), kt.astype(BF)
                pt = pt + jnp.where(mt_ref[lv] > 0.0, _dot_nt(ktb, qtb), 0.0)
                dqt = _dot(jnp.where(m_ref[lv] > 0.0, dp, 0.0).astype(BF), ktb)
                dkt = _dot(jnp.where(mt_ref[lv] > 0.0, dpt, 0.0).astype(BF), qtb)
                dq = dq + dqt * exq
                dk = dk + dkt * exk
                de[2 * lv].append(dqt * qt)
                de[2 * lv + 1].append(dkt * kt)
            exb = ex[base:base + C, sl]
            exkh = ex[base + C:base + 2 * C, sl]
            elast = ex[base + 2 * C:base + 2 * C + HEAD, sl]
            qhat = qh * exb
            khat = kh * exkh
            dqh = _dot(doh, sth.astype(BF))
            dkh = _dot(vh, dsb)
            dq = dq + dqh * exb
            dk = dk + dkh * exkh
            de[2 * nl].append(dqh * qhat)
            de[2 * nl + 1].append(dkh * khat)
            de[2 * nl + 2].append(dsth * sth * elast)
            dvs.append(_dot(pt.astype(BF), doh) + _dot_nt(khat.astype(BF), dsb))
            dst[h] = elast * dsth + _dot_tn(doh, qhat.astype(BF))
            dqs.append(dq)
            dks.append(dk)
        de_all = jnp.concatenate([jnp.concatenate(b, axis=1) for b in de], axis=0)
        dlogf = _dot(lt_ref[...], jnp.concatenate(_split3(de_all), axis=0))
        dq = jnp.concatenate(dqs, axis=1)
        dk = jnp.concatenate(dks, axis=1)
        dv = jnp.concatenate(dvs, axis=1)
        ind = jnp.where(f > GATE_EPS, 1.0, jnp.where(f == GATE_EPS, 0.5, 0.0))
        df = dlogf * ind / fc - dk
        dfz = df * (1.0 - lbv) * sg * (1.0 - sg)
        dlb_ref[...] += _colsum(df * (1.0 - sg))
        dqr = dq * (sq * (1.0 + qr * (1.0 - sq)))
        dp_ref[...] = jnp.concatenate([dqr.astype(BF), dfz.astype(BF), dv.astype(BF), dgr_ref[...]], axis=1)

    def cblk(cb):
        return pl.BlockSpec((C, D), lambda i: (nC - 1 - i, cb))

    def whole(a):
        nd = a.ndim
        return pl.BlockSpec(a.shape, lambda i: (0,) * nd)

    return pl.pallas_call(
        body, name=name, grid=(nC,),
        out_shape=[jax.ShapeDtypeStruct((T, D4), BF), jax.ShapeDtypeStruct((1, D), F32)],
        in_specs=[cblk(0), cblk(1), cblk(2), pl.BlockSpec((1, D), lambda i: (0, 0)),
                  pl.BlockSpec((C, D), lambda i: (nC - 1 - i, 0)),
                  pl.BlockSpec((C, D), lambda i: (nC - 1 - i, 0)),
                  pl.BlockSpec((1, H, HEAD, HEAD), lambda i: (nC - 1 - i, 0, 0, 0)),
                  whole(L3), whole(LT3), whole(m), whole(mT)],
        out_specs=[pl.BlockSpec((C, D4), lambda i: (nC - 1 - i, 0)),
                   pl.BlockSpec((1, D), lambda i: (0, 0))],
        scratch_shapes=[pltpu.VMEM((H, HEAD, HEAD), F32)],
        compiler_params=_cp(("arbitrary",)),
    )(proj, proj, proj, lbr, do, dgr, ssave, L3, LT3, m, mT)


def _softmax_rows(lg_ref):
    n = lg_ref.shape[0]
    rows = [lg_ref[l:l + 1, :] for l in range(n)]
    mx = rows[0]
    for r in rows[1:]:
        mx = jnp.maximum(mx, r)
    es = [jnp.exp(r - mx) for r in rows]
    tot = es[0]
    for e in es[1:]:
        tot = tot + e
    return [e / tot for e in es]


def _lb_fwd(logits):
    n, D = logits.shape

    def body(lg_ref, o_ref):
        soft = _softmax_rows(lg_ref)
        acc = jnp.zeros((1, D), F32)
        o_ref[0:1, :] = acc
        for j in range(1, n):
            acc = acc + soft[j]
            o_ref[j:j + 1, :] = acc

    return pl.pallas_call(body, name="lb_fwd", out_shape=jax.ShapeDtypeStruct((n, D), F32))(logits)


def _small_reduce(parts, logits, lb_row0):
    _, R, D = parts.shape
    n = logits.shape[0]

    def body(p_ref, lg_ref, o_ref):
        acc = p_ref[0]
        for d in range(1, N_DEV):
            acc = acc + p_ref[d]
        o_ref[...] = acc
        soft = _softmax_rows(lg_ref)
        dlb = [o_ref[lb_row0 + j:lb_row0 + j + 1, :] for j in range(n)]
        dsoft = [jnp.zeros((1, D), F32)]
        for l in range(1, n):
            s = dlb[l]
            for j in range(l + 1, n):
                s = s + dlb[j]
            dsoft.append(s)
        dot = soft[0] * dsoft[0]
        for l in range(1, n):
            dot = dot + soft[l] * dsoft[l]
        for l in range(n):
            o_ref[lb_row0 + l:lb_row0 + l + 1, :] = soft[l] * (dsoft[l] - dot)

    return pl.pallas_call(body, name="small_reduce", out_shape=jax.ShapeDtypeStruct((R, D), F32))(parts, logits)


def _adam_math(g, w, m, v):
    m2 = ADAM_B1 * m + (1.0 - ADAM_B1) * g
    v2 = ADAM_B2 * v + (1.0 - ADAM_B2) * (g * g)
    mh = m2 / (1.0 - ADAM_B1 ** ADAM_STEP)
    vh = v2 / (1.0 - ADAM_B2 ** ADAM_STEP)
    delta = -ADAM_LR * (mh / (jnp.sqrt(vh) + ADAM_EPS) + ADAM_WD * w)
    return delta, m2, v2


def _adamw(recv, w, m, v, name):
    nsrc = recv.shape[0]
    shp = w.shape
    cols = shp[-1]
    rows = int(np.prod(shp[:-1]))
    r2 = recv.reshape(nsrc, rows, cols)
    tr = min(rows, max(8, (1 << 20) // (cols * nsrc)))
    while rows % tr:
        tr //= 2

    def body(r_ref, w_ref, m_ref, v_ref, g_ref, d_ref, nm_ref, nv_ref):
        g = r_ref[0]
        for s in range(1, nsrc):
            g = g + r_ref[s]
        delta, m2, v2 = _adam_math(g, w_ref[...], m_ref[...], v_ref[...])
        g_ref[...] = g
        d_ref[...] = delta
        nm_ref[...] = m2
        nv_ref[...] = v2

    blk = pl.BlockSpec((tr, cols), lambda i: (i, 0))
    outs = pl.pallas_call(
        body, name=name, grid=(rows // tr,),
        out_shape=[jax.ShapeDtypeStruct((rows, cols), F32)] * 4,
        in_specs=[pl.BlockSpec((nsrc, tr, cols), lambda i: (0, i, 0)), blk, blk, blk],
        out_specs=[blk] * 4,
        compiler_params=_cp(("parallel",)),
    )(r2, w.reshape(rows, cols), m.reshape(rows, cols), v.reshape(rows, cols))
    return [o.reshape(shp) for o in outs]


def _pack(arrs, lane):
    flat = jnp.concatenate([a.reshape(-1) for a in arrs])
    n = flat.shape[0]
    rows = -(-n // lane)
    rows = -(-rows // 8) * 8
    flat = jnp.pad(flat, (0, rows * lane - n))
    return flat.reshape(rows, lane)


def _unpack(packed, shapes):
    flat = packed.reshape(-1)
    out, off = [], 0
    for s in shapes:
        n = int(np.prod(s))
        out.append(flat[off:off + n].reshape(s))
        off += n
    return out


def kernel(x, ln_mix_g, ln_mix_b, ln_ffn_g, ln_ffn_b, ffn_w1, ffn_w2, a_w_in, a_lb_logits, a_norm_g, a_w_out, b_w_pw1, b_b_pw1, b_w_dw, b_b_dw, b_ln_g, b_ln_b, b_w_pw2, b_b_pw2, loss_target, m_ln_mix_g, m_ln_mix_b, m_ln_ffn_g, m_ln_ffn_b, m_ffn_w1, m_ffn_w2, m_a_w_in, m_a_lb_logits, m_a_norm_g, m_a_w_out, m_b_w_pw1, m_b_b_pw1, m_b_w_dw, m_b_b_dw, m_b_ln_g, m_b_ln_b, m_b_w_pw2, m_b_b_pw2, v_ln_mix_g, v_ln_mix_b, v_ln_ffn_g, v_ln_ffn_b, v_ffn_w1, v_ffn_w2, v_a_w_in, v_a_lb_logits, v_a_norm_g, v_a_w_out, v_b_w_pw1, v_b_b_pw1, v_b_w_dw, v_b_b_dw, v_b_ln_g, v_b_ln_b, v_b_w_pw2, v_b_b_pw2):
    T, D = x.shape[1], x.shape[2]
    nA, nB = a_w_in.shape[0], b_w_pw1.shape[0]
    me = 4 * lax.axis_index("x") + 2 * lax.axis_index("y") + lax.axis_index("c")
    xin = x[0]
    tgt = loss_target[0]

    small_names = [b_b_pw1, b_w_dw, b_b_dw, b_ln_g, b_ln_b, b_b_pw2]
    sp = _pack(small_names, 128)
    W1, W2, Win, Wout, Wp1, Wp2, spg = _exchange(
        "gather",
        [ffn_w1.astype(BF), ffn_w2.astype(BF), a_w_in.astype(BF), a_w_out.astype(BF),
         b_w_pw1.astype(BF), b_w_pw2.astype(BF), sp], "gather_weights")
    sm = [jnp.stack(p) for p in zip(*[_unpack(spg[d], [a.shape for a in small_names]) for d in range(N_DEV)])]
    bpw1 = jnp.transpose(sm[0], (1, 0, 2)).reshape(nB, 1, 2 * D)
    wdw = jnp.transpose(sm[1], (1, 2, 0, 3)).reshape(nB, CONV_W, D)
    wdw = jnp.pad(wdw, ((0, 0), (0, HALO - CONV_W), (0, 0)))
    bdw, blng, blnb, bpw2 = [jnp.transpose(s, (1, 0, 2)).reshape(nB, 1, D) for s in sm[2:]]
    lb_all = _lb_fwd(a_lb_logits)

    saved = []
    h = xin
    for i in range(DEPTH):
        j = i // 2
        sv = {"xin": h}
        if i % 2 == 0:
            proj = _mm(h, Win, j, "nn_col", out_dtype=F32, name=f"a{j}_proj")
            o, ssave = _rec_fwd(proj, lb_all[j:j + 1], f"a{j}_rec_fwd")
            yg = _rms_gate_fwd(o, proj, a_norm_g[j:j + 1], f"a{j}_gate_fwd")
            z1 = _mm(yg, Wout, j, "nn_row", out_dtype=F32, name=f"a{j}_out",
                     epi=lambda acc, r: acc + ALPHA * r, extras=[h])
            sv.update(proj=proj, o=o, ssave=ssave, yg=yg)
        else:
            u = _mm(h, Wp1, j, "nn_col", out_dtype=F32, name=f"b{j}_pw1",
                    epi=lambda acc, b: acc + b, extras=[bpw1[j]])
            glu = _glu_fwd(u, f"b{j}_glu")
            cv, s = _conv_fwd(glu, wdw[j], bdw[j], blng[j], blnb[j], f"b{j}_conv")
            z1 = _mm(s, Wp2, j, "nn_row", out_dtype=F32, name=f"b{j}_pw2",
                     epi=lambda acc, b, r: acc + b + ALPHA * r, extras=[bpw2[j], h])
            sv.update(u=u, glu=glu, cv=cv, s=s)
        x1 = _ln_fwd(z1, ln_mix_g[i:i + 1], ln_mix_b[i:i + 1], f"l{i}_ln_mix")
        hh = _mm(x1, W1, i, "nn_col", out_dtype=BF, name=f"l{i}_ffn_up")
        z2 = _mm(hh, W2, i, "nn_row", out_dtype=F32, name=f"l{i}_ffn_down",
                 a_fn=lambda t: jnp.square(jnp.maximum(t, 0)),
                 epi=lambda acc, r: acc + ALPHA * r, extras=[x1])
        h = _ln_fwd(z2, ln_ffn_g[i:i + 1], ln_ffn_b[i:i + 1], f"l{i}_ln_ffn")
        sv.update(z1=z1, x1=x1, hh=hh, z2=z2)
        saved.append(sv)

    dx, loss_row = _loss_k(h, tgt, "loss")

    gW1 = jnp.zeros((N_DEV,) + ffn_w1.shape, F32)
    gW2 = jnp.zeros((N_DEV,) + ffn_w2.shape, F32)
    gWin = jnp.zeros((N_DEV,) + a_w_in.shape, F32)
    gWout = jnp.zeros((N_DEV,) + a_w_out.shape, F32)
    gWp1 = jnp.zeros((N_DEV,) + b_w_pw1.shape, F32)
    gWp2 = jnp.zeros((N_DEV,) + b_w_pw2.shape, F32)
    rows = {}
    for i in reversed(range(DEPTH)):
        j = i // 2
        sv = saved[i]
        dz2, dg, db, _ = _ln_bwd(dx, sv["z2"], ln_ffn_g[i:i + 1], f"l{i}_ln_ffn_bwd")
        rows[("ffn_g", i)], rows[("ffn_b", i)] = dg, db
        dh = _mm(dz2, W2, i, "nt_row", out_dtype=BF, name=f"l{i}_ffn_down_dx",
                 epi=lambda acc, hv: acc * (2.0 * jnp.maximum(hv.astype(F32), 0.0)), extras=[sv["hh"]])
        gW2 = _mm_tn(gW2, sv["hh"], dz2, i, "row", name=f"l{i}_ffn_down_dw",
                     a_fn=lambda t: jnp.square(jnp.maximum(t, 0)))
        gW1 = _mm_tn(gW1, sv["x1"], dh, i, "col", name=f"l{i}_ffn_up_dw")
        dx1 = _mm(dh, W1, i, "nt_col", out_dtype=F32, name=f"l{i}_ffn_up_dx",
                  epi=lambda acc, r: acc + ALPHA * r, extras=[dz2])
        dz1, dg, db, dz1sum = _ln_bwd(dx1, sv["z1"], ln_mix_g[i:i + 1], f"l{i}_ln_mix_bwd")
        rows[("mix_g", i)], rows[("mix_b", i)] = dg, db
        if i % 2 == 0:
            gWout = _mm_tn(gWout, sv["yg"], dz1, j, "row", name=f"a{j}_out_dw")
            dyg = _mm(dz1, Wout, j, "nt_row", out_dtype=F32, name=f"a{j}_out_dx")
            do, dgr, dng = _rms_gate_bwd(dyg, sv["o"], sv["proj"], a_norm_g[j:j + 1], f"a{j}_gate_bwd")
            rows[("ng", j)] = dng
            dproj, dlb = _rec_bwd(sv["proj"], lb_all[j:j + 1], do, dgr, sv["ssave"], f"a{j}_rec_bwd")
            rows[("lb", j)] = dlb
            gWin = _mm_tn(gWin, sv["xin"], dproj, j, "col", name=f"a{j}_proj_dw")
            dx = _mm(dproj, Win, j, "nt_col", out_dtype=F32, name=f"a{j}_proj_dx",
                     epi=lambda acc, r: acc + ALPHA * r, extras=[dz1])
        else:
            rows[("bpw2", j)] = dz1sum
            gWp2 = _mm_tn(gWp2, sv["s"], dz1, j, "row", name=f"b{j}_pw2_dw")
            ds = _mm(dz1, Wp2, j, "nt_row", out_dtype=F32, name=f"b{j}_pw2_dx")
            dc, dlg, dlb_, dcs = _silu_ln_bwd(ds, sv["cv"], blng[j], blnb[j], f"b{j}_ln_bwd")
            rows[("blng", j)], rows[("blnb", j)], rows[("bdw", j)] = dlg, dlb_, dcs
            dglu, dwdw = _conv_bwd(dc, sv["glu"], wdw[j], f"b{j}_conv_bwd")
            rows[("wdw", j)] = dwdw[:CONV_W]
            du, dbu = _glu_bwd(dglu, sv["u"], f"b{j}_glu_bwd")
            rows[("bpw1", j)] = dbu.reshape(2, D)
            gWp1 = _mm_tn(gWp1, sv["xin"], du, j, "col", name=f"b{j}_pw1_dw")
            dx = _mm(du, Wp1, j, "nt_col", out_dtype=F32, name=f"b{j}_pw1_dx",
                     epi=lambda acc, r: acc + ALPHA * r, extras=[dz1])
    grad_x = dx[None]

    order = ([("mix_g", i) for i in range(DEPTH)] + [("mix_b", i) for i in range(DEPTH)]
             + [("ffn_g", i) for i in range(DEPTH)] + [("ffn_b", i) for i in range(DEPTH)])
    lb_row0 = len(order)
    order += [("lb", j) for j in range(nA)] + [("ng", j) for j in range(nA)]
    for j in range(nB):
        order += [("bpw1", j), ("wdw", j), ("bdw", j), ("blng", j), ("blnb", j), ("bpw2", j)]
    pieces, offs, off = [], {}, 0
    for key in order:
        offs[key] = off
        pieces.append(rows[key])
        off += rows[key].shape[0]
    offs["loss"] = off
    pieces.append(loss_row)
    off += 1
    part = jnp.concatenate(pieces, axis=0)
    part = jnp.pad(part, ((0, -off % 8), (0, 0)))
    parts = _exchange("gather", [part], "gather_small_grads")[0]
    G = _small_reduce(parts, a_lb_logits, lb_row0)
    loss = jnp.sum(G[offs["loss"]])

    def rep(kind, n):
        return jnp.concatenate([G[offs[(kind, i)]:offs[(kind, i)] + 1] for i in range(n)], axis=0)

    def shard_cols(full, width):
        return lax.dynamic_slice_in_dim(full, me * width, width, axis=full.ndim - 1)

    g_small = {
        "ln_mix_g": rep("mix_g", DEPTH), "ln_mix_b": rep("mix_b", DEPTH),
        "ln_ffn_g": rep("ffn_g", DEPTH), "ln_ffn_b": rep("ffn_b", DEPTH),
        "a_lb_logits": rep("lb", nA), "a_norm_g": rep("ng", nA),
        "b_b_pw1": shard_cols(jnp.stack([G[offs[("bpw1", j)]:offs[("bpw1", j)] + 2].reshape(2 * D)
                                         for j in range(nB)]), 2 * D // N_DEV),
        "b_w_dw": shard_cols(jnp.stack([G[offs[("wdw", j)]:offs[("wdw", j)] + CONV_W] for j in range(nB)]),
                             D // N_DEV),
        "b_b_dw": shard_cols(rep("bdw", nB), D // N_DEV),
        "b_ln_g": shard_cols(rep("blng", nB), D // N_DEV),
        "b_ln_b": shard_cols(rep("blnb", nB), D // N_DEV),
        "b_b_pw2": shard_cols(rep("bpw2", nB), D // N_DEV),
    }
    small_w = {"ln_mix_g": (ln_mix_g, m_ln_mix_g, v_ln_mix_g), "ln_mix_b": (ln_mix_b, m_ln_mix_b, v_ln_mix_b),
               "ln_ffn_g": (ln_ffn_g, m_ln_ffn_g, v_ln_ffn_g), "ln_ffn_b": (ln_ffn_b, m_ln_ffn_b, v_ln_ffn_b),
               "a_lb_logits": (a_lb_logits, m_a_lb_logits, v_a_lb_logits),
               "a_norm_g": (a_norm_g, m_a_norm_g, v_a_norm_g),
               "b_b_pw1": (b_b_pw1, m_b_b_pw1, v_b_b_pw1), "b_w_dw": (b_w_dw, m_b_w_dw, v_b_w_dw),
               "b_b_dw": (b_b_dw, m_b_b_dw, v_b_b_dw), "b_ln_g": (b_ln_g, m_b_ln_g, v_b_ln_g),
               "b_ln_b": (b_ln_b, m_b_ln_b, v_b_ln_b), "b_b_pw2": (b_b_pw2, m_b_b_pw2, v_b_b_pw2)}
    snames = list(small_w)
    sshapes = [small_w[k][0].shape for k in snames]
    pg = _pack([g_small[k] for k in snames], 1024)
    pw, pm, pv = [_pack([small_w[k][q] for k in snames], 1024) for q in range(3)]
    sres = _adamw(pg[None], pw, pm, pv, "adamw_small")
    sres = [dict(zip(snames, _unpack(r, sshapes))) for r in sres]

    recv = _exchange("scatter", [gW1, gW2, gWin, gWout, gWp1, gWp2], "scatter_grads")
    big = {}
    for nm, rv, (w, m, v) in zip(
            ["ffn_w1", "ffn_w2", "a_w_in", "a_w_out", "b_w_pw1", "b_w_pw2"], recv,
            [(ffn_w1, m_ffn_w1, v_ffn_w1), (ffn_w2, m_ffn_w2, v_ffn_w2), (a_w_in, m_a_w_in, v_a_w_in),
             (a_w_out, m_a_w_out, v_a_w_out), (b_w_pw1, m_b_w_pw1, v_b_w_pw1), (b_w_pw2, m_b_w_pw2, v_b_w_pw2)]):
        big[nm] = _adamw(rv, w, m, v, f"adamw_{nm}")

    names = ["ln_mix_g", "ln_mix_b", "ln_ffn_g", "ln_ffn_b", "ffn_w1", "ffn_w2", "a_w_in", "a_lb_logits",
             "a_norm_g", "a_w_out", "b_w_pw1", "b_b_pw1", "b_w_dw", "b_b_dw", "b_ln_g", "b_ln_b", "b_w_pw2",
             "b_b_pw2"]
    out = [loss, grad_x]
    for q in range(4):
        for nm in names:
            out.append(big[nm][q] if nm in big else sres[q][nm])
    return tuple(out)
```

```python
import functools

import numpy as np
import jax
import jax.numpy as jnp
from jax import lax
from jax.experimental import pallas as pl
from jax.experimental.pallas import tpu as pltpu

F32 = jnp.float32
BF = jnp.bfloat16

N_DEV = 8
DEPTH = 4
HEAD = 128
CHUNK = 32
SUB = 8
CONV_W = 31
HALO = 32
ALPHA = (2.0 * DEPTH) ** 0.25
LN_EPS = 1e-5
RMS_EPS = 1e-6
GATE_EPS = 1e-6
ADAM_LR = 0.001
ADAM_B1 = 0.9
ADAM_B2 = 0.999
ADAM_EPS = 1e-08
ADAM_WD = 0.01
ADAM_STEP = 10
VMEM_LIMIT = 56 * 1024 * 1024
MESH = pl.DeviceIdType.MESH


def _cp(sem=None):
    return pltpu.CompilerParams(vmem_limit_bytes=VMEM_LIMIT, dimension_semantics=sem)


def _dot(a, b):
    return jnp.dot(a, b, preferred_element_type=F32)


def _dot_nt(a, b):
    return lax.dot_general(a, b, (((1,), (1,)), ((), ())), preferred_element_type=F32)


def _dot_tn(a, b):
    return lax.dot_general(a, b, (((0,), (0,)), ((), ())), preferred_element_type=F32)


def _sig(x):
    return jax.nn.sigmoid(x)


def _exchange(mode, arrays, name):
    n = len(arrays)
    if mode == "gather":
        out_shape = [jax.ShapeDtypeStruct((N_DEV,) + a.shape, a.dtype) for a in arrays]
    else:
        out_shape = [jax.ShapeDtypeStruct(a.shape, a.dtype) for a in arrays]

    def body(*refs):
        ins, outs = refs[:n], refs[n:2 * n]
        send_sems, recv_sems, loc_sems = refs[2 * n:]
        x, y, c = lax.axis_index("x"), lax.axis_index("y"), lax.axis_index("c")
        me = 4 * x + 2 * y + c
        started = []
        for t in range(n):
            src = ins[t] if mode == "gather" else ins[t].at[me]
            lc = pltpu.make_async_copy(src, outs[t].at[me], loc_sems.at[t])
            lc.start()
            started.append(lc)
        remote = []
        for r in range(1, N_DEV):
            px = (1 - x) if (r >> 2) & 1 else x
            py = (1 - y) if (r >> 1) & 1 else y
            pc = (1 - c) if r & 1 else c
            pid = 4 * px + 2 * py + pc
            for t in range(n):
                src = ins[t] if mode == "gather" else ins[t].at[pid]
                cp = pltpu.make_async_remote_copy(
                    src_ref=src, dst_ref=outs[t].at[me],
                    send_sem=send_sems.at[t, r - 1], recv_sem=recv_sems.at[t, r - 1],
                    device_id=(px, py, pc), device_id_type=MESH)
                cp.start()
                remote.append(cp)
        for cp in remote:
            cp.wait_send()
        for cp in remote:
            cp.wait_recv()
        for lc in started:
            lc.wait()

    anyspec = pl.BlockSpec(memory_space=pl.ANY)
    return pl.pallas_call(
        body, name=name, out_shape=out_shape,
        in_specs=[anyspec] * n, out_specs=[anyspec] * n,
        scratch_shapes=[pltpu.SemaphoreType.DMA((n, N_DEV - 1)),
                        pltpu.SemaphoreType.DMA((n, N_DEV - 1)),
                        pltpu.SemaphoreType.DMA((n,))],
        compiler_params=pltpu.CompilerParams(has_side_effects=True),
    )(*arrays)


def _mm(a, w, l, kind, *, out_dtype, name, tm=256, a_fn=None, epi=None, extras=()):
    T, Ka = a.shape
    _, _, d2, d3 = w.shape
    n_out = {"nn_col": N_DEV * d3, "nn_row": d3, "nt_col": d2, "nt_row": N_DEV * d2}[kind]
    tm = min(tm, T)
    ne = len(extras)

    def body(*refs):
        a_ref, w_ref = refs[0], refs[1]
        e_refs = refs[2:2 + ne]
        o_ref = refs[2 + ne]
        av = a_ref[...]
        if a_fn is not None:
            av = a_fn(av)
        av = av.astype(BF)
        if kind == "nn_col":
            acc = jnp.concatenate([_dot(av, w_ref[j]) for j in range(N_DEV)], axis=1)
        elif kind == "nn_row":
            acc = _dot(av, w_ref[...].reshape(N_DEV * d2, d3))
        elif kind == "nt_col":
            acc = _dot_nt(av[:, 0:d3], w_ref[0])
            for j in range(1, N_DEV):
                acc = acc + _dot_nt(av[:, j * d3:(j + 1) * d3], w_ref[j])
        else:
            acc = jnp.concatenate([_dot_nt(av, w_ref[j]) for j in range(N_DEV)], axis=1)
        if epi is not None:
            acc = epi(acc, *[e[...] for e in e_refs])
        o_ref[...] = acc.astype(out_dtype)

    in_specs = [pl.BlockSpec((tm, Ka), lambda i: (i, 0)),
                pl.BlockSpec((N_DEV, None, d2, d3), lambda i: (0, l, 0, 0))]
    for e in extras:
        if e.shape[0] == 1:
            in_specs.append(pl.BlockSpec((1, n_out), lambda i: (0, 0)))
        else:
            in_specs.append(pl.BlockSpec((tm, n_out), lambda i: (i, 0)))
    return pl.pallas_call(
        body, name=name, grid=(T // tm,),
        out_shape=jax.ShapeDtypeStruct((T, n_out), out_dtype),
        in_specs=in_specs, out_specs=pl.BlockSpec((tm, n_out), lambda i: (i, 0)),
        compiler_params=_cp(("parallel",)),
    )(a, w, *extras)


def _mm_tn(g, s, b, l, kind, *, name, tt=512, b_fn=None):
    T, ws = s.shape
    wb = b.shape[1]
    _, _, d2, d3 = g.shape
    tt = min(tt, T)
    nsteps = T // tt
    cw = wb // N_DEV

    def body(g_in, s_ref, b_ref, o_ref, acc):
        del g_in
        t = pl.program_id(0)
        st = s_ref[...].astype(F32).T.astype(BF)
        bv = b_ref[...]
        if b_fn is not None:
            bv = b_fn(bv)
        bv = bv.astype(BF)
        for j in range(N_DEV):
            part = _dot(st, bv[:, j * cw:(j + 1) * cw])

            @pl.when(t == 0)
            def _():
                acc[:, j * cw:(j + 1) * cw] = part

            @pl.when(t > 0)
            def _():
                acc[:, j * cw:(j + 1) * cw] += part

        @pl.when(t == nsteps - 1)
        def _():
            for j in range(N_DEV):
                if kind == "cols":
                    blk = acc[:, j * d3:(j + 1) * d3]
                elif kind == "rows":
                    blk = acc[j * d2:(j + 1) * d2, :]
                else:
                    blk = acc[:, j * d2:(j + 1) * d2].T
                o_ref[j] = blk.astype(BF)

    return pl.pallas_call(
        body, name=name, grid=(nsteps,),
        out_shape=jax.ShapeDtypeStruct(g.shape, BF),
        in_specs=[pl.BlockSpec(memory_space=pl.ANY),
                  pl.BlockSpec((tt, ws), lambda t: (t, 0)),
                  pl.BlockSpec((tt, wb), lambda t: (t, 0))],
        out_specs=pl.BlockSpec((N_DEV, None, d2, d3), lambda t: (0, l, 0, 0)),
        scratch_shapes=[pltpu.VMEM((ws, wb), F32)],
        input_output_aliases={0: 0},
        compiler_params=_cp(("arbitrary",)),
    )(g, s, b)


def _rw(fn, tiles, rows, outs, sums, *, name, tr=256):
    T = tiles[0][0].shape[0]
    tr = min(tr, T)
    nt, nr, no, ns = len(tiles), len(rows), len(outs), len(sums)

    def body(*refs):
        t_refs, r_refs = refs[:nt], refs[nt:nt + nr]
        o_refs = refs[nt + nr:nt + nr + no]
        s_refs = refs[nt + nr + no:]
        res = fn(*[t[...] for t in t_refs], *[r[...] for r in r_refs])
        for o_ref, val in zip(o_refs, res[:no]):
            o_ref[...] = val.astype(o_ref.dtype)
        if ns:
            first = pl.program_id(0) == 0

            @pl.when(first)
            def _():
                for s_ref, val in zip(s_refs, res[no:]):
                    s_ref[...] = val

            @pl.when(jnp.logical_not(first))
            def _():
                for s_ref, val in zip(s_refs, res[no:]):
                    s_ref[...] += val

    in_specs = [pl.BlockSpec((tr, w), functools.partial(lambda i, cb: (i, cb), cb=cb)) for (_, w, cb) in tiles]
    in_specs += [pl.BlockSpec(r.shape, lambda i: (0, 0)) for r in rows]
    out_shape = [jax.ShapeDtypeStruct((T, w), dt) for (w, dt) in outs]
    out_shape += [jax.ShapeDtypeStruct((1, w), F32) for w in sums]
    out_specs = [pl.BlockSpec((tr, w), lambda i: (i, 0)) for (w, _) in outs]
    out_specs += [pl.BlockSpec((1, w), lambda i: (0, 0)) for w in sums]
    return pl.pallas_call(
        body, name=name, grid=(T // tr,), out_shape=out_shape,
        in_specs=in_specs, out_specs=out_specs,
        compiler_params=_cp(("arbitrary",)),
    )(*[t[0] for t in tiles], *rows)


def _ln_stats(z):
    mu = jnp.mean(z, axis=-1, keepdims=True)
    zc = z - mu
    var = jnp.mean(zc * zc, axis=-1, keepdims=True)
    rstd = lax.rsqrt(var + LN_EPS)
    return zc * rstd, rstd


def _ln_bwd_core(dy, zh, rstd, g):
    dzh = dy * g
    m1 = jnp.mean(dzh, axis=-1, keepdims=True)
    m2 = jnp.mean(dzh * zh, axis=-1, keepdims=True)
    return rstd * (dzh - m1 - zh * m2)


def _colsum(v):
    return jnp.sum(v, axis=0, keepdims=True)


def _ln_fwd(z, g, b, name):
    D = z.shape[1]

    def fn(zt, gr, br):
        zh, _ = _ln_stats(zt)
        return (zh * gr + br,)

    return _rw(fn, [(z, D, 0)], [g, b], [(D, F32)], [], name=name)[0]


def _ln_bwd(dy, z, g, name):
    D = z.shape[1]

    def fn(dyt, zt, gr):
        zh, rstd = _ln_stats(zt)
        dz = _ln_bwd_core(dyt, zh, rstd, gr)
        return dz, _colsum(dyt * zh), _colsum(dyt), _colsum(dz)

    return _rw(fn, [(dy, D, 0), (z, D, 0)], [g], [(D, F32)], [D, D, D], name=name)


def _heads(D):
    return D // HEAD


def _rms_parts(o, D):
    xs, rs = [], []
    for h in range(_heads(D)):
        oh = o[:, h * HEAD:(h + 1) * HEAD]
        r = lax.rsqrt(jnp.mean(oh * oh, axis=-1, keepdims=True) + RMS_EPS)
        xs.append(oh * r)
        rs.append(r)
    return xs, rs


def _rms_gate_fwd(o, proj, ng, name):
    D = o.shape[1]

    def fn(ot, grt, ngr):
        xs, _ = _rms_parts(ot, D)
        xh = jnp.concatenate(xs, axis=1)
        return (xh * ngr * (grt * _sig(grt)),)

    return _rw(fn, [(o, D, 0), (proj, D, 3)], [ng], [(D, BF)], [], name=name)[0]


def _rms_gate_bwd(dy, o, proj, ng, name):
    D = o.shape[1]

    def fn(dyt, ot, grt, ngr):
        xs, rs = _rms_parts(ot, D)
        xh = jnp.concatenate(xs, axis=1)
        sg = _sig(grt)
        on = xh * ngr
        dgr = dyt * on * (sg * (1.0 + grt * (1.0 - sg)))
        don = dyt * (grt * sg)
        dxh = don * ngr
        dos = []
        for h in range(_heads(D)):
            sl = slice(h * HEAD, (h + 1) * HEAD)
            m = jnp.mean(dxh[:, sl] * xs[h], axis=-1, keepdims=True)
            dos.append(rs[h] * (dxh[:, sl] - xs[h] * m))
        return jnp.concatenate(dos, axis=1), dgr, _colsum(don * xh)

    return _rw(fn, [(dy, D, 0), (o, D, 0), (proj, D, 3)], [ng], [(D, F32), (D, BF)], [D], name=name)


def _glu_fwd(u, name):
    D = u.shape[1] // 2

    def fn(at, gt):
        return (at * _sig(gt),)

    return _rw(fn, [(u, D, 0), (u, D, 1)], [], [(D, F32)], [], name=name)[0]


def _glu_bwd(dglu, u, name):
    D = u.shape[1] // 2

    def fn(dg, at, gt):
        sg = _sig(gt)
        du = jnp.concatenate([dg * sg, dg * at * sg * (1.0 - sg)], axis=1)
        return du, _colsum(du)

    return _rw(fn, [(dglu, D, 0), (u, D, 0), (u, D, 1)], [], [(2 * D, BF)], [2 * D], name=name)


def _silu_ln_bwd(ds, c, g, b, name):
    D = c.shape[1]

    def fn(dst, ct, gr, br):
        zh, rstd = _ln_stats(ct)
        ln = zh * gr + br
        sg = _sig(ln)
        dln = dst * (sg * (1.0 + ln * (1.0 - sg)))
        dc = _ln_bwd_core(dln, zh, rstd, gr)
        return dc, _colsum(dln * zh), _colsum(dln), _colsum(dc)

    return _rw(fn, [(ds, D, 0), (c, D, 0)], [g, b], [(D, F32)], [D, D, D], name=name)


def _loss_k(y, tgt, name):
    D = y.shape[1]

    def fn(yt, tt):
        e = yt - tt
        return e * (1.0 / D), _colsum(e * e) * (0.5 / D)

    return _rw(fn, [(y, D, 0), (tgt, D, 0)], [], [(D, F32)], [D], name=name)


def _shifted(ext, n, tr):
    for b in range(8):
        rb = ext if b == 0 else pltpu.roll(ext, n - b, 0)
        for a in range(HALO // 8 + 1):
            o = 8 * a + b
            if o <= HALO:
                yield o, rb[8 * a:8 * a + tr, :]


def _conv_fwd(glu, w, bdw, g, b, name, tr=256):
    T, D = glu.shape
    tr = min(tr, T // 2)
    n = tr + HALO
    hb = tr // HALO

    def body(cur_ref, halo_ref, w_ref, bdw_ref, g_ref, b_ref, c_ref, s_ref):
        i = pl.program_id(0)
        halo = jnp.where(i > 0, halo_ref[...], 0.0)
        ext = jnp.concatenate([halo, cur_ref[...]], axis=0)
        acc = jnp.zeros((tr, D), F32)
        for o, sh in _shifted(ext, n, tr):
            k = o - (HALO - CONV_W + 1)
            if 0 <= k < CONV_W:
                acc = acc + w_ref[k:k + 1, :] * sh
        cv = acc + bdw_ref[...]
        c_ref[...] = cv
        zh, _ = _ln_stats(cv)
        ln = zh * g_ref[...] + b_ref[...]
        s_ref[...] = (ln * _sig(ln)).astype(BF)

    row = pl.BlockSpec((1, D), lambda i: (0, 0))
    return pl.pallas_call(
        body, name=name, grid=(T // tr,),
        out_shape=[jax.ShapeDtypeStruct((T, D), F32), jax.ShapeDtypeStruct((T, D), BF)],
        in_specs=[pl.BlockSpec((tr, D), lambda i: (i, 0)),
                  pl.BlockSpec((HALO, D), lambda i: (jnp.maximum(i * hb - 1, 0), 0)),
                  pl.BlockSpec((HALO, D), lambda i: (0, 0)), row, row, row],
        out_specs=[pl.BlockSpec((tr, D), lambda i: (i, 0))] * 2,
        compiler_params=_cp(("parallel",)),
    )(glu, glu, w, bdw, g, b)


def _conv_bwd(dc, glu, w, name, tr=256):
    T, D = glu.shape
    tr = min(tr, T // 2)
    n = tr + HALO
    hb = tr // HALO
    nsteps = T // tr
    last_hb = T // HALO - 1

    def body(dc_ref, nxt_ref, glu_ref, prv_ref, w_ref, dg_ref, dw_ref):
        i = pl.program_id(0)
        dcv = dc_ref[...]
        nxt = jnp.where(i < nsteps - 1, nxt_ref[...], 0.0)
        ext2 = jnp.concatenate([dcv, nxt], axis=0)
        acc = jnp.zeros((tr, D), F32)
        for o, sh in _shifted(ext2, n, tr):
            k = CONV_W - 1 - o
            if 0 <= k < CONV_W:
                acc = acc + w_ref[k:k + 1, :] * sh
        dg_ref[...] = acc

        @pl.when(i == 0)
        def _():
            dw_ref[...] = jnp.zeros((HALO, D), F32)

        prv = jnp.where(i > 0, prv_ref[...], 0.0)
        ext = jnp.concatenate([prv, glu_ref[...]], axis=0)
        for o, sh in _shifted(ext, n, tr):
            k = o - (HALO - CONV_W + 1)
            if 0 <= k < CONV_W:
                dw_ref[k:k + 1, :] += _colsum(dcv * sh)

    return pl.pallas_call(
        body, name=name, grid=(nsteps,),
        out_shape=[jax.ShapeDtypeStruct((T, D), F32), jax.ShapeDtypeStruct((HALO, D), F32)],
        in_specs=[pl.BlockSpec((tr, D), lambda i: (i, 0)),
                  pl.BlockSpec((HALO, D), lambda i: (jnp.minimum((i + 1) * hb, last_hb), 0)),
                  pl.BlockSpec((tr, D), lambda i: (i, 0)),
                  pl.BlockSpec((HALO, D), lambda i: (jnp.maximum(i * hb - 1, 0), 0)),
                  pl.BlockSpec((HALO, D), lambda i: (0, 0))],
        out_specs=[pl.BlockSpec((tr, D), lambda i: (i, 0)),
                   pl.BlockSpec((HALO, D), lambda i: (0, 0))],
        compiler_params=_cp(("arbitrary",)),
    )(dc, dc, glu, glu, w)


def _rec_consts(C):
    t = np.arange(C)
    lb = (t[None, :] <= t[:, None]).astype(np.float32)

    def cum_at(idx):
        return (t[None, :] <= idx[:, None]).astype(np.float32)

    blocks, masks = [], []
    mid = SUB * (t // SUB) + SUB // 2 - 1
    eq = lb - cum_at(mid)
    blocks += [eq, -eq]
    masks.append(((t[:, None] // SUB) == (t[None, :] // SUB)) & (t[None, :] <= t[:, None]))
    nb = SUB
    while nb < C:
        odd = (t // nb) % 2 == 1
        e_t = nb * (t // nb) - 1
        e_s = nb * (t // nb) + nb - 1
        blocks.append(np.where(odd[:, None], lb - cum_at(e_t), 0.0))
        blocks.append(np.where(~odd[:, None], cum_at(e_s) - lb, 0.0))
        masks.append(((t[:, None] // (2 * nb)) == (t[None, :] // (2 * nb))) & odd[:, None] & ~odd[None, :])
        nb *= 2
    blocks += [lb, 1.0 - lb, np.ones((HEAD, C), np.float32)]
    L = np.concatenate(blocks, axis=0).astype(np.float32)
    L3 = np.concatenate([L, L, L], axis=1)
    LT3 = np.concatenate([L.T, L.T, L.T], axis=1)
    m = np.stack(masks).astype(np.float32)
    mT = np.transpose(m, (0, 2, 1)).copy()
    return (jnp.asarray(L3, BF), jnp.asarray(LT3, BF), jnp.asarray(m), jnp.asarray(mT), len(masks))


def _split3(x):
    h = x.astype(BF)
    r = x - h.astype(F32)
    m = r.astype(BF)
    lo = (r - m.astype(F32)).astype(BF)
    return h, m, lo


def _gates(qr, fz, lbr):
    sq = _sig(qr)
    q = qr * sq
    sg = _sig(fz)
    f = lbr + (1.0 - lbr) * sg
    fc = jnp.maximum(f, GATE_EPS)
    return q, 1.0 - f, jnp.log(fc), sq, sg, f, fc


def _rec_fwd(proj, lbr, name):
    T, D4 = proj.shape
    D = D4 // 4
    H = _heads(D)
    C = CHUNK
    nC = T // C
    L3, _, m, _, nl = _rec_consts(C)
    R = L3.shape[0]

    def body(q_ref, f_ref, v_ref, lb_ref, l_ref, m_ref, o_ref, s_ref, st):
        @pl.when(pl.program_id(0) == 0)
        def _():
            st[...] = jnp.zeros((H, HEAD, HEAD), F32)

        q, k, logf = _gates(q_ref[...], f_ref[...], lb_ref[...])[:3]
        ex = jnp.exp(_dot(l_ref[...], jnp.concatenate(_split3(logf), axis=0)))
        vb = v_ref[...].astype(BF)
        s_ref[0] = st[...]
        outs = []
        for h in range(H):
            sl = slice(h * HEAD, (h + 1) * HEAD)
            qh, kh = q[:, sl], k[:, sl]
            p = jnp.zeros((C, C), F32)
            for lv in range(nl):
                qt = (qh * ex[2 * lv * C:(2 * lv + 1) * C, sl]).astype(BF)
                kt = (kh * ex[(2 * lv + 1) * C:(2 * lv + 2) * C, sl]).astype(BF)
                p = p + jnp.where(m_ref[lv] > 0.0, _dot_nt(qt, kt), 0.0)
            base = 2 * nl * C
            qhat = (qh * ex[base:base + C, sl]).astype(BF)
            khat = (kh * ex[base + C:base + 2 * C, sl]).astype(BF)
            elast = ex[base + 2 * C:base + 2 * C + HEAD, sl]
            sth = st[h]
            outs.append(_dot(p.astype(BF), vb[:, sl]) + _dot_nt(qhat, sth.astype(BF)))
            st[h] = elast * sth + _dot_tn(vb[:, sl], khat)
        o_ref[...] = jnp.concatenate(outs, axis=1)

    def cblk(cb):
        return pl.BlockSpec((C, D), lambda i: (i, cb))

    return pl.pallas_call(
        body, name=name, grid=(nC,),
        out_shape=[jax.ShapeDtypeStruct((T, D), F32), jax.ShapeDtypeStruct((nC, H, HEAD, HEAD), F32)],
        in_specs=[cblk(0), cblk(1), cblk(2), pl.BlockSpec((1, D), lambda i: (0, 0)),
                  pl.BlockSpec(L3.shape, lambda i: (0, 0)), pl.BlockSpec(m.shape, lambda i: (0, 0, 0))],
        out_specs=[pl.BlockSpec((C, D), lambda i: (i, 0)),
                   pl.BlockSpec((1, H, HEAD, HEAD), lambda i: (i, 0, 0, 0))],
        scratch_shapes=[pltpu.VMEM((H, HEAD, HEAD), F32)],
        compiler_params=_cp(("arbitrary",)),
    )(proj, proj, proj, lbr, L3, m)


def _rec_bwd(proj, lbr, do, dgr, ssave, name):
    T, D4 = proj.shape
    D = D4 // 4
    H = _heads(D)
    C = CHUNK
    nC = T // C
    L3, LT3, m, mT, nl = _rec_consts(C)

    def body(q_ref, f_ref, v_ref, lb_ref, do_ref, dgr_ref, s_ref, l_ref, lt_ref, m_ref, mt_ref,
             dp_ref, dlb_ref, dst):
        @pl.when(pl.program_id(0) == 0)
        def _():
            dst[...] = jnp.zeros((H, HEAD, HEAD), F32)
            dlb_ref[...] = jnp.zeros((1, D), F32)

        qr = q_ref[...]
        lbv = lb_ref[...]
        q, k, logf, sq, sg, f, fc = _gates(qr, f_ref[...], lbv)
        ex = jnp.exp(_dot(l_ref[...], jnp.concatenate(_split3(logf), axis=0)))
        vb = v_ref[...].astype(BF)
        dob = do_ref[...].astype(BF)
        base = 2 * nl * C
        de = [[] for _ in range(2 * nl + 3)]
        dqs, dks, dvs = [], [], []
        for h in range(H):
            sl = slice(h * HEAD, (h + 1) * HEAD)
            qh, kh, vh, doh = q[:, sl], k[:, sl], vb[:, sl], dob[:, sl]
            dp = _dot_nt(doh, vh)
            dpt = _dot_nt(vh, doh)
            sth = s_ref[0, h]
            dsth = dst[h]
            dsb = dsth.astype(BF)
            pt = jnp.zeros((C, C), F32)
            dq = jnp.zeros((C, HEAD), F32)
            dk = jnp.zeros((C, HEAD), F32)
            for lv in range(nl):
                exq = ex[2 * lv * C:(2 * lv + 1) * C, sl]
                exk = ex[(2 * lv + 1) * C:(2 * lv + 2) * C, sl]
                qt = qh * exq
                kt = kh * exk
                qtb, ktb = qt.astype(BF), kt.astype(BF)
                pt = pt + jnp.where(mt_ref[lv] > 0.0, _dot_nt(ktb, qtb), 0.0)
                dqt = _dot(jnp.where(m_ref[lv] > 0.0, dp, 0.0).astype(BF), ktb)
                dkt = _dot(jnp.where(mt_ref[lv] > 0.0, dpt, 0.0).astype(BF), qtb)
                dq = dq + dqt * exq
                dk = dk + dkt * exk
                de[2 * lv].append(dqt * qt)
                de[2 * lv + 1].append(dkt * kt)
            exb = ex[base:base + C, sl]
            exkh = ex[base + C:base + 2 * C, sl]
            elast = ex[base + 2 * C:base + 2 * C + HEAD, sl]
            qhat = qh * exb
            khat = kh * exkh
            dqh = _dot(doh, sth.astype(BF))
            dkh = _dot(vh, dsb)
            dq = dq + dqh * exb
            dk = dk + dkh * exkh
            de[2 * nl].append(dqh * qhat)
            de[2 * nl + 1].append(dkh * khat)
            de[2 * nl + 2].append(dsth * sth * elast)
            dvs.append(_dot(pt.astype(BF), doh) + _dot_nt(khat.astype(BF), dsb))
            dst[h] = elast * dsth + _dot_tn(doh, qhat.astype(BF))
            dqs.append(dq)
            dks.append(dk)
        de_all = jnp.concatenate([jnp.concatenate(b, axis=1) for b in de], axis=0)
        dlogf = _dot(lt_ref[...], jnp.concatenate(_split3(de_all), axis=0))
        dq = jnp.concatenate(dqs, axis=1)
        dk = jnp.concatenate(dks, axis=1)
        dv = jnp.concatenate(dvs, axis=1)
        ind = jnp.where(f > GATE_EPS, 1.0, jnp.where(f == GATE_EPS, 0.5, 0.0))
        df = dlogf * ind / fc - dk
        dfz = df * (1.0 - lbv) * sg * (1.0 - sg)
        dlb_ref[...] += _colsum(df * (1.0 - sg))
        dqr = dq * (sq * (1.0 + qr * (1.0 - sq)))
        dp_ref[...] = jnp.concatenate([dqr.astype(BF), dfz.astype(BF), dv.astype(BF), dgr_ref[...]], axis=1)

    def cblk(cb):
        return pl.BlockSpec((C, D), lambda i: (nC - 1 - i, cb))

    def whole(a):
        nd = a.ndim
        return pl.BlockSpec(a.shape, lambda i: (0,) * nd)

    return pl.pallas_call(
        body, name=name, grid=(nC,),
        out_shape=[jax.ShapeDtypeStruct((T, D4), BF), jax.ShapeDtypeStruct((1, D), F32)],
        in_specs=[cblk(0), cblk(1), cblk(2), pl.BlockSpec((1, D), lambda i: (0, 0)),
                  pl.BlockSpec((C, D), lambda i: (nC - 1 - i, 0)),
                  pl.BlockSpec((C, D), lambda i: (nC - 1 - i, 0)),
                  pl.BlockSpec((1, H, HEAD, HEAD), lambda i: (nC - 1 - i, 0, 0, 0)),
                  whole(L3), whole(LT3), whole(m), whole(mT)],
        out_specs=[pl.BlockSpec((C, D4), lambda i: (nC - 1 - i, 0)),
                   pl.BlockSpec((1, D), lambda i: (0, 0))],
        scratch_shapes=[pltpu.VMEM((H, HEAD, HEAD), F32)],
        compiler_params=_cp(("arbitrary",)),
    )(proj, proj, proj, lbr, do, dgr, ssave, L3, LT3, m, mT)


def _softmax_rows(lg_ref):
    n = lg_ref.shape[0]
    rows = [lg_ref[l:l + 1, :] for l in range(n)]
    mx = rows[0]
    for r in rows[1:]:
        mx = jnp.maximum(mx, r)
    es = [jnp.exp(r - mx) for r in rows]
    tot = es[0]
    for e in es[1:]:
        tot = tot + e
    return [e / tot for e in es]


def _lb_fwd(logits):
    n, D = logits.shape

    def body(lg_ref, o_ref):
        soft = _softmax_rows(lg_ref)
        acc = jnp.zeros((1, D), F32)
        o_ref[0:1, :] = acc
        for j in range(1, n):
            acc = acc + soft[j]
            o_ref[j:j + 1, :] = acc

    return pl.pallas_call(body, name="lb_fwd", out_shape=jax.ShapeDtypeStruct((n, D), F32))(logits)


def _small_reduce(parts, logits, lb_row0):
    _, R, D = parts.shape
    n = logits.shape[0]

    def body(p_ref, lg_ref, o_ref):
        acc = p_ref[0]
        for d in range(1, N_DEV):
            acc = acc + p_ref[d]
        o_ref[...] = acc
        soft = _softmax_rows(lg_ref)
        dlb = [o_ref[lb_row0 + j:lb_row0 + j + 1, :] for j in range(n)]
        dsoft = [jnp.zeros((1, D), F32)]
        for l in range(1, n):
            s = dlb[l]
            for j in range(l + 1, n):
                s = s + dlb[j]
            dsoft.append(s)
        dot = soft[0] * dsoft[0]
        for l in range(1, n):
            dot = dot + soft[l] * dsoft[l]
        for l in range(n):
            o_ref[lb_row0 + l:lb_row0 + l + 1, :] = soft[l] * (dsoft[l] - dot)

    return pl.pallas_call(body, name="small_reduce", out_shape=jax.ShapeDtypeStruct((R, D), F32))(parts, logits)


def _adam_math(g, w, m, v):
    m2 = ADAM_B1 * m + (1.0 - ADAM_B1) * g
    v2 = ADAM_B2 * v + (1.0 - ADAM_B2) * (g * g)
    mh = m2 / (1.0 - ADAM_B1 ** ADAM_STEP)
    vh = v2 / (1.0 - ADAM_B2 ** ADAM_STEP)
    delta = -ADAM_LR * (mh / (jnp.sqrt(vh) + ADAM_EPS) + ADAM_WD * w)
    return delta, m2, v2


def _adamw(recv, w, m, v, name):
    nsrc = recv.shape[0]
    shp = w.shape
    cols = shp[-1]
    rows = int(np.prod(shp[:-1]))
    r2 = recv.reshape(nsrc, rows, cols)
    tr = min(rows, max(8, (1 << 20) // (cols * nsrc)))
    while rows % tr:
        tr //= 2

    def body(r_ref, w_ref, m_ref, v_ref, g_ref, d_ref, nm_ref, nv_ref):
        g = r_ref[0].astype(F32)
        for s in range(1, nsrc):
            g = g + r_ref[s].astype(F32)
        delta, m2, v2 = _adam_math(g, w_ref[...], m_ref[...], v_ref[...])
        g_ref[...] = g
        d_ref[...] = delta
        nm_ref[...] = m2
        nv_ref[...] = v2

    blk = pl.BlockSpec((tr, cols), lambda i: (i, 0))
    outs = pl.pallas_call(
        body, name=name, grid=(rows // tr,),
        out_shape=[jax.ShapeDtypeStruct((rows, cols), F32)] * 4,
        in_specs=[pl.BlockSpec((nsrc, tr, cols), lambda i: (0, i, 0)), blk, blk, blk],
        out_specs=[blk] * 4,
        compiler_params=_cp(("parallel",)),
    )(r2, w.reshape(rows, cols), m.reshape(rows, cols), v.reshape(rows, cols))
    return [o.reshape(shp) for o in outs]


def _pack(arrs, lane):
    flat = jnp.concatenate([a.reshape(-1) for a in arrs])
    n = flat.shape[0]
    rows = -(-n // lane)
    rows = -(-rows // 8) * 8
    flat = jnp.pad(flat, (0, rows * lane - n))
    return flat.reshape(rows, lane)


def _unpack(packed, shapes):
    flat = packed.reshape(-1)
    out, off = [], 0
    for s in shapes:
        n = int(np.prod(s))
        out.append(flat[off:off + n].reshape(s))
        off += n
    return out


def kernel(x, ln_mix_g, ln_mix_b, ln_ffn_g, ln_ffn_b, ffn_w1, ffn_w2, a_w_in, a_lb_logits, a_norm_g, a_w_out, b_w_pw1, b_b_pw1, b_w_dw, b_b_dw, b_ln_g, b_ln_b, b_w_pw2, b_b_pw2, loss_target, m_ln_mix_g, m_ln_mix_b, m_ln_ffn_g, m_ln_ffn_b, m_ffn_w1, m_ffn_w2, m_a_w_in, m_a_lb_logits, m_a_norm_g, m_a_w_out, m_b_w_pw1, m_b_b_pw1, m_b_w_dw, m_b_b_dw, m_b_ln_g, m_b_ln_b, m_b_w_pw2, m_b_b_pw2, v_ln_mix_g, v_ln_mix_b, v_ln_ffn_g, v_ln_ffn_b, v_ffn_w1, v_ffn_w2, v_a_w_in, v_a_lb_logits, v_a_norm_g, v_a_w_out, v_b_w_pw1, v_b_b_pw1, v_b_w_dw, v_b_b_dw, v_b_ln_g, v_b_ln_b, v_b_w_pw2, v_b_b_pw2):
    T, D = x.shape[1], x.shape[2]
    nA, nB = a_w_in.shape[0], b_w_pw1.shape[0]
    me = 4 * lax.axis_index("x") + 2 * lax.axis_index("y") + lax.axis_index("c")
    xin = x[0]
    tgt = loss_target[0]

    small_names = [b_b_pw1, b_w_dw, b_b_dw, b_ln_g, b_ln_b, b_b_pw2]
    sp = _pack(small_names, 128)
    W1, W2, Win, Wout, Wp1, Wp2, spg = _exchange(
        "gather",
        [ffn_w1.astype(BF), ffn_w2.astype(BF), a_w_in.astype(BF), a_w_out.astype(BF),
         b_w_pw1.astype(BF), b_w_pw2.astype(BF), sp], "gather_weights")
    sm = [jnp.stack(p) for p in zip(*[_unpack(spg[d], [a.shape for a in small_names]) for d in range(N_DEV)])]
    bpw1 = jnp.transpose(sm[0], (1, 0, 2)).reshape(nB, 1, 2 * D)
    wdw = jnp.transpose(sm[1], (1, 2, 0, 3)).reshape(nB, CONV_W, D)
    wdw = jnp.pad(wdw, ((0, 0), (0, HALO - CONV_W), (0, 0)))
    bdw, blng, blnb, bpw2 = [jnp.transpose(s, (1, 0, 2)).reshape(nB, 1, D) for s in sm[2:]]
    lb_all = _lb_fwd(a_lb_logits)

    saved = []
    h = xin
    for i in range(DEPTH):
        j = i // 2
        sv = {"xin": h}
        if i % 2 == 0:
            proj = _mm(h, Win, j, "nn_col", out_dtype=F32, name=f"a{j}_proj")
            o, ssave = _rec_fwd(proj, lb_all[j:j + 1], f"a{j}_rec_fwd")
            yg = _rms_gate_fwd(o, proj, a_norm_g[j:j + 1], f"a{j}_gate_fwd")
            z1 = _mm(yg, Wout, j, "nn_row", out_dtype=F32, name=f"a{j}_out",
                     epi=lambda acc, r: acc + ALPHA * r, extras=[h])
            sv.update(proj=proj, o=o, ssave=ssave, yg=yg)
        else:
            u = _mm(h, Wp1, j, "nn_col", out_dtype=F32, name=f"b{j}_pw1",
                    epi=lambda acc, b: acc + b, extras=[bpw1[j]])
            glu = _glu_fwd(u, f"b{j}_glu")
            cv, s = _conv_fwd(glu, wdw[j], bdw[j], blng[j], blnb[j], f"b{j}_conv")
            z1 = _mm(s, Wp2, j, "nn_row", out_dtype=F32, name=f"b{j}_pw2",
                     epi=lambda acc, b, r: acc + b + ALPHA * r, extras=[bpw2[j], h])
            sv.update(u=u, glu=glu, cv=cv, s=s)
        x1 = _ln_fwd(z1, ln_mix_g[i:i + 1], ln_mix_b[i:i + 1], f"l{i}_ln_mix")
        hh = _mm(x1, W1, i, "nn_col", out_dtype=BF, name=f"l{i}_ffn_up")
        z2 = _mm(hh, W2, i, "nn_row", out_dtype=F32, name=f"l{i}_ffn_down",
                 a_fn=lambda t: jnp.square(jnp.maximum(t, 0)),
                 epi=lambda acc, r: acc + ALPHA * r, extras=[x1])
        h = _ln_fwd(z2, ln_ffn_g[i:i + 1], ln_ffn_b[i:i + 1], f"l{i}_ln_ffn")
        sv.update(z1=z1, x1=x1, hh=hh, z2=z2)
        saved.append(sv)

    dx, loss_row = _loss_k(h, tgt, "loss")

    gW1 = jnp.zeros((N_DEV,) + ffn_w1.shape, BF)
    gW2 = jnp.zeros((N_DEV,) + ffn_w2.shape, BF)
    gWin = jnp.zeros((N_DEV,) + a_w_in.shape, BF)
    gWout = jnp.zeros((N_DEV,) + a_w_out.shape, BF)
    gWp1 = jnp.zeros((N_DEV,) + b_w_pw1.shape, BF)
    gWp2 = jnp.zeros((N_DEV,) + b_w_pw2.shape, BF)
    rows = {}
    for i in reversed(range(DEPTH)):
        j = i // 2
        sv = saved[i]
        dz2, dg, db, _ = _ln_bwd(dx, sv["z2"], ln_ffn_g[i:i + 1], f"l{i}_ln_ffn_bwd")
        rows[("ffn_g", i)], rows[("ffn_b", i)] = dg, db
        dh = _mm(dz2, W2, i, "nt_row", out_dtype=BF, name=f"l{i}_ffn_down_dx",
                 epi=lambda acc, hv: acc * (2.0 * jnp.maximum(hv.astype(F32), 0.0)), extras=[sv["hh"]])
        gW2 = _mm_tn(gW2, dz2, sv["hh"], i, "rows_t", name=f"l{i}_ffn_down_dw",
                     b_fn=lambda t: jnp.square(jnp.maximum(t, 0)))
        gW1 = _mm_tn(gW1, sv["x1"], dh, i, "cols", name=f"l{i}_ffn_up_dw")
        dx1 = _mm(dh, W1, i, "nt_col", out_dtype=F32, name=f"l{i}_ffn_up_dx",
                  epi=lambda acc, r: acc + ALPHA * r, extras=[dz2])
        dz1, dg, db, dz1sum = _ln_bwd(dx1, sv["z1"], ln_mix_g[i:i + 1], f"l{i}_ln_mix_bwd")
        rows[("mix_g", i)], rows[("mix_b", i)] = dg, db
        if i % 2 == 0:
            gWout = _mm_tn(gWout, sv["yg"], dz1, j, "rows", name=f"a{j}_out_dw")
            dyg = _mm(dz1, Wout, j, "nt_row", out_dtype=F32, name=f"a{j}_out_dx")
            do, dgr, dng = _rms_gate_bwd(dyg, sv["o"], sv["proj"], a_norm_g[j:j + 1], f"a{j}_gate_bwd")
            rows[("ng", j)] = dng
            dproj, dlb = _rec_bwd(sv["proj"], lb_all[j:j + 1], do, dgr, sv["ssave"], f"a{j}_rec_bwd")
            rows[("lb", j)] = dlb
            gWin = _mm_tn(gWin, sv["xin"], dproj, j, "cols", name=f"a{j}_proj_dw")
            dx = _mm(dproj, Win, j, "nt_col", out_dtype=F32, name=f"a{j}_proj_dx",
                     epi=lambda acc, r: acc + ALPHA * r, extras=[dz1])
        else:
            rows[("bpw2", j)] = dz1sum
            gWp2 = _mm_tn(gWp2, sv["s"], dz1, j, "rows", name=f"b{j}_pw2_dw")
            ds = _mm(dz1, Wp2, j, "nt_row", out_dtype=F32, name=f"b{j}_pw2_dx")
            dc, dlg, dlb_, dcs = _silu_ln_bwd(ds, sv["cv"], blng[j], blnb[j], f"b{j}_ln_bwd")
            rows[("blng", j)], rows[("blnb", j)], rows[("bdw", j)] = dlg, dlb_, dcs
            dglu, dwdw = _conv_bwd(dc, sv["glu"], wdw[j], f"b{j}_conv_bwd")
            rows[("wdw", j)] = dwdw[:CONV_W]
            du, dbu = _glu_bwd(dglu, sv["u"], f"b{j}_glu_bwd")
            rows[("bpw1", j)] = dbu.reshape(2, D)
            gWp1 = _mm_tn(gWp1, sv["xin"], du, j, "cols", name=f"b{j}_pw1_dw")
            dx = _mm(du, Wp1, j, "nt_col", out_dtype=F32, name=f"b{j}_pw1_dx",
                     epi=lambda acc, r: acc + ALPHA * r, extras=[dz1])
    grad_x = dx[None]

    order = ([("mix_g", i) for i in range(DEPTH)] + [("mix_b", i) for i in range(DEPTH)]
             + [("ffn_g", i) for i in range(DEPTH)] + [("ffn_b", i) for i in range(DEPTH)])
    lb_row0 = len(order)
    order += [("lb", j) for j in range(nA)] + [("ng", j) for j in range(nA)]
    for j in range(nB):
        order += [("bpw1", j), ("wdw", j), ("bdw", j), ("blng", j), ("blnb", j), ("bpw2", j)]
    pieces, offs, off = [], {}, 0
    for key in order:
        offs[key] = off
        pieces.append(rows[key])
        off += rows[key].shape[0]
    offs["loss"] = off
    pieces.append(loss_row)
    off += 1
    part = jnp.concatenate(pieces, axis=0)
    part = jnp.pad(part, ((0, -off % 8), (0, 0)))
    parts = _exchange("gather", [part], "gather_small_grads")[0]
    G = _small_reduce(parts, a_lb_logits, lb_row0)
    loss = jnp.sum(G[offs["loss"]])

    def rep(kind, n):
        return jnp.concatenate([G[offs[(kind, i)]:offs[(kind, i)] + 1] for i in range(n)], axis=0)

    def shard_cols(full, width):
        return lax.dynamic_slice_in_dim(full, me * width, width, axis=full.ndim - 1)

    g_small = {
        "ln_mix_g": rep("mix_g", DEPTH), "ln_mix_b": rep("mix_b", DEPTH),
        "ln_ffn_g": rep("ffn_g", DEPTH), "ln_ffn_b": rep("ffn_b", DEPTH),
        "a_lb_logits": rep("lb", nA), "a_norm_g": rep("ng", nA),
        "b_b_pw1": shard_cols(jnp.stack([G[offs[("bpw1", j)]:offs[("bpw1", j)] + 2].reshape(2 * D)
                                         for j in range(nB)]), 2 * D // N_DEV),
        "b_w_dw": shard_cols(jnp.stack([G[offs[("wdw", j)]:offs[("wdw", j)] + CONV_W] for j in range(nB)]),
                             D // N_DEV),
        "b_b_dw": shard_cols(rep("bdw", nB), D // N_DEV),
        "b_ln_g": shard_cols(rep("blng", nB), D // N_DEV),
        "b_ln_b": shard_cols(rep("blnb", nB), D // N_DEV),
        "b_b_pw2": shard_cols(rep("bpw2", nB), D // N_DEV),
    }
    small_w = {"ln_mix_g": (ln_mix_g, m_ln_mix_g, v_ln_mix_g), "ln_mix_b": (ln_mix_b, m_ln_mix_b, v_ln_mix_b),
               "ln_ffn_g": (ln_ffn_g, m_ln_ffn_g, v_ln_ffn_g), "ln_ffn_b": (ln_ffn_b, m_ln_ffn_b, v_ln_ffn_b),
               "a_lb_logits": (a_lb_logits, m_a_lb_logits, v_a_lb_logits),
               "a_norm_g": (a_norm_g, m_a_norm_g, v_a_norm_g),
               "b_b_pw1": (b_b_pw1, m_b_b_pw1, v_b_b_pw1), "b_w_dw": (b_w_dw, m_b_w_dw, v_b_w_dw),
               "b_b_dw": (b_b_dw, m_b_b_dw, v_b_b_dw), "b_ln_g": (b_ln_g, m_b_ln_g, v_b_ln_g),
               "b_ln_b": (b_ln_b, m_b_ln_b, v_b_ln_b), "b_b_pw2": (b_b_pw2, m_b_b_pw2, v_b_b_pw2)}
    snames = list(small_w)
    sshapes = [small_w[k][0].shape for k in snames]
    pg = _pack([g_small[k] for k in snames], 1024)
    pw, pm, pv = [_pack([small_w[k][q] for k in snames], 1024) for q in range(3)]
    sres = _adamw(pg[None], pw, pm, pv, "adamw_small")
    sres = [dict(zip(snames, _unpack(r, sshapes))) for r in sres]

    recv = _exchange("scatter", [gW1, gW2, gWin, gWout, gWp1, gWp2], "scatter_grads")
    big = {}
    for nm, rv, (w, m, v) in zip(
            ["ffn_w1", "ffn_w2", "a_w_in", "a_w_out", "b_w_pw1", "b_w_pw2"], recv,
            [(ffn_w1, m_ffn_w1, v_ffn_w1), (ffn_w2, m_ffn_w2, v_ffn_w2), (a_w_in, m_a_w_in, v_a_w_in),
             (a_w_out, m_a_w_out, v_a_w_out), (b_w_pw1, m_b_w_pw1, v_b_w_pw1), (b_w_pw2, m_b_w_pw2, v_b_w_pw2)]):
        big[nm] = _adamw(rv, w, m, v, f"adamw_{nm}")

    names = ["ln_mix_g", "ln_mix_b", "ln_ffn_g", "ln_ffn_b", "ffn_w1", "ffn_w2", "a_w_in", "a_lb_logits",
             "a_norm_g", "a_w_out", "b_w_pw1", "b_b_pw1", "b_w_dw", "b_b_dw", "b_ln_g", "b_ln_b", "b_w_pw2",
             "b_b_pw2"]
    out = [loss, grad_x]
    for q in range(4):
        for nm in names:
            out.append(big[nm][q] if nm in big else sres[q][nm])
    return tuple(out)
```

```python
import functools

import numpy as np
import jax
import jax.numpy as jnp
from jax import lax
from jax.experimental import pallas as pl
from jax.experimental.pallas import tpu as pltpu

F32 = jnp.float32
BF = jnp.bfloat16

N_DEV = 8
DEPTH = 4
HEAD = 128
CHUNK = 32
SUB = 8
CONV_W = 31
HALO = 32
ALPHA = (2.0 * DEPTH) ** 0.25
LN_EPS = 1e-5
RMS_EPS = 1e-6
GATE_EPS = 1e-6
ADAM_LR = 0.001
ADAM_B1 = 0.9
ADAM_B2 = 0.999
ADAM_EPS = 1e-08
ADAM_WD = 0.01
ADAM_STEP = 10
VMEM_LIMIT = 56 * 1024 * 1024
MESH = pl.DeviceIdType.MESH


def _cp(sem=None):
    return pltpu.CompilerParams(vmem_limit_bytes=VMEM_LIMIT, dimension_semantics=sem)


def _dot(a, b):
    return jnp.dot(a, b, preferred_element_type=F32)


def _dot_nt(a, b):
    return lax.dot_general(a, b, (((1,), (1,)), ((), ())), preferred_element_type=F32)


def _dot_tn(a, b):
    return lax.dot_general(a, b, (((0,), (0,)), ((), ())), preferred_element_type=F32)


def _sig(x):
    return jax.nn.sigmoid(x)


def _distinct(items):
    out = []
    for it in items:
        if not any(it is q for q in out):
            out.append(it)
    return out


def _index_of(items, it):
    return next(i for i, q in enumerate(items) if q is it)


def _comm_copies(mode, plan, src_refs, buf_refs, send_sems, recv_sems, loc_sems):
    x, y, c = lax.axis_index("x"), lax.axis_index("y"), lax.axis_index("c")
    me = 4 * x + 2 * y + c
    locs, rems = [], []
    for k, (si, l, bi) in enumerate(plan):
        src = src_refs[si].at[l] if mode == "gather" else src_refs[si].at[me, l]
        locs.append(pltpu.make_async_copy(src, buf_refs[bi].at[me, l], loc_sems.at[k]))
    for r in range(1, N_DEV):
        px = (1 - x) if (r >> 2) & 1 else x
        py = (1 - y) if (r >> 1) & 1 else y
        pc = (1 - c) if r & 1 else c
        pid = 4 * px + 2 * py + pc
        for k, (si, l, bi) in enumerate(plan):
            src = src_refs[si].at[l] if mode == "gather" else src_refs[si].at[pid, l]
            rems.append(pltpu.make_async_remote_copy(
                src_ref=src, dst_ref=buf_refs[bi].at[me, l],
                send_sem=send_sems.at[k, r - 1], recv_sem=recv_sems.at[k, r - 1],
                device_id=(px, py, pc), device_id_type=MESH))
    return locs, rems


def _call(body, *, name, grid, operands, in_specs, out_shape, out_specs, scratch=(), sem=None,
          aliases=None, comm=None):
    aliases = dict(aliases or {})
    if comm is None:
        return pl.pallas_call(
            body, name=name, grid=grid, out_shape=list(out_shape), in_specs=list(in_specs),
            out_specs=list(out_specs), scratch_shapes=list(scratch), input_output_aliases=aliases,
            compiler_params=_cp(sem),
        )(*operands)
    mode, pieces = comm
    srcs = _distinct([p[0] for p in pieces])
    bufs = _distinct([p[2] for p in pieces])
    plan = [(_index_of(srcs, s), l, _index_of(bufs, b)) for (s, l, b) in pieces]
    n_in, n_out, n_scr, ns, nb, npc = len(operands), len(out_shape), len(scratch), len(srcs), len(bufs), len(plan)
    nsteps = grid[0]

    def wrapped(*refs):
        ins = refs[:n_in]
        src_refs = refs[n_in:n_in + ns]
        o0 = n_in + ns + nb
        outs = refs[o0:o0 + n_out]
        buf_refs = refs[o0 + n_out:o0 + n_out + nb]
        s0 = o0 + n_out + nb
        scr = refs[s0:s0 + n_scr]
        sems = refs[s0 + n_scr:]
        step = pl.program_id(0)

        @pl.when(step == 0)
        def _():
            locs, rems = _comm_copies(mode, plan, src_refs, buf_refs, *sems)
            for d in locs + rems:
                d.start()

        body(*ins, *outs, *scr)

        @pl.when(step == nsteps - 1)
        def _():
            locs, rems = _comm_copies(mode, plan, src_refs, buf_refs, *sems)
            for d in rems:
                d.wait_send()
            for d in rems:
                d.wait_recv()
            for d in locs:
                d.wait()

    anyspec = pl.BlockSpec(memory_space=pl.ANY)
    for k in range(nb):
        aliases[n_in + ns + k] = n_out + k
    res = pl.pallas_call(
        wrapped, name=name, grid=grid,
        out_shape=list(out_shape) + [jax.ShapeDtypeStruct(b.shape, b.dtype) for b in bufs],
        in_specs=list(in_specs) + [anyspec] * (ns + nb),
        out_specs=list(out_specs) + [anyspec] * nb,
        scratch_shapes=list(scratch) + [pltpu.SemaphoreType.DMA((npc, N_DEV - 1)),
                                        pltpu.SemaphoreType.DMA((npc, N_DEV - 1)),
                                        pltpu.SemaphoreType.DMA((npc,))],
        input_output_aliases=aliases,
        compiler_params=pltpu.CompilerParams(vmem_limit_bytes=VMEM_LIMIT, has_side_effects=True,
                                             dimension_semantics=("arbitrary",) * len(grid)),
    )(*operands, *srcs, *bufs)
    return list(res[:n_out]), list(res[n_out:])


def _exchange(comm, name):
    def body():
        pass

    return _call(body, name=name, grid=(1,), operands=[], in_specs=[], out_shape=[], out_specs=[],
                 comm=comm)[1]


def _mm(a, w, l, kind, *, out_dtype, name, tm=256, a_fn=None, epi=None, extras=(), comm=None):
    T, Ka = a.shape
    _, _, d2, d3 = w.shape
    n_out = {"nn_col": N_DEV * d3, "nn_row": d3, "nt_col": d2, "nt_row": N_DEV * d2}[kind]
    tm = min(tm, T)
    ne = len(extras)

    def body(*refs):
        a_ref, w_ref = refs[0], refs[1]
        e_refs = refs[2:2 + ne]
        o_ref = refs[2 + ne]
        av = a_ref[...]
        if a_fn is not None:
            av = a_fn(av)
        av = av.astype(BF)
        if kind == "nn_col":
            acc = jnp.concatenate([_dot(av, w_ref[j]) for j in range(N_DEV)], axis=1)
        elif kind == "nn_row":
            acc = _dot(av, w_ref[...].reshape(N_DEV * d2, d3))
        elif kind == "nt_col":
            acc = _dot_nt(av[:, 0:d3], w_ref[0])
            for j in range(1, N_DEV):
                acc = acc + _dot_nt(av[:, j * d3:(j + 1) * d3], w_ref[j])
        else:
            acc = jnp.concatenate([_dot_nt(av, w_ref[j]) for j in range(N_DEV)], axis=1)
        if epi is not None:
            acc = epi(acc, *[e[...] for e in e_refs])
        o_ref[...] = acc.astype(out_dtype)

    in_specs = [pl.BlockSpec((tm, Ka), lambda i: (i, 0)),
                pl.BlockSpec((N_DEV, None, d2, d3), lambda i: (0, l, 0, 0))]
    for e in extras:
        if e.shape[0] == 1:
            in_specs.append(pl.BlockSpec((1, n_out), lambda i: (0, 0)))
        else:
            in_specs.append(pl.BlockSpec((tm, n_out), lambda i: (i, 0)))
    res = _call(body, name=name, grid=(T // tm,), operands=[a, w, *extras], in_specs=in_specs,
                out_shape=[jax.ShapeDtypeStruct((T, n_out), out_dtype)],
                out_specs=[pl.BlockSpec((tm, n_out), lambda i: (i, 0))], sem=("parallel",), comm=comm)
    return res[0] if comm is None else (res[0][0], res[1])


def _mm_tn(g, s, b, l, kind, *, name, tt=512, b_fn=None):
    T, ws = s.shape
    wb = b.shape[1]
    _, _, d2, d3 = g.shape
    tt = min(tt, T)
    nsteps = T // tt
    cw = wb // N_DEV

    def body(g_in, s_ref, b_ref, o_ref, acc):
        del g_in
        t = pl.program_id(0)
        st = s_ref[...].astype(F32).T.astype(BF)
        bv = b_ref[...]
        if b_fn is not None:
            bv = b_fn(bv)
        bv = bv.astype(BF)
        for j in range(N_DEV):
            part = _dot(st, bv[:, j * cw:(j + 1) * cw])

            @pl.when(t == 0)
            def _():
                acc[:, j * cw:(j + 1) * cw] = part

            @pl.when(t > 0)
            def _():
                acc[:, j * cw:(j + 1) * cw] += part

        @pl.when(t == nsteps - 1)
        def _():
            for j in range(N_DEV):
                if kind == "cols":
                    blk = acc[:, j * d3:(j + 1) * d3]
                elif kind == "rows":
                    blk = acc[j * d2:(j + 1) * d2, :]
                else:
                    blk = acc[:, j * d2:(j + 1) * d2].T
                o_ref[j] = blk.astype(BF)

    return pl.pallas_call(
        body, name=name, grid=(nsteps,),
        out_shape=jax.ShapeDtypeStruct(g.shape, BF),
        in_specs=[pl.BlockSpec(memory_space=pl.ANY),
                  pl.BlockSpec((tt, ws), lambda t: (t, 0)),
                  pl.BlockSpec((tt, wb), lambda t: (t, 0))],
        out_specs=pl.BlockSpec((N_DEV, None, d2, d3), lambda t: (0, l, 0, 0)),
        scratch_shapes=[pltpu.VMEM((ws, wb), F32)],
        input_output_aliases={0: 0},
        compiler_params=_cp(("arbitrary",)),
    )(g, s, b)


def _rw(fn, tiles, rows, outs, sums, *, name, tr=256):
    T = tiles[0][0].shape[0]
    tr = min(tr, T)
    nt, nr, no, ns = len(tiles), len(rows), len(outs), len(sums)

    def body(*refs):
        t_refs, r_refs = refs[:nt], refs[nt:nt + nr]
        o_refs = refs[nt + nr:nt + nr + no]
        s_refs = refs[nt + nr + no:]
        res = fn(*[t[...] for t in t_refs], *[r[...] for r in r_refs])
        for o_ref, val in zip(o_refs, res[:no]):
            o_ref[...] = val.astype(o_ref.dtype)
        if ns:
            first = pl.program_id(0) == 0

            @pl.when(first)
            def _():
                for s_ref, val in zip(s_refs, res[no:]):
                    s_ref[...] = val

            @pl.when(jnp.logical_not(first))
            def _():
                for s_ref, val in zip(s_refs, res[no:]):
                    s_ref[...] += val

    in_specs = [pl.BlockSpec((tr, w), functools.partial(lambda i, cb: (i, cb), cb=cb)) for (_, w, cb) in tiles]
    in_specs += [pl.BlockSpec(r.shape, lambda i: (0, 0)) for r in rows]
    out_shape = [jax.ShapeDtypeStruct((T, w), dt) for (w, dt) in outs]
    out_shape += [jax.ShapeDtypeStruct((1, w), F32) for w in sums]
    out_specs = [pl.BlockSpec((tr, w), lambda i: (i, 0)) for (w, _) in outs]
    out_specs += [pl.BlockSpec((1, w), lambda i: (0, 0)) for w in sums]
    return pl.pallas_call(
        body, name=name, grid=(T // tr,), out_shape=out_shape,
        in_specs=in_specs, out_specs=out_specs,
        compiler_params=_cp(("arbitrary",)),
    )(*[t[0] for t in tiles], *rows)


def _ln_stats(z):
    mu = jnp.mean(z, axis=-1, keepdims=True)
    zc = z - mu
    var = jnp.mean(zc * zc, axis=-1, keepdims=True)
    rstd = lax.rsqrt(var + LN_EPS)
    return zc * rstd, rstd


def _ln_bwd_core(dy, zh, rstd, g):
    dzh = dy * g
    m1 = jnp.mean(dzh, axis=-1, keepdims=True)
    m2 = jnp.mean(dzh * zh, axis=-1, keepdims=True)
    return rstd * (dzh - m1 - zh * m2)


def _colsum(v):
    return jnp.sum(v, axis=0, keepdims=True)


def _ln_fwd(z, g, b, name):
    D = z.shape[1]

    def fn(zt, gr, br):
        zh, _ = _ln_stats(zt)
        return (zh * gr + br,)

    return _rw(fn, [(z, D, 0)], [g, b], [(D, F32)], [], name=name)[0]


def _ln_bwd(dy, z, g, name):
    D = z.shape[1]

    def fn(dyt, zt, gr):
        zh, rstd = _ln_stats(zt)
        dz = _ln_bwd_core(dyt, zh, rstd, gr)
        return dz, _colsum(dyt * zh), _colsum(dyt), _colsum(dz)

    return _rw(fn, [(dy, D, 0), (z, D, 0)], [g], [(D, F32)], [D, D, D], name=name)


def _heads(D):
    return D // HEAD


def _rms_parts(o, D):
    xs, rs = [], []
    for h in range(_heads(D)):
        oh = o[:, h * HEAD:(h + 1) * HEAD]
        r = lax.rsqrt(jnp.mean(oh * oh, axis=-1, keepdims=True) + RMS_EPS)
        xs.append(oh * r)
        rs.append(r)
    return xs, rs


def _rms_gate_fwd(o, proj, ng, name):
    D = o.shape[1]

    def fn(ot, grt, ngr):
        xs, _ = _rms_parts(ot, D)
        xh = jnp.concatenate(xs, axis=1)
        return (xh * ngr * (grt * _sig(grt)),)

    return _rw(fn, [(o, D, 0), (proj, D, 3)], [ng], [(D, BF)], [], name=name)[0]


def _rms_gate_bwd(dy, o, proj, ng, name):
    D = o.shape[1]

    def fn(dyt, ot, grt, ngr):
        xs, rs = _rms_parts(ot, D)
        xh = jnp.concatenate(xs, axis=1)
        sg = _sig(grt)
        on = xh * ngr
        dgr = dyt * on * (sg * (1.0 + grt * (1.0 - sg)))
        don = dyt * (grt * sg)
        dxh = don * ngr
        dos = []
        for h in range(_heads(D)):
            sl = slice(h * HEAD, (h + 1) * HEAD)
            m = jnp.mean(dxh[:, sl] * xs[h], axis=-1, keepdims=True)
            dos.append(rs[h] * (dxh[:, sl] - xs[h] * m))
        return jnp.concatenate(dos, axis=1), dgr, _colsum(don * xh)

    return _rw(fn, [(dy, D, 0), (o, D, 0), (proj, D, 3)], [ng], [(D, F32), (D, BF)], [D], name=name)


def _glu_fwd(u, name):
    D = u.shape[1] // 2

    def fn(at, gt):
        return (at * _sig(gt),)

    return _rw(fn, [(u, D, 0), (u, D, 1)], [], [(D, F32)], [], name=name)[0]


def _glu_bwd(dglu, u, name):
    D = u.shape[1] // 2

    def fn(dg, at, gt):
        sg = _sig(gt)
        du = jnp.concatenate([dg * sg, dg * at * sg * (1.0 - sg)], axis=1)
        return du, _colsum(du)

    return _rw(fn, [(dglu, D, 0), (u, D, 0), (u, D, 1)], [], [(2 * D, BF)], [2 * D], name=name)


def _silu_ln_bwd(ds, c, g, b, name):
    D = c.shape[1]

    def fn(dst, ct, gr, br):
        zh, rstd = _ln_stats(ct)
        ln = zh * gr + br
        sg = _sig(ln)
        dln = dst * (sg * (1.0 + ln * (1.0 - sg)))
        dc = _ln_bwd_core(dln, zh, rstd, gr)
        return dc, _colsum(dln * zh), _colsum(dln), _colsum(dc)

    return _rw(fn, [(ds, D, 0), (c, D, 0)], [g, b], [(D, F32)], [D, D, D], name=name)


def _loss_k(y, tgt, name):
    D = y.shape[1]

    def fn(yt, tt):
        e = yt - tt
        return e * (1.0 / D), _colsum(e * e) * (0.5 / D)

    return _rw(fn, [(y, D, 0), (tgt, D, 0)], [], [(D, F32)], [D], name=name)


def _shifted(ext, n, tr):
    for b in range(8):
        rb = ext if b == 0 else pltpu.roll(ext, n - b, 0)
        for a in range(HALO // 8 + 1):
            o = 8 * a + b
            if o <= HALO:
                yield o, rb[8 * a:8 * a + tr, :]


def _conv_fwd(glu, w, bdw, g, b, name, tr=256, comm=None):
    T, D = glu.shape
    tr = min(tr, T // 2)
    n = tr + HALO
    hb = tr // HALO

    def body(cur_ref, halo_ref, w_ref, bdw_ref, g_ref, b_ref, c_ref, s_ref):
        i = pl.program_id(0)
        halo = jnp.where(i > 0, halo_ref[...], 0.0)
        ext = jnp.concatenate([halo, cur_ref[...]], axis=0)
        acc = jnp.zeros((tr, D), F32)
        for o, sh in _shifted(ext, n, tr):
            k = o - (HALO - CONV_W + 1)
            if 0 <= k < CONV_W:
                acc = acc + w_ref[k:k + 1, :] * sh
        cv = acc + bdw_ref[...]
        c_ref[...] = cv
        zh, _ = _ln_stats(cv)
        ln = zh * g_ref[...] + b_ref[...]
        s_ref[...] = (ln * _sig(ln)).astype(BF)

    row = pl.BlockSpec((1, D), lambda i: (0, 0))
    return _call(
        body, name=name, grid=(T // tr,), operands=[glu, glu, w, bdw, g, b],
        out_shape=[jax.ShapeDtypeStruct((T, D), F32), jax.ShapeDtypeStruct((T, D), BF)],
        in_specs=[pl.BlockSpec((tr, D), lambda i: (i, 0)),
                  pl.BlockSpec((HALO, D), lambda i: (jnp.maximum(i * hb - 1, 0), 0)),
                  pl.BlockSpec((HALO, D), lambda i: (0, 0)), row, row, row],
        out_specs=[pl.BlockSpec((tr, D), lambda i: (i, 0))] * 2,
        sem=("parallel",), comm=comm)


def _conv_bwd(dc, glu, w, name, tr=256, comm=None):
    T, D = glu.shape
    tr = min(tr, T // 2)
    n = tr + HALO
    hb = tr // HALO
    nsteps = T // tr
    last_hb = T // HALO - 1

    def body(dc_ref, nxt_ref, glu_ref, prv_ref, w_ref, dg_ref, dw_ref):
        i = pl.program_id(0)
        dcv = dc_ref[...]
        nxt = jnp.where(i < nsteps - 1, nxt_ref[...], 0.0)
        ext2 = jnp.concatenate([dcv, nxt], axis=0)
        acc = jnp.zeros((tr, D), F32)
        for o, sh in _shifted(ext2, n, tr):
            k = CONV_W - 1 - o
            if 0 <= k < CONV_W:
                acc = acc + w_ref[k:k + 1, :] * sh
        dg_ref[...] = acc

        @pl.when(i == 0)
        def _():
            dw_ref[...] = jnp.zeros((HALO, D), F32)

        prv = jnp.where(i > 0, prv_ref[...], 0.0)
        ext = jnp.concatenate([prv, glu_ref[...]], axis=0)
        for o, sh in _shifted(ext, n, tr):
            k = o - (HALO - CONV_W + 1)
            if 0 <= k < CONV_W:
                dw_ref[k:k + 1, :] += _colsum(dcv * sh)

    return _call(
        body, name=name, grid=(nsteps,), operands=[dc, dc, glu, glu, w],
        out_shape=[jax.ShapeDtypeStruct((T, D), F32), jax.ShapeDtypeStruct((HALO, D), F32)],
        in_specs=[pl.BlockSpec((tr, D), lambda i: (i, 0)),
                  pl.BlockSpec((HALO, D), lambda i: (jnp.minimum((i + 1) * hb, last_hb), 0)),
                  pl.BlockSpec((tr, D), lambda i: (i, 0)),
                  pl.BlockSpec((HALO, D), lambda i: (jnp.maximum(i * hb - 1, 0), 0)),
                  pl.BlockSpec((HALO, D), lambda i: (0, 0))],
        out_specs=[pl.BlockSpec((tr, D), lambda i: (i, 0)),
                   pl.BlockSpec((HALO, D), lambda i: (0, 0))],
        sem=("arbitrary",), comm=comm)


def _rec_consts(C):
    t = np.arange(C)
    lb = (t[None, :] <= t[:, None]).astype(np.float32)

    def cum_at(idx):
        return (t[None, :] <= idx[:, None]).astype(np.float32)

    blocks, masks = [], []
    mid = SUB * (t // SUB) + SUB // 2 - 1
    eq = lb - cum_at(mid)
    blocks += [eq, -eq]
    masks.append(((t[:, None] // SUB) == (t[None, :] // SUB)) & (t[None, :] <= t[:, None]))
    nb = SUB
    while nb < C:
        odd = (t // nb) % 2 == 1
        e_t = nb * (t // nb) - 1
        e_s = nb * (t // nb) + nb - 1
        blocks.append(np.where(odd[:, None], lb - cum_at(e_t), 0.0))
        blocks.append(np.where(~odd[:, None], cum_at(e_s) - lb, 0.0))
        masks.append(((t[:, None] // (2 * nb)) == (t[None, :] // (2 * nb))) & odd[:, None] & ~odd[None, :])
        nb *= 2
    blocks += [lb, 1.0 - lb, np.ones((HEAD, C), np.float32)]
    L = np.concatenate(blocks, axis=0).astype(np.float32)
    L3 = np.concatenate([L, L, L], axis=1)
    LT3 = np.concatenate([L.T, L.T, L.T], axis=1)
    m = np.stack(masks).astype(np.float32)
    mT = np.transpose(m, (0, 2, 1)).copy()
    return (jnp.asarray(L3, BF), jnp.asarray(LT3, BF), jnp.asarray(m), jnp.asarray(mT), len(masks))


def _split3(x):
    h = x.astype(BF)
    r = x - h.astype(F32)
    m = r.astype(BF)
    lo = (r - m.astype(F32)).astype(BF)
    return h, m, lo


def _gates(qr, fz, lbr):
    sq = _sig(qr)
    q = qr * sq
    sg = _sig(fz)
    f = lbr + (1.0 - lbr) * sg
    fc = jnp.maximum(f, GATE_EPS)
    return q, 1.0 - f, jnp.log(fc), sq, sg, f, fc


def _rec_fwd(proj, lbr, name, comm=None):
    T, D4 = proj.shape
    D = D4 // 4
    H = _heads(D)
    C = CHUNK
    nC = T // C
    L3, _, m, _, nl = _rec_consts(C)
    R = L3.shape[0]

    def body(q_ref, f_ref, v_ref, lb_ref, l_ref, m_ref, o_ref, s_ref, st):
        @pl.when(pl.program_id(0) == 0)
        def _():
            st[...] = jnp.zeros((H, HEAD, HEAD), F32)

        q, k, logf = _gates(q_ref[...], f_ref[...], lb_ref[...])[:3]
        ex = jnp.exp(_dot(l_ref[...], jnp.concatenate(_split3(logf), axis=0)))
        vb = v_ref[...].astype(BF)
        s_ref[0] = st[...]
        outs = []
        for h in range(H):
            sl = slice(h * HEAD, (h + 1) * HEAD)
            qh, kh = q[:, sl], k[:, sl]
            p = jnp.zeros((C, C), F32)
            for lv in range(nl):
                qt = (qh * ex[2 * lv * C:(2 * lv + 1) * C, sl]).astype(BF)
                kt = (kh * ex[(2 * lv + 1) * C:(2 * lv + 2) * C, sl]).astype(BF)
                p = p + jnp.where(m_ref[lv] > 0.0, _dot_nt(qt, kt), 0.0)
            base = 2 * nl * C
            qhat = (qh * ex[base:base + C, sl]).astype(BF)
            khat = (kh * ex[base + C:base + 2 * C, sl]).astype(BF)
            elast = ex[base + 2 * C:base + 2 * C + HEAD, sl]
            sth = st[h]
            outs.append(_dot(p.astype(BF), vb[:, sl]) + _dot_nt(qhat, sth.astype(BF)))
            st[h] = elast * sth + _dot_tn(vb[:, sl], khat)
        o_ref[...] = jnp.concatenate(outs, axis=1)

    def cblk(cb):
        return pl.BlockSpec((C, D), lambda i: (i, cb))

    return _call(
        body, name=name, grid=(nC,), operands=[proj, proj, proj, lbr, L3, m],
        out_shape=[jax.ShapeDtypeStruct((T, D), F32), jax.ShapeDtypeStruct((nC, H, HEAD, HEAD), F32)],
        in_specs=[cblk(0), cblk(1), cblk(2), pl.BlockSpec((1, D), lambda i: (0, 0)),
                  pl.BlockSpec(L3.shape, lambda i: (0, 0)), pl.BlockSpec(m.shape, lambda i: (0, 0, 0))],
        out_specs=[pl.BlockSpec((C, D), lambda i: (i, 0)),
                   pl.BlockSpec((1, H, HEAD, HEAD), lambda i: (i, 0, 0, 0))],
        scratch=[pltpu.VMEM((H, HEAD, HEAD), F32)], sem=("arbitrary",), comm=comm)


def _rec_bwd(proj, lbr, do, dgr, ssave, name, comm=None):
    T, D4 = proj.shape
    D = D4 // 4
    H = _heads(D)
    C = CHUNK
    nC = T // C
    L3, LT3, m, mT, nl = _rec_consts(C)

    def body(q_ref, f_ref, v_ref, lb_ref, do_ref, dgr_ref, s_ref, l_ref, lt_ref, m_ref, mt_ref,
             dp_ref, dlb_ref, dst):
        @pl.when(pl.program_id(0) == 0)
        def _():
            dst[...] = jnp.zeros((H, HEAD, HEAD), F32)
            dlb_ref[...] = jnp.zeros((1, D), F32)

        qr = q_ref[...]
        lbv = lb_ref[...]
        q, k, logf, sq, sg, f, fc = _gates(qr, f_ref[...], lbv)
        ex = jnp.exp(_dot(l_ref[...], jnp.concatenate(_split3(logf), axis=0)))
        vb = v_ref[...].astype(BF)
        dob = do_ref[...].astype(BF)
        base = 2 * nl * C
        de = [[] for _ in range(2 * nl + 3)]
        dqs, dks, dvs = [], [], []
        for h in range(H):
            sl = slice(h * HEAD, (h + 1) * HEAD)
            qh, kh, vh, doh = q[:, sl], k[:, sl], vb[:, sl], dob[:, sl]
            dp = _dot_nt(doh, vh)
            dpt = _dot_nt(vh, doh)
            sth = s_ref[0, h]
            dsth = dst[h]
            dsb = dsth.astype(BF)
            pt = jnp.zeros((C, C), F32)
            dq = jnp.zeros((C, HEAD), F32)
            dk = jnp.zeros((C, HEAD), F32)
            for lv in range(nl):
                exq = ex[2 * lv * C:(2 * lv + 1) * C, sl]
                exk = ex[(2 * lv + 1) * C:(2 * lv + 2) * C, sl]
                qt = qh * exq
                kt = kh * exk
                qtb, ktb = qt.astype(BF), kt.astype(BF)
                pt = pt + jnp.where(mt_ref[lv] > 0.0, _dot_nt(ktb, qtb), 0.0)
                dqt = _dot(jnp.where(m_ref[lv] > 0.0, dp, 0.0).astype(BF), ktb)
                dkt = _dot(jnp.where(mt_ref[lv] > 0.0, dpt, 0.0).astype(BF), qtb)
                dq = dq + dqt * exq
                dk = dk + dkt * exk
                de[2 * lv].append(dqt * qt)
                de[2 * lv + 1].append(dkt * kt)
            exb = ex[base:base + C, sl]
            exkh = ex[base + C:base + 2 * C, sl]
            elast = ex[base + 2 * C:base + 2 * C + HEAD, sl]
            qhat = qh * exb
            khat = kh * exkh
            dqh = _dot(doh, sth.astype(BF))
            dkh = _dot(vh, dsb)
            dq = dq + dqh * exb
            dk = dk + dkh * exkh
            de[2 * nl].append(dqh * qhat)
            de[2 * nl + 1].append(dkh * khat)
            de[2 * nl + 2].append(dsth * sth * elast)
            dvs.append(_dot(pt.astype(BF), doh) + _dot_nt(khat.astype(BF), dsb))
            dst[h] = elast * dsth + _dot_tn(doh, qhat.astype(BF))
            dqs.append(dq)
            dks.append(dk)
        de_all = jnp.concatenate([jnp.concatenate(b, axis=1) for b in de], axis=0)
        dlogf = _dot(lt_ref[...], jnp.concatenate(_split3(de_all), axis=0))
        dq = jnp.concatenate(dqs, axis=1)
        dk = jnp.concatenate(dks, axis=1)
        dv = jnp.concatenate(dvs, axis=1)
        ind = jnp.where(f > GATE_EPS, 1.0, jnp.where(f == GATE_EPS, 0.5, 0.0))
        df = dlogf * ind / fc - dk
        dfz = df * (1.0 - lbv) * sg * (1.0 - sg)
        dlb_ref[...] += _colsum(df * (1.0 - sg))
        dqr = dq * (sq * (1.0 + qr * (1.0 - sq)))
        dp_ref[...] = jnp.concatenate([dqr.astype(BF), dfz.astype(BF), dv.astype(BF), dgr_ref[...]], axis=1)

    def cblk(cb):
        return pl.BlockSpec((C, D), lambda i: (nC - 1 - i, cb))

    def whole(a):
        nd = a.ndim
        return pl.BlockSpec(a.shape, lambda i: (0,) * nd)

    return _call(
        body, name=name, grid=(nC,), operands=[proj, proj, proj, lbr, do, dgr, ssave, L3, LT3, m, mT],
        out_shape=[jax.ShapeDtypeStruct((T, D4), BF), jax.ShapeDtypeStruct((1, D), F32)],
        in_specs=[cblk(0), cblk(1), cblk(2), pl.BlockSpec((1, D), lambda i: (0, 0)),
                  pl.BlockSpec((C, D), lambda i: (nC - 1 - i, 0)),
                  pl.BlockSpec((C, D), lambda i: (nC - 1 - i, 0)),
                  pl.BlockSpec((1, H, HEAD, HEAD), lambda i: (nC - 1 - i, 0, 0, 0)),
                  whole(L3), whole(LT3), whole(m), whole(mT)],
        out_specs=[pl.BlockSpec((C, D4), lambda i: (nC - 1 - i, 0)),
                   pl.BlockSpec((1, D), lambda i: (0, 0))],
        scratch=[pltpu.VMEM((H, HEAD, HEAD), F32)], sem=("arbitrary",), comm=comm)


def _softmax_rows(lg_ref):
    n = lg_ref.shape[0]
    rows = [lg_ref[l:l + 1, :] for l in range(n)]
    mx = rows[0]
    for r in rows[1:]:
        mx = jnp.maximum(mx, r)
    es = [jnp.exp(r - mx) for r in rows]
    tot = es[0]
    for e in es[1:]:
        tot = tot + e
    return [e / tot for e in es]


def _lb_fwd(logits):
    n, D = logits.shape

    def body(lg_ref, o_ref):
        soft = _softmax_rows(lg_ref)
        acc = jnp.zeros((1, D), F32)
        o_ref[0:1, :] = acc
        for j in range(1, n):
            acc = acc + soft[j]
            o_ref[j:j + 1, :] = acc

    return pl.pallas_call(body, name="lb_fwd", out_shape=jax.ShapeDtypeStruct((n, D), F32))(logits)


def _small_reduce(parts, logits, lb_row0):
    _, R, D = parts.shape
    n = logits.shape[0]

    def body(p_ref, lg_ref, o_ref):
        acc = p_ref[0]
        for d in range(1, N_DEV):
            acc = acc + p_ref[d]
        o_ref[...] = acc
        soft = _softmax_rows(lg_ref)
        dlb = [o_ref[lb_row0 + j:lb_row0 + j + 1, :] for j in range(n)]
        dsoft = [jnp.zeros((1, D), F32)]
        for l in range(1, n):
            s = dlb[l]
            for j in range(l + 1, n):
                s = s + dlb[j]
            dsoft.append(s)
        dot = soft[0] * dsoft[0]
        for l in range(1, n):
            dot = dot + soft[l] * dsoft[l]
        for l in range(n):
            o_ref[lb_row0 + l:lb_row0 + l + 1, :] = soft[l] * (dsoft[l] - dot)

    return pl.pallas_call(body, name="small_reduce", out_shape=jax.ShapeDtypeStruct((R, D), F32))(parts, logits)


def _adam_math(g, w, m, v):
    m2 = ADAM_B1 * m + (1.0 - ADAM_B1) * g
    v2 = ADAM_B2 * v + (1.0 - ADAM_B2) * (g * g)
    mh = m2 / (1.0 - ADAM_B1 ** ADAM_STEP)
    vh = v2 / (1.0 - ADAM_B2 ** ADAM_STEP)
    delta = -ADAM_LR * (mh / (jnp.sqrt(vh) + ADAM_EPS) + ADAM_WD * w)
    return delta, m2, v2


def _adamw(recv, w, m, v, name):
    nsrc = recv.shape[0]
    shp = w.shape
    cols = shp[-1]
    rows = int(np.prod(shp[:-1]))
    r2 = recv.reshape(nsrc, rows, cols)
    tr = min(rows, max(8, (1 << 20) // (cols * nsrc)))
    while rows % tr:
        tr //= 2

    def body(r_ref, w_ref, m_ref, v_ref, g_ref, d_ref, nm_ref, nv_ref):
        g = r_ref[0].astype(F32)
        for s in range(1, nsrc):
            g = g + r_ref[s].astype(F32)
        delta, m2, v2 = _adam_math(g, w_ref[...], m_ref[...], v_ref[...])
        g_ref[...] = g
        d_ref[...] = delta
        nm_ref[...] = m2
        nv_ref[...] = v2

    blk = pl.BlockSpec((tr, cols), lambda i: (i, 0))
    outs = pl.pallas_call(
        body, name=name, grid=(rows // tr,),
        out_shape=[jax.ShapeDtypeStruct((rows, cols), F32)] * 4,
        in_specs=[pl.BlockSpec((nsrc, tr, cols), lambda i: (0, i, 0)), blk, blk, blk],
        out_specs=[blk] * 4,
        compiler_params=_cp(("parallel",)),
    )(r2, w.reshape(rows, cols), m.reshape(rows, cols), v.reshape(rows, cols))
    return [o.reshape(shp) for o in outs]


def _pack(arrs, lane):
    flat = jnp.concatenate([a.reshape(-1) for a in arrs])
    n = flat.shape[0]
    rows = -(-n // lane)
    rows = -(-rows // 8) * 8
    flat = jnp.pad(flat, (0, rows * lane - n))
    return flat.reshape(rows, lane)


def _unpack(packed, shapes):
    flat = packed.reshape(-1)
    out, off = [], 0
    for s in shapes:
        n = int(np.prod(s))
        out.append(flat[off:off + n].reshape(s))
        off += n
    return out


def kernel(x, ln_mix_g, ln_mix_b, ln_ffn_g, ln_ffn_b, ffn_w1, ffn_w2, a_w_in, a_lb_logits, a_norm_g, a_w_out, b_w_pw1, b_b_pw1, b_w_dw, b_b_dw, b_ln_g, b_ln_b, b_w_pw2, b_b_pw2, loss_target, m_ln_mix_g, m_ln_mix_b, m_ln_ffn_g, m_ln_ffn_b, m_ffn_w1, m_ffn_w2, m_a_w_in, m_a_lb_logits, m_a_norm_g, m_a_w_out, m_b_w_pw1, m_b_b_pw1, m_b_w_dw, m_b_b_dw, m_b_ln_g, m_b_ln_b, m_b_w_pw2, m_b_b_pw2, v_ln_mix_g, v_ln_mix_b, v_ln_ffn_g, v_ln_ffn_b, v_ffn_w1, v_ffn_w2, v_a_w_in, v_a_lb_logits, v_a_norm_g, v_a_w_out, v_b_w_pw1, v_b_b_pw1, v_b_w_dw, v_b_b_dw, v_b_ln_g, v_b_ln_b, v_b_w_pw2, v_b_b_pw2):
    T, D = x.shape[1], x.shape[2]
    nA, nB = a_w_in.shape[0], b_w_pw1.shape[0]
    me = 4 * lax.axis_index("x") + 2 * lax.axis_index("y") + lax.axis_index("c")
    xin = x[0]
    tgt = loss_target[0]

    small_names = [b_b_pw1, b_w_dw, b_b_dw, b_ln_g, b_ln_b, b_b_pw2]
    sp = _pack(small_names, 128)
    wsrc = {"W1": ffn_w1.astype(BF), "W2": ffn_w2.astype(BF), "Win": a_w_in.astype(BF),
            "Wout": a_w_out.astype(BF), "Wp1": b_w_pw1.astype(BF), "Wp2": b_w_pw2.astype(BF), "small": sp[None]}
    W = {k: jnp.zeros((N_DEV,) + v.shape, v.dtype) for k, v in wsrc.items()}
    GW = {k: jnp.zeros((N_DEV,) + v.shape, BF) for k, v in wsrc.items() if k != "small"}
    R = {k: jnp.zeros(v.shape, BF) for k, v in GW.items()}

    def names_of(items):
        out = []
        for n, _ in items:
            if n not in out:
                out.append(n)
        return out

    def hosted(fn, key, sched, mode, src, bufs, *args, **kw):
        items = sched.get(key)
        if items is None:
            return fn(*args, name=key, **kw)
        comm = (mode, [(src[n], l, bufs[n]) for n, l in items])
        outs, new = fn(*args, name=key, comm=comm, **kw)
        for n, b in zip(names_of(items), new):
            bufs[n] = b
        return outs

    first = [("Win", 0), ("Wout", 0), ("W1", 0), ("W2", 0), ("small", 0)]
    for n, b in zip(names_of(first), _exchange(("gather", [(wsrc[n], l, W[n]) for n, l in first]), "gather_first")):
        W[n] = b
    fwd_sched = {"a0_rec_fwd": [("Wp1", 0), ("Wp2", 0), ("W1", 1), ("W2", 1)],
                 "b0_conv": [("Win", 1), ("Wout", 1)], "l1_ffn_up": [("W2", 2)], "l1_ffn_down": [("W1", 2)],
                 "a1_rec_fwd": [("Wp1", 1), ("Wp2", 1), ("W1", 3), ("W2", 3)]}
    bwd_sched = {"a1_rec_bwd": [("W1", 3), ("W2", 3), ("Wp1", 1), ("Wp2", 1)],
                 "b0_conv_bwd": [("W1", 2), ("W2", 2), ("Win", 1), ("Wout", 1)],
                 "a0_rec_bwd": [("W1", 1), ("W2", 1), ("Wp1", 0), ("Wp2", 0)]}
    last = [("W1", 0), ("W2", 0), ("Win", 0), ("Wout", 0)]

    def fwd(fn, key, *args, **kw):
        return hosted(fn, key, fwd_sched, "gather", wsrc, W, *args, **kw)

    def bwd(fn, key, *args, **kw):
        return hosted(fn, key, bwd_sched, "scatter", GW, R, *args, **kw)

    spg = W["small"][:, 0]
    sm = [jnp.stack(p) for p in zip(*[_unpack(spg[d], [a.shape for a in small_names]) for d in range(N_DEV)])]
    bpw1 = jnp.transpose(sm[0], (1, 0, 2)).reshape(nB, 1, 2 * D)
    wdw = jnp.transpose(sm[1], (1, 2, 0, 3)).reshape(nB, CONV_W, D)
    wdw = jnp.pad(wdw, ((0, 0), (0, HALO - CONV_W), (0, 0)))
    bdw, blng, blnb, bpw2 = [jnp.transpose(s, (1, 0, 2)).reshape(nB, 1, D) for s in sm[2:]]
    lb_all = _lb_fwd(a_lb_logits)

    saved = []
    h = xin
    for i in range(DEPTH):
        j = i // 2
        sv = {"xin": h}
        if i % 2 == 0:
            proj = _mm(h, W["Win"], j, "nn_col", out_dtype=F32, name=f"a{j}_proj")
            o, ssave = fwd(_rec_fwd, f"a{j}_rec_fwd", proj, lb_all[j:j + 1])
            yg = _rms_gate_fwd(o, proj, a_norm_g[j:j + 1], f"a{j}_gate_fwd")
            z1 = _mm(yg, W["Wout"], j, "nn_row", out_dtype=F32, name=f"a{j}_out",
                     epi=lambda acc, r: acc + ALPHA * r, extras=[h])
            sv.update(proj=proj, o=o, ssave=ssave, yg=yg)
        else:
            u = _mm(h, W["Wp1"], j, "nn_col", out_dtype=F32, name=f"b{j}_pw1",
                    epi=lambda acc, b: acc + b, extras=[bpw1[j]])
            glu = _glu_fwd(u, f"b{j}_glu")
            cv, s = fwd(_conv_fwd, f"b{j}_conv", glu, wdw[j], bdw[j], blng[j], blnb[j])
            z1 = _mm(s, W["Wp2"], j, "nn_row", out_dtype=F32, name=f"b{j}_pw2",
                     epi=lambda acc, b, r: acc + b + ALPHA * r, extras=[bpw2[j], h])
            sv.update(u=u, glu=glu, cv=cv, s=s)
        x1 = _ln_fwd(z1, ln_mix_g[i:i + 1], ln_mix_b[i:i + 1], f"l{i}_ln_mix")
        hh = fwd(_mm, f"l{i}_ffn_up", x1, W["W1"], i, "nn_col", out_dtype=BF)
        z2 = fwd(_mm, f"l{i}_ffn_down", hh, W["W2"], i, "nn_row", out_dtype=F32,
                 a_fn=lambda t: jnp.square(jnp.maximum(t, 0)),
                 epi=lambda acc, r: acc + ALPHA * r, extras=[x1])
        h = _ln_fwd(z2, ln_ffn_g[i:i + 1], ln_ffn_b[i:i + 1], f"l{i}_ln_ffn")
        sv.update(z1=z1, x1=x1, hh=hh, z2=z2)
        saved.append(sv)

    dx, loss_row = _loss_k(h, tgt, "loss")

    rows = {}
    for i in reversed(range(DEPTH)):
        j = i // 2
        sv = saved[i]
        dz2, dg, db, _ = _ln_bwd(dx, sv["z2"], ln_ffn_g[i:i + 1], f"l{i}_ln_ffn_bwd")
        rows[("ffn_g", i)], rows[("ffn_b", i)] = dg, db
        dh = _mm(dz2, W["W2"], i, "nt_row", out_dtype=BF, name=f"l{i}_ffn_down_dx",
                 epi=lambda acc, hv: acc * (2.0 * jnp.maximum(hv.astype(F32), 0.0)), extras=[sv["hh"]])
        GW["W2"] = _mm_tn(GW["W2"], dz2, sv["hh"], i, "rows_t", name=f"l{i}_ffn_down_dw",
                         b_fn=lambda t: jnp.square(jnp.maximum(t, 0)))
        GW["W1"] = _mm_tn(GW["W1"], sv["x1"], dh, i, "cols", name=f"l{i}_ffn_up_dw")
        dx1 = _mm(dh, W["W1"], i, "nt_col", out_dtype=F32, name=f"l{i}_ffn_up_dx",
                  epi=lambda acc, r: acc + ALPHA * r, extras=[dz2])
        dz1, dg, db, dz1sum = _ln_bwd(dx1, sv["z1"], ln_mix_g[i:i + 1], f"l{i}_ln_mix_bwd")
        rows[("mix_g", i)], rows[("mix_b", i)] = dg, db
        if i % 2 == 0:
            GW["Wout"] = _mm_tn(GW["Wout"], sv["yg"], dz1, j, "rows", name=f"a{j}_out_dw")
            dyg = _mm(dz1, W["Wout"], j, "nt_row", out_dtype=F32, name=f"a{j}_out_dx")
            do, dgr, dng = _rms_gate_bwd(dyg, sv["o"], sv["proj"], a_norm_g[j:j + 1], f"a{j}_gate_bwd")
            rows[("ng", j)] = dng
            dproj, dlb = bwd(_rec_bwd, f"a{j}_rec_bwd", sv["proj"], lb_all[j:j + 1], do, dgr, sv["ssave"])
            rows[("lb", j)] = dlb
            GW["Win"] = _mm_tn(GW["Win"], sv["xin"], dproj, j, "cols", name=f"a{j}_proj_dw")
            dx = _mm(dproj, W["Win"], j, "nt_col", out_dtype=F32, name=f"a{j}_proj_dx",
                     epi=lambda acc, r: acc + ALPHA * r, extras=[dz1])
        else:
            rows[("bpw2", j)] = dz1sum
            GW["Wp2"] = _mm_tn(GW["Wp2"], sv["s"], dz1, j, "rows", name=f"b{j}_pw2_dw")
            ds = _mm(dz1, W["Wp2"], j, "nt_row", out_dtype=F32, name=f"b{j}_pw2_dx")
            dc, dlg, dlb_, dcs = _silu_ln_bwd(ds, sv["cv"], blng[j], blnb[j], f"b{j}_ln_bwd")
            rows[("blng", j)], rows[("blnb", j)], rows[("bdw", j)] = dlg, dlb_, dcs
            dglu, dwdw = bwd(_conv_bwd, f"b{j}_conv_bwd", dc, sv["glu"], wdw[j])
            rows[("wdw", j)] = dwdw[:CONV_W]
            du, dbu = _glu_bwd(dglu, sv["u"], f"b{j}_glu_bwd")
            rows[("bpw1", j)] = dbu.reshape(2, D)
            GW["Wp1"] = _mm_tn(GW["Wp1"], sv["xin"], du, j, "cols", name=f"b{j}_pw1_dw")
            dx = _mm(du, W["Wp1"], j, "nt_col", out_dtype=F32, name=f"b{j}_pw1_dx",
                     epi=lambda acc, r: acc + ALPHA * r, extras=[dz1])
    grad_x = dx[None]

    order = ([("mix_g", i) for i in range(DEPTH)] + [("mix_b", i) for i in range(DEPTH)]
             + [("ffn_g", i) for i in range(DEPTH)] + [("ffn_b", i) for i in range(DEPTH)])
    lb_row0 = len(order)
    order += [("lb", j) for j in range(nA)] + [("ng", j) for j in range(nA)]
    for j in range(nB):
        order += [("bpw1", j), ("wdw", j), ("bdw", j), ("blng", j), ("blnb", j), ("bpw2", j)]
    pieces, offs, off = [], {}, 0
    for key in order:
        offs[key] = off
        pieces.append(rows[key])
        off += rows[key].shape[0]
    offs["loss"] = off
    pieces.append(loss_row)
    off += 1
    part = jnp.concatenate(pieces, axis=0)
    part = jnp.pad(part, ((0, -off % 8), (0, 0)))
    parts = _exchange(("gather", [(part[None], 0, jnp.zeros((N_DEV, 1) + part.shape, F32))]),
                      "gather_small_grads")[0][:, 0]
    G = _small_reduce(parts, a_lb_logits, lb_row0)
    loss = jnp.sum(G[offs["loss"]])

    def rep(kind, n):
        return jnp.concatenate([G[offs[(kind, i)]:offs[(kind, i)] + 1] for i in range(n)], axis=0)

    def shard_cols(full, width):
        return lax.dynamic_slice_in_dim(full, me * width, width, axis=full.ndim - 1)

    g_small = {
        "ln_mix_g": rep("mix_g", DEPTH), "ln_mix_b": rep("mix_b", DEPTH),
        "ln_ffn_g": rep("ffn_g", DEPTH), "ln_ffn_b": rep("ffn_b", DEPTH),
        "a_lb_logits": rep("lb", nA), "a_norm_g": rep("ng", nA),
        "b_b_pw1": shard_cols(jnp.stack([G[offs[("bpw1", j)]:offs[("bpw1", j)] + 2].reshape(2 * D)
                                         for j in range(nB)]), 2 * D // N_DEV),
        "b_w_dw": shard_cols(jnp.stack([G[offs[("wdw", j)]:offs[("wdw", j)] + CONV_W] for j in range(nB)]),
                             D // N_DEV),
        "b_b_dw": shard_cols(rep("bdw", nB), D // N_DEV),
        "b_ln_g": shard_cols(rep("blng", nB), D // N_DEV),
        "b_ln_b": shard_cols(rep("blnb", nB), D // N_DEV),
        "b_b_pw2": shard_cols(rep("bpw2", nB), D // N_DEV),
    }
    small_w = {"ln_mix_g": (ln_mix_g, m_ln_mix_g, v_ln_mix_g), "ln_mix_b": (ln_mix_b, m_ln_mix_b, v_ln_mix_b),
               "ln_ffn_g": (ln_ffn_g, m_ln_ffn_g, v_ln_ffn_g), "ln_ffn_b": (ln_ffn_b, m_ln_ffn_b, v_ln_ffn_b),
               "a_lb_logits": (a_lb_logits, m_a_lb_logits, v_a_lb_logits),
               "a_norm_g": (a_norm_g, m_a_norm_g, v_a_norm_g),
               "b_b_pw1": (b_b_pw1, m_b_b_pw1, v_b_b_pw1), "b_w_dw": (b_w_dw, m_b_w_dw, v_b_w_dw),
               "b_b_dw": (b_b_dw, m_b_b_dw, v_b_b_dw), "b_ln_g": (b_ln_g, m_b_ln_g, v_b_ln_g),
               "b_ln_b": (b_ln_b, m_b_ln_b, v_b_ln_b), "b_b_pw2": (b_b_pw2, m_b_b_pw2, v_b_b_pw2)}
    snames = list(small_w)
    sshapes = [small_w[k][0].shape for k in snames]
    pg = _pack([g_small[k] for k in snames], 1024)
    pw, pm, pv = [_pack([small_w[k][q] for k in snames], 1024) for q in range(3)]
    sres = _adamw(pg[None], pw, pm, pv, "adamw_small")
    sres = [dict(zip(snames, _unpack(r, sshapes))) for r in sres]

    for n, b in zip(names_of(last), _exchange(("scatter", [(GW[n], l, R[n]) for n, l in last]), "scatter_last")):
        R[n] = b
    recv = [R[n] for n in ["W1", "W2", "Win", "Wout", "Wp1", "Wp2"]]
    big = {}
    for nm, rv, (w, m, v) in zip(
            ["ffn_w1", "ffn_w2", "a_w_in", "a_w_out", "b_w_pw1", "b_w_pw2"], recv,
            [(ffn_w1, m_ffn_w1, v_ffn_w1), (ffn_w2, m_ffn_w2, v_ffn_w2), (a_w_in, m_a_w_in, v_a_w_in),
             (a_w_out, m_a_w_out, v_a_w_out), (b_w_pw1, m_b_w_pw1, v_b_w_pw1), (b_w_pw2, m_b_w_pw2, v_b_w_pw2)]):
        big[nm] = _adamw(rv, w, m, v, f"adamw_{nm}")

    names = ["ln_mix_g", "ln_mix_b", "ln_ffn_g", "ln_ffn_b", "ffn_w1", "ffn_w2", "a_w_in", "a_lb_logits",
             "a_norm_g", "a_w_out", "b_w_pw1", "b_b_pw1", "b_w_dw", "b_b_dw", "b_ln_g", "b_ln_b", "b_w_pw2",
             "b_b_pw2"]
    out = [loss, grad_x]
    for q in range(4):
        for nm in names:
            out.append(big[nm][q] if nm in big else sres[q][nm])
    return tuple(out)
```

```python
import functools

import numpy as np
import jax
import jax.numpy as jnp
from jax import lax
from jax.experimental import pallas as pl
from jax.experimental.pallas import tpu as pltpu

F32 = jnp.float32
BF = jnp.bfloat16

N_DEV = 8
DEPTH = 4
HEAD = 128
CHUNK = 32
SUB = 8
CONV_W = 31
HALO = 32
ALPHA = (2.0 * DEPTH) ** 0.25
LN_EPS = 1e-5
RMS_EPS = 1e-6
GATE_EPS = 1e-6
ADAM_LR = 0.001
ADAM_B1 = 0.9
ADAM_B2 = 0.999
ADAM_EPS = 1e-08
ADAM_WD = 0.01
ADAM_STEP = 10
VMEM_LIMIT = 56 * 1024 * 1024
MESH = pl.DeviceIdType.MESH


def _cp(sem=None):
    return pltpu.CompilerParams(vmem_limit_bytes=VMEM_LIMIT, dimension_semantics=sem)


def _dot(a, b):
    return jnp.dot(a, b, preferred_element_type=F32)


def _dot_nt(a, b):
    return lax.dot_general(a, b, (((1,), (1,)), ((), ())), preferred_element_type=F32)


def _dot_tn(a, b):
    return lax.dot_general(a, b, (((0,), (0,)), ((), ())), preferred_element_type=F32)


def _sig(x):
    return jax.nn.sigmoid(x)


def _distinct(items):
    out = []
    for it in items:
        if not any(it is q for q in out):
            out.append(it)
    return out


def _index_of(items, it):
    return next(i for i, q in enumerate(items) if q is it)


def _comm_copies(mode, plan, src_refs, buf_refs, send_sems, recv_sems, loc_sems):
    x, y, c = lax.axis_index("x"), lax.axis_index("y"), lax.axis_index("c")
    me = 4 * x + 2 * y + c
    locs, rems = [], []
    for k, (si, l, bi) in enumerate(plan):
        src = src_refs[si].at[l] if mode == "gather" else src_refs[si].at[me, l]
        locs.append(pltpu.make_async_copy(src, buf_refs[bi].at[me, l], loc_sems.at[k]))
    for r in range(1, N_DEV):
        px = (1 - x) if (r >> 2) & 1 else x
        py = (1 - y) if (r >> 1) & 1 else y
        pc = (1 - c) if r & 1 else c
        pid = 4 * px + 2 * py + pc
        for k, (si, l, bi) in enumerate(plan):
            src = src_refs[si].at[l] if mode == "gather" else src_refs[si].at[pid, l]
            rems.append(pltpu.make_async_remote_copy(
                src_ref=src, dst_ref=buf_refs[bi].at[me, l],
                send_sem=send_sems.at[k, r - 1], recv_sem=recv_sems.at[k, r - 1],
                device_id=(px, py, pc), device_id_type=MESH))
    return locs, rems


def _call(body, *, name, grid, operands, in_specs, out_shape, out_specs, scratch=(), sem=None,
          aliases=None, comm=None):
    aliases = dict(aliases or {})
    if comm is None:
        return pl.pallas_call(
            body, name=name, grid=grid, out_shape=list(out_shape), in_specs=list(in_specs),
            out_specs=list(out_specs), scratch_shapes=list(scratch), input_output_aliases=aliases,
            compiler_params=_cp(sem),
        )(*operands)
    mode, pieces = comm
    srcs = _distinct([p[0] for p in pieces])
    bufs = _distinct([p[2] for p in pieces])
    plan = [(_index_of(srcs, s), l, _index_of(bufs, b)) for (s, l, b) in pieces]
    n_in, n_out, n_scr, ns, nb, npc = len(operands), len(out_shape), len(scratch), len(srcs), len(bufs), len(plan)
    nsteps = grid[0]

    def wrapped(*refs):
        ins = refs[:n_in]
        src_refs = refs[n_in:n_in + ns]
        o0 = n_in + ns + nb
        outs = refs[o0:o0 + n_out]
        buf_refs = refs[o0 + n_out:o0 + n_out + nb]
        s0 = o0 + n_out + nb
        scr = refs[s0:s0 + n_scr]
        sems = refs[s0 + n_scr:]
        step = pl.program_id(0)

        @pl.when(step == 0)
        def _():
            locs, rems = _comm_copies(mode, plan, src_refs, buf_refs, *sems)
            for d in locs + rems:
                d.start()

        body(*ins, *outs, *scr)

        @pl.when(step == nsteps - 1)
        def _():
            locs, rems = _comm_copies(mode, plan, src_refs, buf_refs, *sems)
            for d in rems:
                d.wait_send()
            for d in rems:
                d.wait_recv()
            for d in locs:
                d.wait()

    anyspec = pl.BlockSpec(memory_space=pl.ANY)
    for k in range(nb):
        aliases[n_in + ns + k] = n_out + k
    res = pl.pallas_call(
        wrapped, name=name, grid=grid,
        out_shape=list(out_shape) + [jax.ShapeDtypeStruct(b.shape, b.dtype) for b in bufs],
        in_specs=list(in_specs) + [anyspec] * (ns + nb),
        out_specs=list(out_specs) + [anyspec] * nb,
        scratch_shapes=list(scratch) + [pltpu.SemaphoreType.DMA((npc, N_DEV - 1)),
                                        pltpu.SemaphoreType.DMA((npc, N_DEV - 1)),
                                        pltpu.SemaphoreType.DMA((npc,))],
        input_output_aliases=aliases,
        compiler_params=pltpu.CompilerParams(vmem_limit_bytes=VMEM_LIMIT, has_side_effects=True,
                                             dimension_semantics=("arbitrary",) * len(grid)),
    )(*operands, *srcs, *bufs)
    return list(res[:n_out]), list(res[n_out:])


def _exchange(comm, name):
    def body():
        pass

    return _call(body, name=name, grid=(1,), operands=[], in_specs=[], out_shape=[], out_specs=[],
                 comm=comm)[1]


def _mm(a, w, l, kind, *, out_dtype, name, tm=256, a_fn=None, epi=None, extras=(), comm=None):
    T, Ka = a.shape
    _, _, d2, d3 = w.shape
    n_out = {"nn_col": N_DEV * d3, "nn_row": d3, "nt_col": d2, "nt_row": N_DEV * d2}[kind]
    tm = min(tm, T)
    ne = len(extras)

    def body(*refs):
        a_ref, w_ref = refs[0], refs[1]
        e_refs = refs[2:2 + ne]
        o_ref = refs[2 + ne]
        av = a_ref[...]
        if a_fn is not None:
            av = a_fn(av)
        av = av.astype(BF)
        if kind == "nn_col":
            acc = jnp.concatenate([_dot(av, w_ref[j]) for j in range(N_DEV)], axis=1)
        elif kind == "nn_row":
            acc = _dot(av, w_ref[...].reshape(N_DEV * d2, d3))
        elif kind == "nt_col":
            acc = _dot_nt(av[:, 0:d3], w_ref[0])
            for j in range(1, N_DEV):
                acc = acc + _dot_nt(av[:, j * d3:(j + 1) * d3], w_ref[j])
        else:
            acc = jnp.concatenate([_dot_nt(av, w_ref[j]) for j in range(N_DEV)], axis=1)
        if epi is not None:
            acc = epi(acc, *[e[...] for e in e_refs])
        o_ref[...] = acc.astype(out_dtype)

    in_specs = [pl.BlockSpec((tm, Ka), lambda i: (i, 0)),
                pl.BlockSpec((N_DEV, None, d2, d3), lambda i: (0, l, 0, 0))]
    for e in extras:
        if e.shape[0] == 1:
            in_specs.append(pl.BlockSpec((1, n_out), lambda i: (0, 0)))
        else:
            in_specs.append(pl.BlockSpec((tm, n_out), lambda i: (i, 0)))
    res = _call(body, name=name, grid=(T // tm,), operands=[a, w, *extras], in_specs=in_specs,
                out_shape=[jax.ShapeDtypeStruct((T, n_out), out_dtype)],
                out_specs=[pl.BlockSpec((tm, n_out), lambda i: (i, 0))], sem=("parallel",), comm=comm)
    return res[0] if comm is None else (res[0][0], res[1])


def _mm_tn(g, s, b, l, kind, *, name, tt=1024, b_fn=None):
    T, ws = s.shape
    wb = b.shape[1]
    _, _, d2, d3 = g.shape
    tt = min(tt, T)
    nsteps = T // tt
    cw = wb // N_DEV

    def body(g_in, s_ref, b_ref, o_ref, acc, stage, sem):
        del g_in
        t = pl.program_id(0)

        @pl.when(t == 0)
        def _():
            acc[...] = jnp.zeros((ws, wb), F32)

        st = s_ref[...].astype(F32).T.astype(BF)
        for j in range(N_DEV):
            bv = b_ref[:, j * cw:(j + 1) * cw]
            if b_fn is not None:
                bv = b_fn(bv)
            acc[:, j * cw:(j + 1) * cw] += _dot(st, bv.astype(BF))

        @pl.when(t == nsteps - 1)
        def _():
            for j in range(N_DEV):
                if kind == "cols":
                    blk = acc[:, j * d3:(j + 1) * d3]
                elif kind == "rows":
                    blk = acc[j * d2:(j + 1) * d2, :]
                else:
                    blk = acc[:, j * d2:(j + 1) * d2].T
                stage[...] = blk.astype(BF)
                cp = pltpu.make_async_copy(stage, o_ref.at[j, l], sem)
                cp.start()
                cp.wait()

    return pl.pallas_call(
        body, name=name, grid=(nsteps,),
        out_shape=jax.ShapeDtypeStruct(g.shape, BF),
        in_specs=[pl.BlockSpec(memory_space=pl.ANY),
                  pl.BlockSpec((tt, ws), lambda t: (t, 0)),
                  pl.BlockSpec((tt, wb), lambda t: (t, 0))],
        out_specs=pl.BlockSpec(memory_space=pl.ANY),
        scratch_shapes=[pltpu.VMEM((ws, wb), F32), pltpu.VMEM((d2, d3), BF), pltpu.SemaphoreType.DMA],
        input_output_aliases={0: 0},
        compiler_params=_cp(("arbitrary",)),
    )(g, s, b)


def _rw(fn, tiles, rows, outs, sums, *, name, tr=256):
    T = tiles[0][0].shape[0]
    tr = min(tr, T)
    nt, nr, no, ns = len(tiles), len(rows), len(outs), len(sums)

    def body(*refs):
        t_refs, r_refs = refs[:nt], refs[nt:nt + nr]
        o_refs = refs[nt + nr:nt + nr + no]
        s_refs = refs[nt + nr + no:]
        res = fn(*[t[...] for t in t_refs], *[r[...] for r in r_refs])
        for o_ref, val in zip(o_refs, res[:no]):
            o_ref[...] = val.astype(o_ref.dtype)
        if ns:
            first = pl.program_id(0) == 0

            @pl.when(first)
            def _():
                for s_ref, val in zip(s_refs, res[no:]):
                    s_ref[...] = val

            @pl.when(jnp.logical_not(first))
            def _():
                for s_ref, val in zip(s_refs, res[no:]):
                    s_ref[...] += val

    in_specs = [pl.BlockSpec((tr, w), functools.partial(lambda i, cb: (i, cb), cb=cb)) for (_, w, cb) in tiles]
    in_specs += [pl.BlockSpec(r.shape, lambda i: (0, 0)) for r in rows]
    out_shape = [jax.ShapeDtypeStruct((T, w), dt) for (w, dt) in outs]
    out_shape += [jax.ShapeDtypeStruct((1, w), F32) for w in sums]
    out_specs = [pl.BlockSpec((tr, w), lambda i: (i, 0)) for (w, _) in outs]
    out_specs += [pl.BlockSpec((1, w), lambda i: (0, 0)) for w in sums]
    return pl.pallas_call(
        body, name=name, grid=(T // tr,), out_shape=out_shape,
        in_specs=in_specs, out_specs=out_specs,
        compiler_params=_cp(("arbitrary",)),
    )(*[t[0] for t in tiles], *rows)


def _ln_stats(z):
    mu = jnp.mean(z, axis=-1, keepdims=True)
    zc = z - mu
    var = jnp.mean(zc * zc, axis=-1, keepdims=True)
    rstd = lax.rsqrt(var + LN_EPS)
    return zc * rstd, rstd


def _ln_bwd_core(dy, zh, rstd, g):
    dzh = dy * g
    m1 = jnp.mean(dzh, axis=-1, keepdims=True)
    m2 = jnp.mean(dzh * zh, axis=-1, keepdims=True)
    return rstd * (dzh - m1 - zh * m2)


def _colsum(v):
    return jnp.sum(v, axis=0, keepdims=True)


def _ln_fwd(z, g, b, name):
    D = z.shape[1]

    def fn(zt, gr, br):
        zh, _ = _ln_stats(zt)
        return (zh * gr + br,)

    return _rw(fn, [(z, D, 0)], [g, b], [(D, F32)], [], name=name)[0]


def _ln_bwd(dy, z, g, name):
    D = z.shape[1]

    def fn(dyt, zt, gr):
        zh, rstd = _ln_stats(zt)
        dz = _ln_bwd_core(dyt, zh, rstd, gr)
        return dz, _colsum(dyt * zh), _colsum(dyt), _colsum(dz)

    return _rw(fn, [(dy, D, 0), (z, D, 0)], [g], [(D, F32)], [D, D, D], name=name)


def _heads(D):
    return D // HEAD


def _rms_parts(o, D):
    xs, rs = [], []
    for h in range(_heads(D)):
        oh = o[:, h * HEAD:(h + 1) * HEAD]
        r = lax.rsqrt(jnp.mean(oh * oh, axis=-1, keepdims=True) + RMS_EPS)
        xs.append(oh * r)
        rs.append(r)
    return xs, rs


def _rms_gate_fwd(o, proj, ng, name):
    D = o.shape[1]

    def fn(ot, grt, ngr):
        xs, _ = _rms_parts(ot, D)
        xh = jnp.concatenate(xs, axis=1)
        return (xh * ngr * (grt * _sig(grt)),)

    return _rw(fn, [(o, D, 0), (proj, D, 3)], [ng], [(D, BF)], [], name=name)[0]


def _rms_gate_bwd(dy, o, proj, ng, name):
    D = o.shape[1]

    def fn(dyt, ot, grt, ngr):
        xs, rs = _rms_parts(ot, D)
        xh = jnp.concatenate(xs, axis=1)
        sg = _sig(grt)
        on = xh * ngr
        dgr = dyt * on * (sg * (1.0 + grt * (1.0 - sg)))
        don = dyt * (grt * sg)
        dxh = don * ngr
        dos = []
        for h in range(_heads(D)):
            sl = slice(h * HEAD, (h + 1) * HEAD)
            m = jnp.mean(dxh[:, sl] * xs[h], axis=-1, keepdims=True)
            dos.append(rs[h] * (dxh[:, sl] - xs[h] * m))
        return jnp.concatenate(dos, axis=1), dgr, _colsum(don * xh)

    return _rw(fn, [(dy, D, 0), (o, D, 0), (proj, D, 3)], [ng], [(D, F32), (D, BF)], [D], name=name)


def _glu_fwd(u, name):
    D = u.shape[1] // 2

    def fn(at, gt):
        return (at * _sig(gt),)

    return _rw(fn, [(u, D, 0), (u, D, 1)], [], [(D, F32)], [], name=name)[0]


def _glu_bwd(dglu, u, name):
    D = u.shape[1] // 2

    def fn(dg, at, gt):
        sg = _sig(gt)
        du = jnp.concatenate([dg * sg, dg * at * sg * (1.0 - sg)], axis=1)
        return du, _colsum(du)

    return _rw(fn, [(dglu, D, 0), (u, D, 0), (u, D, 1)], [], [(2 * D, BF)], [2 * D], name=name)


def _silu_ln_bwd(ds, c, g, b, name):
    D = c.shape[1]

    def fn(dst, ct, gr, br):
        zh, rstd = _ln_stats(ct)
        ln = zh * gr + br
        sg = _sig(ln)
        dln = dst * (sg * (1.0 + ln * (1.0 - sg)))
        dc = _ln_bwd_core(dln, zh, rstd, gr)
        return dc, _colsum(dln * zh), _colsum(dln), _colsum(dc)

    return _rw(fn, [(ds, D, 0), (c, D, 0)], [g, b], [(D, F32)], [D, D, D], name=name)


def _loss_k(y, tgt, name):
    D = y.shape[1]

    def fn(yt, tt):
        e = yt - tt
        return e * (1.0 / D), _colsum(e * e) * (0.5 / D)

    return _rw(fn, [(y, D, 0), (tgt, D, 0)], [], [(D, F32)], [D], name=name)


def _shifted(ext, n, tr):
    for b in range(8):
        rb = ext if b == 0 else pltpu.roll(ext, n - b, 0)
        for a in range(HALO // 8 + 1):
            o = 8 * a + b
            if o <= HALO:
                yield o, rb[8 * a:8 * a + tr, :]


def _conv_fwd(glu, w, bdw, g, b, name, tr=256, comm=None):
    T, D = glu.shape
    tr = min(tr, T // 2)
    n = tr + HALO
    hb = tr // HALO

    def body(cur_ref, halo_ref, w_ref, bdw_ref, g_ref, b_ref, c_ref, s_ref):
        i = pl.program_id(0)
        halo = jnp.where(i > 0, halo_ref[...], 0.0)
        ext = jnp.concatenate([halo, cur_ref[...]], axis=0)
        acc = jnp.zeros((tr, D), F32)
        for o, sh in _shifted(ext, n, tr):
            k = o - (HALO - CONV_W + 1)
            if 0 <= k < CONV_W:
                acc = acc + w_ref[k:k + 1, :] * sh
        cv = acc + bdw_ref[...]
        c_ref[...] = cv
        zh, _ = _ln_stats(cv)
        ln = zh * g_ref[...] + b_ref[...]
        s_ref[...] = (ln * _sig(ln)).astype(BF)

    row = pl.BlockSpec((1, D), lambda i: (0, 0))
    return _call(
        body, name=name, grid=(T // tr,), operands=[glu, glu, w, bdw, g, b],
        out_shape=[jax.ShapeDtypeStruct((T, D), F32), jax.ShapeDtypeStruct((T, D), BF)],
        in_specs=[pl.BlockSpec((tr, D), lambda i: (i, 0)),
                  pl.BlockSpec((HALO, D), lambda i: (jnp.maximum(i * hb - 1, 0), 0)),
                  pl.BlockSpec((HALO, D), lambda i: (0, 0)), row, row, row],
        out_specs=[pl.BlockSpec((tr, D), lambda i: (i, 0))] * 2,
        sem=("parallel",), comm=comm)


def _conv_bwd(dc, glu, w, name, tr=256, comm=None):
    T, D = glu.shape
    tr = min(tr, T // 2)
    n = tr + HALO
    hb = tr // HALO
    nsteps = T // tr
    last_hb = T // HALO - 1

    def body(dc_ref, nxt_ref, glu_ref, prv_ref, w_ref, dg_ref, dw_ref):
        i = pl.program_id(0)
        dcv = dc_ref[...]
        nxt = jnp.where(i < nsteps - 1, nxt_ref[...], 0.0)
        ext2 = jnp.concatenate([dcv, nxt], axis=0)
        acc = jnp.zeros((tr, D), F32)
        for o, sh in _shifted(ext2, n, tr):
            k = CONV_W - 1 - o
            if 0 <= k < CONV_W:
                acc = acc + w_ref[k:k + 1, :] * sh
        dg_ref[...] = acc

        @pl.when(i == 0)
        def _():
            dw_ref[...] = jnp.zeros((HALO, D), F32)

        prv = jnp.where(i > 0, prv_ref[...], 0.0)
        ext = jnp.concatenate([prv, glu_ref[...]], axis=0)
        for o, sh in _shifted(ext, n, tr):
            k = o - (HALO - CONV_W + 1)
            if 0 <= k < CONV_W:
                dw_ref[k:k + 1, :] += _colsum(dcv * sh)

    return _call(
        body, name=name, grid=(nsteps,), operands=[dc, dc, glu, glu, w],
        out_shape=[jax.ShapeDtypeStruct((T, D), F32), jax.ShapeDtypeStruct((HALO, D), F32)],
        in_specs=[pl.BlockSpec((tr, D), lambda i: (i, 0)),
                  pl.BlockSpec((HALO, D), lambda i: (jnp.minimum((i + 1) * hb, last_hb), 0)),
                  pl.BlockSpec((tr, D), lambda i: (i, 0)),
                  pl.BlockSpec((HALO, D), lambda i: (jnp.maximum(i * hb - 1, 0), 0)),
                  pl.BlockSpec((HALO, D), lambda i: (0, 0))],
        out_specs=[pl.BlockSpec((tr, D), lambda i: (i, 0)),
                   pl.BlockSpec((HALO, D), lambda i: (0, 0))],
        sem=("arbitrary",), comm=comm)


def _rec_consts(C):
    t = np.arange(C)
    lb = (t[None, :] <= t[:, None]).astype(np.float32)

    def cum_at(idx):
        return (t[None, :] <= idx[:, None]).astype(np.float32)

    blocks, masks = [], []
    mid = SUB * (t // SUB) + SUB // 2 - 1
    eq = lb - cum_at(mid)
    blocks += [eq, -eq]
    masks.append(((t[:, None] // SUB) == (t[None, :] // SUB)) & (t[None, :] <= t[:, None]))
    nb = SUB
    while nb < C:
        odd = (t // nb) % 2 == 1
        e_t = nb * (t // nb) - 1
        e_s = nb * (t // nb) + nb - 1
        blocks.append(np.where(odd[:, None], lb - cum_at(e_t), 0.0))
        blocks.append(np.where(~odd[:, None], cum_at(e_s) - lb, 0.0))
        masks.append(((t[:, None] // (2 * nb)) == (t[None, :] // (2 * nb))) & odd[:, None] & ~odd[None, :])
        nb *= 2
    blocks += [lb, 1.0 - lb, np.ones((HEAD, C), np.float32)]
    L = np.concatenate(blocks, axis=0).astype(np.float32)
    L3 = np.concatenate([L, L, L], axis=1)
    LT3 = np.concatenate([L.T, L.T, L.T], axis=1)
    m = np.stack(masks).astype(np.float32)
    mT = np.transpose(m, (0, 2, 1)).copy()
    return (jnp.asarray(L3, BF), jnp.asarray(LT3, BF), jnp.asarray(m), jnp.asarray(mT), len(masks))


def _split3(x):
    h = x.astype(BF)
    r = x - h.astype(F32)
    m = r.astype(BF)
    lo = (r - m.astype(F32)).astype(BF)
    return h, m, lo


def _gates(qr, fz, lbr):
    sq = _sig(qr)
    q = qr * sq
    sg = _sig(fz)
    f = lbr + (1.0 - lbr) * sg
    fc = jnp.maximum(f, GATE_EPS)
    return q, 1.0 - f, jnp.log(fc), sq, sg, f, fc


def _rec_fwd(proj, lbr, name, comm=None):
    T, D4 = proj.shape
    D = D4 // 4
    H = _heads(D)
    C = CHUNK
    nC = T // C
    L3, _, m, _, nl = _rec_consts(C)
    R = L3.shape[0]

    def body(q_ref, f_ref, v_ref, lb_ref, l_ref, m_ref, o_ref, s_ref, st):
        @pl.when(pl.program_id(0) == 0)
        def _():
            st[...] = jnp.zeros((H, HEAD, HEAD), F32)

        q, k, logf = _gates(q_ref[...], f_ref[...], lb_ref[...])[:3]
        ex = jnp.exp(_dot(l_ref[...], jnp.concatenate(_split3(logf), axis=0)))
        vb = v_ref[...].astype(BF)
        s_ref[0] = st[...]
        outs = []
        for h in range(H):
            sl = slice(h * HEAD, (h + 1) * HEAD)
            qh, kh = q[:, sl], k[:, sl]
            p = jnp.zeros((C, C), F32)
            for lv in range(nl):
                qt = (qh * ex[2 * lv * C:(2 * lv + 1) * C, sl]).astype(BF)
                kt = (kh * ex[(2 * lv + 1) * C:(2 * lv + 2) * C, sl]).astype(BF)
                p = p + jnp.where(m_ref[lv] > 0.0, _dot_nt(qt, kt), 0.0)
            base = 2 * nl * C
            qhat = (qh * ex[base:base + C, sl]).astype(BF)
            khat = (kh * ex[base + C:base + 2 * C, sl]).astype(BF)
            elast = ex[base + 2 * C:base + 2 * C + HEAD, sl]
            sth = st[h]
            outs.append(_dot(p.astype(BF), vb[:, sl]) + _dot_nt(qhat, sth.astype(BF)))
            st[h] = elast * sth + _dot_tn(vb[:, sl], khat)
        o_ref[...] = jnp.concatenate(outs, axis=1)

    def cblk(cb):
        return pl.BlockSpec((C, D), lambda i: (i, cb))

    return _call(
        body, name=name, grid=(nC,), operands=[proj, proj, proj, lbr, L3, m],
        out_shape=[jax.ShapeDtypeStruct((T, D), F32), jax.ShapeDtypeStruct((nC, H, HEAD, HEAD), F32)],
        in_specs=[cblk(0), cblk(1), cblk(2), pl.BlockSpec((1, D), lambda i: (0, 0)),
                  pl.BlockSpec(L3.shape, lambda i: (0, 0)), pl.BlockSpec(m.shape, lambda i: (0, 0, 0))],
        out_specs=[pl.BlockSpec((C, D), lambda i: (i, 0)),
                   pl.BlockSpec((1, H, HEAD, HEAD), lambda i: (i, 0, 0, 0))],
        scratch=[pltpu.VMEM((H, HEAD, HEAD), F32)], sem=("arbitrary",), comm=comm)


def _rec_bwd(proj, lbr, do, dgr, ssave, name, comm=None):
    T, D4 = proj.shape
    D = D4 // 4
    H = _heads(D)
    C = CHUNK
    nC = T // C
    L3, LT3, m, mT, nl = _rec_consts(C)

    def body(q_ref, f_ref, v_ref, lb_ref, do_ref, dgr_ref, s_ref, l_ref, lt_ref, m_ref, mt_ref,
             dp_ref, dlb_ref, dst):
        @pl.when(pl.program_id(0) == 0)
        def _():
            dst[...] = jnp.zeros((H, HEAD, HEAD), F32)
            dlb_ref[...] = jnp.zeros((1, D), F32)

        qr = q_ref[...]
        lbv = lb_ref[...]
        q, k, logf, sq, sg, f, fc = _gates(qr, f_ref[...], lbv)
        ex = jnp.exp(_dot(l_ref[...], jnp.concatenate(_split3(logf), axis=0)))
        vb = v_ref[...].astype(BF)
        dob = do_ref[...].astype(BF)
        base = 2 * nl * C
        de = [[] for _ in range(2 * nl + 3)]
        dqs, dks, dvs = [], [], []
        for h in range(H):
            sl = slice(h * HEAD, (h + 1) * HEAD)
            qh, kh, vh, doh = q[:, sl], k[:, sl], vb[:, sl], dob[:, sl]
            dp = _dot_nt(doh, vh)
            dpt = _dot_nt(vh, doh)
            sth = s_ref[0, h]
            dsth = dst[h]
            dsb = dsth.astype(BF)
            pt = jnp.zeros((C, C), F32)
            dq = jnp.zeros((C, HEAD), F32)
            dk = jnp.zeros((C, HEAD), F32)
            for lv in range(nl):
                exq = ex[2 * lv * C:(2 * lv + 1) * C, sl]
                exk = ex[(2 * lv + 1) * C:(2 * lv + 2) * C, sl]
                qt = qh * exq
                kt = kh * exk
                qtb, ktb = qt.astype(BF), kt.astype(BF)
                pt = pt + jnp.where(mt_ref[lv] > 0.0, _dot_nt(ktb, qtb), 0.0)
                dqt = _dot(jnp.where(m_ref[lv] > 0.0, dp, 0.0).astype(BF), ktb)
                dkt = _dot(jnp.where(mt_ref[lv] > 0.0, dpt, 0.0).astype(BF), qtb)
                dq = dq + dqt * exq
                dk = dk + dkt * exk
                de[2 * lv].append(dqt * qt)
                de[2 * lv + 1].append(dkt * kt)
            exb = ex[base:base + C, sl]
            exkh = ex[base + C:base + 2 * C, sl]
            elast = ex[base + 2 * C:base + 2 * C + HEAD, sl]
            qhat = qh * exb
            khat = kh * exkh
            dqh = _dot(doh, sth.astype(BF))
            dkh = _dot(vh, dsb)
            dq = dq + dqh * exb
            dk = dk + dkh * exkh
            de[2 * nl].append(dqh * qhat)
            de[2 * nl + 1].append(dkh * khat)
            de[2 * nl + 2].append(dsth * sth * elast)
            dvs.append(_dot(pt.astype(BF), doh) + _dot_nt(khat.astype(BF), dsb))
            dst[h] = elast * dsth + _dot_tn(doh, qhat.astype(BF))
            dqs.append(dq)
            dks.append(dk)
        de_all = jnp.concatenate([jnp.concatenate(b, axis=1) for b in de], axis=0)
        dlogf = _dot(lt_ref[...], jnp.concatenate(_split3(de_all), axis=0))
        dq = jnp.concatenate(dqs, axis=1)
        dk = jnp.concatenate(dks, axis=1)
        dv = jnp.concatenate(dvs, axis=1)
        ind = jnp.where(f > GATE_EPS, 1.0, jnp.where(f == GATE_EPS, 0.5, 0.0))
        df = dlogf * ind / fc - dk
        dfz = df * (1.0 - lbv) * sg * (1.0 - sg)
        dlb_ref[...] += _colsum(df * (1.0 - sg))
        dqr = dq * (sq * (1.0 + qr * (1.0 - sq)))
        dp_ref[...] = jnp.concatenate([dqr.astype(BF), dfz.astype(BF), dv.astype(BF), dgr_ref[...]], axis=1)

    def cblk(cb):
        return pl.BlockSpec((C, D), lambda i: (nC - 1 - i, cb))

    def whole(a):
        nd = a.ndim
        return pl.BlockSpec(a.shape, lambda i: (0,) * nd)

    return _call(
        body, name=name, grid=(nC,), operands=[proj, proj, proj, lbr, do, dgr, ssave, L3, LT3, m, mT],
        out_shape=[jax.ShapeDtypeStruct((T, D4), BF), jax.ShapeDtypeStruct((1, D), F32)],
        in_specs=[cblk(0), cblk(1), cblk(2), pl.BlockSpec((1, D), lambda i: (0, 0)),
                  pl.BlockSpec((C, D), lambda i: (nC - 1 - i, 0)),
                  pl.BlockSpec((C, D), lambda i: (nC - 1 - i, 0)),
                  pl.BlockSpec((1, H, HEAD, HEAD), lambda i: (nC - 1 - i, 0, 0, 0)),
                  whole(L3), whole(LT3), whole(m), whole(mT)],
        out_specs=[pl.BlockSpec((C, D4), lambda i: (nC - 1 - i, 0)),
                   pl.BlockSpec((1, D), lambda i: (0, 0))],
        scratch=[pltpu.VMEM((H, HEAD, HEAD), F32)], sem=("arbitrary",), comm=comm)


def _softmax_rows(lg_ref):
    n = lg_ref.shape[0]
    rows = [lg_ref[l:l + 1, :] for l in range(n)]
    mx = rows[0]
    for r in rows[1:]:
        mx = jnp.maximum(mx, r)
    es = [jnp.exp(r - mx) for r in rows]
    tot = es[0]
    for e in es[1:]:
        tot = tot + e
    return [e / tot for e in es]


def _lb_fwd(logits):
    n, D = logits.shape

    def body(lg_ref, o_ref):
        soft = _softmax_rows(lg_ref)
        acc = jnp.zeros((1, D), F32)
        o_ref[0:1, :] = acc
        for j in range(1, n):
            acc = acc + soft[j]
            o_ref[j:j + 1, :] = acc

    return pl.pallas_call(body, name="lb_fwd", out_shape=jax.ShapeDtypeStruct((n, D), F32))(logits)


def _small_reduce(parts, logits, lb_row0):
    _, R, D = parts.shape
    n = logits.shape[0]

    def body(p_ref, lg_ref, o_ref):
        acc = p_ref[0]
        for d in range(1, N_DEV):
            acc = acc + p_ref[d]
        o_ref[...] = acc
        soft = _softmax_rows(lg_ref)
        dlb = [o_ref[lb_row0 + j:lb_row0 + j + 1, :] for j in range(n)]
        dsoft = [jnp.zeros((1, D), F32)]
        for l in range(1, n):
            s = dlb[l]
            for j in range(l + 1, n):
                s = s + dlb[j]
            dsoft.append(s)
        dot = soft[0] * dsoft[0]
        for l in range(1, n):
            dot = dot + soft[l] * dsoft[l]
        for l in range(n):
            o_ref[lb_row0 + l:lb_row0 + l + 1, :] = soft[l] * (dsoft[l] - dot)

    return pl.pallas_call(body, name="small_reduce", out_shape=jax.ShapeDtypeStruct((R, D), F32))(parts, logits)


def _adam_math(g, w, m, v):
    m2 = ADAM_B1 * m + (1.0 - ADAM_B1) * g
    v2 = ADAM_B2 * v + (1.0 - ADAM_B2) * (g * g)
    mh = m2 / (1.0 - ADAM_B1 ** ADAM_STEP)
    vh = v2 / (1.0 - ADAM_B2 ** ADAM_STEP)
    delta = -ADAM_LR * (mh / (jnp.sqrt(vh) + ADAM_EPS) + ADAM_WD * w)
    return delta, m2, v2


def _adamw(recv, w, m, v, name):
    nsrc = recv.shape[0]
    shp = w.shape
    cols = shp[-1]
    rows = int(np.prod(shp[:-1]))
    r2 = recv.reshape(nsrc, rows, cols)
    tr = min(rows, max(8, (1 << 20) // (cols * nsrc)))
    while rows % tr:
        tr //= 2

    def body(r_ref, w_ref, m_ref, v_ref, g_ref, d_ref, nm_ref, nv_ref):
        g = r_ref[0].astype(F32)
        for s in range(1, nsrc):
            g = g + r_ref[s].astype(F32)
        delta, m2, v2 = _adam_math(g, w_ref[...], m_ref[...], v_ref[...])
        g_ref[...] = g
        d_ref[...] = delta
        nm_ref[...] = m2
        nv_ref[...] = v2

    blk = pl.BlockSpec((tr, cols), lambda i: (i, 0))
    outs = pl.pallas_call(
        body, name=name, grid=(rows // tr,),
        out_shape=[jax.ShapeDtypeStruct((rows, cols), F32)] * 4,
        in_specs=[pl.BlockSpec((nsrc, tr, cols), lambda i: (0, i, 0)), blk, blk, blk],
        out_specs=[blk] * 4,
        compiler_params=_cp(("parallel",)),
    )(r2, w.reshape(rows, cols), m.reshape(rows, cols), v.reshape(rows, cols))
    return [o.reshape(shp) for o in outs]


def _pack(arrs, lane):
    flat = jnp.concatenate([a.reshape(-1) for a in arrs])
    n = flat.shape[0]
    rows = -(-n // lane)
    rows = -(-rows // 8) * 8
    flat = jnp.pad(flat, (0, rows * lane - n))
    return flat.reshape(rows, lane)


def _unpack(packed, shapes):
    flat = packed.reshape(-1)
    out, off = [], 0
    for s in shapes:
        n = int(np.prod(s))
        out.append(flat[off:off + n].reshape(s))
        off += n
    return out


def kernel(x, ln_mix_g, ln_mix_b, ln_ffn_g, ln_ffn_b, ffn_w1, ffn_w2, a_w_in, a_lb_logits, a_norm_g, a_w_out, b_w_pw1, b_b_pw1, b_w_dw, b_b_dw, b_ln_g, b_ln_b, b_w_pw2, b_b_pw2, loss_target, m_ln_mix_g, m_ln_mix_b, m_ln_ffn_g, m_ln_ffn_b, m_ffn_w1, m_ffn_w2, m_a_w_in, m_a_lb_logits, m_a_norm_g, m_a_w_out, m_b_w_pw1, m_b_b_pw1, m_b_w_dw, m_b_b_dw, m_b_ln_g, m_b_ln_b, m_b_w_pw2, m_b_b_pw2, v_ln_mix_g, v_ln_mix_b, v_ln_ffn_g, v_ln_ffn_b, v_ffn_w1, v_ffn_w2, v_a_w_in, v_a_lb_logits, v_a_norm_g, v_a_w_out, v_b_w_pw1, v_b_b_pw1, v_b_w_dw, v_b_b_dw, v_b_ln_g, v_b_ln_b, v_b_w_pw2, v_b_b_pw2):
    T, D = x.shape[1], x.shape[2]
    nA, nB = a_w_in.shape[0], b_w_pw1.shape[0]
    me = 4 * lax.axis_index("x") + 2 * lax.axis_index("y") + lax.axis_index("c")
    xin = x[0]
    tgt = loss_target[0]

    small_names = [b_b_pw1, b_w_dw, b_b_dw, b_ln_g, b_ln_b, b_b_pw2]
    sp = _pack(small_names, 128)
    wsrc = {"W1": ffn_w1.astype(BF), "W2": ffn_w2.astype(BF), "Win": a_w_in.astype(BF),
            "Wout": a_w_out.astype(BF), "Wp1": b_w_pw1.astype(BF), "Wp2": b_w_pw2.astype(BF), "small": sp[None]}
    W = {k: jnp.zeros((N_DEV,) + v.shape, v.dtype) for k, v in wsrc.items()}
    GW = {k: jnp.zeros((N_DEV,) + v.shape, BF) for k, v in wsrc.items() if k != "small"}
    R = {k: jnp.zeros(v.shape, BF) for k, v in GW.items()}

    def names_of(items):
        out = []
        for n, _ in items:
            if n not in out:
                out.append(n)
        return out

    def hosted(fn, key, sched, mode, src, bufs, *args, **kw):
        items = sched.get(key)
        if items is None:
            return fn(*args, name=key, **kw)
        comm = (mode, [(src[n], l, bufs[n]) for n, l in items])
        outs, new = fn(*args, name=key, comm=comm, **kw)
        for n, b in zip(names_of(items), new):
            bufs[n] = b
        return outs

    first = [("Win", 0), ("small", 0)]
    for n, b in zip(names_of(first), _exchange(("gather", [(wsrc[n], l, W[n]) for n, l in first]), "gather_first")):
        W[n] = b
    fwd_sched = {"a0_proj": [("Wout", 0), ("Wp1", 0)], "a0_rec_fwd": [("W1", 0), ("W2", 0), ("Wp2", 0)],
                 "l0_ffn_up": [("W2", 1)], "l0_ffn_down": [("W1", 1)],
                 "b0_conv": [("Win", 1), ("Wout", 1)], "l1_ffn_up": [("W2", 2)], "l1_ffn_down": [("W1", 2)],
                 "a1_rec_fwd": [("Wp1", 1), ("Wp2", 1), ("W1", 3), ("W2", 3)]}
    bwd_sched = {"a1_rec_bwd": [("W1", 3), ("W2", 3), ("Wp1", 1), ("Wp2", 1)],
                 "b0_conv_bwd": [("W1", 2), ("W2", 2), ("Win", 1), ("Wout", 1)],
                 "a0_rec_bwd": [("W1", 1), ("W2", 1), ("Wp1", 0), ("Wp2", 0), ("W1", 0), ("W2", 0), ("Wout", 0)],
                 "a0_proj_dx": [("Win", 0)]}

    def fwd(fn, key, *args, **kw):
        return hosted(fn, key, fwd_sched, "gather", wsrc, W, *args, **kw)

    def bwd(fn, key, *args, **kw):
        return hosted(fn, key, bwd_sched, "scatter", GW, R, *args, **kw)

    spg = W["small"][:, 0]
    sm = [jnp.stack(p) for p in zip(*[_unpack(spg[d], [a.shape for a in small_names]) for d in range(N_DEV)])]
    bpw1 = jnp.transpose(sm[0], (1, 0, 2)).reshape(nB, 1, 2 * D)
    wdw = jnp.transpose(sm[1], (1, 2, 0, 3)).reshape(nB, CONV_W, D)
    wdw = jnp.pad(wdw, ((0, 0), (0, HALO - CONV_W), (0, 0)))
    bdw, blng, blnb, bpw2 = [jnp.transpose(s, (1, 0, 2)).reshape(nB, 1, D) for s in sm[2:]]
    lb_all = _lb_fwd(a_lb_logits)

    saved = []
    h = xin
    for i in range(DEPTH):
        j = i // 2
        sv = {"xin": h}
        if i % 2 == 0:
            proj = fwd(_mm, f"a{j}_proj", h, W["Win"], j, "nn_col", out_dtype=F32)
            o, ssave = fwd(_rec_fwd, f"a{j}_rec_fwd", proj, lb_all[j:j + 1])
            yg = _rms_gate_fwd(o, proj, a_norm_g[j:j + 1], f"a{j}_gate_fwd")
            z1 = _mm(yg, W["Wout"], j, "nn_row", out_dtype=F32, name=f"a{j}_out",
                     epi=lambda acc, r: acc + ALPHA * r, extras=[h])
            sv.update(proj=proj, o=o, ssave=ssave, yg=yg)
        else:
            u = _mm(h, W["Wp1"], j, "nn_col", out_dtype=F32, name=f"b{j}_pw1",
                    epi=lambda acc, b: acc + b, extras=[bpw1[j]])
            glu = _glu_fwd(u, f"b{j}_glu")
            cv, s = fwd(_conv_fwd, f"b{j}_conv", glu, wdw[j], bdw[j], blng[j], blnb[j])
            z1 = _mm(s, W["Wp2"], j, "nn_row", out_dtype=F32, name=f"b{j}_pw2",
                     epi=lambda acc, b, r: acc + b + ALPHA * r, extras=[bpw2[j], h])
            sv.update(u=u, glu=glu, cv=cv, s=s)
        x1 = _ln_fwd(z1, ln_mix_g[i:i + 1], ln_mix_b[i:i + 1], f"l{i}_ln_mix")
        hh = fwd(_mm, f"l{i}_ffn_up", x1, W["W1"], i, "nn_col", out_dtype=BF)
        z2 = fwd(_mm, f"l{i}_ffn_down", hh, W["W2"], i, "nn_row", out_dtype=F32,
                 a_fn=lambda t: jnp.square(jnp.maximum(t, 0)),
                 epi=lambda acc, r: acc + ALPHA * r, extras=[x1])
        h = _ln_fwd(z2, ln_ffn_g[i:i + 1], ln_ffn_b[i:i + 1], f"l{i}_ln_ffn")
        sv.update(z1=z1, x1=x1, hh=hh, z2=z2)
        saved.append(sv)

    dx, loss_row = _loss_k(h, tgt, "loss")

    rows = {}
    for i in reversed(range(DEPTH)):
        j = i // 2
        sv = saved[i]
        dz2, dg, db, _ = _ln_bwd(dx, sv["z2"], ln_ffn_g[i:i + 1], f"l{i}_ln_ffn_bwd")
        rows[("ffn_g", i)], rows[("ffn_b", i)] = dg, db
        dh = _mm(dz2, W["W2"], i, "nt_row", out_dtype=BF, name=f"l{i}_ffn_down_dx",
                 epi=lambda acc, hv: acc * (2.0 * jnp.maximum(hv.astype(F32), 0.0)), extras=[sv["hh"]])
        GW["W2"] = _mm_tn(GW["W2"], dz2, sv["hh"], i, "rows_t", name=f"l{i}_ffn_down_dw",
                         b_fn=lambda t: jnp.square(jnp.maximum(t, 0)))
        GW["W1"] = _mm_tn(GW["W1"], sv["x1"], dh, i, "cols", name=f"l{i}_ffn_up_dw")
        dx1 = _mm(dh, W["W1"], i, "nt_col", out_dtype=F32, name=f"l{i}_ffn_up_dx",
                  epi=lambda acc, r: acc + ALPHA * r, extras=[dz2])
        dz1, dg, db, dz1sum = _ln_bwd(dx1, sv["z1"], ln_mix_g[i:i + 1], f"l{i}_ln_mix_bwd")
        rows[("mix_g", i)], rows[("mix_b", i)] = dg, db
        if i % 2 == 0:
            GW["Wout"] = _mm_tn(GW["Wout"], sv["yg"], dz1, j, "rows", name=f"a{j}_out_dw")
            dyg = _mm(dz1, W["Wout"], j, "nt_row", out_dtype=F32, name=f"a{j}_out_dx")
            do, dgr, dng = _rms_gate_bwd(dyg, sv["o"], sv["proj"], a_norm_g[j:j + 1], f"a{j}_gate_bwd")
            rows[("ng", j)] = dng
            dproj, dlb = bwd(_rec_bwd, f"a{j}_rec_bwd", sv["proj"], lb_all[j:j + 1], do, dgr, sv["ssave"])
            rows[("lb", j)] = dlb
            GW["Win"] = _mm_tn(GW["Win"], sv["xin"], dproj, j, "cols", name=f"a{j}_proj_dw")
            dx = bwd(_mm, f"a{j}_proj_dx", dproj, W["Win"], j, "nt_col", out_dtype=F32,
                     epi=lambda acc, r: acc + ALPHA * r, extras=[dz1])
        else:
            rows[("bpw2", j)] = dz1sum
            GW["Wp2"] = _mm_tn(GW["Wp2"], sv["s"], dz1, j, "rows", name=f"b{j}_pw2_dw")
            ds = _mm(dz1, W["Wp2"], j, "nt_row", out_dtype=F32, name=f"b{j}_pw2_dx")
            dc, dlg, dlb_, dcs = _silu_ln_bwd(ds, sv["cv"], blng[j], blnb[j], f"b{j}_ln_bwd")
            rows[("blng", j)], rows[("blnb", j)], rows[("bdw", j)] = dlg, dlb_, dcs
            dglu, dwdw = bwd(_conv_bwd, f"b{j}_conv_bwd", dc, sv["glu"], wdw[j])
            rows[("wdw", j)] = dwdw[:CONV_W]
            du, dbu = _glu_bwd(dglu, sv["u"], f"b{j}_glu_bwd")
            rows[("bpw1", j)] = dbu.reshape(2, D)
            GW["Wp1"] = _mm_tn(GW["Wp1"], sv["xin"], du, j, "cols", name=f"b{j}_pw1_dw")
            dx = _mm(du, W["Wp1"], j, "nt_col", out_dtype=F32, name=f"b{j}_pw1_dx",
                     epi=lambda acc, r: acc + ALPHA * r, extras=[dz1])
    grad_x = dx[None]

    order = ([("mix_g", i) for i in range(DEPTH)] + [("mix_b", i) for i in range(DEPTH)]
             + [("ffn_g", i) for i in range(DEPTH)] + [("ffn_b", i) for i in range(DEPTH)])
    lb_row0 = len(order)
    order += [("lb", j) for j in range(nA)] + [("ng", j) for j in range(nA)]
    for j in range(nB):
        order += [("bpw1", j), ("wdw", j), ("bdw", j), ("blng", j), ("blnb", j), ("bpw2", j)]
    pieces, offs, off = [], {}, 0
    for key in order:
        offs[key] = off
        pieces.append(rows[key])
        off += rows[key].shape[0]
    offs["loss"] = off
    pieces.append(loss_row)
    off += 1
    part = jnp.concatenate(pieces, axis=0)
    part = jnp.pad(part, ((0, -off % 8), (0, 0)))
    parts = _exchange(("gather", [(part[None], 0, jnp.zeros((N_DEV, 1) + part.shape, F32))]),
                      "gather_small_grads")[0][:, 0]
    G = _small_reduce(parts, a_lb_logits, lb_row0)
    loss = jnp.sum(G[offs["loss"]])

    def rep(kind, n):
        return jnp.concatenate([G[offs[(kind, i)]:offs[(kind, i)] + 1] for i in range(n)], axis=0)

    def shard_cols(full, width):
        return lax.dynamic_slice_in_dim(full, me * width, width, axis=full.ndim - 1)

    g_small = {
        "ln_mix_g": rep("mix_g", DEPTH), "ln_mix_b": rep("mix_b", DEPTH),
        "ln_ffn_g": rep("ffn_g", DEPTH), "ln_ffn_b": rep("ffn_b", DEPTH),
        "a_lb_logits": rep("lb", nA), "a_norm_g": rep("ng", nA),
        "b_b_pw1": shard_cols(jnp.stack([G[offs[("bpw1", j)]:offs[("bpw1", j)] + 2].reshape(2 * D)
                                         for j in range(nB)]), 2 * D // N_DEV),
        "b_w_dw": shard_cols(jnp.stack([G[offs[("wdw", j)]:offs[("wdw", j)] + CONV_W] for j in range(nB)]),
                             D // N_DEV),
        "b_b_dw": shard_cols(rep("bdw", nB), D // N_DEV),
        "b_ln_g": shard_cols(rep("blng", nB), D // N_DEV),
        "b_ln_b": shard_cols(rep("blnb", nB), D // N_DEV),
        "b_b_pw2": shard_cols(rep("bpw2", nB), D // N_DEV),
    }
    small_w = {"ln_mix_g": (ln_mix_g, m_ln_mix_g, v_ln_mix_g), "ln_mix_b": (ln_mix_b, m_ln_mix_b, v_ln_mix_b),
               "ln_ffn_g": (ln_ffn_g, m_ln_ffn_g, v_ln_ffn_g), "ln_ffn_b": (ln_ffn_b, m_ln_ffn_b, v_ln_ffn_b),
               "a_lb_logits": (a_lb_logits, m_a_lb_logits, v_a_lb_logits),
               "a_norm_g": (a_norm_g, m_a_norm_g, v_a_norm_g),
               "b_b_pw1": (b_b_pw1, m_b_b_pw1, v_b_b_pw1), "b_w_dw": (b_w_dw, m_b_w_dw, v_b_w_dw),
               "b_b_dw": (b_b_dw, m_b_b_dw, v_b_b_dw), "b_ln_g": (b_ln_g, m_b_ln_g, v_b_ln_g),
               "b_ln_b": (b_ln_b, m_b_ln_b, v_b_ln_b), "b_b_pw2": (b_b_pw2, m_b_b_pw2, v_b_b_pw2)}
    snames = list(small_w)
    sshapes = [small_w[k][0].shape for k in snames]
    pg = _pack([g_small[k] for k in snames], 1024)
    pw, pm, pv = [_pack([small_w[k][q] for k in snames], 1024) for q in range(3)]
    sres = _adamw(pg[None], pw, pm, pv, "adamw_small")
    sres = [dict(zip(snames, _unpack(r, sshapes))) for r in sres]

    recv = [R[n] for n in ["W1", "W2", "Win", "Wout", "Wp1", "Wp2"]]
    big = {}
    for nm, rv, (w, m, v) in zip(
            ["ffn_w1", "ffn_w2", "a_w_in", "a_w_out", "b_w_pw1", "b_w_pw2"], recv,
            [(ffn_w1, m_ffn_w1, v_ffn_w1), (ffn_w2, m_ffn_w2, v_ffn_w2), (a_w_in, m_a_w_in, v_a_w_in),
             (a_w_out, m_a_w_out, v_a_w_out), (b_w_pw1, m_b_w_pw1, v_b_w_pw1), (b_w_pw2, m_b_w_pw2, v_b_w_pw2)]):
        big[nm] = _adamw(rv, w, m, v, f"adamw_{nm}")

    names = ["ln_mix_g", "ln_mix_b", "ln_ffn_g", "ln_ffn_b", "ffn_w1", "ffn_w2", "a_w_in", "a_lb_logits",
             "a_norm_g", "a_w_out", "b_w_pw1", "b_b_pw1", "b_w_dw", "b_b_dw", "b_ln_g", "b_ln_b", "b_w_pw2",
             "b_b_pw2"]
    out = [loss, grad_x]
    for q in range(4):
        for nm in names:
            out.append(big[nm][q] if nm in big else sres[q][nm])
    return tuple(out)
```

```python
import functools

import numpy as np
import jax
import jax.numpy as jnp
from jax import lax
from jax.experimental import pallas as pl
from jax.experimental.pallas import tpu as pltpu

F32 = jnp.float32
BF = jnp.bfloat16

N_DEV = 8
DEPTH = 4
HEAD = 128
CHUNK = 64
CHUNK_BWD = 64
SUB = 8
CONV_W = 31
HALO = 32
ALPHA = (2.0 * DEPTH) ** 0.25
LN_EPS = 1e-5
RMS_EPS = 1e-6
GATE_EPS = 1e-6
ADAM_LR = 0.001
ADAM_B1 = 0.9
ADAM_B2 = 0.999
ADAM_EPS = 1e-08
ADAM_WD = 0.01
ADAM_STEP = 10
VMEM_LIMIT = 56 * 1024 * 1024
MESH = pl.DeviceIdType.MESH


def _cp(sem=None):
    return pltpu.CompilerParams(vmem_limit_bytes=VMEM_LIMIT, dimension_semantics=sem)


def _dot(a, b):
    return jnp.dot(a, b, preferred_element_type=F32)


def _dot_nt(a, b):
    return lax.dot_general(a, b, (((1,), (1,)), ((), ())), preferred_element_type=F32)


def _dot_tn(a, b):
    return lax.dot_general(a, b, (((0,), (0,)), ((), ())), preferred_element_type=F32)


def _sig(x):
    return jax.nn.sigmoid(x)


def _distinct(items):
    out = []
    for it in items:
        if not any(it is q for q in out):
            out.append(it)
    return out


def _index_of(items, it):
    return next(i for i, q in enumerate(items) if q is it)


def _comm_copies(mode, plan, src_refs, buf_refs, send_sems, recv_sems, loc_sems):
    x, y, c = lax.axis_index("x"), lax.axis_index("y"), lax.axis_index("c")
    me = 4 * x + 2 * y + c
    locs, rems = [], []
    for k, (si, l, bi) in enumerate(plan):
        src = src_refs[si].at[l] if mode == "gather" else src_refs[si].at[me, l]
        locs.append(pltpu.make_async_copy(src, buf_refs[bi].at[me, l], loc_sems.at[k]))
    for r in range(1, N_DEV):
        px = (1 - x) if (r >> 2) & 1 else x
        py = (1 - y) if (r >> 1) & 1 else y
        pc = (1 - c) if r & 1 else c
        pid = 4 * px + 2 * py + pc
        for k, (si, l, bi) in enumerate(plan):
            src = src_refs[si].at[l] if mode == "gather" else src_refs[si].at[pid, l]
            rems.append(pltpu.make_async_remote_copy(
                src_ref=src, dst_ref=buf_refs[bi].at[me, l],
                send_sem=send_sems.at[k, r - 1], recv_sem=recv_sems.at[k, r - 1],
                device_id=(px, py, pc), device_id_type=MESH))
    return locs, rems


def _call(body, *, name, grid, operands, in_specs, out_shape, out_specs, scratch=(), sem=None,
          aliases=None, comm=None):
    aliases = dict(aliases or {})
    if comm is None:
        return pl.pallas_call(
            body, name=name, grid=grid, out_shape=list(out_shape), in_specs=list(in_specs),
            out_specs=list(out_specs), scratch_shapes=list(scratch), input_output_aliases=aliases,
            compiler_params=_cp(sem),
        )(*operands)
    mode, pieces = comm
    srcs = _distinct([p[0] for p in pieces])
    bufs = _distinct([p[2] for p in pieces])
    plan = [(_index_of(srcs, s), l, _index_of(bufs, b)) for (s, l, b) in pieces]
    n_in, n_out, n_scr, ns, nb, npc = len(operands), len(out_shape), len(scratch), len(srcs), len(bufs), len(plan)
    nsteps = grid[0]

    def wrapped(*refs):
        ins = refs[:n_in]
        src_refs = refs[n_in:n_in + ns]
        o0 = n_in + ns + nb
        outs = refs[o0:o0 + n_out]
        buf_refs = refs[o0 + n_out:o0 + n_out + nb]
        s0 = o0 + n_out + nb
        scr = refs[s0:s0 + n_scr]
        sems = refs[s0 + n_scr:]
        step = pl.program_id(0)

        @pl.when(step == 0)
        def _():
            locs, rems = _comm_copies(mode, plan, src_refs, buf_refs, *sems)
            for d in locs + rems:
                d.start()

        body(*ins, *outs, *scr)

        @pl.when(step == nsteps - 1)
        def _():
            locs, rems = _comm_copies(mode, plan, src_refs, buf_refs, *sems)
            for d in rems:
                d.wait_send()
            for d in rems:
                d.wait_recv()
            for d in locs:
                d.wait()

    anyspec = pl.BlockSpec(memory_space=pl.ANY)
    for k in range(nb):
        aliases[n_in + ns + k] = n_out + k
    res = pl.pallas_call(
        wrapped, name=name, grid=grid,
        out_shape=list(out_shape) + [jax.ShapeDtypeStruct(b.shape, b.dtype) for b in bufs],
        in_specs=list(in_specs) + [anyspec] * (ns + nb),
        out_specs=list(out_specs) + [anyspec] * nb,
        scratch_shapes=list(scratch) + [pltpu.SemaphoreType.DMA((npc, N_DEV - 1)),
                                        pltpu.SemaphoreType.DMA((npc, N_DEV - 1)),
                                        pltpu.SemaphoreType.DMA((npc,))],
        input_output_aliases=aliases,
        compiler_params=pltpu.CompilerParams(vmem_limit_bytes=VMEM_LIMIT, has_side_effects=True,
                                             dimension_semantics=("arbitrary",) * len(grid)),
    )(*operands, *srcs, *bufs)
    return list(res[:n_out]), list(res[n_out:])


def _exchange(comm, name):
    def body():
        pass

    return _call(body, name=name, grid=(1,), operands=[], in_specs=[], out_shape=[], out_specs=[],
                 comm=comm)[1]


def _mm(a, w, l, kind, *, out_dtype, name, tm=256, a_fn=None, epi=None, extras=(), comm=None,
        outs=None, sums=()):
    T, Ka = a.shape
    _, _, d2, d3 = w.shape
    n_out = {"nn_col": N_DEV * d3, "nn_row": d3, "nt_col": d2, "nt_row": N_DEV * d2}[kind]
    tm = min(tm, T)
    ne = len(extras)
    single = outs is None and not sums
    outs = [(n_out, out_dtype)] if outs is None else outs
    no, ns = len(outs), len(sums)

    def body(*refs):
        a_ref, w_ref = refs[0], refs[1]
        e_refs = refs[2:2 + ne]
        o_refs = refs[2 + ne:2 + ne + no]
        s_refs = refs[2 + ne + no:]
        av = a_ref[...]
        if a_fn is not None:
            av = a_fn(av)
        av = av.astype(BF)
        if kind == "nn_col":
            acc = jnp.concatenate([_dot(av, w_ref[j]) for j in range(N_DEV)], axis=1)
        elif kind == "nn_row":
            acc = _dot(av, w_ref[...].reshape(N_DEV * d2, d3))
        elif kind == "nt_col":
            acc = _dot_nt(av[:, 0:d3], w_ref[0])
            for j in range(1, N_DEV):
                acc = acc + _dot_nt(av[:, j * d3:(j + 1) * d3], w_ref[j])
        else:
            acc = jnp.concatenate([_dot_nt(av, w_ref[j]) for j in range(N_DEV)], axis=1)
        res = acc if epi is None else epi(acc, *[e[...] for e in e_refs])
        res = res if isinstance(res, tuple) else (res,)
        for o_ref, val in zip(o_refs, res[:no]):
            o_ref[...] = val.astype(o_ref.dtype)
        if ns:
            first = pl.program_id(0) == 0

            @pl.when(first)
            def _():
                for s_ref, val in zip(s_refs, res[no:]):
                    s_ref[...] = val

            @pl.when(jnp.logical_not(first))
            def _():
                for s_ref, val in zip(s_refs, res[no:]):
                    s_ref[...] += val

    in_specs = [pl.BlockSpec((tm, Ka), lambda i: (i, 0)),
                pl.BlockSpec((N_DEV, None, d2, d3), lambda i: (0, l, 0, 0))]
    for e in extras:
        if e.shape[0] == 1:
            in_specs.append(pl.BlockSpec((1, n_out), lambda i: (0, 0)))
        else:
            in_specs.append(pl.BlockSpec((tm, n_out), lambda i: (i, 0)))
    out_shape = [jax.ShapeDtypeStruct((T, wd), dt) for wd, dt in outs]
    out_shape += [jax.ShapeDtypeStruct((1, wd), F32) for wd in sums]
    out_specs = [pl.BlockSpec((tm, wd), lambda i: (i, 0)) for wd, _ in outs]
    out_specs += [pl.BlockSpec((1, wd), lambda i: (0, 0)) for wd in sums]
    res = _call(body, name=name, grid=(T // tm,), operands=[a, w, *extras], in_specs=in_specs,
                out_shape=out_shape, out_specs=out_specs,
                sem=("arbitrary",) if ns else ("parallel",), comm=comm)
    if comm is None:
        return res[0] if single else res
    return (res[0][0] if single else res[0]), res[1]


def _mm_tn(g, s, b, l, kind, *, name, tt=1024, b_fn=None):
    T, ws = s.shape
    wb = b.shape[1]
    _, _, d2, d3 = g.shape
    tt = min(tt, T)
    nsteps = T // tt
    cw = wb // N_DEV

    def body(g_in, s_ref, b_ref, o_ref, acc, stage, sem):
        del g_in
        t = pl.program_id(0)

        @pl.when(t == 0)
        def _():
            acc[...] = jnp.zeros((ws, wb), F32)

        st = s_ref[...].astype(F32).T.astype(BF)
        for j in range(N_DEV):
            bv = b_ref[:, j * cw:(j + 1) * cw]
            if b_fn is not None:
                bv = b_fn(bv)
            acc[:, j * cw:(j + 1) * cw] += _dot(st, bv.astype(BF))

        @pl.when(t == nsteps - 1)
        def _():
            for j in range(N_DEV):
                if kind == "cols":
                    blk = acc[:, j * d3:(j + 1) * d3]
                elif kind == "rows":
                    blk = acc[j * d2:(j + 1) * d2, :]
                else:
                    blk = acc[:, j * d2:(j + 1) * d2].T
                stage[...] = blk.astype(BF)
                cp = pltpu.make_async_copy(stage, o_ref.at[j, l], sem)
                cp.start()
                cp.wait()

    return pl.pallas_call(
        body, name=name, grid=(nsteps,),
        out_shape=jax.ShapeDtypeStruct(g.shape, BF),
        in_specs=[pl.BlockSpec(memory_space=pl.ANY),
                  pl.BlockSpec((tt, ws), lambda t: (t, 0)),
                  pl.BlockSpec((tt, wb), lambda t: (t, 0))],
        out_specs=pl.BlockSpec(memory_space=pl.ANY),
        scratch_shapes=[pltpu.VMEM((ws, wb), F32), pltpu.VMEM((d2, d3), BF), pltpu.SemaphoreType.DMA],
        input_output_aliases={0: 0},
        compiler_params=_cp(("arbitrary",)),
    )(g, s, b)


def _rw(fn, tiles, rows, outs, sums, *, name, tr=256):
    T = tiles[0][0].shape[0]
    tr = min(tr, T)
    nt, nr, no, ns = len(tiles), len(rows), len(outs), len(sums)

    def body(*refs):
        t_refs, r_refs = refs[:nt], refs[nt:nt + nr]
        o_refs = refs[nt + nr:nt + nr + no]
        s_refs = refs[nt + nr + no:]
        res = fn(*[t[...] for t in t_refs], *[r[...] for r in r_refs])
        for o_ref, val in zip(o_refs, res[:no]):
            o_ref[...] = val.astype(o_ref.dtype)
        if ns:
            first = pl.program_id(0) == 0

            @pl.when(first)
            def _():
                for s_ref, val in zip(s_refs, res[no:]):
                    s_ref[...] = val

            @pl.when(jnp.logical_not(first))
            def _():
                for s_ref, val in zip(s_refs, res[no:]):
                    s_ref[...] += val

    in_specs = [pl.BlockSpec((tr, w), functools.partial(lambda i, cb: (i, cb), cb=cb)) for (_, w, cb) in tiles]
    in_specs += [pl.BlockSpec(r.shape, lambda i: (0, 0)) for r in rows]
    out_shape = [jax.ShapeDtypeStruct((T, w), dt) for (w, dt) in outs]
    out_shape += [jax.ShapeDtypeStruct((1, w), F32) for w in sums]
    out_specs = [pl.BlockSpec((tr, w), lambda i: (i, 0)) for (w, _) in outs]
    out_specs += [pl.BlockSpec((1, w), lambda i: (0, 0)) for w in sums]
    return pl.pallas_call(
        body, name=name, grid=(T // tr,), out_shape=out_shape,
        in_specs=in_specs, out_specs=out_specs,
        compiler_params=_cp(("arbitrary",)),
    )(*[t[0] for t in tiles], *rows)


def _ln_stats(z):
    mu = jnp.mean(z, axis=-1, keepdims=True)
    zc = z - mu
    var = jnp.mean(zc * zc, axis=-1, keepdims=True)
    rstd = lax.rsqrt(var + LN_EPS)
    return zc * rstd, rstd


def _ln_bwd_core(dy, zh, rstd, g):
    dzh = dy * g
    m1 = jnp.mean(dzh, axis=-1, keepdims=True)
    m2 = jnp.mean(dzh * zh, axis=-1, keepdims=True)
    return rstd * (dzh - m1 - zh * m2)


def _colsum(v):
    return jnp.sum(v, axis=0, keepdims=True)


def _z_and_ln(z, g, b):
    zh, _ = _ln_stats(z)
    return z, zh * g + b


def _ln_bwd_tile(dy, z, g):
    zh, rstd = _ln_stats(z)
    dz = _ln_bwd_core(dy, zh, rstd, g)
    return dz, _colsum(dy * zh), _colsum(dy), _colsum(dz)


def _ln_fwd(z, g, b, name):
    D = z.shape[1]

    def fn(zt, gr, br):
        zh, _ = _ln_stats(zt)
        return (zh * gr + br,)

    return _rw(fn, [(z, D, 0)], [g, b], [(D, F32)], [], name=name)[0]


def _ln_bwd(dy, z, g, name):
    D = z.shape[1]

    def fn(dyt, zt, gr):
        zh, rstd = _ln_stats(zt)
        dz = _ln_bwd_core(dyt, zh, rstd, gr)
        return dz, _colsum(dyt * zh), _colsum(dyt), _colsum(dz)

    return _rw(fn, [(dy, D, 0), (z, D, 0)], [g], [(D, F32)], [D, D, D], name=name)


def _heads(D):
    return D // HEAD


def _rms_parts(o, D):
    xs, rs = [], []
    for h in range(_heads(D)):
        oh = o[:, h * HEAD:(h + 1) * HEAD]
        r = lax.rsqrt(jnp.mean(oh * oh, axis=-1, keepdims=True) + RMS_EPS)
        xs.append(oh * r)
        rs.append(r)
    return xs, rs


def _rms_gate_fwd(o, proj, ng, name):
    D = o.shape[1]

    def fn(ot, grt, ngr):
        xs, _ = _rms_parts(ot, D)
        xh = jnp.concatenate(xs, axis=1)
        return (xh * ngr * (grt * _sig(grt)),)

    return _rw(fn, [(o, D, 0), (proj, D, 3)], [ng], [(D, BF)], [], name=name)[0]


def _rms_gate_bwd(dy, o, proj, ng, name):
    D = o.shape[1]

    def fn(dyt, ot, grt, ngr):
        xs, rs = _rms_parts(ot, D)
        xh = jnp.concatenate(xs, axis=1)
        sg = _sig(grt)
        on = xh * ngr
        dgr = dyt * on * (sg * (1.0 + grt * (1.0 - sg)))
        don = dyt * (grt * sg)
        dxh = don * ngr
        dos = []
        for h in range(_heads(D)):
            sl = slice(h * HEAD, (h + 1) * HEAD)
            m = jnp.mean(dxh[:, sl] * xs[h], axis=-1, keepdims=True)
            dos.append(rs[h] * (dxh[:, sl] - xs[h] * m))
        return jnp.concatenate(dos, axis=1), dgr, _colsum(don * xh)

    return _rw(fn, [(dy, D, 0), (o, D, 0), (proj, D, 3)], [ng], [(D, F32), (D, BF)], [D], name=name)


def _glu_fwd(u, name):
    D = u.shape[1] // 2

    def fn(at, gt):
        return (at * _sig(gt),)

    return _rw(fn, [(u, D, 0), (u, D, 1)], [], [(D, F32)], [], name=name)[0]


def _glu_bwd(dglu, u, name):
    D = u.shape[1] // 2

    def fn(dg, at, gt):
        sg = _sig(gt)
        du = jnp.concatenate([dg * sg, dg * at * sg * (1.0 - sg)], axis=1)
        return du, _colsum(du)

    return _rw(fn, [(dglu, D, 0), (u, D, 0), (u, D, 1)], [], [(2 * D, BF)], [2 * D], name=name)


def _silu_ln_bwd(ds, c, g, b, name):
    D = c.shape[1]

    def fn(dst, ct, gr, br):
        zh, rstd = _ln_stats(ct)
        ln = zh * gr + br
        sg = _sig(ln)
        dln = dst * (sg * (1.0 + ln * (1.0 - sg)))
        dc = _ln_bwd_core(dln, zh, rstd, gr)
        return dc, _colsum(dln * zh), _colsum(dln), _colsum(dc)

    return _rw(fn, [(ds, D, 0), (c, D, 0)], [g, b], [(D, F32)], [D, D, D], name=name)


def _loss_k(y, tgt, name):
    D = y.shape[1]

    def fn(yt, tt):
        e = yt - tt
        return e * (1.0 / D), _colsum(e * e) * (0.5 / D)

    return _rw(fn, [(y, D, 0), (tgt, D, 0)], [], [(D, F32)], [D], name=name)


def _shifted(ext, n, tr):
    for b in range(8):
        rb = ext if b == 0 else pltpu.roll(ext, n - b, 0)
        for a in range(HALO // 8 + 1):
            o = 8 * a + b
            if o <= HALO:
                yield o, rb[8 * a:8 * a + tr, :]


def _conv_fwd(glu, w, bdw, g, b, name, tr=256, comm=None):
    T, D = glu.shape
    tr = min(tr, T // 2)
    n = tr + HALO
    hb = tr // HALO

    def body(cur_ref, halo_ref, w_ref, bdw_ref, g_ref, b_ref, c_ref, s_ref):
        i = pl.program_id(0)
        halo = jnp.where(i > 0, halo_ref[...], 0.0)
        ext = jnp.concatenate([halo, cur_ref[...]], axis=0)
        acc = jnp.zeros((tr, D), F32)
        for o, sh in _shifted(ext, n, tr):
            k = o - (HALO - CONV_W + 1)
            if 0 <= k < CONV_W:
                acc = acc + w_ref[k:k + 1, :] * sh
        cv = acc + bdw_ref[...]
        c_ref[...] = cv
        zh, _ = _ln_stats(cv)
        ln = zh * g_ref[...] + b_ref[...]
        s_ref[...] = (ln * _sig(ln)).astype(BF)

    row = pl.BlockSpec((1, D), lambda i: (0, 0))
    return _call(
        body, name=name, grid=(T // tr,), operands=[glu, glu, w, bdw, g, b],
        out_shape=[jax.ShapeDtypeStruct((T, D), F32), jax.ShapeDtypeStruct((T, D), BF)],
        in_specs=[pl.BlockSpec((tr, D), lambda i: (i, 0)),
                  pl.BlockSpec((HALO, D), lambda i: (jnp.maximum(i * hb - 1, 0), 0)),
                  pl.BlockSpec((HALO, D), lambda i: (0, 0)), row, row, row],
        out_specs=[pl.BlockSpec((tr, D), lambda i: (i, 0))] * 2,
        sem=("parallel",), comm=comm)


def _conv_bwd(dc, glu, w, name, tr=256, comm=None):
    T, D = glu.shape
    tr = min(tr, T // 2)
    n = tr + HALO
    hb = tr // HALO
    nsteps = T // tr
    last_hb = T // HALO - 1

    def body(dc_ref, nxt_ref, glu_ref, prv_ref, w_ref, dg_ref, dw_ref):
        i = pl.program_id(0)
        dcv = dc_ref[...]
        nxt = jnp.where(i < nsteps - 1, nxt_ref[...], 0.0)
        ext2 = jnp.concatenate([dcv, nxt], axis=0)
        acc = jnp.zeros((tr, D), F32)
        for o, sh in _shifted(ext2, n, tr):
            k = CONV_W - 1 - o
            if 0 <= k < CONV_W:
                acc = acc + w_ref[k:k + 1, :] * sh
        dg_ref[...] = acc

        @pl.when(i == 0)
        def _():
            dw_ref[...] = jnp.zeros((HALO, D), F32)

        prv = jnp.where(i > 0, prv_ref[...], 0.0)
        ext = jnp.concatenate([prv, glu_ref[...]], axis=0)
        for o, sh in _shifted(ext, n, tr):
            k = o - (HALO - CONV_W + 1)
            if 0 <= k < CONV_W:
                dw_ref[k:k + 1, :] += _colsum(dcv * sh)

    return _call(
        body, name=name, grid=(nsteps,), operands=[dc, dc, glu, glu, w],
        out_shape=[jax.ShapeDtypeStruct((T, D), F32), jax.ShapeDtypeStruct((HALO, D), F32)],
        in_specs=[pl.BlockSpec((tr, D), lambda i: (i, 0)),
                  pl.BlockSpec((HALO, D), lambda i: (jnp.minimum((i + 1) * hb, last_hb), 0)),
                  pl.BlockSpec((tr, D), lambda i: (i, 0)),
                  pl.BlockSpec((HALO, D), lambda i: (jnp.maximum(i * hb - 1, 0), 0)),
                  pl.BlockSpec((HALO, D), lambda i: (0, 0))],
        out_specs=[pl.BlockSpec((tr, D), lambda i: (i, 0)),
                   pl.BlockSpec((HALO, D), lambda i: (0, 0))],
        sem=("arbitrary",), comm=comm)


def _rec_consts(C):
    t = np.arange(C)
    lb = (t[None, :] <= t[:, None]).astype(np.float32)

    def cum_at(idx):
        return (t[None, :] <= idx[:, None]).astype(np.float32)

    blocks, masks = [], []
    mid = SUB * (t // SUB) + SUB // 2 - 1
    eq = lb - cum_at(mid)
    blocks += [eq, -eq]
    masks.append(((t[:, None] // SUB) == (t[None, :] // SUB)) & (t[None, :] <= t[:, None]))
    nb = SUB
    while nb < C:
        odd = (t // nb) % 2 == 1
        e_t = nb * (t // nb) - 1
        e_s = nb * (t // nb) + nb - 1
        blocks.append(np.where(odd[:, None], lb - cum_at(e_t), 0.0))
        blocks.append(np.where(~odd[:, None], cum_at(e_s) - lb, 0.0))
        masks.append(((t[:, None] // (2 * nb)) == (t[None, :] // (2 * nb))) & odd[:, None] & ~odd[None, :])
        nb *= 2
    blocks += [lb, 1.0 - lb, np.ones((HEAD, C), np.float32)]
    L = np.concatenate(blocks, axis=0).astype(np.float32)
    L3 = np.concatenate([L, L, L], axis=1)
    LT3 = np.concatenate([L.T, L.T, L.T], axis=1)
    m = np.stack(masks).astype(np.float32)
    mT = np.transpose(m, (0, 2, 1)).copy()
    return (jnp.asarray(L3, BF), jnp.asarray(LT3, BF), jnp.asarray(m), jnp.asarray(mT), len(masks))


def _split3(x):
    h = x.astype(BF)
    r = x - h.astype(F32)
    m = r.astype(BF)
    lo = (r - m.astype(F32)).astype(BF)
    return h, m, lo


def _gates(qr, fz, lbr):
    sq = _sig(qr)
    q = qr * sq
    sg = _sig(fz)
    f = lbr + (1.0 - lbr) * sg
    fc = jnp.maximum(f, GATE_EPS)
    return q, 1.0 - f, jnp.log(fc), sq, sg, f, fc


def _rec_fwd(proj, lbr, name, comm=None):
    T, D4 = proj.shape
    D = D4 // 4
    H = _heads(D)
    C = CHUNK
    nC = T // C
    L3, _, m, _, nl = _rec_consts(C)
    R = L3.shape[0]

    def body(q_ref, f_ref, v_ref, lb_ref, l_ref, m_ref, o_ref, s_ref, st):
        @pl.when(pl.program_id(0) == 0)
        def _():
            st[...] = jnp.zeros((H, HEAD, HEAD), F32)

        q, k, logf = _gates(q_ref[...], f_ref[...], lb_ref[...])[:3]
        ex = jnp.exp(_dot(l_ref[...], jnp.concatenate(_split3(logf), axis=0)))
        vb = v_ref[...].astype(BF)
        s_ref[0] = st[...]
        outs = []
        for h in range(H):
            sl = slice(h * HEAD, (h + 1) * HEAD)
            qh, kh = q[:, sl], k[:, sl]
            p = jnp.zeros((C, C), F32)
            for lv in range(nl):
                qt = (qh * ex[2 * lv * C:(2 * lv + 1) * C, sl]).astype(BF)
                kt = (kh * ex[(2 * lv + 1) * C:(2 * lv + 2) * C, sl]).astype(BF)
                p = p + jnp.where(m_ref[lv] > 0.0, _dot_nt(qt, kt), 0.0)
            base = 2 * nl * C
            qhat = (qh * ex[base:base + C, sl]).astype(BF)
            khat = (kh * ex[base + C:base + 2 * C, sl]).astype(BF)
            elast = ex[base + 2 * C:base + 2 * C + HEAD, sl]
            sth = st[h]
            outs.append(_dot(p.astype(BF), vb[:, sl]) + _dot_nt(qhat, sth.astype(BF)))
            st[h] = elast * sth + _dot_tn(vb[:, sl], khat)
        o_ref[...] = jnp.concatenate(outs, axis=1)

    def cblk(cb):
        return pl.BlockSpec((C, D), lambda i: (i, cb))

    return _call(
        body, name=name, grid=(nC,), operands=[proj, proj, proj, lbr, L3, m],
        out_shape=[jax.ShapeDtypeStruct((T, D), F32), jax.ShapeDtypeStruct((nC, H, HEAD, HEAD), F32)],
        in_specs=[cblk(0), cblk(1), cblk(2), pl.BlockSpec((1, D), lambda i: (0, 0)),
                  pl.BlockSpec(L3.shape, lambda i: (0, 0)), pl.BlockSpec(m.shape, lambda i: (0, 0, 0))],
        out_specs=[pl.BlockSpec((C, D), lambda i: (i, 0)),
                   pl.BlockSpec((1, H, HEAD, HEAD), lambda i: (i, 0, 0, 0))],
        scratch=[pltpu.VMEM((H, HEAD, HEAD), F32)], sem=("arbitrary",), comm=comm)


def _rec_bwd(proj, lbr, do, dgr, ssave, name, comm=None):
    T, D4 = proj.shape
    D = D4 // 4
    H = _heads(D)
    C = min(CHUNK_BWD, T // 2)
    nC = T // C
    stride = C // CHUNK
    L3, LT3, m, mT, nl = _rec_consts(C)

    def body(q_ref, f_ref, v_ref, lb_ref, do_ref, dgr_ref, s_ref, l_ref, lt_ref, m_ref, mt_ref,
             dp_ref, dlb_ref, dst):
        @pl.when(pl.program_id(0) == 0)
        def _():
            dst[...] = jnp.zeros((H, HEAD, HEAD), F32)
            dlb_ref[...] = jnp.zeros((1, D), F32)

        qr = q_ref[...]
        lbv = lb_ref[...]
        q, k, logf, sq, sg, f, fc = _gates(qr, f_ref[...], lbv)
        ex = jnp.exp(_dot(l_ref[...], jnp.concatenate(_split3(logf), axis=0)))
        vb = v_ref[...].astype(BF)
        dob = do_ref[...].astype(BF)
        base = 2 * nl * C
        de = [[] for _ in range(2 * nl + 3)]
        dqs, dks, dvs = [], [], []
        for h in range(H):
            sl = slice(h * HEAD, (h + 1) * HEAD)
            qh, kh, vh, doh = q[:, sl], k[:, sl], vb[:, sl], dob[:, sl]
            dp = _dot_nt(doh, vh)
            dpt = _dot_nt(vh, doh)
            sth = s_ref[0, h]
            dsth = dst[h]
            dsb = dsth.astype(BF)
            pt = jnp.zeros((C, C), F32)
            dq = jnp.zeros((C, HEAD), F32)
            dk = jnp.zeros((C, HEAD), F32)
            for lv in range(nl):
                exq = ex[2 * lv * C:(2 * lv + 1) * C, sl]
                exk = ex[(2 * lv + 1) * C:(2 * lv + 2) * C, sl]
                qt = qh * exq
                kt = kh * exk
                qtb, ktb = qt.astype(BF), kt.astype(BF)
                pt = pt + jnp.where(mt_ref[lv] > 0.0, _dot_nt(ktb, qtb), 0.0)
                dqt = _dot(jnp.where(m_ref[lv] > 0.0, dp, 0.0).astype(BF), ktb)
                dkt = _dot(jnp.where(mt_ref[lv] > 0.0, dpt, 0.0).astype(BF), qtb)
                dq = dq + dqt * exq
                dk = dk + dkt * exk
                de[2 * lv].append(dqt * qt)
                de[2 * lv + 1].append(dkt * kt)
            exb = ex[base:base + C, sl]
            exkh = ex[base + C:base + 2 * C, sl]
            elast = ex[base + 2 * C:base + 2 * C + HEAD, sl]
            qhat = qh * exb
            khat = kh * exkh
            dqh = _dot(doh, sth.astype(BF))
            dkh = _dot(vh, dsb)
            dq = dq + dqh * exb
            dk = dk + dkh * exkh
            de[2 * nl].append(dqh * qhat)
            de[2 * nl + 1].append(dkh * khat)
            de[2 * nl + 2].append(dsth * sth * elast)
            dvs.append(_dot(pt.astype(BF), doh) + _dot_nt(khat.astype(BF), dsb))
            dst[h] = elast * dsth + _dot_tn(doh, qhat.astype(BF))
            dqs.append(dq)
            dks.append(dk)
        de_all = jnp.concatenate([jnp.concatenate(b, axis=1) for b in de], axis=0)
        dlogf = _dot(lt_ref[...], jnp.concatenate(_split3(de_all), axis=0))
        dq = jnp.concatenate(dqs, axis=1)
        dk = jnp.concatenate(dks, axis=1)
        dv = jnp.concatenate(dvs, axis=1)
        ind = jnp.where(f > GATE_EPS, 1.0, jnp.where(f == GATE_EPS, 0.5, 0.0))
        df = dlogf * ind / fc - dk
        dfz = df * (1.0 - lbv) * sg * (1.0 - sg)
        dlb_ref[...] += _colsum(df * (1.0 - sg))
        dqr = dq * (sq * (1.0 + qr * (1.0 - sq)))
        dp_ref[...] = jnp.concatenate([dqr.astype(BF), dfz.astype(BF), dv.astype(BF), dgr_ref[...]], axis=1)

    def cblk(cb):
        return pl.BlockSpec((C, D), lambda i: (nC - 1 - i, cb))

    def whole(a):
        nd = a.ndim
        return pl.BlockSpec(a.shape, lambda i: (0,) * nd)

    return _call(
        body, name=name, grid=(nC,), operands=[proj, proj, proj, lbr, do, dgr, ssave, L3, LT3, m, mT],
        out_shape=[jax.ShapeDtypeStruct((T, D4), BF), jax.ShapeDtypeStruct((1, D), F32)],
        in_specs=[cblk(0), cblk(1), cblk(2), pl.BlockSpec((1, D), lambda i: (0, 0)),
                  pl.BlockSpec((C, D), lambda i: (nC - 1 - i, 0)),
                  pl.BlockSpec((C, D), lambda i: (nC - 1 - i, 0)),
                  pl.BlockSpec((1, H, HEAD, HEAD), lambda i: (stride * (nC - 1 - i), 0, 0, 0)),
                  whole(L3), whole(LT3), whole(m), whole(mT)],
        out_specs=[pl.BlockSpec((C, D4), lambda i: (nC - 1 - i, 0)),
                   pl.BlockSpec((1, D), lambda i: (0, 0))],
        scratch=[pltpu.VMEM((H, HEAD, HEAD), F32)], sem=("arbitrary",), comm=comm)


def _softmax_rows(lg_ref):
    n = lg_ref.shape[0]
    rows = [lg_ref[l:l + 1, :] for l in range(n)]
    mx = rows[0]
    for r in rows[1:]:
        mx = jnp.maximum(mx, r)
    es = [jnp.exp(r - mx) for r in rows]
    tot = es[0]
    for e in es[1:]:
        tot = tot + e
    return [e / tot for e in es]


def _lb_fwd(logits):
    n, D = logits.shape

    def body(lg_ref, o_ref):
        soft = _softmax_rows(lg_ref)
        acc = jnp.zeros((1, D), F32)
        o_ref[0:1, :] = acc
        for j in range(1, n):
            acc = acc + soft[j]
            o_ref[j:j + 1, :] = acc

    return pl.pallas_call(body, name="lb_fwd", out_shape=jax.ShapeDtypeStruct((n, D), F32))(logits)


def _small_reduce(parts, logits, lb_row0):
    _, R, D = parts.shape
    n = logits.shape[0]

    def body(p_ref, lg_ref, o_ref):
        acc = p_ref[0]
        for d in range(1, N_DEV):
            acc = acc + p_ref[d]
        o_ref[...] = acc
        soft = _softmax_rows(lg_ref)
        dlb = [o_ref[lb_row0 + j:lb_row0 + j + 1, :] for j in range(n)]
        dsoft = [jnp.zeros((1, D), F32)]
        for l in range(1, n):
            s = dlb[l]
            for j in range(l + 1, n):
                s = s + dlb[j]
            dsoft.append(s)
        dot = soft[0] * dsoft[0]
        for l in range(1, n):
            dot = dot + soft[l] * dsoft[l]
        for l in range(n):
            o_ref[lb_row0 + l:lb_row0 + l + 1, :] = soft[l] * (dsoft[l] - dot)

    return pl.pallas_call(body, name="small_reduce", out_shape=jax.ShapeDtypeStruct((R, D), F32))(parts, logits)


def _adam_math(g, w, m, v):
    m2 = ADAM_B1 * m + (1.0 - ADAM_B1) * g
    v2 = ADAM_B2 * v + (1.0 - ADAM_B2) * (g * g)
    mh = m2 / (1.0 - ADAM_B1 ** ADAM_STEP)
    vh = v2 / (1.0 - ADAM_B2 ** ADAM_STEP)
    delta = -ADAM_LR * (mh / (jnp.sqrt(vh) + ADAM_EPS) + ADAM_WD * w)
    return delta, m2, v2


def _adamw(recv, w, m, v, name):
    nsrc = recv.shape[0]
    shp = w.shape
    cols = shp[-1]
    rows = int(np.prod(shp[:-1]))
    r2 = recv.reshape(nsrc, rows, cols)
    tr = min(rows, max(8, (1 << 20) // (cols * nsrc)))
    while rows % tr:
        tr //= 2

    def body(r_ref, w_ref, m_ref, v_ref, g_ref, d_ref, nm_ref, nv_ref):
        g = r_ref[0].astype(F32)
        for s in range(1, nsrc):
            g = g + r_ref[s].astype(F32)
        delta, m2, v2 = _adam_math(g, w_ref[...], m_ref[...], v_ref[...])
        g_ref[...] = g
        d_ref[...] = delta
        nm_ref[...] = m2
        nv_ref[...] = v2

    blk = pl.BlockSpec((tr, cols), lambda i: (i, 0))
    outs = pl.pallas_call(
        body, name=name, grid=(rows // tr,),
        out_shape=[jax.ShapeDtypeStruct((rows, cols), F32)] * 4,
        in_specs=[pl.BlockSpec((nsrc, tr, cols), lambda i: (0, i, 0)), blk, blk, blk],
        out_specs=[blk] * 4,
        compiler_params=_cp(("parallel",)),
    )(r2, w.reshape(rows, cols), m.reshape(rows, cols), v.reshape(rows, cols))
    return [o.reshape(shp) for o in outs]


def _pack(arrs, lane):
    flat = jnp.concatenate([a.reshape(-1) for a in arrs])
    n = flat.shape[0]
    rows = -(-n // lane)
    rows = -(-rows // 8) * 8
    flat = jnp.pad(flat, (0, rows * lane - n))
    return flat.reshape(rows, lane)


def _unpack(packed, shapes):
    flat = packed.reshape(-1)
    out, off = [], 0
    for s in shapes:
        n = int(np.prod(s))
        out.append(flat[off:off + n].reshape(s))
        off += n
    return out


def kernel(x, ln_mix_g, ln_mix_b, ln_ffn_g, ln_ffn_b, ffn_w1, ffn_w2, a_w_in, a_lb_logits, a_norm_g, a_w_out, b_w_pw1, b_b_pw1, b_w_dw, b_b_dw, b_ln_g, b_ln_b, b_w_pw2, b_b_pw2, loss_target, m_ln_mix_g, m_ln_mix_b, m_ln_ffn_g, m_ln_ffn_b, m_ffn_w1, m_ffn_w2, m_a_w_in, m_a_lb_logits, m_a_norm_g, m_a_w_out, m_b_w_pw1, m_b_b_pw1, m_b_w_dw, m_b_b_dw, m_b_ln_g, m_b_ln_b, m_b_w_pw2, m_b_b_pw2, v_ln_mix_g, v_ln_mix_b, v_ln_ffn_g, v_ln_ffn_b, v_ffn_w1, v_ffn_w2, v_a_w_in, v_a_lb_logits, v_a_norm_g, v_a_w_out, v_b_w_pw1, v_b_b_pw1, v_b_w_dw, v_b_b_dw, v_b_ln_g, v_b_ln_b, v_b_w_pw2, v_b_b_pw2):
    T, D = x.shape[1], x.shape[2]
    nA, nB = a_w_in.shape[0], b_w_pw1.shape[0]
    me = 4 * lax.axis_index("x") + 2 * lax.axis_index("y") + lax.axis_index("c")
    xin = x[0]
    tgt = loss_target[0]

    small_names = [b_b_pw1, b_w_dw, b_b_dw, b_ln_g, b_ln_b, b_b_pw2]
    sp = _pack(small_names, 128)
    wsrc = {"W1": ffn_w1.astype(BF), "W2": ffn_w2.astype(BF), "Win": a_w_in.astype(BF),
            "Wout": a_w_out.astype(BF), "Wp1": b_w_pw1.astype(BF), "Wp2": b_w_pw2.astype(BF), "small": sp[None]}
    W = {k: jnp.zeros((N_DEV,) + v.shape, v.dtype) for k, v in wsrc.items()}
    GW = {k: jnp.zeros((N_DEV,) + v.shape, BF) for k, v in wsrc.items() if k != "small"}
    R = {k: jnp.zeros(v.shape, BF) for k, v in GW.items()}

    def names_of(items):
        out = []
        for n, _ in items:
            if n not in out:
                out.append(n)
        return out

    def hosted(fn, key, sched, mode, src, bufs, *args, **kw):
        items = sched.get(key)
        if items is None:
            return fn(*args, name=key, **kw)
        comm = (mode, [(src[n], l, bufs[n]) for n, l in items])
        outs, new = fn(*args, name=key, comm=comm, **kw)
        for n, b in zip(names_of(items), new):
            bufs[n] = b
        return outs

    first = [("Win", 0), ("small", 0)]
    for n, b in zip(names_of(first), _exchange(("gather", [(wsrc[n], l, W[n]) for n, l in first]), "gather_first")):
        W[n] = b
    fwd_sched = {"a0_proj": [("Wout", 0), ("Wp1", 0)], "a0_rec_fwd": [("W1", 0), ("W2", 0), ("Wp2", 0)],
                 "l0_ffn_up": [("W2", 1)], "l0_ffn_down": [("W1", 1)],
                 "b0_conv": [("Win", 1), ("Wout", 1)], "l1_ffn_up": [("W2", 2)], "l1_ffn_down": [("W1", 2)],
                 "a1_rec_fwd": [("Wp1", 1), ("Wp2", 1), ("W1", 3), ("W2", 3)]}
    bwd_sched = {"a1_rec_bwd": [("W1", 3), ("W2", 3), ("Wp1", 1), ("Wp2", 1)],
                 "b0_conv_bwd": [("W1", 2), ("W2", 2), ("Win", 1), ("Wout", 1)],
                 "a0_rec_bwd": [("W1", 1), ("W2", 1), ("Wp1", 0), ("Wp2", 0), ("W1", 0), ("W2", 0), ("Wout", 0)],
                 "a0_proj_dx": [("Win", 0)]}

    def fwd(fn, key, *args, **kw):
        return hosted(fn, key, fwd_sched, "gather", wsrc, W, *args, **kw)

    def bwd(fn, key, *args, **kw):
        return hosted(fn, key, bwd_sched, "scatter", GW, R, *args, **kw)

    spg = W["small"][:, 0]
    sm = [jnp.stack(p) for p in zip(*[_unpack(spg[d], [a.shape for a in small_names]) for d in range(N_DEV)])]
    bpw1 = jnp.transpose(sm[0], (1, 0, 2)).reshape(nB, 1, 2 * D)
    wdw = jnp.transpose(sm[1], (1, 2, 0, 3)).reshape(nB, CONV_W, D)
    wdw = jnp.pad(wdw, ((0, 0), (0, HALO - CONV_W), (0, 0)))
    bdw, blng, blnb, bpw2 = [jnp.transpose(s, (1, 0, 2)).reshape(nB, 1, D) for s in sm[2:]]
    lb_all = _lb_fwd(a_lb_logits)

    zx = [(D, F32), (D, F32)]
    ln_mix = [(ln_mix_g[i:i + 1], ln_mix_b[i:i + 1]) for i in range(DEPTH)]
    ln_ffn = [(ln_ffn_g[i:i + 1], ln_ffn_b[i:i + 1]) for i in range(DEPTH)]
    saved = []
    h = xin
    for i in range(DEPTH):
        j = i // 2
        sv = {"xin": h}
        if i % 2 == 0:
            proj = fwd(_mm, f"a{j}_proj", h, W["Win"], j, "nn_col", out_dtype=F32)
            o, ssave = fwd(_rec_fwd, f"a{j}_rec_fwd", proj, lb_all[j:j + 1])
            yg = _rms_gate_fwd(o, proj, a_norm_g[j:j + 1], f"a{j}_gate_fwd")
            z1, x1 = _mm(yg, W["Wout"], j, "nn_row", out_dtype=F32, name=f"a{j}_out", outs=zx,
                         epi=lambda acc, r, g, b: _z_and_ln(acc + ALPHA * r, g, b), extras=[h, *ln_mix[i]])
            sv.update(proj=proj, o=o, ssave=ssave, yg=yg)
        else:
            u = _mm(h, W["Wp1"], j, "nn_col", out_dtype=F32, name=f"b{j}_pw1",
                    epi=lambda acc, b: acc + b, extras=[bpw1[j]])
            glu = _glu_fwd(u, f"b{j}_glu")
            cv, s = fwd(_conv_fwd, f"b{j}_conv", glu, wdw[j], bdw[j], blng[j], blnb[j])
            z1, x1 = _mm(s, W["Wp2"], j, "nn_row", out_dtype=F32, name=f"b{j}_pw2", outs=zx,
                         epi=lambda acc, bb, r, g, b: _z_and_ln(acc + bb + ALPHA * r, g, b),
                         extras=[bpw2[j], h, *ln_mix[i]])
            sv.update(u=u, glu=glu, cv=cv, s=s)
        hh = fwd(_mm, f"l{i}_ffn_up", x1, W["W1"], i, "nn_col", out_dtype=BF)
        z2, h = fwd(_mm, f"l{i}_ffn_down", hh, W["W2"], i, "nn_row", out_dtype=F32, outs=zx,
                    a_fn=lambda t: jnp.square(jnp.maximum(t, 0)),
                    epi=lambda acc, r, g, b: _z_and_ln(acc + ALPHA * r, g, b), extras=[x1, *ln_ffn[i]])
        sv.update(z1=z1, x1=x1, hh=hh, z2=z2)
        saved.append(sv)

    dx, loss_row = _loss_k(h, tgt, "loss")

    rows = {}
    top = DEPTH - 1
    dz2, dg, db, _ = _ln_bwd(dx, saved[top]["z2"], ln_ffn[top][0], f"l{top}_ln_ffn_bwd")
    rows[("ffn_g", top)], rows[("ffn_b", top)] = dg, db
    ln_out = dict(outs=[(D, F32)], sums=[D, D, D])
    for i in reversed(range(DEPTH)):
        j = i // 2
        sv = saved[i]
        if i > 0:
            below = dict(epi=lambda acc, r, z, g: _ln_bwd_tile(acc + ALPHA * r, z, g), **ln_out)
            below_extras = [saved[i - 1]["z2"], ln_ffn[i - 1][0]]
        else:
            below = dict(epi=lambda acc, r: acc + ALPHA * r)
            below_extras = []
        dh = _mm(dz2, W["W2"], i, "nt_row", out_dtype=BF, name=f"l{i}_ffn_down_dx",
                 epi=lambda acc, hv: acc * (2.0 * jnp.maximum(hv.astype(F32), 0.0)), extras=[sv["hh"]])
        GW["W2"] = _mm_tn(GW["W2"], dz2, sv["hh"], i, "rows_t", name=f"l{i}_ffn_down_dw",
                         b_fn=lambda t: jnp.square(jnp.maximum(t, 0)))
        GW["W1"] = _mm_tn(GW["W1"], sv["x1"], dh, i, "cols", name=f"l{i}_ffn_up_dw")
        dz1, dg, db, dz1sum = _mm(dh, W["W1"], i, "nt_col", out_dtype=F32, name=f"l{i}_ffn_up_dx",
                                  epi=lambda acc, r, z, g: _ln_bwd_tile(acc + ALPHA * r, z, g),
                                  extras=[dz2, sv["z1"], ln_mix[i][0]], **ln_out)
        rows[("mix_g", i)], rows[("mix_b", i)] = dg, db
        if i % 2 == 0:
            GW["Wout"] = _mm_tn(GW["Wout"], sv["yg"], dz1, j, "rows", name=f"a{j}_out_dw")
            dyg = _mm(dz1, W["Wout"], j, "nt_row", out_dtype=F32, name=f"a{j}_out_dx")
            do, dgr, dng = _rms_gate_bwd(dyg, sv["o"], sv["proj"], a_norm_g[j:j + 1], f"a{j}_gate_bwd")
            rows[("ng", j)] = dng
            dproj, dlb = bwd(_rec_bwd, f"a{j}_rec_bwd", sv["proj"], lb_all[j:j + 1], do, dgr, sv["ssave"])
            rows[("lb", j)] = dlb
            GW["Win"] = _mm_tn(GW["Win"], sv["xin"], dproj, j, "cols", name=f"a{j}_proj_dw")
            res = bwd(_mm, f"a{j}_proj_dx", dproj, W["Win"], j, "nt_col", out_dtype=F32,
                      extras=[dz1, *below_extras], **below)
        else:
            rows[("bpw2", j)] = dz1sum
            GW["Wp2"] = _mm_tn(GW["Wp2"], sv["s"], dz1, j, "rows", name=f"b{j}_pw2_dw")
            ds = _mm(dz1, W["Wp2"], j, "nt_row", out_dtype=F32, name=f"b{j}_pw2_dx")
            dc, dlg, dlb_, dcs = _silu_ln_bwd(ds, sv["cv"], blng[j], blnb[j], f"b{j}_ln_bwd")
            rows[("blng", j)], rows[("blnb", j)], rows[("bdw", j)] = dlg, dlb_, dcs
            dglu, dwdw = bwd(_conv_bwd, f"b{j}_conv_bwd", dc, sv["glu"], wdw[j])
            rows[("wdw", j)] = dwdw[:CONV_W]
            du, dbu = _glu_bwd(dglu, sv["u"], f"b{j}_glu_bwd")
            rows[("bpw1", j)] = dbu.reshape(2, D)
            GW["Wp1"] = _mm_tn(GW["Wp1"], sv["xin"], du, j, "cols", name=f"b{j}_pw1_dw")
            res = _mm(du, W["Wp1"], j, "nt_col", out_dtype=F32, name=f"b{j}_pw1_dx",
                      extras=[dz1, *below_extras], **below)
        if i > 0:
            dz2, dg, db, _ = res
            rows[("ffn_g", i - 1)], rows[("ffn_b", i - 1)] = dg, db
        else:
            dx = res
    grad_x = dx[None]

    order = ([("mix_g", i) for i in range(DEPTH)] + [("mix_b", i) for i in range(DEPTH)]
             + [("ffn_g", i) for i in range(DEPTH)] + [("ffn_b", i) for i in range(DEPTH)])
    lb_row0 = len(order)
    order += [("lb", j) for j in range(nA)] + [("ng", j) for j in range(nA)]
    for j in range(nB):
        order += [("bpw1", j), ("wdw", j), ("bdw", j), ("blng", j), ("blnb", j), ("bpw2", j)]
    pieces, offs, off = [], {}, 0
    for key in order:
        offs[key] = off
        pieces.append(rows[key])
        off += rows[key].shape[0]
    offs["loss"] = off
    pieces.append(loss_row)
    off += 1
    part = jnp.concatenate(pieces, axis=0)
    part = jnp.pad(part, ((0, -off % 8), (0, 0)))
    parts = _exchange(("gather", [(part[None], 0, jnp.zeros((N_DEV, 1) + part.shape, F32))]),
                      "gather_small_grads")[0][:, 0]
    G = _small_reduce(parts, a_lb_logits, lb_row0)
    loss = jnp.sum(G[offs["loss"]])

    def rep(kind, n):
        return jnp.concatenate([G[offs[(kind, i)]:offs[(kind, i)] + 1] for i in range(n)], axis=0)

    def shard_cols(full, width):
        return lax.dynamic_slice_in_dim(full, me * width, width, axis=full.ndim - 1)

    g_small = {
        "ln_mix_g": rep("mix_g", DEPTH), "ln_mix_b": rep("mix_b", DEPTH),
        "ln_ffn_g": rep("ffn_g", DEPTH), "ln_ffn_b": rep("ffn_b", DEPTH),
        "a_lb_logits": rep("lb", nA), "a_norm_g": rep("ng", nA),
        "b_b_pw1": shard_cols(jnp.stack([G[offs[("bpw1", j)]:offs[("bpw1", j)] + 2].reshape(2 * D)
                                         for j in range(nB)]), 2 * D // N_DEV),
        "b_w_dw": shard_cols(jnp.stack([G[offs[("wdw", j)]:offs[("wdw", j)] + CONV_W] for j in range(nB)]),
                             D // N_DEV),
        "b_b_dw": shard_cols(rep("bdw", nB), D // N_DEV),
        "b_ln_g": shard_cols(rep("blng", nB), D // N_DEV),
        "b_ln_b": shard_cols(rep("blnb", nB), D // N_DEV),
        "b_b_pw2": shard_cols(rep("bpw2", nB), D // N_DEV),
    }
    small_w = {"ln_mix_g": (ln_mix_g, m_ln_mix_g, v_ln_mix_g), "ln_mix_b": (ln_mix_b, m_ln_mix_b, v_ln_mix_b),
               "ln_ffn_g": (ln_ffn_g, m_ln_ffn_g, v_ln_ffn_g), "ln_ffn_b": (ln_ffn_b, m_ln_ffn_b, v_ln_ffn_b),
               "a_lb_logits": (a_lb_logits, m_a_lb_logits, v_a_lb_logits),
               "a_norm_g": (a_norm_g, m_a_norm_g, v_a_norm_g),
               "b_b_pw1": (b_b_pw1, m_b_b_pw1, v_b_b_pw1), "b_w_dw": (b_w_dw, m_b_w_dw, v_b_w_dw),
               "b_b_dw": (b_b_dw, m_b_b_dw, v_b_b_dw), "b_ln_g": (b_ln_g, m_b_ln_g, v_b_ln_g),
               "b_ln_b": (b_ln_b, m_b_ln_b, v_b_ln_b), "b_b_pw2": (b_b_pw2, m_b_b_pw2, v_b_b_pw2)}
    snames = list(small_w)
    sshapes = [small_w[k][0].shape for k in snames]
    pg = _pack([g_small[k] for k in snames], 1024)
    pw, pm, pv = [_pack([small_w[k][q] for k in snames], 1024) for q in range(3)]
    sres = _adamw(pg[None], pw, pm, pv, "adamw_small")
    sres = [dict(zip(snames, _unpack(r, sshapes))) for r in sres]

    recv = [R[n] for n in ["W1", "W2", "Win", "Wout", "Wp1", "Wp2"]]
    big = {}
    for nm, rv, (w, m, v) in zip(
            ["ffn_w1", "ffn_w2", "a_w_in", "a_w_out", "b_w_pw1", "b_w_pw2"], recv,
            [(ffn_w1, m_ffn_w1, v_ffn_w1), (ffn_w2, m_ffn_w2, v_ffn_w2), (a_w_in, m_a_w_in, v_a_w_in),
             (a_w_out, m_a_w_out, v_a_w_out), (b_w_pw1, m_b_w_pw1, v_b_w_pw1), (b_w_pw2, m_b_w_pw2, v_b_w_pw2)]):
        big[nm] = _adamw(rv, w, m, v, f"adamw_{nm}")

    names = ["ln_mix_g", "ln_mix_b", "ln_ffn_g", "ln_ffn_b", "ffn_w1", "ffn_w2", "a_w_in", "a_lb_logits",
             "a_norm_g", "a_w_out", "b_w_pw1", "b_b_pw1", "b_w_dw", "b_b_dw", "b_ln_g", "b_ln_b", "b_w_pw2",
             "b_b_pw2"]
    out = [loss, grad_x]
    for q in range(4):
        for nm in names:
            out.append(big[nm][q] if nm in big else sres[q][nm])
    return tuple(out)
```

```python
import functools

import numpy as np
import jax
import jax.numpy as jnp
from jax import lax
from jax.experimental import pallas as pl
from jax.experimental.pallas import tpu as pltpu

F32 = jnp.float32
BF = jnp.bfloat16

N_DEV = 8
DEPTH = 4
HEAD = 128
CHUNK = 64
CHUNK_BWD = 64
SUB = 8
CONV_W = 31
HALO = 32
CONV_ROWS = 128
ALPHA = (2.0 * DEPTH) ** 0.25
LN_EPS = 1e-5
RMS_EPS = 1e-6
GATE_EPS = 1e-6
ADAM_LR = 0.001
ADAM_B1 = 0.9
ADAM_B2 = 0.999
ADAM_EPS = 1e-08
ADAM_WD = 0.01
ADAM_STEP = 10
VMEM_LIMIT = 56 * 1024 * 1024
MESH = pl.DeviceIdType.MESH


def _cp(sem=None):
    return pltpu.CompilerParams(vmem_limit_bytes=VMEM_LIMIT, dimension_semantics=sem)


def _dot(a, b):
    return jnp.dot(a, b, preferred_element_type=F32)


def _dot_nt(a, b):
    return lax.dot_general(a, b, (((1,), (1,)), ((), ())), preferred_element_type=F32)


def _dot_tn(a, b):
    return lax.dot_general(a, b, (((0,), (0,)), ((), ())), preferred_element_type=F32)


def _sig(x):
    return jax.nn.sigmoid(x)


def _distinct(items):
    out = []
    for it in items:
        if not any(it is q for q in out):
            out.append(it)
    return out


def _index_of(items, it):
    return next(i for i, q in enumerate(items) if q is it)


def _comm_copies(mode, plan, src_refs, buf_refs, send_sems, recv_sems, loc_sems):
    x, y, c = lax.axis_index("x"), lax.axis_index("y"), lax.axis_index("c")
    me = 4 * x + 2 * y + c
    locs, rems = [], []
    for k, (si, l, bi) in enumerate(plan):
        src = src_refs[si].at[l] if mode == "gather" else src_refs[si].at[me, l]
        locs.append(pltpu.make_async_copy(src, buf_refs[bi].at[me, l], loc_sems.at[k]))
    for r in range(1, N_DEV):
        px = (1 - x) if (r >> 2) & 1 else x
        py = (1 - y) if (r >> 1) & 1 else y
        pc = (1 - c) if r & 1 else c
        pid = 4 * px + 2 * py + pc
        for k, (si, l, bi) in enumerate(plan):
            src = src_refs[si].at[l] if mode == "gather" else src_refs[si].at[pid, l]
            rems.append(pltpu.make_async_remote_copy(
                src_ref=src, dst_ref=buf_refs[bi].at[me, l],
                send_sem=send_sems.at[k, r - 1], recv_sem=recv_sems.at[k, r - 1],
                device_id=(px, py, pc), device_id_type=MESH))
    return locs, rems


def _call(body, *, name, grid, operands, in_specs, out_shape, out_specs, scratch=(), sem=None,
          aliases=None, comm=None):
    aliases = dict(aliases or {})
    if comm is None:
        return pl.pallas_call(
            body, name=name, grid=grid, out_shape=list(out_shape), in_specs=list(in_specs),
            out_specs=list(out_specs), scratch_shapes=list(scratch), input_output_aliases=aliases,
            compiler_params=_cp(sem),
        )(*operands)
    mode, pieces = comm
    srcs = _distinct([p[0] for p in pieces])
    bufs = _distinct([p[2] for p in pieces])
    plan = [(_index_of(srcs, s), l, _index_of(bufs, b)) for (s, l, b) in pieces]
    n_in, n_out, n_scr, ns, nb, npc = len(operands), len(out_shape), len(scratch), len(srcs), len(bufs), len(plan)
    nsteps = grid[0]
    old = [b for b in bufs if not isinstance(b, jax.ShapeDtypeStruct)]
    nbi = len(old)

    def wrapped(*refs):
        ins = refs[:n_in]
        src_refs = refs[n_in:n_in + ns]
        o0 = n_in + ns + nbi
        outs = refs[o0:o0 + n_out]
        buf_refs = refs[o0 + n_out:o0 + n_out + nb]
        s0 = o0 + n_out + nb
        scr = refs[s0:s0 + n_scr]
        sems = refs[s0 + n_scr:]
        step = pl.program_id(0)

        @pl.when(step == 0)
        def _():
            locs, rems = _comm_copies(mode, plan, src_refs, buf_refs, *sems)
            for d in locs + rems:
                d.start()

        body(*ins, *outs, *scr)

        @pl.when(step == nsteps - 1)
        def _():
            locs, rems = _comm_copies(mode, plan, src_refs, buf_refs, *sems)
            for d in rems:
                d.wait_send()
            for d in rems:
                d.wait_recv()
            for d in locs:
                d.wait()

    anyspec = pl.BlockSpec(memory_space=pl.ANY)
    for k, b in enumerate(old):
        aliases[n_in + ns + k] = n_out + _index_of(bufs, b)
    res = pl.pallas_call(
        wrapped, name=name, grid=grid,
        out_shape=list(out_shape) + [jax.ShapeDtypeStruct(b.shape, b.dtype) for b in bufs],
        in_specs=list(in_specs) + [anyspec] * (ns + nbi),
        out_specs=list(out_specs) + [anyspec] * nb,
        scratch_shapes=list(scratch) + [pltpu.SemaphoreType.DMA((npc, N_DEV - 1)),
                                        pltpu.SemaphoreType.DMA((npc, N_DEV - 1)),
                                        pltpu.SemaphoreType.DMA((npc,))],
        input_output_aliases=aliases,
        compiler_params=pltpu.CompilerParams(vmem_limit_bytes=VMEM_LIMIT, has_side_effects=True,
                                             dimension_semantics=("arbitrary",) * len(grid)),
    )(*operands, *srcs, *old)
    return list(res[:n_out]), list(res[n_out:])


def _exchange(comm, name):
    def body():
        pass

    return _call(body, name=name, grid=(1,), operands=[], in_specs=[], out_shape=[], out_specs=[],
                 comm=comm)[1]


def _mm(a, w, l, kind, *, out_dtype, name, tm=512, a_fn=None, epi=None, extras=(), comm=None,
        outs=None, sums=()):
    T, Ka = a.shape
    _, _, d2, d3 = w.shape
    n_out = {"nn_col": N_DEV * d3, "nn_row": d3, "nt_col": d2, "nt_row": N_DEV * d2}[kind]
    tm = min(tm, T)
    ne = len(extras)
    single = outs is None and not sums
    outs = [(n_out, out_dtype)] if outs is None else outs
    no, ns = len(outs), len(sums)

    def body(*refs):
        a_ref, w_ref = refs[0], refs[1]
        e_refs = refs[2:2 + ne]
        o_refs = refs[2 + ne:2 + ne + no]
        s_refs = refs[2 + ne + no:]
        av = a_ref[...]
        if a_fn is not None:
            av = a_fn(av)
        av = av.astype(BF)
        if kind == "nn_col":
            acc = jnp.concatenate([_dot(av, w_ref[j]) for j in range(N_DEV)], axis=1)
        elif kind == "nn_row":
            acc = _dot(av, w_ref[...].reshape(N_DEV * d2, d3))
        elif kind == "nt_col":
            acc = _dot_nt(av[:, 0:d3], w_ref[0])
            for j in range(1, N_DEV):
                acc = acc + _dot_nt(av[:, j * d3:(j + 1) * d3], w_ref[j])
        else:
            acc = jnp.concatenate([_dot_nt(av, w_ref[j]) for j in range(N_DEV)], axis=1)
        res = acc if epi is None else epi(acc, *[e[...] for e in e_refs])
        res = res if isinstance(res, tuple) else (res,)
        for o_ref, val in zip(o_refs, res[:no]):
            o_ref[...] = val.astype(o_ref.dtype)
        if ns:
            first = pl.program_id(0) == 0

            @pl.when(first)
            def _():
                for s_ref, val in zip(s_refs, res[no:]):
                    s_ref[...] = val

            @pl.when(jnp.logical_not(first))
            def _():
                for s_ref, val in zip(s_refs, res[no:]):
                    s_ref[...] += val

    in_specs = [pl.BlockSpec((tm, Ka), lambda i: (i, 0)),
                pl.BlockSpec((N_DEV, None, d2, d3), lambda i: (0, l, 0, 0))]
    for e in extras:
        if e.shape[0] == 1:
            in_specs.append(pl.BlockSpec((1, n_out), lambda i: (0, 0)))
        else:
            in_specs.append(pl.BlockSpec((tm, n_out), lambda i: (i, 0)))
    out_shape = [jax.ShapeDtypeStruct((T, wd), dt) for wd, dt in outs]
    out_shape += [jax.ShapeDtypeStruct((1, wd), F32) for wd in sums]
    out_specs = [pl.BlockSpec((tm, wd), lambda i: (i, 0)) for wd, _ in outs]
    out_specs += [pl.BlockSpec((1, wd), lambda i: (0, 0)) for wd in sums]
    res = _call(body, name=name, grid=(T // tm,), operands=[a, w, *extras], in_specs=in_specs,
                out_shape=out_shape, out_specs=out_specs,
                sem=("arbitrary",) if ns else ("parallel",), comm=comm)
    if comm is None:
        return res[0] if single else res
    return (res[0][0] if single else res[0]), res[1]


def _mm_tn(g, s, b, l, kind, *, name, tt=1024, b_fn=None):
    T, ws = s.shape
    wb = b.shape[1]
    _, _, d2, d3 = g.shape
    tt = min(tt, T)
    nsteps = T // tt
    cw = wb // N_DEV

    fresh = isinstance(g, jax.ShapeDtypeStruct)

    def body(*refs):
        s_ref, b_ref, o_ref, acc, stage, sem = refs if fresh else refs[1:]
        t = pl.program_id(0)

        @pl.when(t == 0)
        def _():
            acc[...] = jnp.zeros((ws, wb), F32)

        st = s_ref[...].astype(F32).T.astype(BF)
        for j in range(N_DEV):
            bv = b_ref[:, j * cw:(j + 1) * cw]
            if b_fn is not None:
                bv = b_fn(bv)
            acc[:, j * cw:(j + 1) * cw] += _dot(st, bv.astype(BF))

        @pl.when(t == nsteps - 1)
        def _():
            for j in range(N_DEV):
                if kind == "cols":
                    blk = acc[:, j * d3:(j + 1) * d3]
                elif kind == "rows":
                    blk = acc[j * d2:(j + 1) * d2, :]
                else:
                    blk = acc[:, j * d2:(j + 1) * d2].T
                stage[...] = blk.astype(BF)
                cp = pltpu.make_async_copy(stage, o_ref.at[j, l], sem)
                cp.start()
                cp.wait()

    anyspec = pl.BlockSpec(memory_space=pl.ANY)
    return pl.pallas_call(
        body, name=name, grid=(nsteps,),
        out_shape=jax.ShapeDtypeStruct(g.shape, BF),
        in_specs=([] if fresh else [anyspec]) + [pl.BlockSpec((tt, ws), lambda t: (t, 0)),
                                                 pl.BlockSpec((tt, wb), lambda t: (t, 0))],
        out_specs=anyspec,
        scratch_shapes=[pltpu.VMEM((ws, wb), F32), pltpu.VMEM((d2, d3), BF), pltpu.SemaphoreType.DMA],
        input_output_aliases={} if fresh else {0: 0},
        compiler_params=_cp(("arbitrary",)),
    )(*([] if fresh else [g]), s, b)


def _rw(fn, tiles, rows, outs, sums, *, name, tr=256):
    T = tiles[0][0].shape[0]
    tr = min(tr, T)
    nt, nr, no, ns = len(tiles), len(rows), len(outs), len(sums)

    def body(*refs):
        t_refs, r_refs = refs[:nt], refs[nt:nt + nr]
        o_refs = refs[nt + nr:nt + nr + no]
        s_refs = refs[nt + nr + no:]
        res = fn(*[t[...] for t in t_refs], *[r[...] for r in r_refs])
        for o_ref, val in zip(o_refs, res[:no]):
            o_ref[...] = val.astype(o_ref.dtype)
        if ns:
            first = pl.program_id(0) == 0

            @pl.when(first)
            def _():
                for s_ref, val in zip(s_refs, res[no:]):
                    s_ref[...] = val

            @pl.when(jnp.logical_not(first))
            def _():
                for s_ref, val in zip(s_refs, res[no:]):
                    s_ref[...] += val

    in_specs = [pl.BlockSpec((tr, w), functools.partial(lambda i, cb: (i, cb), cb=cb)) for (_, w, cb) in tiles]
    in_specs += [pl.BlockSpec(r.shape, lambda i: (0, 0)) for r in rows]
    out_shape = [jax.ShapeDtypeStruct((T, w), dt) for (w, dt) in outs]
    out_shape += [jax.ShapeDtypeStruct((1, w), F32) for w in sums]
    out_specs = [pl.BlockSpec((tr, w), lambda i: (i, 0)) for (w, _) in outs]
    out_specs += [pl.BlockSpec((1, w), lambda i: (0, 0)) for w in sums]
    return pl.pallas_call(
        body, name=name, grid=(T // tr,), out_shape=out_shape,
        in_specs=in_specs, out_specs=out_specs,
        compiler_params=_cp(("arbitrary",)),
    )(*[t[0] for t in tiles], *rows)


def _ln_stats(z):
    mu = jnp.mean(z, axis=-1, keepdims=True)
    zc = z - mu
    var = jnp.mean(zc * zc, axis=-1, keepdims=True)
    rstd = lax.rsqrt(var + LN_EPS)
    return zc * rstd, rstd


def _ln_bwd_core(dy, zh, rstd, g):
    dzh = dy * g
    m1 = jnp.mean(dzh, axis=-1, keepdims=True)
    m2 = jnp.mean(dzh * zh, axis=-1, keepdims=True)
    return rstd * (dzh - m1 - zh * m2)


def _colsum(v):
    return jnp.sum(v, axis=0, keepdims=True)


def _z_and_ln(z, g, b):
    zh, _ = _ln_stats(z)
    return z, zh * g + b


def _ln_bwd_tile(dy, z, g):
    zh, rstd = _ln_stats(z)
    dz = _ln_bwd_core(dy, zh, rstd, g)
    return dz, _colsum(dy * zh), _colsum(dy), _colsum(dz)


def _ln_fwd(z, g, b, name):
    D = z.shape[1]

    def fn(zt, gr, br):
        zh, _ = _ln_stats(zt)
        return (zh * gr + br,)

    return _rw(fn, [(z, D, 0)], [g, b], [(D, F32)], [], name=name)[0]


def _ln_bwd(dy, z, g, name):
    D = z.shape[1]

    def fn(dyt, zt, gr):
        zh, rstd = _ln_stats(zt)
        dz = _ln_bwd_core(dyt, zh, rstd, gr)
        return dz, _colsum(dyt * zh), _colsum(dyt), _colsum(dz)

    return _rw(fn, [(dy, D, 0), (z, D, 0)], [g], [(D, F32)], [D, D, D], name=name)


def _heads(D):
    return D // HEAD


def _rms_parts(o, D):
    xs, rs = [], []
    for h in range(_heads(D)):
        oh = o[:, h * HEAD:(h + 1) * HEAD]
        r = lax.rsqrt(jnp.mean(oh * oh, axis=-1, keepdims=True) + RMS_EPS)
        xs.append(oh * r)
        rs.append(r)
    return xs, rs


def _rms_gate_fwd(o, proj, ng, name):
    D = o.shape[1]

    def fn(ot, grt, ngr):
        xs, _ = _rms_parts(ot, D)
        xh = jnp.concatenate(xs, axis=1)
        return (xh * ngr * (grt * _sig(grt)),)

    return _rw(fn, [(o, D, 0), (proj, D, 3)], [ng], [(D, BF)], [], name=name)[0]


def _rms_gate_bwd(dy, o, proj, ng, name):
    D = o.shape[1]

    def fn(dyt, ot, grt, ngr):
        xs, rs = _rms_parts(ot, D)
        xh = jnp.concatenate(xs, axis=1)
        sg = _sig(grt)
        on = xh * ngr
        dgr = dyt * on * (sg * (1.0 + grt * (1.0 - sg)))
        don = dyt * (grt * sg)
        dxh = don * ngr
        dos = []
        for h in range(_heads(D)):
            sl = slice(h * HEAD, (h + 1) * HEAD)
            m = jnp.mean(dxh[:, sl] * xs[h], axis=-1, keepdims=True)
            dos.append(rs[h] * (dxh[:, sl] - xs[h] * m))
        return jnp.concatenate(dos, axis=1), dgr, _colsum(don * xh)

    return _rw(fn, [(dy, D, 0), (o, D, 0), (proj, D, 3)], [ng], [(D, F32), (D, BF)], [D], name=name)


def _glu_fwd(u, name):
    D = u.shape[1] // 2

    def fn(at, gt):
        return (at * _sig(gt),)

    return _rw(fn, [(u, D, 0), (u, D, 1)], [], [(D, F32)], [], name=name)[0]


def _glu_bwd(dglu, u, name):
    D = u.shape[1] // 2

    def fn(dg, at, gt):
        sg = _sig(gt)
        du = jnp.concatenate([dg * sg, dg * at * sg * (1.0 - sg)], axis=1)
        return du, _colsum(du)

    return _rw(fn, [(dglu, D, 0), (u, D, 0), (u, D, 1)], [], [(2 * D, BF)], [2 * D], name=name)


def _silu_ln_bwd(ds, c, g, b, name):
    D = c.shape[1]

    def fn(dst, ct, gr, br):
        zh, rstd = _ln_stats(ct)
        ln = zh * gr + br
        sg = _sig(ln)
        dln = dst * (sg * (1.0 + ln * (1.0 - sg)))
        dc = _ln_bwd_core(dln, zh, rstd, gr)
        return dc, _colsum(dln * zh), _colsum(dln), _colsum(dc)

    return _rw(fn, [(ds, D, 0), (c, D, 0)], [g, b], [(D, F32)], [D, D, D], name=name)


def _loss_k(y, tgt, name):
    D = y.shape[1]

    def fn(yt, tt):
        e = yt - tt
        return e * (1.0 / D), _colsum(e * e) * (0.5 / D)

    return _rw(fn, [(y, D, 0), (tgt, D, 0)], [], [(D, F32)], [D], name=name)


def _shifted(ext, n, tr):
    for b in range(8):
        rb = ext if b == 0 else pltpu.roll(ext, n - b, 0)
        for a in range(HALO // 8 + 1):
            o = 8 * a + b
            if o <= HALO:
                yield o, rb[8 * a:8 * a + tr, :]


def _lane_strips(D):
    return [slice(c, c + 128) for c in range(0, D, 128)]


def _conv_fwd(glu, w, bdw, g, b, name, tr=256, comm=None):
    T, D = glu.shape
    tr = min(tr, T // 2)
    rt = min(CONV_ROWS, tr)
    hb = tr // HALO

    def body(cur_ref, halo_ref, w_ref, bdw_ref, g_ref, b_ref, c_ref, s_ref):
        i = pl.program_id(0)
        for cs in _lane_strips(D):
            halo = jnp.where(i > 0, halo_ref[:, cs], 0.0)
            for r0 in range(0, tr, rt):
                if r0 == 0:
                    ext = jnp.concatenate([halo, cur_ref[0:rt, cs]], axis=0)
                else:
                    ext = cur_ref[r0 - HALO:r0 + rt, cs]
                acc = None
                for o, sh in _shifted(ext, rt + HALO, rt):
                    k = o - (HALO - CONV_W + 1)
                    if 0 <= k < CONV_W:
                        term = w_ref[k:k + 1, cs] * sh
                        acc = term if acc is None else acc + term
                c_ref[r0:r0 + rt, cs] = acc + bdw_ref[:, cs]
        cv = c_ref[...]
        zh, _ = _ln_stats(cv)
        ln = zh * g_ref[...] + b_ref[...]
        s_ref[...] = (ln * _sig(ln)).astype(BF)

    row = pl.BlockSpec((1, D), lambda i: (0, 0))
    return _call(
        body, name=name, grid=(T // tr,), operands=[glu, glu, w, bdw, g, b],
        out_shape=[jax.ShapeDtypeStruct((T, D), F32), jax.ShapeDtypeStruct((T, D), BF)],
        in_specs=[pl.BlockSpec((tr, D), lambda i: (i, 0)),
                  pl.BlockSpec((HALO, D), lambda i: (jnp.maximum(i * hb - 1, 0), 0)),
                  pl.BlockSpec((HALO, D), lambda i: (0, 0)), row, row, row],
        out_specs=[pl.BlockSpec((tr, D), lambda i: (i, 0))] * 2,
        sem=("parallel",), comm=comm)


def _conv_bwd(dc, glu, w, name, tr=256, comm=None):
    T, D = glu.shape
    tr = min(tr, T // 2)
    rt = min(CONV_ROWS, tr)
    hb = tr // HALO
    nsteps = T // tr
    last_hb = T // HALO - 1

    def body(dc_ref, nxt_ref, glu_ref, prv_ref, w_ref, dg_ref, dw_ref):
        i = pl.program_id(0)

        @pl.when(i == 0)
        def _():
            dw_ref[...] = jnp.zeros((HALO, D), F32)

        for cs in _lane_strips(D):
            nxt = jnp.where(i < nsteps - 1, nxt_ref[:, cs], 0.0)
            prv = jnp.where(i > 0, prv_ref[:, cs], 0.0)
            dws = [None] * CONV_W
            for r0 in range(0, tr, rt):
                if r0 + rt == tr:
                    ext2 = jnp.concatenate([dc_ref[r0:tr, cs], nxt], axis=0)
                else:
                    ext2 = dc_ref[r0:r0 + rt + HALO, cs]
                acc = None
                for o, sh in _shifted(ext2, rt + HALO, rt):
                    k = CONV_W - 1 - o
                    if 0 <= k < CONV_W:
                        term = w_ref[k:k + 1, cs] * sh
                        acc = term if acc is None else acc + term
                dg_ref[r0:r0 + rt, cs] = acc
                if r0 == 0:
                    ext = jnp.concatenate([prv, glu_ref[0:rt, cs]], axis=0)
                else:
                    ext = glu_ref[r0 - HALO:r0 + rt, cs]
                dcs = dc_ref[r0:r0 + rt, cs]
                for o, sh in _shifted(ext, rt + HALO, rt):
                    k = o - (HALO - CONV_W + 1)
                    if 0 <= k < CONV_W:
                        part = _colsum(dcs * sh)
                        dws[k] = part if dws[k] is None else dws[k] + part
            for k in range(CONV_W):
                dw_ref[k:k + 1, cs] += dws[k]

    return _call(
        body, name=name, grid=(nsteps,), operands=[dc, dc, glu, glu, w],
        out_shape=[jax.ShapeDtypeStruct((T, D), F32), jax.ShapeDtypeStruct((HALO, D), F32)],
        in_specs=[pl.BlockSpec((tr, D), lambda i: (i, 0)),
                  pl.BlockSpec((HALO, D), lambda i: (jnp.minimum((i + 1) * hb, last_hb), 0)),
                  pl.BlockSpec((tr, D), lambda i: (i, 0)),
                  pl.BlockSpec((HALO, D), lambda i: (jnp.maximum(i * hb - 1, 0), 0)),
                  pl.BlockSpec((HALO, D), lambda i: (0, 0))],
        out_specs=[pl.BlockSpec((tr, D), lambda i: (i, 0)),
                   pl.BlockSpec((HALO, D), lambda i: (0, 0))],
        sem=("arbitrary",), comm=comm)


def _rec_consts(C):
    t = np.arange(C)
    lb = (t[None, :] <= t[:, None]).astype(np.float32)

    def cum_at(idx):
        return (t[None, :] <= idx[:, None]).astype(np.float32)

    blocks, masks = [], []
    mid = SUB * (t // SUB) + SUB // 2 - 1
    eq = lb - cum_at(mid)
    blocks += [eq, -eq]
    masks.append(((t[:, None] // SUB) == (t[None, :] // SUB)) & (t[None, :] <= t[:, None]))
    nb = SUB
    while nb < C:
        odd = (t // nb) % 2 == 1
        e_t = nb * (t // nb) - 1
        e_s = nb * (t // nb) + nb - 1
        blocks.append(np.where(odd[:, None], lb - cum_at(e_t), 0.0))
        blocks.append(np.where(~odd[:, None], cum_at(e_s) - lb, 0.0))
        masks.append(((t[:, None] // (2 * nb)) == (t[None, :] // (2 * nb))) & odd[:, None] & ~odd[None, :])
        nb *= 2
    blocks += [lb, 1.0 - lb, np.ones((HEAD, C), np.float32)]
    L = np.concatenate(blocks, axis=0).astype(np.float32)
    L3 = np.concatenate([L, L, L], axis=1)
    LT3 = np.concatenate([L.T, L.T, L.T], axis=1)
    m = np.stack(masks).astype(np.float32)
    mT = np.transpose(m, (0, 2, 1)).copy()
    return (jnp.asarray(L3, BF), jnp.asarray(LT3, BF), jnp.asarray(m), jnp.asarray(mT), len(masks))


def _split3(x):
    h = x.astype(BF)
    r = x - h.astype(F32)
    m = r.astype(BF)
    lo = (r - m.astype(F32)).astype(BF)
    return h, m, lo


def _gates(qr, fz, lbr):
    sq = _sig(qr)
    q = qr * sq
    sg = _sig(fz)
    f = lbr + (1.0 - lbr) * sg
    fc = jnp.maximum(f, GATE_EPS)
    return q, 1.0 - f, jnp.log(fc), sq, sg, f, fc


def _rec_fwd(proj, lbr, name, comm=None):
    T, D4 = proj.shape
    D = D4 // 4
    H = _heads(D)
    C = CHUNK
    nC = T // C
    L3, _, m, _, nl = _rec_consts(C)
    R = L3.shape[0]

    def body(q_ref, f_ref, v_ref, lb_ref, l_ref, m_ref, o_ref, s_ref, st):
        @pl.when(pl.program_id(0) == 0)
        def _():
            st[...] = jnp.zeros((H, HEAD, HEAD), F32)

        q, k, logf = _gates(q_ref[...], f_ref[...], lb_ref[...])[:3]
        ex = jnp.exp(_dot(l_ref[...], jnp.concatenate(_split3(logf), axis=0)))
        vb = v_ref[...].astype(BF)
        s_ref[0] = st[...]
        outs = []
        for h in range(H):
            sl = slice(h * HEAD, (h + 1) * HEAD)
            qh, kh = q[:, sl], k[:, sl]
            p = jnp.zeros((C, C), F32)
            for lv in range(nl):
                qt = (qh * ex[2 * lv * C:(2 * lv + 1) * C, sl]).astype(BF)
                kt = (kh * ex[(2 * lv + 1) * C:(2 * lv + 2) * C, sl]).astype(BF)
                p = p + jnp.where(m_ref[lv] > 0.0, _dot_nt(qt, kt), 0.0)
            base = 2 * nl * C
            qhat = (qh * ex[base:base + C, sl]).astype(BF)
            khat = (kh * ex[base + C:base + 2 * C, sl]).astype(BF)
            elast = ex[base + 2 * C:base + 2 * C + HEAD, sl]
            sth = st[h]
            outs.append(_dot(p.astype(BF), vb[:, sl]) + _dot_nt(qhat, sth.astype(BF)))
            st[h] = elast * sth + _dot_tn(vb[:, sl], khat)
        o_ref[...] = jnp.concatenate(outs, axis=1)

    def cblk(cb):
        return pl.BlockSpec((C, D), lambda i: (i, cb))

    return _call(
        body, name=name, grid=(nC,), operands=[proj, proj, proj, lbr, L3, m],
        out_shape=[jax.ShapeDtypeStruct((T, D), F32), jax.ShapeDtypeStruct((nC, H, HEAD, HEAD), F32)],
        in_specs=[cblk(0), cblk(1), cblk(2), pl.BlockSpec((1, D), lambda i: (0, 0)),
                  pl.BlockSpec(L3.shape, lambda i: (0, 0)), pl.BlockSpec(m.shape, lambda i: (0, 0, 0))],
        out_specs=[pl.BlockSpec((C, D), lambda i: (i, 0)),
                   pl.BlockSpec((1, H, HEAD, HEAD), lambda i: (i, 0, 0, 0))],
        scratch=[pltpu.VMEM((H, HEAD, HEAD), F32)], sem=("arbitrary",), comm=comm)


def _rec_bwd(proj, lbr, do, dgr, ssave, name, comm=None):
    T, D4 = proj.shape
    D = D4 // 4
    H = _heads(D)
    C = min(CHUNK_BWD, T // 2)
    nC = T // C
    stride = C // CHUNK
    L3, LT3, m, mT, nl = _rec_consts(C)

    def body(q_ref, f_ref, v_ref, lb_ref, do_ref, dgr_ref, s_ref, l_ref, lt_ref, m_ref, mt_ref,
             dp_ref, dlb_ref, dst):
        @pl.when(pl.program_id(0) == 0)
        def _():
            dst[...] = jnp.zeros((H, HEAD, HEAD), F32)
            dlb_ref[...] = jnp.zeros((1, D), F32)

        qr = q_ref[...]
        lbv = lb_ref[...]
        q, k, logf, sq, sg, f, fc = _gates(qr, f_ref[...], lbv)
        ex = jnp.exp(_dot(l_ref[...], jnp.concatenate(_split3(logf), axis=0)))
        vb = v_ref[...].astype(BF)
        dob = do_ref[...].astype(BF)
        base = 2 * nl * C
        de = [[] for _ in range(2 * nl + 3)]
        dqs, dks, dvs = [], [], []
        for h in range(H):
            sl = slice(h * HEAD, (h + 1) * HEAD)
            qh, kh, vh, doh = q[:, sl], k[:, sl], vb[:, sl], dob[:, sl]
            dp = _dot_nt(doh, vh)
            dpt = _dot_nt(vh, doh)
            sth = s_ref[0, h]
            dsth = dst[h]
            dsb = dsth.astype(BF)
            pt = jnp.zeros((C, C), F32)
            dq = jnp.zeros((C, HEAD), F32)
            dk = jnp.zeros((C, HEAD), F32)
            for lv in range(nl):
                exq = ex[2 * lv * C:(2 * lv + 1) * C, sl]
                exk = ex[(2 * lv + 1) * C:(2 * lv + 2) * C, sl]
                qt = qh * exq
                kt = kh * exk
                qtb, ktb = qt.astype(BF), kt.astype(BF)
                pt = pt + jnp.where(mt_ref[lv] > 0.0, _dot_nt(ktb, qtb), 0.0)
                dqt = _dot(jnp.where(m_ref[lv] > 0.0, dp, 0.0).astype(BF), ktb)
                dkt = _dot(jnp.where(mt_ref[lv] > 0.0, dpt, 0.0).astype(BF), qtb)
                dq = dq + dqt * exq
                dk = dk + dkt * exk
                de[2 * lv].append(dqt * qt)
                de[2 * lv + 1].append(dkt * kt)
            exb = ex[base:base + C, sl]
            exkh = ex[base + C:base + 2 * C, sl]
            elast = ex[base + 2 * C:base + 2 * C + HEAD, sl]
            qhat = qh * exb
            khat = kh * exkh
            dqh = _dot(doh, sth.astype(BF))
            dkh = _dot(vh, dsb)
            dq = dq + dqh * exb
            dk = dk + dkh * exkh
            de[2 * nl].append(dqh * qhat)
            de[2 * nl + 1].append(dkh * khat)
            de[2 * nl + 2].append(dsth * sth * elast)
            dvs.append(_dot(pt.astype(BF), doh) + _dot_nt(khat.astype(BF), dsb))
            dst[h] = elast * dsth + _dot_tn(doh, qhat.astype(BF))
            dqs.append(dq)
            dks.append(dk)
        de_all = jnp.concatenate([jnp.concatenate(b, axis=1) for b in de], axis=0)
        dlogf = _dot(lt_ref[...], jnp.concatenate(_split3(de_all), axis=0))
        dq = jnp.concatenate(dqs, axis=1)
        dk = jnp.concatenate(dks, axis=1)
        dv = jnp.concatenate(dvs, axis=1)
        ind = jnp.where(f > GATE_EPS, 1.0, jnp.where(f == GATE_EPS, 0.5, 0.0))
        df = dlogf * ind / fc - dk
        dfz = df * (1.0 - lbv) * sg * (1.0 - sg)
        dlb_ref[...] += _colsum(df * (1.0 - sg))
        dqr = dq * (sq * (1.0 + qr * (1.0 - sq)))
        dp_ref[...] = jnp.concatenate([dqr.astype(BF), dfz.astype(BF), dv.astype(BF), dgr_ref[...]], axis=1)

    def cblk(cb):
        return pl.BlockSpec((C, D), lambda i: (nC - 1 - i, cb))

    def whole(a):
        nd = a.ndim
        return pl.BlockSpec(a.shape, lambda i: (0,) * nd)

    return _call(
        body, name=name, grid=(nC,), operands=[proj, proj, proj, lbr, do, dgr, ssave, L3, LT3, m, mT],
        out_shape=[jax.ShapeDtypeStruct((T, D4), BF), jax.ShapeDtypeStruct((1, D), F32)],
        in_specs=[cblk(0), cblk(1), cblk(2), pl.BlockSpec((1, D), lambda i: (0, 0)),
                  pl.BlockSpec((C, D), lambda i: (nC - 1 - i, 0)),
                  pl.BlockSpec((C, D), lambda i: (nC - 1 - i, 0)),
                  pl.BlockSpec((1, H, HEAD, HEAD), lambda i: (stride * (nC - 1 - i), 0, 0, 0)),
                  whole(L3), whole(LT3), whole(m), whole(mT)],
        out_specs=[pl.BlockSpec((C, D4), lambda i: (nC - 1 - i, 0)),
                   pl.BlockSpec((1, D), lambda i: (0, 0))],
        scratch=[pltpu.VMEM((H, HEAD, HEAD), F32)], sem=("arbitrary",), comm=comm)


def _softmax_rows(lg_ref):
    n = lg_ref.shape[0]
    rows = [lg_ref[l:l + 1, :] for l in range(n)]
    mx = rows[0]
    for r in rows[1:]:
        mx = jnp.maximum(mx, r)
    es = [jnp.exp(r - mx) for r in rows]
    tot = es[0]
    for e in es[1:]:
        tot = tot + e
    return [e / tot for e in es]


def _lb_fwd(logits):
    n, D = logits.shape

    def body(lg_ref, o_ref):
        soft = _softmax_rows(lg_ref)
        acc = jnp.zeros((1, D), F32)
        o_ref[0:1, :] = acc
        for j in range(1, n):
            acc = acc + soft[j]
            o_ref[j:j + 1, :] = acc

    return pl.pallas_call(body, name="lb_fwd", out_shape=jax.ShapeDtypeStruct((n, D), F32))(logits)


def _small_reduce(parts, logits, lb_row0):
    _, R, D = parts.shape
    n = logits.shape[0]

    def body(p_ref, lg_ref, o_ref):
        acc = p_ref[0]
        for d in range(1, N_DEV):
            acc = acc + p_ref[d]
        o_ref[...] = acc
        soft = _softmax_rows(lg_ref)
        dlb = [o_ref[lb_row0 + j:lb_row0 + j + 1, :] for j in range(n)]
        dsoft = [jnp.zeros((1, D), F32)]
        for l in range(1, n):
            s = dlb[l]
            for j in range(l + 1, n):
                s = s + dlb[j]
            dsoft.append(s)
        dot = soft[0] * dsoft[0]
        for l in range(1, n):
            dot = dot + soft[l] * dsoft[l]
        for l in range(n):
            o_ref[lb_row0 + l:lb_row0 + l + 1, :] = soft[l] * (dsoft[l] - dot)

    return pl.pallas_call(body, name="small_reduce", out_shape=jax.ShapeDtypeStruct((R, D), F32))(parts, logits)


def _adam_math(g, w, m, v):
    m2 = ADAM_B1 * m + (1.0 - ADAM_B1) * g
    v2 = ADAM_B2 * v + (1.0 - ADAM_B2) * (g * g)
    mh = m2 / (1.0 - ADAM_B1 ** ADAM_STEP)
    vh = v2 / (1.0 - ADAM_B2 ** ADAM_STEP)
    delta = -ADAM_LR * (mh / (jnp.sqrt(vh) + ADAM_EPS) + ADAM_WD * w)
    return delta, m2, v2


def _adamw(recv, w, m, v, name):
    nsrc = recv.shape[0]
    shp = w.shape
    cols = shp[-1]
    rows = int(np.prod(shp[:-1]))
    r2 = recv.reshape(nsrc, rows, cols)
    tr = min(rows, max(8, (1 << 20) // (cols * nsrc)))
    while rows % tr:
        tr //= 2

    def body(r_ref, w_ref, m_ref, v_ref, g_ref, d_ref, nm_ref, nv_ref):
        g = r_ref[0].astype(F32)
        for s in range(1, nsrc):
            g = g + r_ref[s].astype(F32)
        delta, m2, v2 = _adam_math(g, w_ref[...], m_ref[...], v_ref[...])
        g_ref[...] = g
        d_ref[...] = delta
        nm_ref[...] = m2
        nv_ref[...] = v2

    blk = pl.BlockSpec((tr, cols), lambda i: (i, 0))
    outs = pl.pallas_call(
        body, name=name, grid=(rows // tr,),
        out_shape=[jax.ShapeDtypeStruct((rows, cols), F32)] * 4,
        in_specs=[pl.BlockSpec((nsrc, tr, cols), lambda i: (0, i, 0)), blk, blk, blk],
        out_specs=[blk] * 4,
        compiler_params=_cp(("parallel",)),
    )(r2, w.reshape(rows, cols), m.reshape(rows, cols), v.reshape(rows, cols))
    return [o.reshape(shp) for o in outs]


def _pack(arrs, lane):
    flat = jnp.concatenate([a.reshape(-1) for a in arrs])
    n = flat.shape[0]
    rows = -(-n // lane)
    rows = -(-rows // 8) * 8
    flat = jnp.pad(flat, (0, rows * lane - n))
    return flat.reshape(rows, lane)


def _unpack(packed, shapes):
    flat = packed.reshape(-1)
    out, off = [], 0
    for s in shapes:
        n = int(np.prod(s))
        out.append(flat[off:off + n].reshape(s))
        off += n
    return out


def kernel(x, ln_mix_g, ln_mix_b, ln_ffn_g, ln_ffn_b, ffn_w1, ffn_w2, a_w_in, a_lb_logits, a_norm_g, a_w_out, b_w_pw1, b_b_pw1, b_w_dw, b_b_dw, b_ln_g, b_ln_b, b_w_pw2, b_b_pw2, loss_target, m_ln_mix_g, m_ln_mix_b, m_ln_ffn_g, m_ln_ffn_b, m_ffn_w1, m_ffn_w2, m_a_w_in, m_a_lb_logits, m_a_norm_g, m_a_w_out, m_b_w_pw1, m_b_b_pw1, m_b_w_dw, m_b_b_dw, m_b_ln_g, m_b_ln_b, m_b_w_pw2, m_b_b_pw2, v_ln_mix_g, v_ln_mix_b, v_ln_ffn_g, v_ln_ffn_b, v_ffn_w1, v_ffn_w2, v_a_w_in, v_a_lb_logits, v_a_norm_g, v_a_w_out, v_b_w_pw1, v_b_b_pw1, v_b_w_dw, v_b_b_dw, v_b_ln_g, v_b_ln_b, v_b_w_pw2, v_b_b_pw2):
    T, D = x.shape[1], x.shape[2]
    nA, nB = a_w_in.shape[0], b_w_pw1.shape[0]
    me = 4 * lax.axis_index("x") + 2 * lax.axis_index("y") + lax.axis_index("c")
    xin = x[0]
    tgt = loss_target[0]

    small_names = [b_b_pw1, b_w_dw, b_b_dw, b_ln_g, b_ln_b, b_b_pw2]
    sp = _pack(small_names, 128)
    wsrc = {"W1": ffn_w1.astype(BF), "W2": ffn_w2.astype(BF), "Win": a_w_in.astype(BF),
            "Wout": a_w_out.astype(BF), "Wp1": b_w_pw1.astype(BF), "Wp2": b_w_pw2.astype(BF), "small": sp[None]}
    W = {k: jax.ShapeDtypeStruct((N_DEV,) + v.shape, v.dtype) for k, v in wsrc.items()}
    GW = {k: jax.ShapeDtypeStruct((N_DEV,) + v.shape, BF) for k, v in wsrc.items() if k != "small"}
    R = {k: jax.ShapeDtypeStruct(v.shape, BF) for k, v in GW.items()}

    def names_of(items):
        out = []
        for n, _ in items:
            if n not in out:
                out.append(n)
        return out

    def hosted(fn, key, sched, mode, src, bufs, *args, **kw):
        items = sched.get(key)
        if items is None:
            return fn(*args, name=key, **kw)
        comm = (mode, [(src[n], l, bufs[n]) for n, l in items])
        outs, new = fn(*args, name=key, comm=comm, **kw)
        for n, b in zip(names_of(items), new):
            bufs[n] = b
        return outs

    first = [("Win", 0), ("small", 0)]
    for n, b in zip(names_of(first), _exchange(("gather", [(wsrc[n], l, W[n]) for n, l in first]), "gather_first")):
        W[n] = b
    fwd_sched = {"a0_proj": [("Wout", 0), ("Wp1", 0)], "a0_rec_fwd": [("W1", 0), ("W2", 0), ("Wp2", 0)],
                 "l0_ffn_up": [("W2", 1)], "l0_ffn_down": [("W1", 1)],
                 "b0_conv": [("Win", 1), ("Wout", 1)], "l1_ffn_up": [("W2", 2)], "l1_ffn_down": [("W1", 2)],
                 "a1_rec_fwd": [("Wp1", 1), ("Wp2", 1), ("W1", 3), ("W2", 3)]}
    bwd_sched = {"a1_rec_bwd": [("W1", 3), ("W2", 3), ("Wp1", 1), ("Wp2", 1)],
                 "b0_conv_bwd": [("W1", 2), ("W2", 2), ("Win", 1), ("Wout", 1)],
                 "a0_rec_bwd": [("W1", 1), ("W2", 1), ("Wp1", 0), ("Wp2", 0), ("W1", 0), ("W2", 0), ("Wout", 0)],
                 "a0_proj_dx": [("Win", 0)]}

    def fwd(fn, key, *args, **kw):
        return hosted(fn, key, fwd_sched, "gather", wsrc, W, *args, **kw)

    def bwd(fn, key, *args, **kw):
        return hosted(fn, key, bwd_sched, "scatter", GW, R, *args, **kw)

    spg = W["small"][:, 0]
    sm = [jnp.stack(p) for p in zip(*[_unpack(spg[d], [a.shape for a in small_names]) for d in range(N_DEV)])]
    bpw1 = jnp.transpose(sm[0], (1, 0, 2)).reshape(nB, 1, 2 * D)
    wdw = jnp.transpose(sm[1], (1, 2, 0, 3)).reshape(nB, CONV_W, D)
    wdw = jnp.pad(wdw, ((0, 0), (0, HALO - CONV_W), (0, 0)))
    bdw, blng, blnb, bpw2 = [jnp.transpose(s, (1, 0, 2)).reshape(nB, 1, D) for s in sm[2:]]
    lb_all = _lb_fwd(a_lb_logits)

    zx = [(D, F32), (D, F32)]
    ln_mix = [(ln_mix_g[i:i + 1], ln_mix_b[i:i + 1]) for i in range(DEPTH)]
    ln_ffn = [(ln_ffn_g[i:i + 1], ln_ffn_b[i:i + 1]) for i in range(DEPTH)]
    saved = []
    h = xin
    for i in range(DEPTH):
        j = i // 2
        sv = {"xin": h}
        if i % 2 == 0:
            proj = fwd(_mm, f"a{j}_proj", h, W["Win"], j, "nn_col", out_dtype=F32)
            o, ssave = fwd(_rec_fwd, f"a{j}_rec_fwd", proj, lb_all[j:j + 1])
            yg = _rms_gate_fwd(o, proj, a_norm_g[j:j + 1], f"a{j}_gate_fwd")
            z1, x1 = _mm(yg, W["Wout"], j, "nn_row", out_dtype=F32, name=f"a{j}_out", outs=zx,
                         epi=lambda acc, r, g, b: _z_and_ln(acc + ALPHA * r, g, b), extras=[h, *ln_mix[i]])
            sv.update(proj=proj, o=o, ssave=ssave, yg=yg)
        else:
            u = _mm(h, W["Wp1"], j, "nn_col", out_dtype=F32, name=f"b{j}_pw1",
                    epi=lambda acc, b: acc + b, extras=[bpw1[j]])
            glu = _glu_fwd(u, f"b{j}_glu")
            cv, s = fwd(_conv_fwd, f"b{j}_conv", glu, wdw[j], bdw[j], blng[j], blnb[j])
            z1, x1 = _mm(s, W["Wp2"], j, "nn_row", out_dtype=F32, name=f"b{j}_pw2", outs=zx,
                         epi=lambda acc, bb, r, g, b: _z_and_ln(acc + bb + ALPHA * r, g, b),
                         extras=[bpw2[j], h, *ln_mix[i]])
            sv.update(u=u, glu=glu, cv=cv, s=s)
        hh = fwd(_mm, f"l{i}_ffn_up", x1, W["W1"], i, "nn_col", out_dtype=BF)
        z2, h = fwd(_mm, f"l{i}_ffn_down", hh, W["W2"], i, "nn_row", out_dtype=F32, outs=zx,
                    a_fn=lambda t: jnp.square(jnp.maximum(t, 0)),
                    epi=lambda acc, r, g, b: _z_and_ln(acc + ALPHA * r, g, b), extras=[x1, *ln_ffn[i]])
        sv.update(z1=z1, x1=x1, hh=hh, z2=z2)
        saved.append(sv)

    dx, loss_row = _loss_k(h, tgt, "loss")

    rows = {}
    top = DEPTH - 1
    dz2, dg, db, _ = _ln_bwd(dx, saved[top]["z2"], ln_ffn[top][0], f"l{top}_ln_ffn_bwd")
    rows[("ffn_g", top)], rows[("ffn_b", top)] = dg, db
    ln_out = dict(outs=[(D, F32)], sums=[D, D, D])
    for i in reversed(range(DEPTH)):
        j = i // 2
        sv = saved[i]
        if i > 0:
            below = dict(epi=lambda acc, r, z, g: _ln_bwd_tile(acc + ALPHA * r, z, g), **ln_out)
            below_extras = [saved[i - 1]["z2"], ln_ffn[i - 1][0]]
        else:
            below = dict(epi=lambda acc, r: acc + ALPHA * r)
            below_extras = []
        dh = _mm(dz2, W["W2"], i, "nt_row", out_dtype=BF, name=f"l{i}_ffn_down_dx",
                 epi=lambda acc, hv: acc * (2.0 * jnp.maximum(hv.astype(F32), 0.0)), extras=[sv["hh"]])
        GW["W2"] = _mm_tn(GW["W2"], dz2, sv["hh"], i, "rows_t", name=f"l{i}_ffn_down_dw",
                         b_fn=lambda t: jnp.square(jnp.maximum(t, 0)))
        GW["W1"] = _mm_tn(GW["W1"], sv["x1"], dh, i, "cols", name=f"l{i}_ffn_up_dw")
        dz1, dg, db, dz1sum = _mm(dh, W["W1"], i, "nt_col", out_dtype=F32, name=f"l{i}_ffn_up_dx",
                                  epi=lambda acc, r, z, g: _ln_bwd_tile(acc + ALPHA * r, z, g),
                                  extras=[dz2, sv["z1"], ln_mix[i][0]], **ln_out)
        rows[("mix_g", i)], rows[("mix_b", i)] = dg, db
        if i % 2 == 0:
            GW["Wout"] = _mm_tn(GW["Wout"], sv["yg"], dz1, j, "rows", name=f"a{j}_out_dw")
            dyg = _mm(dz1, W["Wout"], j, "nt_row", out_dtype=F32, name=f"a{j}_out_dx")
            do, dgr, dng = _rms_gate_bwd(dyg, sv["o"], sv["proj"], a_norm_g[j:j + 1], f"a{j}_gate_bwd")
            rows[("ng", j)] = dng
            dproj, dlb = bwd(_rec_bwd, f"a{j}_rec_bwd", sv["proj"], lb_all[j:j + 1], do, dgr, sv["ssave"])
            rows[("lb", j)] = dlb
            GW["Win"] = _mm_tn(GW["Win"], sv["xin"], dproj, j, "cols", name=f"a{j}_proj_dw")
            res = bwd(_mm, f"a{j}_proj_dx", dproj, W["Win"], j, "nt_col", out_dtype=F32,
                      extras=[dz1, *below_extras], **below)
        else:
            rows[("bpw2", j)] = dz1sum
            GW["Wp2"] = _mm_tn(GW["Wp2"], sv["s"], dz1, j, "rows", name=f"b{j}_pw2_dw")
            ds = _mm(dz1, W["Wp2"], j, "nt_row", out_dtype=F32, name=f"b{j}_pw2_dx")
            dc, dlg, dlb_, dcs = _silu_ln_bwd(ds, sv["cv"], blng[j], blnb[j], f"b{j}_ln_bwd")
            rows[("blng", j)], rows[("blnb", j)], rows[("bdw", j)] = dlg, dlb_, dcs
            dglu, dwdw = bwd(_conv_bwd, f"b{j}_conv_bwd", dc, sv["glu"], wdw[j])
            rows[("wdw", j)] = dwdw[:CONV_W]
            du, dbu = _glu_bwd(dglu, sv["u"], f"b{j}_glu_bwd")
            rows[("bpw1", j)] = dbu.reshape(2, D)
            GW["Wp1"] = _mm_tn(GW["Wp1"], sv["xin"], du, j, "cols", name=f"b{j}_pw1_dw")
            res = _mm(du, W["Wp1"], j, "nt_col", out_dtype=F32, name=f"b{j}_pw1_dx",
                      extras=[dz1, *below_extras], **below)
        if i > 0:
            dz2, dg, db, _ = res
            rows[("ffn_g", i - 1)], rows[("ffn_b", i - 1)] = dg, db
        else:
            dx = res
    grad_x = dx[None]

    order = ([("mix_g", i) for i in range(DEPTH)] + [("mix_b", i) for i in range(DEPTH)]
             + [("ffn_g", i) for i in range(DEPTH)] + [("ffn_b", i) for i in range(DEPTH)])
    lb_row0 = len(order)
    order += [("lb", j) for j in range(nA)] + [("ng", j) for j in range(nA)]
    for j in range(nB):
        order += [("bpw1", j), ("wdw", j), ("bdw", j), ("blng", j), ("blnb", j), ("bpw2", j)]
    pieces, offs, off = [], {}, 0
    for key in order:
        offs[key] = off
        pieces.append(rows[key])
        off += rows[key].shape[0]
    offs["loss"] = off
    pieces.append(loss_row)
    off += 1
    part = jnp.concatenate(pieces, axis=0)
    part = jnp.pad(part, ((0, -off % 8), (0, 0)))
    parts = _exchange(("gather", [(part[None], 0, jax.ShapeDtypeStruct((N_DEV, 1) + part.shape, F32))]),
                      "gather_small_grads")[0][:, 0]
    G = _small_reduce(parts, a_lb_logits, lb_row0)
    loss = jnp.sum(G[offs["loss"]])

    def rep(kind, n):
        return jnp.concatenate([G[offs[(kind, i)]:offs[(kind, i)] + 1] for i in range(n)], axis=0)

    def shard_cols(full, width):
        return lax.dynamic_slice_in_dim(full, me * width, width, axis=full.ndim - 1)

    g_small = {
        "ln_mix_g": rep("mix_g", DEPTH), "ln_mix_b": rep("mix_b", DEPTH),
        "ln_ffn_g": rep("ffn_g", DEPTH), "ln_ffn_b": rep("ffn_b", DEPTH),
        "a_lb_logits": rep("lb", nA), "a_norm_g": rep("ng", nA),
        "b_b_pw1": shard_cols(jnp.stack([G[offs[("bpw1", j)]:offs[("bpw1", j)] + 2].reshape(2 * D)
                                         for j in range(nB)]), 2 * D // N_DEV),
        "b_w_dw": shard_cols(jnp.stack([G[offs[("wdw", j)]:offs[("wdw", j)] + CONV_W] for j in range(nB)]),
                             D // N_DEV),
        "b_b_dw": shard_cols(rep("bdw", nB), D // N_DEV),
        "b_ln_g": shard_cols(rep("blng", nB), D // N_DEV),
        "b_ln_b": shard_cols(rep("blnb", nB), D // N_DEV),
        "b_b_pw2": shard_cols(rep("bpw2", nB), D // N_DEV),
    }
    small_w = {"ln_mix_g": (ln_mix_g, m_ln_mix_g, v_ln_mix_g), "ln_mix_b": (ln_mix_b, m_ln_mix_b, v_ln_mix_b),
               "ln_ffn_g": (ln_ffn_g, m_ln_ffn_g, v_ln_ffn_g), "ln_ffn_b": (ln_ffn_b, m_ln_ffn_b, v_ln_ffn_b),
               "a_lb_logits": (a_lb_logits, m_a_lb_logits, v_a_lb_logits),
               "a_norm_g": (a_norm_g, m_a_norm_g, v_a_norm_g),
               "b_b_pw1": (b_b_pw1, m_b_b_pw1, v_b_b_pw1), "b_w_dw": (b_w_dw, m_b_w_dw, v_b_w_dw),
               "b_b_dw": (b_b_dw, m_b_b_dw, v_b_b_dw), "b_ln_g": (b_ln_g, m_b_ln_g, v_b_ln_g),
               "b_ln_b": (b_ln_b, m_b_ln_b, v_b_ln_b), "b_b_pw2": (b_b_pw2, m_b_b_pw2, v_b_b_pw2)}
    snames = list(small_w)
    sshapes = [small_w[k][0].shape for k in snames]
    pg = _pack([g_small[k] for k in snames], 1024)
    pw, pm, pv = [_pack([small_w[k][q] for k in snames], 1024) for q in range(3)]
    sres = _adamw(pg[None], pw, pm, pv, "adamw_small")
    sres = [dict(zip(snames, _unpack(r, sshapes))) for r in sres]

    recv = [R[n] for n in ["W1", "W2", "Win", "Wout", "Wp1", "Wp2"]]
    big = {}
    for nm, rv, (w, m, v) in zip(
            ["ffn_w1", "ffn_w2", "a_w_in", "a_w_out", "b_w_pw1", "b_w_pw2"], recv,
            [(ffn_w1, m_ffn_w1, v_ffn_w1), (ffn_w2, m_ffn_w2, v_ffn_w2), (a_w_in, m_a_w_in, v_a_w_in),
             (a_w_out, m_a_w_out, v_a_w_out), (b_w_pw1, m_b_w_pw1, v_b_w_pw1), (b_w_pw2, m_b_w_pw2, v_b_w_pw2)]):
        big[nm] = _adamw(rv, w, m, v, f"adamw_{nm}")

    names = ["ln_mix_g", "ln_mix_b", "ln_ffn_g", "ln_ffn_b", "ffn_w1", "ffn_w2", "a_w_in", "a_lb_logits",
             "a_norm_g", "a_w_out", "b_w_pw1", "b_b_pw1", "b_w_dw", "b_b_dw", "b_ln_g", "b_ln_b", "b_w_pw2",
             "b_b_pw2"]
    out = [loss, grad_x]
    for q in range(4):
        for nm in names:
            out.append(big[nm][q] if nm in big else sres[q][nm])
    return tuple(out)
```

```python
import functools

import numpy as np
import jax
import jax.numpy as jnp
from jax import lax
from jax.experimental import pallas as pl
from jax.experimental.pallas import tpu as pltpu

F32 = jnp.float32
BF = jnp.bfloat16

N_DEV = 8
DEPTH = 4
HEAD = 128
CHUNK = 64
CHUNK_BWD = 64
SUB = 8
CONV_W = 31
HALO = 32
CONV_ROWS = 128
ALPHA = (2.0 * DEPTH) ** 0.25
LN_EPS = 1e-5
RMS_EPS = 1e-6
GATE_EPS = 1e-6
ADAM_LR = 0.001
ADAM_B1 = 0.9
ADAM_B2 = 0.999
ADAM_EPS = 1e-08
ADAM_WD = 0.01
ADAM_STEP = 10
VMEM_LIMIT = 56 * 1024 * 1024
MESH = pl.DeviceIdType.MESH


def _cp(sem=None):
    return pltpu.CompilerParams(vmem_limit_bytes=VMEM_LIMIT, dimension_semantics=sem)


def _dot(a, b):
    return jnp.dot(a, b, preferred_element_type=F32)


def _dot_nt(a, b):
    return lax.dot_general(a, b, (((1,), (1,)), ((), ())), preferred_element_type=F32)


def _dot_tn(a, b):
    return lax.dot_general(a, b, (((0,), (0,)), ((), ())), preferred_element_type=F32)


def _sig(x):
    return jax.nn.sigmoid(x)


def _distinct(items):
    out = []
    for it in items:
        if not any(it is q for q in out):
            out.append(it)
    return out


def _index_of(items, it):
    return next(i for i, q in enumerate(items) if q is it)


def _comm_copies(mode, plan, src_refs, buf_refs, send_sems, recv_sems, loc_sems):
    x, y, c = lax.axis_index("x"), lax.axis_index("y"), lax.axis_index("c")
    me = 4 * x + 2 * y + c
    locs, rems = [], []
    for k, (si, l, bi) in enumerate(plan):
        src = src_refs[si].at[l] if mode == "gather" else src_refs[si].at[me, l]
        locs.append(pltpu.make_async_copy(src, buf_refs[bi].at[me, l], loc_sems.at[k]))
    for r in range(1, N_DEV):
        px = (1 - x) if (r >> 2) & 1 else x
        py = (1 - y) if (r >> 1) & 1 else y
        pc = (1 - c) if r & 1 else c
        pid = 4 * px + 2 * py + pc
        for k, (si, l, bi) in enumerate(plan):
            src = src_refs[si].at[l] if mode == "gather" else src_refs[si].at[pid, l]
            rems.append(pltpu.make_async_remote_copy(
                src_ref=src, dst_ref=buf_refs[bi].at[me, l],
                send_sem=send_sems.at[k, r - 1], recv_sem=recv_sems.at[k, r - 1],
                device_id=(px, py, pc), device_id_type=MESH))
    return locs, rems


def _call(body, *, name, grid, operands, in_specs, out_shape, out_specs, scratch=(), sem=None,
          aliases=None, comm=None):
    aliases = dict(aliases or {})
    if comm is None:
        return pl.pallas_call(
            body, name=name, grid=grid, out_shape=list(out_shape), in_specs=list(in_specs),
            out_specs=list(out_specs), scratch_shapes=list(scratch), input_output_aliases=aliases,
            compiler_params=_cp(sem),
        )(*operands)
    mode, pieces = comm
    srcs = _distinct([p[0] for p in pieces])
    bufs = _distinct([p[2] for p in pieces])
    plan = [(_index_of(srcs, s), l, _index_of(bufs, b)) for (s, l, b) in pieces]
    n_in, n_out, n_scr, ns, nb, npc = len(operands), len(out_shape), len(scratch), len(srcs), len(bufs), len(plan)
    nsteps = grid[0]
    old = [b for b in bufs if not isinstance(b, jax.ShapeDtypeStruct)]
    nbi = len(old)

    def wrapped(*refs):
        ins = refs[:n_in]
        src_refs = refs[n_in:n_in + ns]
        o0 = n_in + ns + nbi
        outs = refs[o0:o0 + n_out]
        buf_refs = refs[o0 + n_out:o0 + n_out + nb]
        s0 = o0 + n_out + nb
        scr = refs[s0:s0 + n_scr]
        sems = refs[s0 + n_scr:]
        step = pl.program_id(0)

        @pl.when(step == 0)
        def _():
            locs, rems = _comm_copies(mode, plan, src_refs, buf_refs, *sems)
            for d in locs + rems:
                d.start()

        body(*ins, *outs, *scr)

        @pl.when(step == nsteps - 1)
        def _():
            locs, rems = _comm_copies(mode, plan, src_refs, buf_refs, *sems)
            for d in rems:
                d.wait_send()
            for d in rems:
                d.wait_recv()
            for d in locs:
                d.wait()

    anyspec = pl.BlockSpec(memory_space=pl.ANY)
    for k, b in enumerate(old):
        aliases[n_in + ns + k] = n_out + _index_of(bufs, b)
    res = pl.pallas_call(
        wrapped, name=name, grid=grid,
        out_shape=list(out_shape) + [jax.ShapeDtypeStruct(b.shape, b.dtype) for b in bufs],
        in_specs=list(in_specs) + [anyspec] * (ns + nbi),
        out_specs=list(out_specs) + [anyspec] * nb,
        scratch_shapes=list(scratch) + [pltpu.SemaphoreType.DMA((npc, N_DEV - 1)),
                                        pltpu.SemaphoreType.DMA((npc, N_DEV - 1)),
                                        pltpu.SemaphoreType.DMA((npc,))],
        input_output_aliases=aliases,
        compiler_params=pltpu.CompilerParams(vmem_limit_bytes=VMEM_LIMIT, has_side_effects=True,
                                             dimension_semantics=("arbitrary",) * len(grid)),
    )(*operands, *srcs, *old)
    return list(res[:n_out]), list(res[n_out:])


def _exchange(comm, name):
    def body():
        pass

    return _call(body, name=name, grid=(1,), operands=[], in_specs=[], out_shape=[], out_specs=[],
                 comm=comm)[1]


def _mm(a, w, l, kind, *, out_dtype, name, tm=512, a_fn=None, epi=None, extras=(), comm=None,
        outs=None, sums=()):
    T, Ka = a.shape
    _, _, d2, d3 = w.shape
    n_out = {"nn_col": N_DEV * d3, "nn_row": d3, "nt_col": d2, "nt_row": N_DEV * d2}[kind]
    tm = min(tm, T)
    ne = len(extras)
    single = outs is None and not sums
    outs = [(n_out, out_dtype)] if outs is None else outs
    no, ns = len(outs), len(sums)

    def body(*refs):
        a_ref, w_ref = refs[0], refs[1]
        e_refs = refs[2:2 + ne]
        o_refs = refs[2 + ne:2 + ne + no]
        s_refs = refs[2 + ne + no:]
        av = a_ref[...]
        if a_fn is not None:
            av = a_fn(av)
        av = av.astype(BF)
        if kind == "nn_col":
            acc = jnp.concatenate([_dot(av, w_ref[j]) for j in range(N_DEV)], axis=1)
        elif kind == "nn_row":
            acc = _dot(av, w_ref[...].reshape(N_DEV * d2, d3))
        elif kind == "nt_col":
            acc = _dot_nt(av[:, 0:d3], w_ref[0])
            for j in range(1, N_DEV):
                acc = acc + _dot_nt(av[:, j * d3:(j + 1) * d3], w_ref[j])
        else:
            acc = jnp.concatenate([_dot_nt(av, w_ref[j]) for j in range(N_DEV)], axis=1)
        res = acc if epi is None else epi(acc, *[e[...] for e in e_refs])
        res = res if isinstance(res, tuple) else (res,)
        for o_ref, val in zip(o_refs, res[:no]):
            o_ref[...] = val.astype(o_ref.dtype)
        if ns:
            first = pl.program_id(0) == 0

            @pl.when(first)
            def _():
                for s_ref, val in zip(s_refs, res[no:]):
                    s_ref[...] = val

            @pl.when(jnp.logical_not(first))
            def _():
                for s_ref, val in zip(s_refs, res[no:]):
                    s_ref[...] += val

    in_specs = [pl.BlockSpec((tm, Ka), lambda i: (i, 0)),
                pl.BlockSpec((N_DEV, None, d2, d3), lambda i: (0, l, 0, 0))]
    extras = [e if isinstance(e, tuple) else (e, 0) for e in extras]
    for e, cb in extras:
        if e.shape[0] == 1:
            in_specs.append(pl.BlockSpec((1, n_out), lambda i: (0, 0)))
        else:
            in_specs.append(pl.BlockSpec((tm, n_out), functools.partial(lambda i, cb: (i, cb), cb=cb)))
    extras = [e for e, _ in extras]
    out_shape = [jax.ShapeDtypeStruct((T, wd), dt) for wd, dt in outs]
    out_shape += [jax.ShapeDtypeStruct((1, wd), F32) for wd in sums]
    out_specs = [pl.BlockSpec((tm, wd), lambda i: (i, 0)) for wd, _ in outs]
    out_specs += [pl.BlockSpec((1, wd), lambda i: (0, 0)) for wd in sums]
    res = _call(body, name=name, grid=(T // tm,), operands=[a, w, *extras], in_specs=in_specs,
                out_shape=out_shape, out_specs=out_specs,
                sem=("arbitrary",) if ns else ("parallel",), comm=comm)
    if comm is None:
        return res[0] if single else res
    return (res[0][0] if single else res[0]), res[1]


def _mm_tn(g, s, b, l, kind, *, name, tt=1024, b_fn=None):
    T, ws = s.shape
    wb = b.shape[1]
    _, _, d2, d3 = g.shape
    tt = min(tt, T)
    nsteps = T // tt
    cw = wb // N_DEV

    fresh = isinstance(g, jax.ShapeDtypeStruct)

    def body(*refs):
        s_ref, b_ref, o_ref, acc, stage, sem = refs if fresh else refs[1:]
        t = pl.program_id(0)

        @pl.when(t == 0)
        def _():
            acc[...] = jnp.zeros((ws, wb), F32)

        st = s_ref[...].astype(F32).T.astype(BF)
        for j in range(N_DEV):
            bv = b_ref[:, j * cw:(j + 1) * cw]
            if b_fn is not None:
                bv = b_fn(bv)
            acc[:, j * cw:(j + 1) * cw] += _dot(st, bv.astype(BF))

        @pl.when(t == nsteps - 1)
        def _():
            for j in range(N_DEV):
                if kind == "cols":
                    blk = acc[:, j * d3:(j + 1) * d3]
                elif kind == "rows":
                    blk = acc[j * d2:(j + 1) * d2, :]
                else:
                    blk = acc[:, j * d2:(j + 1) * d2].T
                stage[...] = blk.astype(BF)
                cp = pltpu.make_async_copy(stage, o_ref.at[j, l], sem)
                cp.start()
                cp.wait()

    anyspec = pl.BlockSpec(memory_space=pl.ANY)
    return pl.pallas_call(
        body, name=name, grid=(nsteps,),
        out_shape=jax.ShapeDtypeStruct(g.shape, BF),
        in_specs=([] if fresh else [anyspec]) + [pl.BlockSpec((tt, ws), lambda t: (t, 0)),
                                                 pl.BlockSpec((tt, wb), lambda t: (t, 0))],
        out_specs=anyspec,
        scratch_shapes=[pltpu.VMEM((ws, wb), F32), pltpu.VMEM((d2, d3), BF), pltpu.SemaphoreType.DMA],
        input_output_aliases={} if fresh else {0: 0},
        compiler_params=_cp(("arbitrary",)),
    )(*([] if fresh else [g]), s, b)


def _rw(fn, tiles, rows, outs, sums, *, name, tr=256):
    T = tiles[0][0].shape[0]
    tr = min(tr, T)
    nt, nr, no, ns = len(tiles), len(rows), len(outs), len(sums)

    def body(*refs):
        t_refs, r_refs = refs[:nt], refs[nt:nt + nr]
        o_refs = refs[nt + nr:nt + nr + no]
        s_refs = refs[nt + nr + no:]
        res = fn(*[t[...] for t in t_refs], *[r[...] for r in r_refs])
        for o_ref, val in zip(o_refs, res[:no]):
            o_ref[...] = val.astype(o_ref.dtype)
        if ns:
            first = pl.program_id(0) == 0

            @pl.when(first)
            def _():
                for s_ref, val in zip(s_refs, res[no:]):
                    s_ref[...] = val

            @pl.when(jnp.logical_not(first))
            def _():
                for s_ref, val in zip(s_refs, res[no:]):
                    s_ref[...] += val

    in_specs = [pl.BlockSpec((tr, w), functools.partial(lambda i, cb: (i, cb), cb=cb)) for (_, w, cb) in tiles]
    in_specs += [pl.BlockSpec(r.shape, lambda i: (0, 0)) for r in rows]
    out_shape = [jax.ShapeDtypeStruct((T, w), dt) for (w, dt) in outs]
    out_shape += [jax.ShapeDtypeStruct((1, w), F32) for w in sums]
    out_specs = [pl.BlockSpec((tr, w), lambda i: (i, 0)) for (w, _) in outs]
    out_specs += [pl.BlockSpec((1, w), lambda i: (0, 0)) for w in sums]
    return pl.pallas_call(
        body, name=name, grid=(T // tr,), out_shape=out_shape,
        in_specs=in_specs, out_specs=out_specs,
        compiler_params=_cp(("arbitrary",)),
    )(*[t[0] for t in tiles], *rows)


def _ln_stats(z):
    mu = jnp.mean(z, axis=-1, keepdims=True)
    zc = z - mu
    var = jnp.mean(zc * zc, axis=-1, keepdims=True)
    rstd = lax.rsqrt(var + LN_EPS)
    return zc * rstd, rstd


def _ln_bwd_core(dy, zh, rstd, g):
    dzh = dy * g
    m1 = jnp.mean(dzh, axis=-1, keepdims=True)
    m2 = jnp.mean(dzh * zh, axis=-1, keepdims=True)
    return rstd * (dzh - m1 - zh * m2)


def _colsum(v):
    return jnp.sum(v, axis=0, keepdims=True)


def _z_and_ln(z, g, b):
    zh, _ = _ln_stats(z)
    return z, zh * g + b


def _ln_bwd_tile(dy, z, g):
    zh, rstd = _ln_stats(z)
    dz = _ln_bwd_core(dy, zh, rstd, g)
    return dz, _colsum(dy * zh), _colsum(dy), _colsum(dz)


def _ln_bwd(dy, z, g, name):
    D = z.shape[1]

    def fn(dyt, zt, gr):
        zh, rstd = _ln_stats(zt)
        dz = _ln_bwd_core(dyt, zh, rstd, gr)
        return dz, _colsum(dyt * zh), _colsum(dyt), _colsum(dz)

    return _rw(fn, [(dy, D, 0), (z, D, 0)], [g], [(D, F32)], [D, D, D], name=name)


def _heads(D):
    return D // HEAD


def _rms_parts(o, D):
    xs, rs = [], []
    for h in range(_heads(D)):
        oh = o[:, h * HEAD:(h + 1) * HEAD]
        r = lax.rsqrt(jnp.mean(oh * oh, axis=-1, keepdims=True) + RMS_EPS)
        xs.append(oh * r)
        rs.append(r)
    return xs, rs


def _rms_gate_fwd(o, proj, ng, name):
    D = o.shape[1]

    def fn(ot, grt, ngr):
        xs, _ = _rms_parts(ot, D)
        xh = jnp.concatenate(xs, axis=1)
        return (xh * ngr * (grt * _sig(grt)),)

    return _rw(fn, [(o, D, 0), (proj, D, 3)], [ng], [(D, BF)], [], name=name)[0]


def _gate_bwd_tile(dyt, ot, grt, ngr):
    D = ot.shape[1]
    xs, rs = _rms_parts(ot, D)
    xh = jnp.concatenate(xs, axis=1)
    sg = _sig(grt)
    on = xh * ngr
    dgr = dyt * on * (sg * (1.0 + grt * (1.0 - sg)))
    don = dyt * (grt * sg)
    dxh = don * ngr
    dos = []
    for h in range(_heads(D)):
        sl = slice(h * HEAD, (h + 1) * HEAD)
        m = jnp.mean(dxh[:, sl] * xs[h], axis=-1, keepdims=True)
        dos.append(rs[h] * (dxh[:, sl] - xs[h] * m))
    return jnp.concatenate(dos, axis=1), dgr, _colsum(don * xh)


def _u_and_glu(u):
    D = u.shape[1] // 2
    return u, u[:, :D] * _sig(u[:, D:])


def _silu_ln_bwd_tile(dst, ct, gr, br):
    zh, rstd = _ln_stats(ct)
    ln = zh * gr + br
    sg = _sig(ln)
    dln = dst * (sg * (1.0 + ln * (1.0 - sg)))
    dc = _ln_bwd_core(dln, zh, rstd, gr)
    return dc, _colsum(dln * zh), _colsum(dln), _colsum(dc)


def _loss_k(y, tgt, name):
    D = y.shape[1]

    def fn(yt, tt):
        e = yt - tt
        return e * (1.0 / D), _colsum(e * e) * (0.5 / D)

    return _rw(fn, [(y, D, 0), (tgt, D, 0)], [], [(D, F32)], [D], name=name)


def _shifted(ext, n, tr):
    for b in range(8):
        rb = ext if b == 0 else pltpu.roll(ext, n - b, 0)
        for a in range(HALO // 8 + 1):
            o = 8 * a + b
            if o <= HALO:
                yield o, rb[8 * a:8 * a + tr, :]


def _lane_strips(D):
    return [slice(c, c + 128) for c in range(0, D, 128)]


def _conv_fwd(glu, w, bdw, g, b, name, tr=256, comm=None):
    T, D = glu.shape
    tr = min(tr, T // 2)
    rt = min(CONV_ROWS, tr)
    hb = tr // HALO

    def body(cur_ref, halo_ref, w_ref, bdw_ref, g_ref, b_ref, c_ref, s_ref):
        i = pl.program_id(0)
        for cs in _lane_strips(D):
            halo = jnp.where(i > 0, halo_ref[:, cs], 0.0)
            for r0 in range(0, tr, rt):
                if r0 == 0:
                    ext = jnp.concatenate([halo, cur_ref[0:rt, cs]], axis=0)
                else:
                    ext = cur_ref[r0 - HALO:r0 + rt, cs]
                acc = None
                for o, sh in _shifted(ext, rt + HALO, rt):
                    k = o - (HALO - CONV_W + 1)
                    if 0 <= k < CONV_W:
                        term = w_ref[k:k + 1, cs] * sh
                        acc = term if acc is None else acc + term
                c_ref[r0:r0 + rt, cs] = acc + bdw_ref[:, cs]
        cv = c_ref[...]
        zh, _ = _ln_stats(cv)
        ln = zh * g_ref[...] + b_ref[...]
        s_ref[...] = (ln * _sig(ln)).astype(BF)

    row = pl.BlockSpec((1, D), lambda i: (0, 0))
    return _call(
        body, name=name, grid=(T // tr,), operands=[glu, glu, w, bdw, g, b],
        out_shape=[jax.ShapeDtypeStruct((T, D), F32), jax.ShapeDtypeStruct((T, D), BF)],
        in_specs=[pl.BlockSpec((tr, D), lambda i: (i, 0)),
                  pl.BlockSpec((HALO, D), lambda i: (jnp.maximum(i * hb - 1, 0), 0)),
                  pl.BlockSpec((HALO, D), lambda i: (0, 0)), row, row, row],
        out_specs=[pl.BlockSpec((tr, D), lambda i: (i, 0))] * 2,
        sem=("parallel",), comm=comm)


def _conv_bwd(dc, glu, u, w, name, tr=256, comm=None):
    T, D = glu.shape
    tr = min(tr, T // 2)
    rt = min(CONV_ROWS, tr)
    hb = tr // HALO
    nsteps = T // tr
    last_hb = T // HALO - 1

    def body(dc_ref, nxt_ref, glu_ref, prv_ref, ua_ref, ug_ref, w_ref, du_ref, dw_ref, dbu_ref):
        i = pl.program_id(0)

        @pl.when(i == 0)
        def _():
            dw_ref[...] = jnp.zeros((HALO, D), F32)
            dbu_ref[...] = jnp.zeros((1, 2 * D), F32)

        for cs in _lane_strips(D):
            nxt = jnp.where(i < nsteps - 1, nxt_ref[:, cs], 0.0)
            prv = jnp.where(i > 0, prv_ref[:, cs], 0.0)
            dws = [None] * CONV_W
            dba = dbg = None
            gs = slice(D + cs.start, D + cs.stop)
            for r0 in range(0, tr, rt):
                if r0 + rt == tr:
                    ext2 = jnp.concatenate([dc_ref[r0:tr, cs], nxt], axis=0)
                else:
                    ext2 = dc_ref[r0:r0 + rt + HALO, cs]
                acc = None
                for o, sh in _shifted(ext2, rt + HALO, rt):
                    k = CONV_W - 1 - o
                    if 0 <= k < CONV_W:
                        term = w_ref[k:k + 1, cs] * sh
                        acc = term if acc is None else acc + term
                sg = _sig(ug_ref[r0:r0 + rt, cs])
                da = acc * sg
                dgt = acc * ua_ref[r0:r0 + rt, cs] * sg * (1.0 - sg)
                du_ref[r0:r0 + rt, cs] = da.astype(BF)
                du_ref[r0:r0 + rt, gs] = dgt.astype(BF)
                dba = _colsum(da) if dba is None else dba + _colsum(da)
                dbg = _colsum(dgt) if dbg is None else dbg + _colsum(dgt)
                if r0 == 0:
                    ext = jnp.concatenate([prv, glu_ref[0:rt, cs]], axis=0)
                else:
                    ext = glu_ref[r0 - HALO:r0 + rt, cs]
                dcs = dc_ref[r0:r0 + rt, cs]
                for o, sh in _shifted(ext, rt + HALO, rt):
                    k = o - (HALO - CONV_W + 1)
                    if 0 <= k < CONV_W:
                        part = _colsum(dcs * sh)
                        dws[k] = part if dws[k] is None else dws[k] + part
            for k in range(CONV_W):
                dw_ref[k:k + 1, cs] += dws[k]
            dbu_ref[:, cs] += dba
            dbu_ref[:, gs] += dbg

    return _call(
        body, name=name, grid=(nsteps,), operands=[dc, dc, glu, glu, u, u, w],
        out_shape=[jax.ShapeDtypeStruct((T, 2 * D), BF), jax.ShapeDtypeStruct((HALO, D), F32),
                   jax.ShapeDtypeStruct((1, 2 * D), F32)],
        in_specs=[pl.BlockSpec((tr, D), lambda i: (i, 0)),
                  pl.BlockSpec((HALO, D), lambda i: (jnp.minimum((i + 1) * hb, last_hb), 0)),
                  pl.BlockSpec((tr, D), lambda i: (i, 0)),
                  pl.BlockSpec((HALO, D), lambda i: (jnp.maximum(i * hb - 1, 0), 0)),
                  pl.BlockSpec((tr, D), lambda i: (i, 0)),
                  pl.BlockSpec((tr, D), lambda i: (i, 1)),
                  pl.BlockSpec((HALO, D), lambda i: (0, 0))],
        out_specs=[pl.BlockSpec((tr, 2 * D), lambda i: (i, 0)),
                   pl.BlockSpec((HALO, D), lambda i: (0, 0)),
                   pl.BlockSpec((1, 2 * D), lambda i: (0, 0))],
        sem=("arbitrary",), comm=comm)


def _rec_consts(C):
    t = np.arange(C)
    lb = (t[None, :] <= t[:, None]).astype(np.float32)

    def cum_at(idx):
        return (t[None, :] <= idx[:, None]).astype(np.float32)

    blocks, masks = [], []
    mid = SUB * (t // SUB) + SUB // 2 - 1
    eq = lb - cum_at(mid)
    blocks += [eq, -eq]
    masks.append(((t[:, None] // SUB) == (t[None, :] // SUB)) & (t[None, :] <= t[:, None]))
    nb = SUB
    while nb < C:
        odd = (t // nb) % 2 == 1
        e_t = nb * (t // nb) - 1
        e_s = nb * (t // nb) + nb - 1
        blocks.append(np.where(odd[:, None], lb - cum_at(e_t), 0.0))
        blocks.append(np.where(~odd[:, None], cum_at(e_s) - lb, 0.0))
        masks.append(((t[:, None] // (2 * nb)) == (t[None, :] // (2 * nb))) & odd[:, None] & ~odd[None, :])
        nb *= 2
    blocks += [lb, 1.0 - lb, np.ones((HEAD, C), np.float32)]
    L = np.concatenate(blocks, axis=0).astype(np.float32)
    L3 = np.concatenate([L, L, L], axis=1)
    LT3 = np.concatenate([L.T, L.T, L.T], axis=1)
    m = np.stack(masks).astype(np.float32)
    mT = np.transpose(m, (0, 2, 1)).copy()
    return (jnp.asarray(L3, BF), jnp.asarray(LT3, BF), jnp.asarray(m), jnp.asarray(mT), len(masks))


def _split3(x):
    h = x.astype(BF)
    r = x - h.astype(F32)
    m = r.astype(BF)
    lo = (r - m.astype(F32)).astype(BF)
    return h, m, lo


def _gates(qr, fz, lbr):
    sq = _sig(qr)
    q = qr * sq
    sg = _sig(fz)
    f = lbr + (1.0 - lbr) * sg
    fc = jnp.maximum(f, GATE_EPS)
    return q, 1.0 - f, jnp.log(fc), sq, sg, f, fc


def _rec_fwd(proj, lbr, name, comm=None):
    T, D4 = proj.shape
    D = D4 // 4
    H = _heads(D)
    C = CHUNK
    nC = T // C
    L3, _, m, _, nl = _rec_consts(C)
    R = L3.shape[0]

    def body(q_ref, f_ref, v_ref, lb_ref, l_ref, m_ref, o_ref, s_ref, st):
        @pl.when(pl.program_id(0) == 0)
        def _():
            st[...] = jnp.zeros((H, HEAD, HEAD), F32)

        q, k, logf = _gates(q_ref[...], f_ref[...], lb_ref[...])[:3]
        ex = jnp.exp(_dot(l_ref[...], jnp.concatenate(_split3(logf), axis=0)))
        vb = v_ref[...].astype(BF)
        s_ref[0] = st[...]
        outs = []
        for h in range(H):
            sl = slice(h * HEAD, (h + 1) * HEAD)
            qh, kh = q[:, sl], k[:, sl]
            p = jnp.zeros((C, C), F32)
            for lv in range(nl):
                qt = (qh * ex[2 * lv * C:(2 * lv + 1) * C, sl]).astype(BF)
                kt = (kh * ex[(2 * lv + 1) * C:(2 * lv + 2) * C, sl]).astype(BF)
                p = p + jnp.where(m_ref[lv] > 0.0, _dot_nt(qt, kt), 0.0)
            base = 2 * nl * C
            qhat = (qh * ex[base:base + C, sl]).astype(BF)
            khat = (kh * ex[base + C:base + 2 * C, sl]).astype(BF)
            elast = ex[base + 2 * C:base + 2 * C + HEAD, sl]
            sth = st[h]
            outs.append(_dot(p.astype(BF), vb[:, sl]) + _dot_nt(qhat, sth.astype(BF)))
            st[h] = elast * sth + _dot_tn(vb[:, sl], khat)
        o_ref[...] = jnp.concatenate(outs, axis=1)

    def cblk(cb):
        return pl.BlockSpec((C, D), lambda i: (i, cb))

    return _call(
        body, name=name, grid=(nC,), operands=[proj, proj, proj, lbr, L3, m],
        out_shape=[jax.ShapeDtypeStruct((T, D), F32), jax.ShapeDtypeStruct((nC, H, HEAD, HEAD), F32)],
        in_specs=[cblk(0), cblk(1), cblk(2), pl.BlockSpec((1, D), lambda i: (0, 0)),
                  pl.BlockSpec(L3.shape, lambda i: (0, 0)), pl.BlockSpec(m.shape, lambda i: (0, 0, 0))],
        out_specs=[pl.BlockSpec((C, D), lambda i: (i, 0)),
                   pl.BlockSpec((1, H, HEAD, HEAD), lambda i: (i, 0, 0, 0))],
        scratch=[pltpu.VMEM((H, HEAD, HEAD), F32)], sem=("arbitrary",), comm=comm)


def _rec_bwd(proj, lbr, do, dgr, ssave, name, comm=None):
    T, D4 = proj.shape
    D = D4 // 4
    H = _heads(D)
    C = min(CHUNK_BWD, T // 2)
    nC = T // C
    stride = C // CHUNK
    L3, LT3, m, mT, nl = _rec_consts(C)

    def body(q_ref, f_ref, v_ref, lb_ref, do_ref, dgr_ref, s_ref, l_ref, lt_ref, m_ref, mt_ref,
             dp_ref, dlb_ref, dst):
        @pl.when(pl.program_id(0) == 0)
        def _():
            dst[...] = jnp.zeros((H, HEAD, HEAD), F32)
            dlb_ref[...] = jnp.zeros((1, D), F32)

        qr = q_ref[...]
        lbv = lb_ref[...]
        q, k, logf, sq, sg, f, fc = _gates(qr, f_ref[...], lbv)
        ex = jnp.exp(_dot(l_ref[...], jnp.concatenate(_split3(logf), axis=0)))
        vb = v_ref[...].astype(BF)
        dob = do_ref[...].astype(BF)
        base = 2 * nl * C
        de = [[] for _ in range(2 * nl + 3)]
        dqs, dks, dvs = [], [], []
        for h in range(H):
            sl = slice(h * HEAD, (h + 1) * HEAD)
            qh, kh, vh, doh = q[:, sl], k[:, sl], vb[:, sl], dob[:, sl]
            dp = _dot_nt(doh, vh)
            dpt = _dot_nt(vh, doh)
            sth = s_ref[0, h]
            dsth = dst[h]
            dsb = dsth.astype(BF)
            pt = jnp.zeros((C, C), F32)
            dq = jnp.zeros((C, HEAD), F32)
            dk = jnp.zeros((C, HEAD), F32)
            for lv in range(nl):
                exq = ex[2 * lv * C:(2 * lv + 1) * C, sl]
                exk = ex[(2 * lv + 1) * C:(2 * lv + 2) * C, sl]
                qt = qh * exq
                kt = kh * exk
                qtb, ktb = qt.astype(BF), kt.astype(BF)
                pt = pt + jnp.where(mt_ref[lv] > 0.0, _dot_nt(ktb, qtb), 0.0)
                dqt = _dot(jnp.where(m_ref[lv] > 0.0, dp, 0.0).astype(BF), ktb)
                dkt = _dot(jnp.where(mt_ref[lv] > 0.0, dpt, 0.0).astype(BF), qtb)
                dq = dq + dqt * exq
                dk = dk + dkt * exk
                de[2 * lv].append(dqt * qt)
                de[2 * lv + 1].append(dkt * kt)
            exb = ex[base:base + C, sl]
            exkh = ex[base + C:base + 2 * C, sl]
            elast = ex[base + 2 * C:base + 2 * C + HEAD, sl]
            qhat = qh * exb
            khat = kh * exkh
            dqh = _dot(doh, sth.astype(BF))
            dkh = _dot(vh, dsb)
            dq = dq + dqh * exb
            dk = dk + dkh * exkh
            de[2 * nl].append(dqh * qhat)
            de[2 * nl + 1].append(dkh * khat)
            de[2 * nl + 2].append(dsth * sth * elast)
            dvs.append(_dot(pt.astype(BF), doh) + _dot_nt(khat.astype(BF), dsb))
            dst[h] = elast * dsth + _dot_tn(doh, qhat.astype(BF))
            dqs.append(dq)
            dks.append(dk)
        de_all = jnp.concatenate([jnp.concatenate(b, axis=1) for b in de], axis=0)
        dlogf = _dot(lt_ref[...], jnp.concatenate(_split3(de_all), axis=0))
        dq = jnp.concatenate(dqs, axis=1)
        dk = jnp.concatenate(dks, axis=1)
        dv = jnp.concatenate(dvs, axis=1)
        ind = jnp.where(f > GATE_EPS, 1.0, jnp.where(f == GATE_EPS, 0.5, 0.0))
        df = dlogf * ind / fc - dk
        dfz = df * (1.0 - lbv) * sg * (1.0 - sg)
        dlb_ref[...] += _colsum(df * (1.0 - sg))
        dqr = dq * (sq * (1.0 + qr * (1.0 - sq)))
        dp_ref[...] = jnp.concatenate([dqr.astype(BF), dfz.astype(BF), dv.astype(BF), dgr_ref[...]], axis=1)

    def cblk(cb):
        return pl.BlockSpec((C, D), lambda i: (nC - 1 - i, cb))

    def whole(a):
        nd = a.ndim
        return pl.BlockSpec(a.shape, lambda i: (0,) * nd)

    return _call(
        body, name=name, grid=(nC,), operands=[proj, proj, proj, lbr, do, dgr, ssave, L3, LT3, m, mT],
        out_shape=[jax.ShapeDtypeStruct((T, D4), BF), jax.ShapeDtypeStruct((1, D), F32)],
        in_specs=[cblk(0), cblk(1), cblk(2), pl.BlockSpec((1, D), lambda i: (0, 0)),
                  pl.BlockSpec((C, D), lambda i: (nC - 1 - i, 0)),
                  pl.BlockSpec((C, D), lambda i: (nC - 1 - i, 0)),
                  pl.BlockSpec((1, H, HEAD, HEAD), lambda i: (stride * (nC - 1 - i), 0, 0, 0)),
                  whole(L3), whole(LT3), whole(m), whole(mT)],
        out_specs=[pl.BlockSpec((C, D4), lambda i: (nC - 1 - i, 0)),
                   pl.BlockSpec((1, D), lambda i: (0, 0))],
        scratch=[pltpu.VMEM((H, HEAD, HEAD), F32)], sem=("arbitrary",), comm=comm)


def _softmax_rows(lg_ref):
    n = lg_ref.shape[0]
    rows = [lg_ref[l:l + 1, :] for l in range(n)]
    mx = rows[0]
    for r in rows[1:]:
        mx = jnp.maximum(mx, r)
    es = [jnp.exp(r - mx) for r in rows]
    tot = es[0]
    for e in es[1:]:
        tot = tot + e
    return [e / tot for e in es]


def _lb_fwd(logits):
    n, D = logits.shape

    def body(lg_ref, o_ref):
        soft = _softmax_rows(lg_ref)
        acc = jnp.zeros((1, D), F32)
        o_ref[0:1, :] = acc
        for j in range(1, n):
            acc = acc + soft[j]
            o_ref[j:j + 1, :] = acc

    return pl.pallas_call(body, name="lb_fwd", out_shape=jax.ShapeDtypeStruct((n, D), F32))(logits)


def _small_reduce(parts, logits, lb_row0):
    _, R, D = parts.shape
    n = logits.shape[0]

    def body(p_ref, lg_ref, o_ref):
        acc = p_ref[0]
        for d in range(1, N_DEV):
            acc = acc + p_ref[d]
        o_ref[...] = acc
        soft = _softmax_rows(lg_ref)
        dlb = [o_ref[lb_row0 + j:lb_row0 + j + 1, :] for j in range(n)]
        dsoft = [jnp.zeros((1, D), F32)]
        for l in range(1, n):
            s = dlb[l]
            for j in range(l + 1, n):
                s = s + dlb[j]
            dsoft.append(s)
        dot = soft[0] * dsoft[0]
        for l in range(1, n):
            dot = dot + soft[l] * dsoft[l]
        for l in range(n):
            o_ref[lb_row0 + l:lb_row0 + l + 1, :] = soft[l] * (dsoft[l] - dot)

    return pl.pallas_call(body, name="small_reduce", out_shape=jax.ShapeDtypeStruct((R, D), F32))(parts, logits)


def _adam_math(g, w, m, v):
    m2 = ADAM_B1 * m + (1.0 - ADAM_B1) * g
    v2 = ADAM_B2 * v + (1.0 - ADAM_B2) * (g * g)
    mh = m2 / (1.0 - ADAM_B1 ** ADAM_STEP)
    vh = v2 / (1.0 - ADAM_B2 ** ADAM_STEP)
    delta = -ADAM_LR * (mh / (jnp.sqrt(vh) + ADAM_EPS) + ADAM_WD * w)
    return delta, m2, v2


def _adamw(recv, w, m, v, name):
    nsrc = recv.shape[0]
    shp = w.shape
    cols = shp[-1]
    rows = int(np.prod(shp[:-1]))
    r2 = recv.reshape(nsrc, rows, cols)
    tr = min(rows, max(8, (1 << 20) // (cols * nsrc)))
    while rows % tr:
        tr //= 2

    def body(r_ref, w_ref, m_ref, v_ref, g_ref, d_ref, nm_ref, nv_ref):
        g = r_ref[0].astype(F32)
        for s in range(1, nsrc):
            g = g + r_ref[s].astype(F32)
        delta, m2, v2 = _adam_math(g, w_ref[...], m_ref[...], v_ref[...])
        g_ref[...] = g
        d_ref[...] = delta
        nm_ref[...] = m2
        nv_ref[...] = v2

    blk = pl.BlockSpec((tr, cols), lambda i: (i, 0))
    outs = pl.pallas_call(
        body, name=name, grid=(rows // tr,),
        out_shape=[jax.ShapeDtypeStruct((rows, cols), F32)] * 4,
        in_specs=[pl.BlockSpec((nsrc, tr, cols), lambda i: (0, i, 0)), blk, blk, blk],
        out_specs=[blk] * 4,
        compiler_params=_cp(("parallel",)),
    )(r2, w.reshape(rows, cols), m.reshape(rows, cols), v.reshape(rows, cols))
    return [o.reshape(shp) for o in outs]


def _pack(arrs, lane):
    flat = jnp.concatenate([a.reshape(-1) for a in arrs])
    n = flat.shape[0]
    rows = -(-n // lane)
    rows = -(-rows // 8) * 8
    flat = jnp.pad(flat, (0, rows * lane - n))
    return flat.reshape(rows, lane)


def _unpack(packed, shapes):
    flat = packed.reshape(-1)
    out, off = [], 0
    for s in shapes:
        n = int(np.prod(s))
        out.append(flat[off:off + n].reshape(s))
        off += n
    return out


def kernel(x, ln_mix_g, ln_mix_b, ln_ffn_g, ln_ffn_b, ffn_w1, ffn_w2, a_w_in, a_lb_logits, a_norm_g, a_w_out, b_w_pw1, b_b_pw1, b_w_dw, b_b_dw, b_ln_g, b_ln_b, b_w_pw2, b_b_pw2, loss_target, m_ln_mix_g, m_ln_mix_b, m_ln_ffn_g, m_ln_ffn_b, m_ffn_w1, m_ffn_w2, m_a_w_in, m_a_lb_logits, m_a_norm_g, m_a_w_out, m_b_w_pw1, m_b_b_pw1, m_b_w_dw, m_b_b_dw, m_b_ln_g, m_b_ln_b, m_b_w_pw2, m_b_b_pw2, v_ln_mix_g, v_ln_mix_b, v_ln_ffn_g, v_ln_ffn_b, v_ffn_w1, v_ffn_w2, v_a_w_in, v_a_lb_logits, v_a_norm_g, v_a_w_out, v_b_w_pw1, v_b_b_pw1, v_b_w_dw, v_b_b_dw, v_b_ln_g, v_b_ln_b, v_b_w_pw2, v_b_b_pw2):
    T, D = x.shape[1], x.shape[2]
    nA, nB = a_w_in.shape[0], b_w_pw1.shape[0]
    me = 4 * lax.axis_index("x") + 2 * lax.axis_index("y") + lax.axis_index("c")
    xin = x[0]
    tgt = loss_target[0]

    small_names = [b_b_pw1, b_w_dw, b_b_dw, b_ln_g, b_ln_b, b_b_pw2]
    sp = _pack(small_names, 128)
    wsrc = {"W1": ffn_w1.astype(BF), "W2": ffn_w2.astype(BF), "Win": a_w_in.astype(BF),
            "Wout": a_w_out.astype(BF), "Wp1": b_w_pw1.astype(BF), "Wp2": b_w_pw2.astype(BF), "small": sp[None]}
    W = {k: jax.ShapeDtypeStruct((N_DEV,) + v.shape, v.dtype) for k, v in wsrc.items()}
    GW = {k: jax.ShapeDtypeStruct((N_DEV,) + v.shape, BF) for k, v in wsrc.items() if k != "small"}
    R = {k: jax.ShapeDtypeStruct(v.shape, BF) for k, v in GW.items()}

    def names_of(items):
        out = []
        for n, _ in items:
            if n not in out:
                out.append(n)
        return out

    def hosted(fn, key, sched, mode, src, bufs, *args, **kw):
        items = sched.get(key)
        if items is None:
            return fn(*args, name=key, **kw)
        comm = (mode, [(src[n], l, bufs[n]) for n, l in items])
        outs, new = fn(*args, name=key, comm=comm, **kw)
        for n, b in zip(names_of(items), new):
            bufs[n] = b
        return outs

    first = [("Win", 0), ("small", 0)]
    for n, b in zip(names_of(first), _exchange(("gather", [(wsrc[n], l, W[n]) for n, l in first]), "gather_first")):
        W[n] = b
    fwd_sched = {"a0_proj": [("Wout", 0), ("Wp1", 0)], "a0_rec_fwd": [("W1", 0), ("W2", 0), ("Wp2", 0)],
                 "l0_ffn_up": [("W2", 1)], "l0_ffn_down": [("W1", 1)],
                 "b0_conv": [("Win", 1), ("Wout", 1)], "l1_ffn_up": [("W2", 2)], "l1_ffn_down": [("W1", 2)],
                 "a1_rec_fwd": [("Wp1", 1), ("Wp2", 1), ("W1", 3), ("W2", 3)]}
    bwd_sched = {"a1_rec_bwd": [("W1", 3), ("W2", 3), ("Wp1", 1), ("Wp2", 1), ("W1", 2), ("W2", 2)],
                 "b0_conv_bwd": [("Win", 1), ("Wout", 1)],
                 "a0_rec_bwd": [("W1", 1), ("W2", 1), ("Wp1", 0), ("Wp2", 0), ("W1", 0), ("W2", 0), ("Wout", 0)],
                 "a0_proj_dx": [("Win", 0)]}

    def fwd(fn, key, *args, **kw):
        return hosted(fn, key, fwd_sched, "gather", wsrc, W, *args, **kw)

    def bwd(fn, key, *args, **kw):
        return hosted(fn, key, bwd_sched, "scatter", GW, R, *args, **kw)

    spg = W["small"][:, 0]
    sm = [jnp.stack(p) for p in zip(*[_unpack(spg[d], [a.shape for a in small_names]) for d in range(N_DEV)])]
    bpw1 = jnp.transpose(sm[0], (1, 0, 2)).reshape(nB, 1, 2 * D)
    wdw = jnp.transpose(sm[1], (1, 2, 0, 3)).reshape(nB, CONV_W, D)
    wdw = jnp.pad(wdw, ((0, 0), (0, HALO - CONV_W), (0, 0)))
    bdw, blng, blnb, bpw2 = [jnp.transpose(s, (1, 0, 2)).reshape(nB, 1, D) for s in sm[2:]]
    lb_all = _lb_fwd(a_lb_logits)

    zx = [(D, F32), (D, F32)]
    ln_mix = [(ln_mix_g[i:i + 1], ln_mix_b[i:i + 1]) for i in range(DEPTH)]
    ln_ffn = [(ln_ffn_g[i:i + 1], ln_ffn_b[i:i + 1]) for i in range(DEPTH)]
    saved = []
    h = xin
    for i in range(DEPTH):
        j = i // 2
        sv = {"xin": h}
        if i % 2 == 0:
            proj = fwd(_mm, f"a{j}_proj", h, W["Win"], j, "nn_col", out_dtype=F32)
            o, ssave = fwd(_rec_fwd, f"a{j}_rec_fwd", proj, lb_all[j:j + 1])
            yg = _rms_gate_fwd(o, proj, a_norm_g[j:j + 1], f"a{j}_gate_fwd")
            z1, x1 = _mm(yg, W["Wout"], j, "nn_row", out_dtype=F32, name=f"a{j}_out", outs=zx,
                         epi=lambda acc, r, g, b: _z_and_ln(acc + ALPHA * r, g, b), extras=[h, *ln_mix[i]])
            sv.update(proj=proj, o=o, ssave=ssave, yg=yg)
        else:
            u, glu = _mm(h, W["Wp1"], j, "nn_col", out_dtype=F32, name=f"b{j}_pw1",
                         outs=[(2 * D, F32), (D, F32)],
                         epi=lambda acc, b: _u_and_glu(acc + b), extras=[bpw1[j]])
            cv, s = fwd(_conv_fwd, f"b{j}_conv", glu, wdw[j], bdw[j], blng[j], blnb[j])
            z1, x1 = _mm(s, W["Wp2"], j, "nn_row", out_dtype=F32, name=f"b{j}_pw2", outs=zx,
                         epi=lambda acc, bb, r, g, b: _z_and_ln(acc + bb + ALPHA * r, g, b),
                         extras=[bpw2[j], h, *ln_mix[i]])
            sv.update(u=u, glu=glu, cv=cv, s=s)
        hh = fwd(_mm, f"l{i}_ffn_up", x1, W["W1"], i, "nn_col", out_dtype=BF)
        z2, h = fwd(_mm, f"l{i}_ffn_down", hh, W["W2"], i, "nn_row", out_dtype=F32, outs=zx,
                    a_fn=lambda t: jnp.square(jnp.maximum(t, 0)),
                    epi=lambda acc, r, g, b: _z_and_ln(acc + ALPHA * r, g, b), extras=[x1, *ln_ffn[i]])
        sv.update(z1=z1, x1=x1, hh=hh, z2=z2)
        saved.append(sv)

    dx, loss_row = _loss_k(h, tgt, "loss")

    rows = {}
    top = DEPTH - 1
    dz2, dg, db, _ = _ln_bwd(dx, saved[top]["z2"], ln_ffn[top][0], f"l{top}_ln_ffn_bwd")
    rows[("ffn_g", top)], rows[("ffn_b", top)] = dg, db
    ln_out = dict(outs=[(D, F32)], sums=[D, D, D])
    for i in reversed(range(DEPTH)):
        j = i // 2
        sv = saved[i]
        if i > 0:
            below = dict(epi=lambda acc, r, z, g: _ln_bwd_tile(acc + ALPHA * r, z, g), **ln_out)
            below_extras = [saved[i - 1]["z2"], ln_ffn[i - 1][0]]
        else:
            below = dict(epi=lambda acc, r: acc + ALPHA * r)
            below_extras = []
        dh = _mm(dz2, W["W2"], i, "nt_row", out_dtype=BF, name=f"l{i}_ffn_down_dx",
                 epi=lambda acc, hv: acc * (2.0 * jnp.maximum(hv.astype(F32), 0.0)), extras=[sv["hh"]])
        GW["W2"] = _mm_tn(GW["W2"], dz2, sv["hh"], i, "rows_t", name=f"l{i}_ffn_down_dw",
                         b_fn=lambda t: jnp.square(jnp.maximum(t, 0)))
        GW["W1"] = _mm_tn(GW["W1"], sv["x1"], dh, i, "cols", name=f"l{i}_ffn_up_dw")
        dz1, dg, db, dz1sum = _mm(dh, W["W1"], i, "nt_col", out_dtype=F32, name=f"l{i}_ffn_up_dx",
                                  epi=lambda acc, r, z, g: _ln_bwd_tile(acc + ALPHA * r, z, g),
                                  extras=[dz2, sv["z1"], ln_mix[i][0]], **ln_out)
        rows[("mix_g", i)], rows[("mix_b", i)] = dg, db
        if i % 2 == 0:
            GW["Wout"] = _mm_tn(GW["Wout"], sv["yg"], dz1, j, "rows", name=f"a{j}_out_dw")
            do, dgr, dng = _mm(dz1, W["Wout"], j, "nt_row", out_dtype=F32, name=f"a{j}_out_dx",
                               outs=[(D, F32), (D, BF)], sums=[D], epi=_gate_bwd_tile,
                               extras=[sv["o"], (sv["proj"], 3), a_norm_g[j:j + 1]])
            rows[("ng", j)] = dng
            dproj, dlb = bwd(_rec_bwd, f"a{j}_rec_bwd", sv["proj"], lb_all[j:j + 1], do, dgr, sv["ssave"])
            rows[("lb", j)] = dlb
            GW["Win"] = _mm_tn(GW["Win"], sv["xin"], dproj, j, "cols", name=f"a{j}_proj_dw")
            res = bwd(_mm, f"a{j}_proj_dx", dproj, W["Win"], j, "nt_col", out_dtype=F32,
                      extras=[dz1, *below_extras], **below)
        else:
            rows[("bpw2", j)] = dz1sum
            GW["Wp2"] = _mm_tn(GW["Wp2"], sv["s"], dz1, j, "rows", name=f"b{j}_pw2_dw")
            dc, dlg, dlb_, dcs = _mm(dz1, W["Wp2"], j, "nt_row", out_dtype=F32, name=f"b{j}_pw2_dx",
                                     outs=[(D, F32)], sums=[D, D, D], epi=_silu_ln_bwd_tile,
                                     extras=[sv["cv"], blng[j], blnb[j]])
            rows[("blng", j)], rows[("blnb", j)], rows[("bdw", j)] = dlg, dlb_, dcs
            du, dwdw, dbu = bwd(_conv_bwd, f"b{j}_conv_bwd", dc, sv["glu"], sv["u"], wdw[j])
            rows[("wdw", j)] = dwdw[:CONV_W]
            rows[("bpw1", j)] = dbu.reshape(2, D)
            GW["Wp1"] = _mm_tn(GW["Wp1"], sv["xin"], du, j, "cols", name=f"b{j}_pw1_dw")
            res = _mm(du, W["Wp1"], j, "nt_col", out_dtype=F32, name=f"b{j}_pw1_dx",
                      extras=[dz1, *below_extras], **below)
        if i > 0:
            dz2, dg, db, _ = res
            rows[("ffn_g", i - 1)], rows[("ffn_b", i - 1)] = dg, db
        else:
            dx = res
    grad_x = dx[None]

    order = ([("mix_g", i) for i in range(DEPTH)] + [("mix_b", i) for i in range(DEPTH)]
             + [("ffn_g", i) for i in range(DEPTH)] + [("ffn_b", i) for i in range(DEPTH)])
    lb_row0 = len(order)
    order += [("lb", j) for j in range(nA)] + [("ng", j) for j in range(nA)]
    for j in range(nB):
        order += [("bpw1", j), ("wdw", j), ("bdw", j), ("blng", j), ("blnb", j), ("bpw2", j)]
    pieces, offs, off = [], {}, 0
    for key in order:
        offs[key] = off
        pieces.append(rows[key])
        off += rows[key].shape[0]
    offs["loss"] = off
    pieces.append(loss_row)
    off += 1
    part = jnp.concatenate(pieces, axis=0)
    part = jnp.pad(part, ((0, -off % 8), (0, 0)))
    parts = _exchange(("gather", [(part[None], 0, jax.ShapeDtypeStruct((N_DEV, 1) + part.shape, F32))]),
                      "gather_small_grads")[0][:, 0]
    G = _small_reduce(parts, a_lb_logits, lb_row0)
    loss = jnp.sum(G[offs["loss"]])

    def rep(kind, n):
        return jnp.concatenate([G[offs[(kind, i)]:offs[(kind, i)] + 1] for i in range(n)], axis=0)

    def shard_cols(full, width):
        return lax.dynamic_slice_in_dim(full, me * width, width, axis=full.ndim - 1)

    g_small = {
        "ln_mix_g": rep("mix_g", DEPTH), "ln_mix_b": rep("mix_b", DEPTH),
        "ln_ffn_g": rep("ffn_g", DEPTH), "ln_ffn_b": rep("ffn_b", DEPTH),
        "a_lb_logits": rep("lb", nA), "a_norm_g": rep("ng", nA),
        "b_b_pw1": shard_cols(jnp.stack([G[offs[("bpw1", j)]:offs[("bpw1", j)] + 2].reshape(2 * D)
                                         for j in range(nB)]), 2 * D // N_DEV),
        "b_w_dw": shard_cols(jnp.stack([G[offs[("wdw", j)]:offs[("wdw", j)] + CONV_W] for j in range(nB)]),
                             D // N_DEV),
        "b_b_dw": shard_cols(rep("bdw", nB), D // N_DEV),
        "b_ln_g": shard_cols(rep("blng", nB), D // N_DEV),
        "b_ln_b": shard_cols(rep("blnb", nB), D // N_DEV),
        "b_b_pw2": shard_cols(rep("bpw2", nB), D // N_DEV),
    }
    small_w = {"ln_mix_g": (ln_mix_g, m_ln_mix_g, v_ln_mix_g), "ln_mix_b": (ln_mix_b, m_ln_mix_b, v_ln_mix_b),
               "ln_ffn_g": (ln_ffn_g, m_ln_ffn_g, v_ln_ffn_g), "ln_ffn_b": (ln_ffn_b, m_ln_ffn_b, v_ln_ffn_b),
               "a_lb_logits": (a_lb_logits, m_a_lb_logits, v_a_lb_logits),
               "a_norm_g": (a_norm_g, m_a_norm_g, v_a_norm_g),
               "b_b_pw1": (b_b_pw1, m_b_b_pw1, v_b_b_pw1), "b_w_dw": (b_w_dw, m_b_w_dw, v_b_w_dw),
               "b_b_dw": (b_b_dw, m_b_b_dw, v_b_b_dw), "b_ln_g": (b_ln_g, m_b_ln_g, v_b_ln_g),
               "b_ln_b": (b_ln_b, m_b_ln_b, v_b_ln_b), "b_b_pw2": (b_b_pw2, m_b_b_pw2, v_b_b_pw2)}
    snames = list(small_w)
    sshapes = [small_w[k][0].shape for k in snames]
    pg = _pack([g_small[k] for k in snames], 1024)
    pw, pm, pv = [_pack([small_w[k][q] for k in snames], 1024) for q in range(3)]
    sres = _adamw(pg[None], pw, pm, pv, "adamw_small")
    sres = [dict(zip(snames, _unpack(r, sshapes))) for r in sres]

    recv = [R[n] for n in ["W1", "W2", "Win", "Wout", "Wp1", "Wp2"]]
    big = {}
    for nm, rv, (w, m, v) in zip(
            ["ffn_w1", "ffn_w2", "a_w_in", "a_w_out", "b_w_pw1", "b_w_pw2"], recv,
            [(ffn_w1, m_ffn_w1, v_ffn_w1), (ffn_w2, m_ffn_w2, v_ffn_w2), (a_w_in, m_a_w_in, v_a_w_in),
             (a_w_out, m_a_w_out, v_a_w_out), (b_w_pw1, m_b_w_pw1, v_b_w_pw1), (b_w_pw2, m_b_w_pw2, v_b_w_pw2)]):
        big[nm] = _adamw(rv, w, m, v, f"adamw_{nm}")

    names = ["ln_mix_g", "ln_mix_b", "ln_ffn_g", "ln_ffn_b", "ffn_w1", "ffn_w2", "a_w_in", "a_lb_logits",
             "a_norm_g", "a_w_out", "b_w_pw1", "b_b_pw1", "b_w_dw", "b_b_dw", "b_ln_g", "b_ln_b", "b_w_pw2",
             "b_b_pw2"]
    out = [loss, grad_x]
    for q in range(4):
        for nm in names:
            out.append(big[nm][q] if nm in big else sres[q][nm])
    return tuple(out)
```

```python
import functools

import numpy as np
import jax
import jax.numpy as jnp
from jax import lax
from jax.experimental import pallas as pl
from jax.experimental.pallas import tpu as pltpu

F32 = jnp.float32
BF = jnp.bfloat16

N_DEV = 8
DEPTH = 4
HEAD = 128
CHUNK = 64
CHUNK_BWD = 64
SUB = 8
CONV_W = 31
HALO = 32
CONV_ROWS = 128
ALPHA = (2.0 * DEPTH) ** 0.25
LN_EPS = 1e-5
RMS_EPS = 1e-6
GATE_EPS = 1e-6
ADAM_LR = 0.001
ADAM_B1 = 0.9
ADAM_B2 = 0.999
ADAM_EPS = 1e-08
ADAM_WD = 0.01
ADAM_STEP = 10
VMEM_LIMIT = 56 * 1024 * 1024
MESH = pl.DeviceIdType.MESH


def _cp(sem=None):
    return pltpu.CompilerParams(vmem_limit_bytes=VMEM_LIMIT, dimension_semantics=sem)


def _dot(a, b):
    return jnp.dot(a, b, preferred_element_type=F32)


def _dot_nt(a, b):
    return lax.dot_general(a, b, (((1,), (1,)), ((), ())), preferred_element_type=F32)


def _dot_tn(a, b):
    return lax.dot_general(a, b, (((0,), (0,)), ((), ())), preferred_element_type=F32)


def _sig(x):
    return jax.nn.sigmoid(x)


def _distinct(items):
    out = []
    for it in items:
        if not any(it is q for q in out):
            out.append(it)
    return out


def _index_of(items, it):
    return next(i for i, q in enumerate(items) if q is it)


def _comm_copies(mode, plan, src_refs, buf_refs, send_sems, recv_sems, loc_sems):
    x, y, c = lax.axis_index("x"), lax.axis_index("y"), lax.axis_index("c")
    me = 4 * x + 2 * y + c
    locs, rems = [], []
    for k, (si, l, bi) in enumerate(plan):
        src = src_refs[si].at[l] if mode == "gather" else src_refs[si].at[me, l]
        locs.append(pltpu.make_async_copy(src, buf_refs[bi].at[me, l], loc_sems.at[k]))
    for r in range(1, N_DEV):
        px = (1 - x) if (r >> 2) & 1 else x
        py = (1 - y) if (r >> 1) & 1 else y
        pc = (1 - c) if r & 1 else c
        pid = 4 * px + 2 * py + pc
        for k, (si, l, bi) in enumerate(plan):
            src = src_refs[si].at[l] if mode == "gather" else src_refs[si].at[pid, l]
            rems.append(pltpu.make_async_remote_copy(
                src_ref=src, dst_ref=buf_refs[bi].at[me, l],
                send_sem=send_sems.at[k, r - 1], recv_sem=recv_sems.at[k, r - 1],
                device_id=(px, py, pc), device_id_type=MESH))
    return locs, rems


def _call(body, *, name, grid, operands, in_specs, out_shape, out_specs, scratch=(), sem=None,
          aliases=None, comm=None):
    aliases = dict(aliases or {})
    if comm is None:
        return pl.pallas_call(
            body, name=name, grid=grid, out_shape=list(out_shape), in_specs=list(in_specs),
            out_specs=list(out_specs), scratch_shapes=list(scratch), input_output_aliases=aliases,
            compiler_params=_cp(sem),
        )(*operands)
    mode, pieces = comm
    srcs = _distinct([p[0] for p in pieces])
    bufs = _distinct([p[2] for p in pieces])
    plan = [(_index_of(srcs, s), l, _index_of(bufs, b)) for (s, l, b) in pieces]
    n_in, n_out, n_scr, ns, nb, npc = len(operands), len(out_shape), len(scratch), len(srcs), len(bufs), len(plan)
    nsteps = grid[0]
    old = [b for b in bufs if not isinstance(b, jax.ShapeDtypeStruct)]
    nbi = len(old)

    def wrapped(*refs):
        ins = refs[:n_in]
        src_refs = refs[n_in:n_in + ns]
        o0 = n_in + ns + nbi
        outs = refs[o0:o0 + n_out]
        buf_refs = refs[o0 + n_out:o0 + n_out + nb]
        s0 = o0 + n_out + nb
        scr = refs[s0:s0 + n_scr]
        sems = refs[s0 + n_scr:]
        step = pl.program_id(0)

        @pl.when(step == 0)
        def _():
            locs, rems = _comm_copies(mode, plan, src_refs, buf_refs, *sems)
            for d in locs + rems:
                d.start()

        body(*ins, *outs, *scr)

        @pl.when(step == nsteps - 1)
        def _():
            locs, rems = _comm_copies(mode, plan, src_refs, buf_refs, *sems)
            for d in rems:
                d.wait_send()
            for d in rems:
                d.wait_recv()
            for d in locs:
                d.wait()

    anyspec = pl.BlockSpec(memory_space=pl.ANY)
    for k, b in enumerate(old):
        aliases[n_in + ns + k] = n_out + _index_of(bufs, b)
    res = pl.pallas_call(
        wrapped, name=name, grid=grid,
        out_shape=list(out_shape) + [jax.ShapeDtypeStruct(b.shape, b.dtype) for b in bufs],
        in_specs=list(in_specs) + [anyspec] * (ns + nbi),
        out_specs=list(out_specs) + [anyspec] * nb,
        scratch_shapes=list(scratch) + [pltpu.SemaphoreType.DMA((npc, N_DEV - 1)),
                                        pltpu.SemaphoreType.DMA((npc, N_DEV - 1)),
                                        pltpu.SemaphoreType.DMA((npc,))],
        input_output_aliases=aliases,
        compiler_params=pltpu.CompilerParams(vmem_limit_bytes=VMEM_LIMIT, has_side_effects=True,
                                             dimension_semantics=("arbitrary",) * len(grid)),
    )(*operands, *srcs, *old)
    return list(res[:n_out]), list(res[n_out:])


def _exchange(comm, name):
    def body():
        pass

    return _call(body, name=name, grid=(1,), operands=[], in_specs=[], out_shape=[], out_specs=[],
                 comm=comm)[1]


def _mm(a, w, l, kind, *, out_dtype, name, tm=512, a_fn=None, epi=None, extras=(), comm=None,
        outs=None, sums=()):
    T, Ka = a.shape
    _, _, d2, d3 = w.shape
    n_out = {"nn_col": N_DEV * d3, "nn_row": d3, "nt_col": d2, "nt_row": N_DEV * d2}[kind]
    tm = min(tm, T)
    ne = len(extras)
    single = outs is None and not sums
    outs = [(n_out, out_dtype)] if outs is None else outs
    no, ns = len(outs), len(sums)

    def body(*refs):
        a_ref, w_ref = refs[0], refs[1]
        e_refs = refs[2:2 + ne]
        o_refs = refs[2 + ne:2 + ne + no]
        s_refs = refs[2 + ne + no:]
        av = a_ref[...]
        if a_fn is not None:
            av = a_fn(av)
        av = av.astype(BF)
        if kind == "nn_col":
            acc = jnp.concatenate([_dot(av, w_ref[j]) for j in range(N_DEV)], axis=1)
        elif kind == "nn_row":
            acc = _dot(av, w_ref[...].reshape(N_DEV * d2, d3))
        elif kind == "nt_col":
            acc = _dot_nt(av[:, 0:d3], w_ref[0])
            for j in range(1, N_DEV):
                acc = acc + _dot_nt(av[:, j * d3:(j + 1) * d3], w_ref[j])
        else:
            acc = jnp.concatenate([_dot_nt(av, w_ref[j]) for j in range(N_DEV)], axis=1)
        res = acc if epi is None else epi(acc, *[e[...] for e in e_refs])
        res = res if isinstance(res, tuple) else (res,)
        for o_ref, val in zip(o_refs, res[:no]):
            o_ref[...] = val.astype(o_ref.dtype)
        if ns:
            first = pl.program_id(0) == 0

            @pl.when(first)
            def _():
                for s_ref, val in zip(s_refs, res[no:]):
                    s_ref[...] = val

            @pl.when(jnp.logical_not(first))
            def _():
                for s_ref, val in zip(s_refs, res[no:]):
                    s_ref[...] += val

    in_specs = [pl.BlockSpec((tm, Ka), lambda i: (i, 0)),
                pl.BlockSpec((N_DEV, None, d2, d3), lambda i: (0, l, 0, 0))]
    extras = [e if isinstance(e, tuple) else (e, 0) for e in extras]
    for e, cb in extras:
        if e.shape[0] == 1:
            in_specs.append(pl.BlockSpec((1, n_out), lambda i: (0, 0)))
        else:
            in_specs.append(pl.BlockSpec((tm, n_out), functools.partial(lambda i, cb: (i, cb), cb=cb)))
    extras = [e for e, _ in extras]
    out_shape = [jax.ShapeDtypeStruct((T, wd), dt) for wd, dt in outs]
    out_shape += [jax.ShapeDtypeStruct((1, wd), F32) for wd in sums]
    out_specs = [pl.BlockSpec((tm, wd), lambda i: (i, 0)) for wd, _ in outs]
    out_specs += [pl.BlockSpec((1, wd), lambda i: (0, 0)) for wd in sums]
    res = _call(body, name=name, grid=(T // tm,), operands=[a, w, *extras], in_specs=in_specs,
                out_shape=out_shape, out_specs=out_specs,
                sem=("arbitrary",) if ns else ("parallel",), comm=comm)
    if comm is None:
        return res[0] if single else res
    return (res[0][0] if single else res[0]), res[1]


def _mm_tn(g, s, b, l, kind, *, name, tt=1024, b_fn=None):
    T, ws = s.shape
    wb = b.shape[1]
    _, _, d2, d3 = g.shape
    tt = min(tt, T)
    nsteps = T // tt
    cw = wb // N_DEV

    fresh = isinstance(g, jax.ShapeDtypeStruct)

    def body(*refs):
        s_ref, b_ref, o_ref, acc, stage, sem = refs if fresh else refs[1:]
        t = pl.program_id(0)

        @pl.when(t == 0)
        def _():
            acc[...] = jnp.zeros((ws, wb), F32)

        st = s_ref[...].astype(F32).T.astype(BF)
        for j in range(N_DEV):
            bv = b_ref[:, j * cw:(j + 1) * cw]
            if b_fn is not None:
                bv = b_fn(bv)
            acc[:, j * cw:(j + 1) * cw] += _dot(st, bv.astype(BF))

        @pl.when(t == nsteps - 1)
        def _():
            for j in range(N_DEV):
                if kind == "cols":
                    blk = acc[:, j * d3:(j + 1) * d3]
                elif kind == "rows":
                    blk = acc[j * d2:(j + 1) * d2, :]
                else:
                    blk = acc[:, j * d2:(j + 1) * d2].T
                stage[...] = blk.astype(BF)
                cp = pltpu.make_async_copy(stage, o_ref.at[j, l], sem)
                cp.start()
                cp.wait()

    anyspec = pl.BlockSpec(memory_space=pl.ANY)
    return pl.pallas_call(
        body, name=name, grid=(nsteps,),
        out_shape=jax.ShapeDtypeStruct(g.shape, BF),
        in_specs=([] if fresh else [anyspec]) + [pl.BlockSpec((tt, ws), lambda t: (t, 0)),
                                                 pl.BlockSpec((tt, wb), lambda t: (t, 0))],
        out_specs=anyspec,
        scratch_shapes=[pltpu.VMEM((ws, wb), F32), pltpu.VMEM((d2, d3), BF), pltpu.SemaphoreType.DMA],
        input_output_aliases={} if fresh else {0: 0},
        compiler_params=_cp(("arbitrary",)),
    )(*([] if fresh else [g]), s, b)


def _ln_stats(z):
    mu = jnp.mean(z, axis=-1, keepdims=True)
    zc = z - mu
    var = jnp.mean(zc * zc, axis=-1, keepdims=True)
    rstd = lax.rsqrt(var + LN_EPS)
    return zc * rstd, rstd


def _ln_bwd_core(dy, zh, rstd, g):
    dzh = dy * g
    m1 = jnp.mean(dzh, axis=-1, keepdims=True)
    m2 = jnp.mean(dzh * zh, axis=-1, keepdims=True)
    return rstd * (dzh - m1 - zh * m2)


def _colsum(v):
    return jnp.sum(v, axis=0, keepdims=True)


def _z_and_ln(z, g, b):
    zh, _ = _ln_stats(z)
    return z, zh * g + b


def _ln_bwd_tile(dy, z, g):
    zh, rstd = _ln_stats(z)
    dz = _ln_bwd_core(dy, zh, rstd, g)
    return dz, _colsum(dy * zh), _colsum(dy), _colsum(dz)


def _heads(D):
    return D // HEAD


def _rms_parts(o, D):
    xs, rs = [], []
    for h in range(_heads(D)):
        oh = o[:, h * HEAD:(h + 1) * HEAD]
        r = lax.rsqrt(jnp.mean(oh * oh, axis=-1, keepdims=True) + RMS_EPS)
        xs.append(oh * r)
        rs.append(r)
    return xs, rs


def _gate_bwd_tile(dyt, ot, grt, ngr):
    D = ot.shape[1]
    xs, rs = _rms_parts(ot, D)
    xh = jnp.concatenate(xs, axis=1)
    sg = _sig(grt)
    on = xh * ngr
    dgr = dyt * on * (sg * (1.0 + grt * (1.0 - sg)))
    don = dyt * (grt * sg)
    dxh = don * ngr
    dos = []
    for h in range(_heads(D)):
        sl = slice(h * HEAD, (h + 1) * HEAD)
        m = jnp.mean(dxh[:, sl] * xs[h], axis=-1, keepdims=True)
        dos.append(rs[h] * (dxh[:, sl] - xs[h] * m))
    return jnp.concatenate(dos, axis=1), dgr, _colsum(don * xh)


def _u_and_glu(u):
    D = u.shape[1] // 2
    return u, u[:, :D] * _sig(u[:, D:])


def _silu_ln_bwd_tile(dst, ct, gr, br):
    zh, rstd = _ln_stats(ct)
    ln = zh * gr + br
    sg = _sig(ln)
    dln = dst * (sg * (1.0 + ln * (1.0 - sg)))
    dc = _ln_bwd_core(dln, zh, rstd, gr)
    return dc, _colsum(dln * zh), _colsum(dln), _colsum(dc)


def _top_tile(z, g, b, tgt):
    D = z.shape[1]
    zh, rstd = _ln_stats(z)
    e = zh * g + b - tgt
    dy = e * (1.0 / D)
    dz = _ln_bwd_core(dy, zh, rstd, g)
    return dz, _colsum(dy * zh), _colsum(dy), _colsum(e * e) * (0.5 / D)


def _shifted(ext, n, tr):
    for b in range(8):
        rb = ext if b == 0 else pltpu.roll(ext, n - b, 0)
        for a in range(HALO // 8 + 1):
            o = 8 * a + b
            if o <= HALO:
                yield o, rb[8 * a:8 * a + tr, :]


def _lane_strips(D):
    return [slice(c, c + 128) for c in range(0, D, 128)]


def _conv_fwd(glu, w, bdw, g, b, name, tr=256, comm=None):
    T, D = glu.shape
    tr = min(tr, T // 2)
    rt = min(CONV_ROWS, tr)
    hb = tr // HALO

    def body(cur_ref, halo_ref, w_ref, bdw_ref, g_ref, b_ref, c_ref, s_ref):
        i = pl.program_id(0)
        for cs in _lane_strips(D):
            halo = jnp.where(i > 0, halo_ref[:, cs], 0.0)
            for r0 in range(0, tr, rt):
                if r0 == 0:
                    ext = jnp.concatenate([halo, cur_ref[0:rt, cs]], axis=0)
                else:
                    ext = cur_ref[r0 - HALO:r0 + rt, cs]
                acc = None
                for o, sh in _shifted(ext, rt + HALO, rt):
                    k = o - (HALO - CONV_W + 1)
                    if 0 <= k < CONV_W:
                        term = w_ref[k:k + 1, cs] * sh
                        acc = term if acc is None else acc + term
                c_ref[r0:r0 + rt, cs] = acc + bdw_ref[:, cs]
        cv = c_ref[...]
        zh, _ = _ln_stats(cv)
        ln = zh * g_ref[...] + b_ref[...]
        s_ref[...] = (ln * _sig(ln)).astype(BF)

    row = pl.BlockSpec((1, D), lambda i: (0, 0))
    return _call(
        body, name=name, grid=(T // tr,), operands=[glu, glu, w, bdw, g, b],
        out_shape=[jax.ShapeDtypeStruct((T, D), F32), jax.ShapeDtypeStruct((T, D), BF)],
        in_specs=[pl.BlockSpec((tr, D), lambda i: (i, 0)),
                  pl.BlockSpec((HALO, D), lambda i: (jnp.maximum(i * hb - 1, 0), 0)),
                  pl.BlockSpec((HALO, D), lambda i: (0, 0)), row, row, row],
        out_specs=[pl.BlockSpec((tr, D), lambda i: (i, 0))] * 2,
        sem=("parallel",), comm=comm)


def _conv_bwd(dc, glu, u, w, name, tr=256, comm=None):
    T, D = glu.shape
    tr = min(tr, T // 2)
    rt = min(CONV_ROWS, tr)
    hb = tr // HALO
    nsteps = T // tr
    last_hb = T // HALO - 1

    def body(dc_ref, nxt_ref, glu_ref, prv_ref, ua_ref, ug_ref, w_ref, du_ref, dw_ref, dbu_ref):
        i = pl.program_id(0)

        @pl.when(i == 0)
        def _():
            dw_ref[...] = jnp.zeros((HALO, D), F32)
            dbu_ref[...] = jnp.zeros((1, 2 * D), F32)

        for cs in _lane_strips(D):
            nxt = jnp.where(i < nsteps - 1, nxt_ref[:, cs], 0.0)
            prv = jnp.where(i > 0, prv_ref[:, cs], 0.0)
            dws = [None] * CONV_W
            dba = dbg = None
            gs = slice(D + cs.start, D + cs.stop)
            for r0 in range(0, tr, rt):
                if r0 + rt == tr:
                    ext2 = jnp.concatenate([dc_ref[r0:tr, cs], nxt], axis=0)
                else:
                    ext2 = dc_ref[r0:r0 + rt + HALO, cs]
                acc = None
                for o, sh in _shifted(ext2, rt + HALO, rt):
                    k = CONV_W - 1 - o
                    if 0 <= k < CONV_W:
                        term = w_ref[k:k + 1, cs] * sh
                        acc = term if acc is None else acc + term
                sg = _sig(ug_ref[r0:r0 + rt, cs])
                da = acc * sg
                dgt = acc * ua_ref[r0:r0 + rt, cs] * sg * (1.0 - sg)
                du_ref[r0:r0 + rt, cs] = da.astype(BF)
                du_ref[r0:r0 + rt, gs] = dgt.astype(BF)
                dba = _colsum(da) if dba is None else dba + _colsum(da)
                dbg = _colsum(dgt) if dbg is None else dbg + _colsum(dgt)
                if r0 == 0:
                    ext = jnp.concatenate([prv, glu_ref[0:rt, cs]], axis=0)
                else:
                    ext = glu_ref[r0 - HALO:r0 + rt, cs]
                dcs = dc_ref[r0:r0 + rt, cs]
                for o, sh in _shifted(ext, rt + HALO, rt):
                    k = o - (HALO - CONV_W + 1)
                    if 0 <= k < CONV_W:
                        part = _colsum(dcs * sh)
                        dws[k] = part if dws[k] is None else dws[k] + part
            for k in range(CONV_W):
                dw_ref[k:k + 1, cs] += dws[k]
            dbu_ref[:, cs] += dba
            dbu_ref[:, gs] += dbg

    return _call(
        body, name=name, grid=(nsteps,), operands=[dc, dc, glu, glu, u, u, w],
        out_shape=[jax.ShapeDtypeStruct((T, 2 * D), BF), jax.ShapeDtypeStruct((HALO, D), F32),
                   jax.ShapeDtypeStruct((1, 2 * D), F32)],
        in_specs=[pl.BlockSpec((tr, D), lambda i: (i, 0)),
                  pl.BlockSpec((HALO, D), lambda i: (jnp.minimum((i + 1) * hb, last_hb), 0)),
                  pl.BlockSpec((tr, D), lambda i: (i, 0)),
                  pl.BlockSpec((HALO, D), lambda i: (jnp.maximum(i * hb - 1, 0), 0)),
                  pl.BlockSpec((tr, D), lambda i: (i, 0)),
                  pl.BlockSpec((tr, D), lambda i: (i, 1)),
                  pl.BlockSpec((HALO, D), lambda i: (0, 0))],
        out_specs=[pl.BlockSpec((tr, 2 * D), lambda i: (i, 0)),
                   pl.BlockSpec((HALO, D), lambda i: (0, 0)),
                   pl.BlockSpec((1, 2 * D), lambda i: (0, 0))],
        sem=("arbitrary",), comm=comm)


def _rec_consts(C):
    t = np.arange(C)
    lb = (t[None, :] <= t[:, None]).astype(np.float32)

    def cum_at(idx):
        return (t[None, :] <= idx[:, None]).astype(np.float32)

    blocks, masks = [], []
    mid = SUB * (t // SUB) + SUB // 2 - 1
    eq = lb - cum_at(mid)
    blocks += [eq, -eq]
    masks.append(((t[:, None] // SUB) == (t[None, :] // SUB)) & (t[None, :] <= t[:, None]))
    nb = SUB
    while nb < C:
        odd = (t // nb) % 2 == 1
        e_t = nb * (t // nb) - 1
        e_s = nb * (t // nb) + nb - 1
        blocks.append(np.where(odd[:, None], lb - cum_at(e_t), 0.0))
        blocks.append(np.where(~odd[:, None], cum_at(e_s) - lb, 0.0))
        masks.append(((t[:, None] // (2 * nb)) == (t[None, :] // (2 * nb))) & odd[:, None] & ~odd[None, :])
        nb *= 2
    blocks += [lb, 1.0 - lb, np.ones((HEAD, C), np.float32)]
    L = np.concatenate(blocks, axis=0).astype(np.float32)
    L3 = np.concatenate([L, L, L], axis=1)
    LT3 = np.concatenate([L.T, L.T, L.T], axis=1)
    m = np.stack(masks).astype(np.float32)
    mT = np.transpose(m, (0, 2, 1)).copy()
    return (jnp.asarray(L3, BF), jnp.asarray(LT3, BF), jnp.asarray(m), jnp.asarray(mT), len(masks))


def _split3(x):
    h = x.astype(BF)
    r = x - h.astype(F32)
    m = r.astype(BF)
    lo = (r - m.astype(F32)).astype(BF)
    return h, m, lo


def _gates(qr, fz, lbr):
    sq = _sig(qr)
    q = qr * sq
    sg = _sig(fz)
    f = lbr + (1.0 - lbr) * sg
    fc = jnp.maximum(f, GATE_EPS)
    return q, 1.0 - f, jnp.log(fc), sq, sg, f, fc


def _rec_fwd(proj, lbr, ng, name, comm=None):
    T, D4 = proj.shape
    D = D4 // 4
    H = _heads(D)
    C = CHUNK
    nC = T // C
    L3, _, m, _, nl = _rec_consts(C)
    R = L3.shape[0]

    def body(q_ref, f_ref, v_ref, gr_ref, lb_ref, ng_ref, l_ref, m_ref, o_ref, y_ref, s_ref, st):
        @pl.when(pl.program_id(0) == 0)
        def _():
            st[...] = jnp.zeros((H, HEAD, HEAD), F32)

        q, k, logf = _gates(q_ref[...], f_ref[...], lb_ref[...])[:3]
        ex = jnp.exp(_dot(l_ref[...], jnp.concatenate(_split3(logf), axis=0)))
        vb = v_ref[...].astype(BF)
        s_ref[0] = st[...]
        outs = []
        for h in range(H):
            sl = slice(h * HEAD, (h + 1) * HEAD)
            qh, kh = q[:, sl], k[:, sl]
            p = jnp.zeros((C, C), F32)
            for lv in range(nl):
                qt = (qh * ex[2 * lv * C:(2 * lv + 1) * C, sl]).astype(BF)
                kt = (kh * ex[(2 * lv + 1) * C:(2 * lv + 2) * C, sl]).astype(BF)
                p = p + jnp.where(m_ref[lv] > 0.0, _dot_nt(qt, kt), 0.0)
            base = 2 * nl * C
            qhat = (qh * ex[base:base + C, sl]).astype(BF)
            khat = (kh * ex[base + C:base + 2 * C, sl]).astype(BF)
            elast = ex[base + 2 * C:base + 2 * C + HEAD, sl]
            sth = st[h]
            outs.append(_dot(p.astype(BF), vb[:, sl]) + _dot_nt(qhat, sth.astype(BF)))
            st[h] = elast * sth + _dot_tn(vb[:, sl], khat)
        o_ref[...] = jnp.concatenate(outs, axis=1)
        xh = jnp.concatenate([oh * lax.rsqrt(jnp.mean(oh * oh, axis=-1, keepdims=True) + RMS_EPS) for oh in outs],
                             axis=1)
        gr = gr_ref[...]
        y_ref[...] = (xh * ng_ref[...] * (gr * _sig(gr))).astype(BF)

    def cblk(cb):
        return pl.BlockSpec((C, D), lambda i: (i, cb))

    return _call(
        body, name=name, grid=(nC,), operands=[proj, proj, proj, proj, lbr, ng, L3, m],
        out_shape=[jax.ShapeDtypeStruct((T, D), F32), jax.ShapeDtypeStruct((T, D), BF),
                   jax.ShapeDtypeStruct((nC, H, HEAD, HEAD), F32)],
        in_specs=[cblk(0), cblk(1), cblk(2), cblk(3), pl.BlockSpec((1, D), lambda i: (0, 0)),
                  pl.BlockSpec((1, D), lambda i: (0, 0)),
                  pl.BlockSpec(L3.shape, lambda i: (0, 0)), pl.BlockSpec(m.shape, lambda i: (0, 0, 0))],
        out_specs=[pl.BlockSpec((C, D), lambda i: (i, 0)), pl.BlockSpec((C, D), lambda i: (i, 0)),
                   pl.BlockSpec((1, H, HEAD, HEAD), lambda i: (i, 0, 0, 0))],
        scratch=[pltpu.VMEM((H, HEAD, HEAD), F32)], sem=("arbitrary",), comm=comm)


def _rec_bwd(proj, lbr, do, dgr, ssave, name, comm=None):
    T, D4 = proj.shape
    D = D4 // 4
    H = _heads(D)
    C = min(CHUNK_BWD, T // 2)
    nC = T // C
    stride = C // CHUNK
    L3, LT3, m, mT, nl = _rec_consts(C)

    def body(q_ref, f_ref, v_ref, lb_ref, do_ref, dgr_ref, s_ref, l_ref, lt_ref, m_ref, mt_ref,
             dp_ref, dlb_ref, dst):
        @pl.when(pl.program_id(0) == 0)
        def _():
            dst[...] = jnp.zeros((H, HEAD, HEAD), F32)
            dlb_ref[...] = jnp.zeros((1, D), F32)

        qr = q_ref[...]
        lbv = lb_ref[...]
        q, k, logf, sq, sg, f, fc = _gates(qr, f_ref[...], lbv)
        ex = jnp.exp(_dot(l_ref[...], jnp.concatenate(_split3(logf), axis=0)))
        vb = v_ref[...].astype(BF)
        dob = do_ref[...].astype(BF)
        base = 2 * nl * C
        de = [[] for _ in range(2 * nl + 3)]
        dqs, dks, dvs = [], [], []
        for h in range(H):
            sl = slice(h * HEAD, (h + 1) * HEAD)
            qh, kh, vh, doh = q[:, sl], k[:, sl], vb[:, sl], dob[:, sl]
            dp = _dot_nt(doh, vh)
            dpt = _dot_nt(vh, doh)
            sth = s_ref[0, h]
            dsth = dst[h]
            dsb = dsth.astype(BF)
            pt = jnp.zeros((C, C), F32)
            dq = jnp.zeros((C, HEAD), F32)
            dk = jnp.zeros((C, HEAD), F32)
            for lv in range(nl):
                exq = ex[2 * lv * C:(2 * lv + 1) * C, sl]
                exk = ex[(2 * lv + 1) * C:(2 * lv + 2) * C, sl]
                qt = qh * exq
                kt = kh * exk
                qtb, ktb = qt.astype(BF), kt.astype(BF)
                pt = pt + jnp.where(mt_ref[lv] > 0.0, _dot_nt(ktb, qtb), 0.0)
                dqt = _dot(jnp.where(m_ref[lv] > 0.0, dp, 0.0).astype(BF), ktb)
                dkt = _dot(jnp.where(mt_ref[lv] > 0.0, dpt, 0.0).astype(BF), qtb)
                dq = dq + dqt * exq
                dk = dk + dkt * exk
                de[2 * lv].append(dqt * qt)
                de[2 * lv + 1].append(dkt * kt)
            exb = ex[base:base + C, sl]
            exkh = ex[base + C:base + 2 * C, sl]
            elast = ex[base + 2 * C:base + 2 * C + HEAD, sl]
            qhat = qh * exb
            khat = kh * exkh
            dqh = _dot(doh, sth.astype(BF))
            dkh = _dot(vh, dsb)
            dq = dq + dqh * exb
            dk = dk + dkh * exkh
            de[2 * nl].append(dqh * qhat)
            de[2 * nl + 1].append(dkh * khat)
            de[2 * nl + 2].append(dsth * sth * elast)
            dvs.append(_dot(pt.astype(BF), doh) + _dot_nt(khat.astype(BF), dsb))
            dst[h] = elast * dsth + _dot_tn(doh, qhat.astype(BF))
            dqs.append(dq)
            dks.append(dk)
        de_all = jnp.concatenate([jnp.concatenate(b, axis=1) for b in de], axis=0)
        dlogf = _dot(lt_ref[...], jnp.concatenate(_split3(de_all), axis=0))
        dq = jnp.concatenate(dqs, axis=1)
        dk = jnp.concatenate(dks, axis=1)
        dv = jnp.concatenate(dvs, axis=1)
        ind = jnp.where(f > GATE_EPS, 1.0, jnp.where(f == GATE_EPS, 0.5, 0.0))
        df = dlogf * ind / fc - dk
        dfz = df * (1.0 - lbv) * sg * (1.0 - sg)
        dlb_ref[...] += _colsum(df * (1.0 - sg))
        dqr = dq * (sq * (1.0 + qr * (1.0 - sq)))
        dp_ref[...] = jnp.concatenate([dqr.astype(BF), dfz.astype(BF), dv.astype(BF), dgr_ref[...]], axis=1)

    def cblk(cb):
        return pl.BlockSpec((C, D), lambda i: (nC - 1 - i, cb))

    def whole(a):
        nd = a.ndim
        return pl.BlockSpec(a.shape, lambda i: (0,) * nd)

    return _call(
        body, name=name, grid=(nC,), operands=[proj, proj, proj, lbr, do, dgr, ssave, L3, LT3, m, mT],
        out_shape=[jax.ShapeDtypeStruct((T, D4), BF), jax.ShapeDtypeStruct((1, D), F32)],
        in_specs=[cblk(0), cblk(1), cblk(2), pl.BlockSpec((1, D), lambda i: (0, 0)),
                  pl.BlockSpec((C, D), lambda i: (nC - 1 - i, 0)),
                  pl.BlockSpec((C, D), lambda i: (nC - 1 - i, 0)),
                  pl.BlockSpec((1, H, HEAD, HEAD), lambda i: (stride * (nC - 1 - i), 0, 0, 0)),
                  whole(L3), whole(LT3), whole(m), whole(mT)],
        out_specs=[pl.BlockSpec((C, D4), lambda i: (nC - 1 - i, 0)),
                   pl.BlockSpec((1, D), lambda i: (0, 0))],
        scratch=[pltpu.VMEM((H, HEAD, HEAD), F32)], sem=("arbitrary",), comm=comm)


def _softmax_rows(lg_ref):
    n = lg_ref.shape[0]
    rows = [lg_ref[l:l + 1, :] for l in range(n)]
    mx = rows[0]
    for r in rows[1:]:
        mx = jnp.maximum(mx, r)
    es = [jnp.exp(r - mx) for r in rows]
    tot = es[0]
    for e in es[1:]:
        tot = tot + e
    return [e / tot for e in es]


def _lb_fwd(logits):
    n, D = logits.shape

    def body(lg_ref, o_ref):
        soft = _softmax_rows(lg_ref)
        acc = jnp.zeros((1, D), F32)
        o_ref[0:1, :] = acc
        for j in range(1, n):
            acc = acc + soft[j]
            o_ref[j:j + 1, :] = acc

    return pl.pallas_call(body, name="lb_fwd", out_shape=jax.ShapeDtypeStruct((n, D), F32))(logits)


def _small_reduce(parts, logits, lb_row0):
    _, R, D = parts.shape
    n = logits.shape[0]

    def body(p_ref, lg_ref, o_ref):
        acc = p_ref[0]
        for d in range(1, N_DEV):
            acc = acc + p_ref[d]
        o_ref[...] = acc
        soft = _softmax_rows(lg_ref)
        dlb = [o_ref[lb_row0 + j:lb_row0 + j + 1, :] for j in range(n)]
        dsoft = [jnp.zeros((1, D), F32)]
        for l in range(1, n):
            s = dlb[l]
            for j in range(l + 1, n):
                s = s + dlb[j]
            dsoft.append(s)
        dot = soft[0] * dsoft[0]
        for l in range(1, n):
            dot = dot + soft[l] * dsoft[l]
        for l in range(n):
            o_ref[lb_row0 + l:lb_row0 + l + 1, :] = soft[l] * (dsoft[l] - dot)

    return pl.pallas_call(body, name="small_reduce", out_shape=jax.ShapeDtypeStruct((R, D), F32))(parts, logits)


def _adam_math(g, w, m, v):
    m2 = ADAM_B1 * m + (1.0 - ADAM_B1) * g
    v2 = ADAM_B2 * v + (1.0 - ADAM_B2) * (g * g)
    mh = m2 / (1.0 - ADAM_B1 ** ADAM_STEP)
    vh = v2 / (1.0 - ADAM_B2 ** ADAM_STEP)
    delta = -ADAM_LR * (mh / (jnp.sqrt(vh) + ADAM_EPS) + ADAM_WD * w)
    return delta, m2, v2


def _adamw(recv, w, m, v, name):
    nsrc = recv.shape[0]
    shp = w.shape
    cols = shp[-1]
    rows = int(np.prod(shp[:-1]))
    r2 = recv.reshape(nsrc, rows, cols)
    tr = min(rows, max(8, (1 << 20) // (cols * nsrc)))
    while rows % tr:
        tr //= 2

    def body(r_ref, w_ref, m_ref, v_ref, g_ref, d_ref, nm_ref, nv_ref):
        g = r_ref[0].astype(F32)
        for s in range(1, nsrc):
            g = g + r_ref[s].astype(F32)
        delta, m2, v2 = _adam_math(g, w_ref[...], m_ref[...], v_ref[...])
        g_ref[...] = g
        d_ref[...] = delta
        nm_ref[...] = m2
        nv_ref[...] = v2

    blk = pl.BlockSpec((tr, cols), lambda i: (i, 0))
    outs = pl.pallas_call(
        body, name=name, grid=(rows // tr,),
        out_shape=[jax.ShapeDtypeStruct((rows, cols), F32)] * 4,
        in_specs=[pl.BlockSpec((nsrc, tr, cols), lambda i: (0, i, 0)), blk, blk, blk],
        out_specs=[blk] * 4,
        compiler_params=_cp(("parallel",)),
    )(r2, w.reshape(rows, cols), m.reshape(rows, cols), v.reshape(rows, cols))
    return [o.reshape(shp) for o in outs]


def _pack(arrs, lane):
    flat = jnp.concatenate([a.reshape(-1) for a in arrs])
    n = flat.shape[0]
    rows = -(-n // lane)
    rows = -(-rows // 8) * 8
    flat = jnp.pad(flat, (0, rows * lane - n))
    return flat.reshape(rows, lane)


def _unpack(packed, shapes):
    flat = packed.reshape(-1)
    out, off = [], 0
    for s in shapes:
        n = int(np.prod(s))
        out.append(flat[off:off + n].reshape(s))
        off += n
    return out


def kernel(x, ln_mix_g, ln_mix_b, ln_ffn_g, ln_ffn_b, ffn_w1, ffn_w2, a_w_in, a_lb_logits, a_norm_g, a_w_out, b_w_pw1, b_b_pw1, b_w_dw, b_b_dw, b_ln_g, b_ln_b, b_w_pw2, b_b_pw2, loss_target, m_ln_mix_g, m_ln_mix_b, m_ln_ffn_g, m_ln_ffn_b, m_ffn_w1, m_ffn_w2, m_a_w_in, m_a_lb_logits, m_a_norm_g, m_a_w_out, m_b_w_pw1, m_b_b_pw1, m_b_w_dw, m_b_b_dw, m_b_ln_g, m_b_ln_b, m_b_w_pw2, m_b_b_pw2, v_ln_mix_g, v_ln_mix_b, v_ln_ffn_g, v_ln_ffn_b, v_ffn_w1, v_ffn_w2, v_a_w_in, v_a_lb_logits, v_a_norm_g, v_a_w_out, v_b_w_pw1, v_b_b_pw1, v_b_w_dw, v_b_b_dw, v_b_ln_g, v_b_ln_b, v_b_w_pw2, v_b_b_pw2):
    T, D = x.shape[1], x.shape[2]
    nA, nB = a_w_in.shape[0], b_w_pw1.shape[0]
    me = 4 * lax.axis_index("x") + 2 * lax.axis_index("y") + lax.axis_index("c")
    xin = x[0]
    tgt = loss_target[0]

    small_names = [b_b_pw1, b_w_dw, b_b_dw, b_ln_g, b_ln_b, b_b_pw2]
    sp = _pack(small_names, 128)
    wsrc = {"W1": ffn_w1.astype(BF), "W2": ffn_w2.astype(BF), "Win": a_w_in.astype(BF),
            "Wout": a_w_out.astype(BF), "Wp1": b_w_pw1.astype(BF), "Wp2": b_w_pw2.astype(BF), "small": sp[None]}
    W = {k: jax.ShapeDtypeStruct((N_DEV,) + v.shape, v.dtype) for k, v in wsrc.items()}
    GW = {k: jax.ShapeDtypeStruct((N_DEV,) + v.shape, BF) for k, v in wsrc.items() if k != "small"}
    R = {k: jax.ShapeDtypeStruct(v.shape, BF) for k, v in GW.items()}

    def names_of(items):
        out = []
        for n, _ in items:
            if n not in out:
                out.append(n)
        return out

    def hosted(fn, key, sched, mode, src, bufs, *args, **kw):
        items = sched.get(key)
        if items is None:
            return fn(*args, name=key, **kw)
        comm = (mode, [(src[n], l, bufs[n]) for n, l in items])
        outs, new = fn(*args, name=key, comm=comm, **kw)
        for n, b in zip(names_of(items), new):
            bufs[n] = b
        return outs

    first = [("Win", 0), ("small", 0)]
    for n, b in zip(names_of(first), _exchange(("gather", [(wsrc[n], l, W[n]) for n, l in first]), "gather_first")):
        W[n] = b
    fwd_sched = {"a0_proj": [("Wout", 0), ("Wp1", 0)], "a0_rec_fwd": [("W1", 0), ("W2", 0), ("Wp2", 0)],
                 "l0_ffn_up": [("W2", 1)], "l0_ffn_down": [("W1", 1)],
                 "b0_conv": [("Win", 1), ("Wout", 1)], "l1_ffn_up": [("W2", 2)], "l1_ffn_down": [("W1", 2)],
                 "a1_rec_fwd": [("Wp1", 1), ("Wp2", 1), ("W1", 3), ("W2", 3)]}
    bwd_sched = {"a1_rec_bwd": [("W1", 3), ("W2", 3), ("Wp1", 1), ("Wp2", 1), ("W1", 2), ("W2", 2)],
                 "b0_conv_bwd": [("Win", 1), ("Wout", 1)],
                 "a0_rec_bwd": [("W1", 1), ("W2", 1), ("Wp1", 0), ("Wp2", 0), ("W1", 0), ("W2", 0), ("Wout", 0)],
                 "a0_proj_dx": [("Win", 0)]}

    def fwd(fn, key, *args, **kw):
        return hosted(fn, key, fwd_sched, "gather", wsrc, W, *args, **kw)

    def bwd(fn, key, *args, **kw):
        return hosted(fn, key, bwd_sched, "scatter", GW, R, *args, **kw)

    spg = W["small"][:, 0]
    sm = [jnp.stack(p) for p in zip(*[_unpack(spg[d], [a.shape for a in small_names]) for d in range(N_DEV)])]
    bpw1 = jnp.transpose(sm[0], (1, 0, 2)).reshape(nB, 1, 2 * D)
    wdw = jnp.transpose(sm[1], (1, 2, 0, 3)).reshape(nB, CONV_W, D)
    wdw = jnp.pad(wdw, ((0, 0), (0, HALO - CONV_W), (0, 0)))
    bdw, blng, blnb, bpw2 = [jnp.transpose(s, (1, 0, 2)).reshape(nB, 1, D) for s in sm[2:]]
    lb_all = _lb_fwd(a_lb_logits)

    zx = [(D, F32), (D, F32)]
    ln_out = dict(outs=[(D, F32)], sums=[D, D, D])
    ln_mix = [(ln_mix_g[i:i + 1], ln_mix_b[i:i + 1]) for i in range(DEPTH)]
    ln_ffn = [(ln_ffn_g[i:i + 1], ln_ffn_b[i:i + 1]) for i in range(DEPTH)]
    saved = []
    h = xin
    for i in range(DEPTH):
        j = i // 2
        sv = {"xin": h}
        if i % 2 == 0:
            proj = fwd(_mm, f"a{j}_proj", h, W["Win"], j, "nn_col", out_dtype=F32)
            o, yg, ssave = fwd(_rec_fwd, f"a{j}_rec_fwd", proj, lb_all[j:j + 1], a_norm_g[j:j + 1])
            z1, x1 = _mm(yg, W["Wout"], j, "nn_row", out_dtype=F32, name=f"a{j}_out", outs=zx,
                         epi=lambda acc, r, g, b: _z_and_ln(acc + ALPHA * r, g, b), extras=[h, *ln_mix[i]])
            sv.update(proj=proj, o=o, ssave=ssave, yg=yg)
        else:
            u, glu = _mm(h, W["Wp1"], j, "nn_col", out_dtype=F32, name=f"b{j}_pw1",
                         outs=[(2 * D, F32), (D, F32)],
                         epi=lambda acc, b: _u_and_glu(acc + b), extras=[bpw1[j]])
            cv, s = fwd(_conv_fwd, f"b{j}_conv", glu, wdw[j], bdw[j], blng[j], blnb[j])
            z1, x1 = _mm(s, W["Wp2"], j, "nn_row", out_dtype=F32, name=f"b{j}_pw2", outs=zx,
                         epi=lambda acc, bb, r, g, b: _z_and_ln(acc + bb + ALPHA * r, g, b),
                         extras=[bpw2[j], h, *ln_mix[i]])
            sv.update(u=u, glu=glu, cv=cv, s=s)
        hh = fwd(_mm, f"l{i}_ffn_up", x1, W["W1"], i, "nn_col", out_dtype=BF)
        relu2 = lambda t: jnp.square(jnp.maximum(t, 0))
        if i < DEPTH - 1:
            z2, h = fwd(_mm, f"l{i}_ffn_down", hh, W["W2"], i, "nn_row", out_dtype=F32, outs=zx, a_fn=relu2,
                        epi=lambda acc, r, g, b: _z_and_ln(acc + ALPHA * r, g, b), extras=[x1, *ln_ffn[i]])
        else:
            z2 = None
            dz2, dg_top, db_top, loss_row = fwd(
                _mm, f"l{i}_ffn_down", hh, W["W2"], i, "nn_row", out_dtype=F32, a_fn=relu2, **ln_out,
                epi=lambda acc, r, g, b, t: _top_tile(acc + ALPHA * r, g, b, t), extras=[x1, *ln_ffn[i], tgt])
        sv.update(z1=z1, x1=x1, hh=hh, z2=z2)
        saved.append(sv)

    rows = {}
    top = DEPTH - 1
    rows[("ffn_g", top)], rows[("ffn_b", top)] = dg_top, db_top
    for i in reversed(range(DEPTH)):
        j = i // 2
        sv = saved[i]
        if i > 0:
            below = dict(epi=lambda acc, r, z, g: _ln_bwd_tile(acc + ALPHA * r, z, g), **ln_out)
            below_extras = [saved[i - 1]["z2"], ln_ffn[i - 1][0]]
        else:
            below = dict(epi=lambda acc, r: acc + ALPHA * r)
            below_extras = []
        dh = _mm(dz2, W["W2"], i, "nt_row", out_dtype=BF, name=f"l{i}_ffn_down_dx",
                 epi=lambda acc, hv: acc * (2.0 * jnp.maximum(hv.astype(F32), 0.0)), extras=[sv["hh"]])
        GW["W2"] = _mm_tn(GW["W2"], dz2, sv["hh"], i, "rows_t", name=f"l{i}_ffn_down_dw",
                         b_fn=lambda t: jnp.square(jnp.maximum(t, 0)))
        GW["W1"] = _mm_tn(GW["W1"], sv["x1"], dh, i, "cols", name=f"l{i}_ffn_up_dw")
        dz1, dg, db, dz1sum = _mm(dh, W["W1"], i, "nt_col", out_dtype=F32, name=f"l{i}_ffn_up_dx",
                                  epi=lambda acc, r, z, g: _ln_bwd_tile(acc + ALPHA * r, z, g),
                                  extras=[dz2, sv["z1"], ln_mix[i][0]], **ln_out)
        rows[("mix_g", i)], rows[("mix_b", i)] = dg, db
        if i % 2 == 0:
            GW["Wout"] = _mm_tn(GW["Wout"], sv["yg"], dz1, j, "rows", name=f"a{j}_out_dw")
            do, dgr, dng = _mm(dz1, W["Wout"], j, "nt_row", out_dtype=F32, name=f"a{j}_out_dx",
                               outs=[(D, F32), (D, BF)], sums=[D], epi=_gate_bwd_tile,
                               extras=[sv["o"], (sv["proj"], 3), a_norm_g[j:j + 1]])
            rows[("ng", j)] = dng
            dproj, dlb = bwd(_rec_bwd, f"a{j}_rec_bwd", sv["proj"], lb_all[j:j + 1], do, dgr, sv["ssave"])
            rows[("lb", j)] = dlb
            GW["Win"] = _mm_tn(GW["Win"], sv["xin"], dproj, j, "cols", name=f"a{j}_proj_dw")
            res = bwd(_mm, f"a{j}_proj_dx", dproj, W["Win"], j, "nt_col", out_dtype=F32,
                      extras=[dz1, *below_extras], **below)
        else:
            rows[("bpw2", j)] = dz1sum
            GW["Wp2"] = _mm_tn(GW["Wp2"], sv["s"], dz1, j, "rows", name=f"b{j}_pw2_dw")
            dc, dlg, dlb_, dcs = _mm(dz1, W["Wp2"], j, "nt_row", out_dtype=F32, name=f"b{j}_pw2_dx",
                                     outs=[(D, F32)], sums=[D, D, D], epi=_silu_ln_bwd_tile,
                                     extras=[sv["cv"], blng[j], blnb[j]])
            rows[("blng", j)], rows[("blnb", j)], rows[("bdw", j)] = dlg, dlb_, dcs
            du, dwdw, dbu = bwd(_conv_bwd, f"b{j}_conv_bwd", dc, sv["glu"], sv["u"], wdw[j])
            rows[("wdw", j)] = dwdw[:CONV_W]
            rows[("bpw1", j)] = dbu.reshape(2, D)
            GW["Wp1"] = _mm_tn(GW["Wp1"], sv["xin"], du, j, "cols", name=f"b{j}_pw1_dw")
            res = _mm(du, W["Wp1"], j, "nt_col", out_dtype=F32, name=f"b{j}_pw1_dx",
                      extras=[dz1, *below_extras], **below)
        if i > 0:
            dz2, dg, db, _ = res
            rows[("ffn_g", i - 1)], rows[("ffn_b", i - 1)] = dg, db
        else:
            dx = res
    grad_x = dx[None]

    order = ([("mix_g", i) for i in range(DEPTH)] + [("mix_b", i) for i in range(DEPTH)]
             + [("ffn_g", i) for i in range(DEPTH)] + [("ffn_b", i) for i in range(DEPTH)])
    lb_row0 = len(order)
    order += [("lb", j) for j in range(nA)] + [("ng", j) for j in range(nA)]
    for j in range(nB):
        order += [("bpw1", j), ("wdw", j), ("bdw", j), ("blng", j), ("blnb", j), ("bpw2", j)]
    pieces, offs, off = [], {}, 0
    for key in order:
        offs[key] = off
        pieces.append(rows[key])
        off += rows[key].shape[0]
    offs["loss"] = off
    pieces.append(loss_row)
    off += 1
    part = jnp.concatenate(pieces, axis=0)
    part = jnp.pad(part, ((0, -off % 8), (0, 0)))
    parts = _exchange(("gather", [(part[None], 0, jax.ShapeDtypeStruct((N_DEV, 1) + part.shape, F32))]),
                      "gather_small_grads")[0][:, 0]
    G = _small_reduce(parts, a_lb_logits, lb_row0)
    loss = jnp.sum(G[offs["loss"]])

    def rep(kind, n):
        return jnp.concatenate([G[offs[(kind, i)]:offs[(kind, i)] + 1] for i in range(n)], axis=0)

    def shard_cols(full, width):
        return lax.dynamic_slice_in_dim(full, me * width, width, axis=full.ndim - 1)

    g_small = {
        "ln_mix_g": rep("mix_g", DEPTH), "ln_mix_b": rep("mix_b", DEPTH),
        "ln_ffn_g": rep("ffn_g", DEPTH), "ln_ffn_b": rep("ffn_b", DEPTH),
        "a_lb_logits": rep("lb", nA), "a_norm_g": rep("ng", nA),
        "b_b_pw1": shard_cols(jnp.stack([G[offs[("bpw1", j)]:offs[("bpw1", j)] + 2].reshape(2 * D)
                                         for j in range(nB)]), 2 * D // N_DEV),
        "b_w_dw": shard_cols(jnp.stack([G[offs[("wdw", j)]:offs[("wdw", j)] + CONV_W] for j in range(nB)]),
                             D // N_DEV),
        "b_b_dw": shard_cols(rep("bdw", nB), D // N_DEV),
        "b_ln_g": shard_cols(rep("blng", nB), D // N_DEV),
        "b_ln_b": shard_cols(rep("blnb", nB), D // N_DEV),
        "b_b_pw2": shard_cols(rep("bpw2", nB), D // N_DEV),
    }
    small_w = {"ln_mix_g": (ln_mix_g, m_ln_mix_g, v_ln_mix_g), "ln_mix_b": (ln_mix_b, m_ln_mix_b, v_ln_mix_b),
               "ln_ffn_g": (ln_ffn_g, m_ln_ffn_g, v_ln_ffn_g), "ln_ffn_b": (ln_ffn_b, m_ln_ffn_b, v_ln_ffn_b),
               "a_lb_logits": (a_lb_logits, m_a_lb_logits, v_a_lb_logits),
               "a_norm_g": (a_norm_g, m_a_norm_g, v_a_norm_g),
               "b_b_pw1": (b_b_pw1, m_b_b_pw1, v_b_b_pw1), "b_w_dw": (b_w_dw, m_b_w_dw, v_b_w_dw),
               "b_b_dw": (b_b_dw, m_b_b_dw, v_b_b_dw), "b_ln_g": (b_ln_g, m_b_ln_g, v_b_ln_g),
               "b_ln_b": (b_ln_b, m_b_ln_b, v_b_ln_b), "b_b_pw2": (b_b_pw2, m_b_b_pw2, v_b_b_pw2)}
    snames = list(small_w)
    sshapes = [small_w[k][0].shape for k in snames]
    pg = _pack([g_small[k] for k in snames], 1024)
    pw, pm, pv = [_pack([small_w[k][q] for k in snames], 1024) for q in range(3)]
    sres = _adamw(pg[None], pw, pm, pv, "adamw_small")
    sres = [dict(zip(snames, _unpack(r, sshapes))) for r in sres]

    recv = [R[n] for n in ["W1", "W2", "Win", "Wout", "Wp1", "Wp2"]]
    big = {}
    for nm, rv, (w, m, v) in zip(
            ["ffn_w1", "ffn_w2", "a_w_in", "a_w_out", "b_w_pw1", "b_w_pw2"], recv,
            [(ffn_w1, m_ffn_w1, v_ffn_w1), (ffn_w2, m_ffn_w2, v_ffn_w2), (a_w_in, m_a_w_in, v_a_w_in),
             (a_w_out, m_a_w_out, v_a_w_out), (b_w_pw1, m_b_w_pw1, v_b_w_pw1), (b_w_pw2, m_b_w_pw2, v_b_w_pw2)]):
        big[nm] = _adamw(rv, w, m, v, f"adamw_{nm}")

    names = ["ln_mix_g", "ln_mix_b", "ln_ffn_g", "ln_ffn_b", "ffn_w1", "ffn_w2", "a_w_in", "a_lb_logits",
             "a_norm_g", "a_w_out", "b_w_pw1", "b_b_pw1", "b_w_dw", "b_b_dw", "b_ln_g", "b_ln_b", "b_w_pw2",
             "b_b_pw2"]
    out = [loss, grad_x]
    for q in range(4):
        for nm in names:
            out.append(big[nm][q] if nm in big else sres[q][nm])
    return tuple(out)
```

```python
import functools

import numpy as np
import jax
import jax.numpy as jnp
from jax import lax
from jax.experimental import pallas as pl
from jax.experimental.pallas import tpu as pltpu

F32 = jnp.float32
BF = jnp.bfloat16

N_DEV = 8
DEPTH = 4
HEAD = 128
CHUNK = 64
CHUNK_BWD = 64
SUB = 8
CONV_W = 31
HALO = 32
CONV_ROWS = 128
ALPHA = (2.0 * DEPTH) ** 0.25
LN_EPS = 1e-5
RMS_EPS = 1e-6
GATE_EPS = 1e-6
ADAM_LR = 0.001
ADAM_B1 = 0.9
ADAM_B2 = 0.999
ADAM_EPS = 1e-08
ADAM_WD = 0.01
ADAM_STEP = 10
VMEM_LIMIT = 56 * 1024 * 1024
MESH = pl.DeviceIdType.MESH


def _cp(sem=None):
    return pltpu.CompilerParams(vmem_limit_bytes=VMEM_LIMIT, dimension_semantics=sem)


def _dot(a, b):
    return jnp.dot(a, b, preferred_element_type=F32)


def _dot_nt(a, b):
    return lax.dot_general(a, b, (((1,), (1,)), ((), ())), preferred_element_type=F32)


def _dot_tn(a, b):
    return lax.dot_general(a, b, (((0,), (0,)), ((), ())), preferred_element_type=F32)


def _sig(x):
    return jax.nn.sigmoid(x)


def _distinct(items):
    out = []
    for it in items:
        if not any(it is q for q in out):
            out.append(it)
    return out


def _index_of(items, it):
    return next(i for i, q in enumerate(items) if q is it)


def _comm_copies(mode, plan, src_refs, buf_refs, send_sems, recv_sems, loc_sems):
    x, y, c = lax.axis_index("x"), lax.axis_index("y"), lax.axis_index("c")
    me = 4 * x + 2 * y + c
    locs, first, passed = [], [], []
    for k, (si, l, bi) in enumerate(plan):
        src = src_refs[si].at[l] if mode == "gather" else src_refs[si].at[me, l]
        locs.append(pltpu.make_async_copy(src, buf_refs[bi].at[me, l], loc_sems.at[k]))

    def remote(k, r, src, slot, l, bi, to):
        return pltpu.make_async_remote_copy(
            src_ref=src, dst_ref=buf_refs[bi].at[slot, l],
            send_sem=send_sems.at[k, r], recv_sem=recv_sems.at[k, r],
            device_id=to, device_id_type=MESH)

    if mode == "scatter":
        for r in range(1, N_DEV):
            px = (1 - x) if (r >> 2) & 1 else x
            py = (1 - y) if (r >> 1) & 1 else y
            pc = (1 - c) if r & 1 else c
            pid = 4 * px + 2 * py + pc
            for k, (si, l, bi) in enumerate(plan):
                first.append(remote(k, r - 1, src_refs[si].at[pid, l], me, l, bi, (px, py, pc)))
        return locs, first, passed
    sibling = (x, y, 1 - c)
    for k, (si, l, bi) in enumerate(plan):
        first.append(remote(k, 0, src_refs[si].at[l], me, l, bi, sibling))
    for r, (qx, qy) in enumerate([(1 - x, y), (x, 1 - y), (1 - x, 1 - y)]):
        qid = 4 * qx + 2 * qy + c
        for k, (si, l, bi) in enumerate(plan):
            first.append(remote(k, 1 + r, src_refs[si].at[l], me, l, bi, (qx, qy, c)))
            passed.append(remote(k, 4 + r, buf_refs[bi].at[qid, l], qid, l, bi, sibling))
    return locs, first, passed


def _call(body, *, name, grid, operands, in_specs, out_shape, out_specs, scratch=(), sem=None,
          aliases=None, comm=None):
    aliases = dict(aliases or {})
    if comm is None:
        return pl.pallas_call(
            body, name=name, grid=grid, out_shape=list(out_shape), in_specs=list(in_specs),
            out_specs=list(out_specs), scratch_shapes=list(scratch), input_output_aliases=aliases,
            compiler_params=_cp(sem),
        )(*operands)
    mode, pieces = comm
    srcs = _distinct([p[0] for p in pieces])
    bufs = _distinct([p[2] for p in pieces])
    plan = [(_index_of(srcs, s), l, _index_of(bufs, b)) for (s, l, b) in pieces]
    n_in, n_out, n_scr, ns, nb, npc = len(operands), len(out_shape), len(scratch), len(srcs), len(bufs), len(plan)
    nsteps = grid[0]
    old = [b for b in bufs if not isinstance(b, jax.ShapeDtypeStruct)]
    nbi = len(old)

    def wrapped(*refs):
        ins = refs[:n_in]
        src_refs = refs[n_in:n_in + ns]
        o0 = n_in + ns + nbi
        outs = refs[o0:o0 + n_out]
        buf_refs = refs[o0 + n_out:o0 + n_out + nb]
        s0 = o0 + n_out + nb
        scr = refs[s0:s0 + n_scr]
        sems = refs[s0 + n_scr:]
        step = pl.program_id(0)

        @pl.when(step == 0)
        def _():
            locs, first, _ = _comm_copies(mode, plan, src_refs, buf_refs, *sems)
            for d in locs + first:
                d.start()

        body(*ins, *outs, *scr)

        @pl.when(step == nsteps - 1)
        def _():
            locs, first, passed = _comm_copies(mode, plan, src_refs, buf_refs, *sems)
            early = first[len(first) - len(passed):] if passed else []
            for d in early:
                d.wait_recv()
            for d in passed:
                d.start()
            for d in first + passed:
                d.wait_send()
            for d in first[:len(first) - len(early)] + passed:
                d.wait_recv()
            for d in locs:
                d.wait()

    anyspec = pl.BlockSpec(memory_space=pl.ANY)
    for k, b in enumerate(old):
        aliases[n_in + ns + k] = n_out + _index_of(bufs, b)
    res = pl.pallas_call(
        wrapped, name=name, grid=grid,
        out_shape=list(out_shape) + [jax.ShapeDtypeStruct(b.shape, b.dtype) for b in bufs],
        in_specs=list(in_specs) + [anyspec] * (ns + nbi),
        out_specs=list(out_specs) + [anyspec] * nb,
        scratch_shapes=list(scratch) + [pltpu.SemaphoreType.DMA((npc, N_DEV - 1)),
                                        pltpu.SemaphoreType.DMA((npc, N_DEV - 1)),
                                        pltpu.SemaphoreType.DMA((npc,))],
        input_output_aliases=aliases,
        compiler_params=pltpu.CompilerParams(vmem_limit_bytes=VMEM_LIMIT, has_side_effects=True,
                                             dimension_semantics=("arbitrary",) * len(grid)),
    )(*operands, *srcs, *old)
    return list(res[:n_out]), list(res[n_out:])


def _exchange(comm, name):
    def body():
        pass

    return _call(body, name=name, grid=(1,), operands=[], in_specs=[], out_shape=[], out_specs=[],
                 comm=comm)[1]


def _mm(a, w, l, kind, *, out_dtype, name, tm=512, a_fn=None, epi=None, extras=(), comm=None,
        outs=None, sums=()):
    T, Ka = a.shape
    _, _, d2, d3 = w.shape
    n_out = {"nn_col": N_DEV * d3, "nn_row": d3, "nt_col": d2, "nt_row": N_DEV * d2}[kind]
    tm = min(tm, T)
    ne = len(extras)
    single = outs is None and not sums
    outs = [(n_out, out_dtype)] if outs is None else outs
    no, ns = len(outs), len(sums)

    def body(*refs):
        a_ref, w_ref = refs[0], refs[1]
        e_refs = refs[2:2 + ne]
        o_refs = refs[2 + ne:2 + ne + no]
        s_refs = refs[2 + ne + no:]
        av = a_ref[...]
        if a_fn is not None:
            av = a_fn(av)
        av = av.astype(BF)
        if kind == "nn_col":
            acc = jnp.concatenate([_dot(av, w_ref[j]) for j in range(N_DEV)], axis=1)
        elif kind == "nn_row":
            acc = _dot(av, w_ref[...].reshape(N_DEV * d2, d3))
        elif kind == "nt_col":
            acc = _dot_nt(av[:, 0:d3], w_ref[0])
            for j in range(1, N_DEV):
                acc = acc + _dot_nt(av[:, j * d3:(j + 1) * d3], w_ref[j])
        else:
            acc = jnp.concatenate([_dot_nt(av, w_ref[j]) for j in range(N_DEV)], axis=1)
        res = acc if epi is None else epi(acc, *[e[...] for e in e_refs])
        res = res if isinstance(res, tuple) else (res,)
        for o_ref, val in zip(o_refs, res[:no]):
            o_ref[...] = val.astype(o_ref.dtype)
        if ns:
            first = pl.program_id(0) == 0

            @pl.when(first)
            def _():
                for s_ref, val in zip(s_refs, res[no:]):
                    s_ref[...] = val

            @pl.when(jnp.logical_not(first))
            def _():
                for s_ref, val in zip(s_refs, res[no:]):
                    s_ref[...] += val

    in_specs = [pl.BlockSpec((tm, Ka), lambda i: (i, 0)),
                pl.BlockSpec((N_DEV, None, d2, d3), lambda i: (0, l, 0, 0))]
    extras = [e if isinstance(e, tuple) else (e, 0) for e in extras]
    for e, cb in extras:
        if e.shape[0] == 1:
            in_specs.append(pl.BlockSpec((1, n_out), lambda i: (0, 0)))
        else:
            in_specs.append(pl.BlockSpec((tm, n_out), functools.partial(lambda i, cb: (i, cb), cb=cb)))
    extras = [e for e, _ in extras]
    out_shape = [jax.ShapeDtypeStruct((T, wd), dt) for wd, dt in outs]
    out_shape += [jax.ShapeDtypeStruct((1, wd), F32) for wd in sums]
    out_specs = [pl.BlockSpec((tm, wd), lambda i: (i, 0)) for wd, _ in outs]
    out_specs += [pl.BlockSpec((1, wd), lambda i: (0, 0)) for wd in sums]
    res = _call(body, name=name, grid=(T // tm,), operands=[a, w, *extras], in_specs=in_specs,
                out_shape=out_shape, out_specs=out_specs,
                sem=("arbitrary",) if ns else ("parallel",), comm=comm)
    if comm is None:
        return res[0] if single else res
    return (res[0][0] if single else res[0]), res[1]


def _mm_tn(g, s, b, l, kind, *, name, tt=1024, b_fn=None):
    T, ws = s.shape
    wb = b.shape[1]
    _, _, d2, d3 = g.shape
    tt = min(tt, T)
    nsteps = T // tt
    cw = wb // N_DEV

    fresh = isinstance(g, jax.ShapeDtypeStruct)

    def body(*refs):
        s_ref, b_ref, o_ref, acc, stage, sem = refs if fresh else refs[1:]
        t = pl.program_id(0)

        @pl.when(t == 0)
        def _():
            acc[...] = jnp.zeros((ws, wb), F32)

        st = s_ref[...].astype(F32).T.astype(BF)
        for j in range(N_DEV):
            bv = b_ref[:, j * cw:(j + 1) * cw]
            if b_fn is not None:
                bv = b_fn(bv)
            acc[:, j * cw:(j + 1) * cw] += _dot(st, bv.astype(BF))

        @pl.when(t == nsteps - 1)
        def _():
            for j in range(N_DEV):
                if kind == "cols":
                    blk = acc[:, j * d3:(j + 1) * d3]
                elif kind == "rows":
                    blk = acc[j * d2:(j + 1) * d2, :]
                else:
                    blk = acc[:, j * d2:(j + 1) * d2].T
                stage[...] = blk.astype(BF)
                cp = pltpu.make_async_copy(stage, o_ref.at[j, l], sem)
                cp.start()
                cp.wait()

    anyspec = pl.BlockSpec(memory_space=pl.ANY)
    return pl.pallas_call(
        body, name=name, grid=(nsteps,),
        out_shape=jax.ShapeDtypeStruct(g.shape, BF),
        in_specs=([] if fresh else [anyspec]) + [pl.BlockSpec((tt, ws), lambda t: (t, 0)),
                                                 pl.BlockSpec((tt, wb), lambda t: (t, 0))],
        out_specs=anyspec,
        scratch_shapes=[pltpu.VMEM((ws, wb), F32), pltpu.VMEM((d2, d3), BF), pltpu.SemaphoreType.DMA],
        input_output_aliases={} if fresh else {0: 0},
        compiler_params=_cp(("arbitrary",)),
    )(*([] if fresh else [g]), s, b)


def _ln_stats(z):
    mu = jnp.mean(z, axis=-1, keepdims=True)
    zc = z - mu
    var = jnp.mean(zc * zc, axis=-1, keepdims=True)
    rstd = lax.rsqrt(var + LN_EPS)
    return zc * rstd, rstd


def _ln_bwd_core(dy, zh, rstd, g):
    dzh = dy * g
    m1 = jnp.mean(dzh, axis=-1, keepdims=True)
    m2 = jnp.mean(dzh * zh, axis=-1, keepdims=True)
    return rstd * (dzh - m1 - zh * m2)


def _colsum(v):
    return jnp.sum(v, axis=0, keepdims=True)


def _z_and_ln(z, g, b):
    zh, _ = _ln_stats(z)
    return z, zh * g + b


def _ln_bwd_tile(dy, z, g):
    zh, rstd = _ln_stats(z)
    dz = _ln_bwd_core(dy, zh, rstd, g)
    return dz, _colsum(dy * zh), _colsum(dy), _colsum(dz)


def _heads(D):
    return D // HEAD


def _rms_parts(o, D):
    xs, rs = [], []
    for h in range(_heads(D)):
        oh = o[:, h * HEAD:(h + 1) * HEAD]
        r = lax.rsqrt(jnp.mean(oh * oh, axis=-1, keepdims=True) + RMS_EPS)
        xs.append(oh * r)
        rs.append(r)
    return xs, rs


def _gate_bwd_tile(dyt, ot, grt, ngr):
    D = ot.shape[1]
    xs, rs = _rms_parts(ot, D)
    xh = jnp.concatenate(xs, axis=1)
    sg = _sig(grt)
    on = xh * ngr
    dgr = dyt * on * (sg * (1.0 + grt * (1.0 - sg)))
    don = dyt * (grt * sg)
    dxh = don * ngr
    dos = []
    for h in range(_heads(D)):
        sl = slice(h * HEAD, (h + 1) * HEAD)
        m = jnp.mean(dxh[:, sl] * xs[h], axis=-1, keepdims=True)
        dos.append(rs[h] * (dxh[:, sl] - xs[h] * m))
    return jnp.concatenate(dos, axis=1), dgr, _colsum(don * xh)


def _u_and_glu(u):
    D = u.shape[1] // 2
    return u, u[:, :D] * _sig(u[:, D:])


def _silu_ln_bwd_tile(dst, ct, gr, br):
    zh, rstd = _ln_stats(ct)
    ln = zh * gr + br
    sg = _sig(ln)
    dln = dst * (sg * (1.0 + ln * (1.0 - sg)))
    dc = _ln_bwd_core(dln, zh, rstd, gr)
    return dc, _colsum(dln * zh), _colsum(dln), _colsum(dc)


def _top_tile(z, g, b, tgt):
    D = z.shape[1]
    zh, rstd = _ln_stats(z)
    e = zh * g + b - tgt
    dy = e * (1.0 / D)
    dz = _ln_bwd_core(dy, zh, rstd, g)
    return dz, _colsum(dy * zh), _colsum(dy), _colsum(e * e) * (0.5 / D)


def _shifted(ext, n, tr):
    for b in range(8):
        rb = ext if b == 0 else pltpu.roll(ext, n - b, 0)
        for a in range(HALO // 8 + 1):
            o = 8 * a + b
            if o <= HALO:
                yield o, rb[8 * a:8 * a + tr, :]


def _lane_strips(D):
    return [slice(c, c + 128) for c in range(0, D, 128)]


def _conv_fwd(glu, w, bdw, g, b, name, tr=256, comm=None):
    T, D = glu.shape
    tr = min(tr, T // 2)
    rt = min(CONV_ROWS, tr)
    hb = tr // HALO

    def body(cur_ref, halo_ref, w_ref, bdw_ref, g_ref, b_ref, c_ref, s_ref):
        i = pl.program_id(0)
        for cs in _lane_strips(D):
            halo = jnp.where(i > 0, halo_ref[:, cs], 0.0)
            for r0 in range(0, tr, rt):
                if r0 == 0:
                    ext = jnp.concatenate([halo, cur_ref[0:rt, cs]], axis=0)
                else:
                    ext = cur_ref[r0 - HALO:r0 + rt, cs]
                acc = None
                for o, sh in _shifted(ext, rt + HALO, rt):
                    k = o - (HALO - CONV_W + 1)
                    if 0 <= k < CONV_W:
                        term = w_ref[k:k + 1, cs] * sh
                        acc = term if acc is None else acc + term
                c_ref[r0:r0 + rt, cs] = acc + bdw_ref[:, cs]
        cv = c_ref[...]
        zh, _ = _ln_stats(cv)
        ln = zh * g_ref[...] + b_ref[...]
        s_ref[...] = (ln * _sig(ln)).astype(BF)

    row = pl.BlockSpec((1, D), lambda i: (0, 0))
    return _call(
        body, name=name, grid=(T // tr,), operands=[glu, glu, w, bdw, g, b],
        out_shape=[jax.ShapeDtypeStruct((T, D), F32), jax.ShapeDtypeStruct((T, D), BF)],
        in_specs=[pl.BlockSpec((tr, D), lambda i: (i, 0)),
                  pl.BlockSpec((HALO, D), lambda i: (jnp.maximum(i * hb - 1, 0), 0)),
                  pl.BlockSpec((HALO, D), lambda i: (0, 0)), row, row, row],
        out_specs=[pl.BlockSpec((tr, D), lambda i: (i, 0))] * 2,
        sem=("parallel",), comm=comm)


def _conv_bwd(dc, glu, u, w, name, tr=256, comm=None):
    T, D = glu.shape
    tr = min(tr, T // 2)
    rt = min(CONV_ROWS, tr)
    hb = tr // HALO
    nsteps = T // tr
    last_hb = T // HALO - 1

    def body(dc_ref, nxt_ref, glu_ref, prv_ref, ua_ref, ug_ref, w_ref, du_ref, dw_ref, dbu_ref):
        i = pl.program_id(0)

        @pl.when(i == 0)
        def _():
            dw_ref[...] = jnp.zeros((HALO, D), F32)
            dbu_ref[...] = jnp.zeros((1, 2 * D), F32)

        for cs in _lane_strips(D):
            nxt = jnp.where(i < nsteps - 1, nxt_ref[:, cs], 0.0)
            prv = jnp.where(i > 0, prv_ref[:, cs], 0.0)
            dws = [None] * CONV_W
            dba = dbg = None
            gs = slice(D + cs.start, D + cs.stop)
            for r0 in range(0, tr, rt):
                if r0 + rt == tr:
                    ext2 = jnp.concatenate([dc_ref[r0:tr, cs], nxt], axis=0)
                else:
                    ext2 = dc_ref[r0:r0 + rt + HALO, cs]
                acc = None
                for o, sh in _shifted(ext2, rt + HALO, rt):
                    k = CONV_W - 1 - o
                    if 0 <= k < CONV_W:
                        term = w_ref[k:k + 1, cs] * sh
                        acc = term if acc is None else acc + term
                sg = _sig(ug_ref[r0:r0 + rt, cs])
                da = acc * sg
                dgt = acc * ua_ref[r0:r0 + rt, cs] * sg * (1.0 - sg)
                du_ref[r0:r0 + rt, cs] = da.astype(BF)
                du_ref[r0:r0 + rt, gs] = dgt.astype(BF)
                dba = _colsum(da) if dba is None else dba + _colsum(da)
                dbg = _colsum(dgt) if dbg is None else dbg + _colsum(dgt)
                if r0 == 0:
                    ext = jnp.concatenate([prv, glu_ref[0:rt, cs]], axis=0)
                else:
                    ext = glu_ref[r0 - HALO:r0 + rt, cs]
                dcs = dc_ref[r0:r0 + rt, cs]
                for o, sh in _shifted(ext, rt + HALO, rt):
                    k = o - (HALO - CONV_W + 1)
                    if 0 <= k < CONV_W:
                        part = _colsum(dcs * sh)
                        dws[k] = part if dws[k] is None else dws[k] + part
            for k in range(CONV_W):
                dw_ref[k:k + 1, cs] += dws[k]
            dbu_ref[:, cs] += dba
            dbu_ref[:, gs] += dbg

    return _call(
        body, name=name, grid=(nsteps,), operands=[dc, dc, glu, glu, u, u, w],
        out_shape=[jax.ShapeDtypeStruct((T, 2 * D), BF), jax.ShapeDtypeStruct((HALO, D), F32),
                   jax.ShapeDtypeStruct((1, 2 * D), F32)],
        in_specs=[pl.BlockSpec((tr, D), lambda i: (i, 0)),
                  pl.BlockSpec((HALO, D), lambda i: (jnp.minimum((i + 1) * hb, last_hb), 0)),
                  pl.BlockSpec((tr, D), lambda i: (i, 0)),
                  pl.BlockSpec((HALO, D), lambda i: (jnp.maximum(i * hb - 1, 0), 0)),
                  pl.BlockSpec((tr, D), lambda i: (i, 0)),
                  pl.BlockSpec((tr, D), lambda i: (i, 1)),
                  pl.BlockSpec((HALO, D), lambda i: (0, 0))],
        out_specs=[pl.BlockSpec((tr, 2 * D), lambda i: (i, 0)),
                   pl.BlockSpec((HALO, D), lambda i: (0, 0)),
                   pl.BlockSpec((1, 2 * D), lambda i: (0, 0))],
        sem=("arbitrary",), comm=comm)


def _rec_consts(C):
    t = np.arange(C)
    lb = (t[None, :] <= t[:, None]).astype(np.float32)

    def cum_at(idx):
        return (t[None, :] <= idx[:, None]).astype(np.float32)

    blocks, masks = [], []
    mid = SUB * (t // SUB) + SUB // 2 - 1
    eq = lb - cum_at(mid)
    blocks += [eq, -eq]
    masks.append(((t[:, None] // SUB) == (t[None, :] // SUB)) & (t[None, :] <= t[:, None]))
    nb = SUB
    while nb < C:
        odd = (t // nb) % 2 == 1
        e_t = nb * (t // nb) - 1
        e_s = nb * (t // nb) + nb - 1
        blocks.append(np.where(odd[:, None], lb - cum_at(e_t), 0.0))
        blocks.append(np.where(~odd[:, None], cum_at(e_s) - lb, 0.0))
        masks.append(((t[:, None] // (2 * nb)) == (t[None, :] // (2 * nb))) & odd[:, None] & ~odd[None, :])
        nb *= 2
    blocks += [lb, 1.0 - lb, np.ones((HEAD, C), np.float32)]
    L = np.concatenate(blocks, axis=0).astype(np.float32)
    L3 = np.concatenate([L, L, L], axis=1)
    LT3 = np.concatenate([L.T, L.T, L.T], axis=1)
    m = np.stack(masks).astype(np.float32)
    mT = np.transpose(m, (0, 2, 1)).copy()
    return (jnp.asarray(L3, BF), jnp.asarray(LT3, BF), jnp.asarray(m), jnp.asarray(mT), len(masks))


def _split3(x):
    h = x.astype(BF)
    r = x - h.astype(F32)
    m = r.astype(BF)
    lo = (r - m.astype(F32)).astype(BF)
    return h, m, lo


def _gates(qr, fz, lbr):
    sq = _sig(qr)
    q = qr * sq
    sg = _sig(fz)
    f = lbr + (1.0 - lbr) * sg
    fc = jnp.maximum(f, GATE_EPS)
    return q, 1.0 - f, jnp.log(fc), sq, sg, f, fc


def _rec_fwd(proj, lbr, ng, name, comm=None):
    T, D4 = proj.shape
    D = D4 // 4
    H = _heads(D)
    C = CHUNK
    nC = T // C
    L3, _, m, _, nl = _rec_consts(C)
    R = L3.shape[0]

    def body(q_ref, f_ref, v_ref, gr_ref, lb_ref, ng_ref, l_ref, m_ref, o_ref, y_ref, s_ref, st):
        @pl.when(pl.program_id(0) == 0)
        def _():
            st[...] = jnp.zeros((H, HEAD, HEAD), F32)

        q, k, logf = _gates(q_ref[...], f_ref[...], lb_ref[...])[:3]
        ex = jnp.exp(_dot(l_ref[...], jnp.concatenate(_split3(logf), axis=0)))
        vb = v_ref[...].astype(BF)
        s_ref[0] = st[...]
        outs = []
        for h in range(H):
            sl = slice(h * HEAD, (h + 1) * HEAD)
            qh, kh = q[:, sl], k[:, sl]
            p = jnp.zeros((C, C), F32)
            for lv in range(nl):
                qt = (qh * ex[2 * lv * C:(2 * lv + 1) * C, sl]).astype(BF)
                kt = (kh * ex[(2 * lv + 1) * C:(2 * lv + 2) * C, sl]).astype(BF)
                p = p + jnp.where(m_ref[lv] > 0.0, _dot_nt(qt, kt), 0.0)
            base = 2 * nl * C
            qhat = (qh * ex[base:base + C, sl]).astype(BF)
            khat = (kh * ex[base + C:base + 2 * C, sl]).astype(BF)
            elast = ex[base + 2 * C:base + 2 * C + HEAD, sl]
            sth = st[h]
            outs.append(_dot(p.astype(BF), vb[:, sl]) + _dot_nt(qhat, sth.astype(BF)))
            st[h] = elast * sth + _dot_tn(vb[:, sl], khat)
        o_ref[...] = jnp.concatenate(outs, axis=1)
        xh = jnp.concatenate([oh * lax.rsqrt(jnp.mean(oh * oh, axis=-1, keepdims=True) + RMS_EPS) for oh in outs],
                             axis=1)
        gr = gr_ref[...]
        y_ref[...] = (xh * ng_ref[...] * (gr * _sig(gr))).astype(BF)

    def cblk(cb):
        return pl.BlockSpec((C, D), lambda i: (i, cb))

    return _call(
        body, name=name, grid=(nC,), operands=[proj, proj, proj, proj, lbr, ng, L3, m],
        out_shape=[jax.ShapeDtypeStruct((T, D), F32), jax.ShapeDtypeStruct((T, D), BF),
                   jax.ShapeDtypeStruct((nC, H, HEAD, HEAD), F32)],
        in_specs=[cblk(0), cblk(1), cblk(2), cblk(3), pl.BlockSpec((1, D), lambda i: (0, 0)),
                  pl.BlockSpec((1, D), lambda i: (0, 0)),
                  pl.BlockSpec(L3.shape, lambda i: (0, 0)), pl.BlockSpec(m.shape, lambda i: (0, 0, 0))],
        out_specs=[pl.BlockSpec((C, D), lambda i: (i, 0)), pl.BlockSpec((C, D), lambda i: (i, 0)),
                   pl.BlockSpec((1, H, HEAD, HEAD), lambda i: (i, 0, 0, 0))],
        scratch=[pltpu.VMEM((H, HEAD, HEAD), F32)], sem=("arbitrary",), comm=comm)


def _rec_bwd(proj, lbr, do, dgr, ssave, name, comm=None):
    T, D4 = proj.shape
    D = D4 // 4
    H = _heads(D)
    C = min(CHUNK_BWD, T // 2)
    nC = T // C
    stride = C // CHUNK
    L3, LT3, m, mT, nl = _rec_consts(C)

    def body(q_ref, f_ref, v_ref, lb_ref, do_ref, dgr_ref, s_ref, l_ref, lt_ref, m_ref, mt_ref,
             dp_ref, dlb_ref, dst):
        @pl.when(pl.program_id(0) == 0)
        def _():
            dst[...] = jnp.zeros((H, HEAD, HEAD), F32)
            dlb_ref[...] = jnp.zeros((1, D), F32)

        qr = q_ref[...]
        lbv = lb_ref[...]
        q, k, logf, sq, sg, f, fc = _gates(qr, f_ref[...], lbv)
        ex = jnp.exp(_dot(l_ref[...], jnp.concatenate(_split3(logf), axis=0)))
        vb = v_ref[...].astype(BF)
        dob = do_ref[...].astype(BF)
        base = 2 * nl * C
        de = [[] for _ in range(2 * nl + 3)]
        dqs, dks, dvs = [], [], []
        for h in range(H):
            sl = slice(h * HEAD, (h + 1) * HEAD)
            qh, kh, vh, doh = q[:, sl], k[:, sl], vb[:, sl], dob[:, sl]
            dp = _dot_nt(doh, vh)
            dpt = _dot_nt(vh, doh)
            sth = s_ref[0, h]
            dsth = dst[h]
            dsb = dsth.astype(BF)
            pt = jnp.zeros((C, C), F32)
            dq = jnp.zeros((C, HEAD), F32)
            dk = jnp.zeros((C, HEAD), F32)
            for lv in range(nl):
                exq = ex[2 * lv * C:(2 * lv + 1) * C, sl]
                exk = ex[(2 * lv + 1) * C:(2 * lv + 2) * C, sl]
                qt = qh * exq
                kt = kh * exk
                qtb, ktb = qt.astype(BF), kt.astype(BF)
                pt = pt + jnp.where(mt_ref[lv] > 0.0, _dot_nt(ktb, qtb), 0.0)
                dqt = _dot(jnp.where(m_ref[lv] > 0.0, dp, 0.0).astype(BF), ktb)
                dkt = _dot(jnp.where(mt_ref[lv] > 0.0, dpt, 0.0).astype(BF), qtb)
                dq = dq + dqt * exq
                dk = dk + dkt * exk
                de[2 * lv].append(dqt * qt)
                de[2 * lv + 1].append(dkt * kt)
            exb = ex[base:base + C, sl]
            exkh = ex[base + C:base + 2 * C, sl]
            elast = ex[base + 2 * C:base + 2 * C + HEAD, sl]
            qhat = qh * exb
            khat = kh * exkh
            dqh = _dot(doh, sth.astype(BF))
            dkh = _dot(vh, dsb)
            dq = dq + dqh * exb
            dk = dk + dkh * exkh
            de[2 * nl].append(dqh * qhat)
            de[2 * nl + 1].append(dkh * khat)
            de[2 * nl + 2].append(dsth * sth * elast)
            dvs.append(_dot(pt.astype(BF), doh) + _dot_nt(khat.astype(BF), dsb))
            dst[h] = elast * dsth + _dot_tn(doh, qhat.astype(BF))
            dqs.append(dq)
            dks.append(dk)
        de_all = jnp.concatenate([jnp.concatenate(b, axis=1) for b in de], axis=0)
        dlogf = _dot(lt_ref[...], jnp.concatenate(_split3(de_all), axis=0))
        dq = jnp.concatenate(dqs, axis=1)
        dk = jnp.concatenate(dks, axis=1)
        dv = jnp.concatenate(dvs, axis=1)
        ind = jnp.where(f > GATE_EPS, 1.0, jnp.where(f == GATE_EPS, 0.5, 0.0))
        df = dlogf * ind / fc - dk
        dfz = df * (1.0 - lbv) * sg * (1.0 - sg)
        dlb_ref[...] += _colsum(df * (1.0 - sg))
        dqr = dq * (sq * (1.0 + qr * (1.0 - sq)))
        dp_ref[...] = jnp.concatenate([dqr.astype(BF), dfz.astype(BF), dv.astype(BF), dgr_ref[...]], axis=1)

    def cblk(cb):
        return pl.BlockSpec((C, D), lambda i: (nC - 1 - i, cb))

    def whole(a):
        nd = a.ndim
        return pl.BlockSpec(a.shape, lambda i: (0,) * nd)

    return _call(
        body, name=name, grid=(nC,), operands=[proj, proj, proj, lbr, do, dgr, ssave, L3, LT3, m, mT],
        out_shape=[jax.ShapeDtypeStruct((T, D4), BF), jax.ShapeDtypeStruct((1, D), F32)],
        in_specs=[cblk(0), cblk(1), cblk(2), pl.BlockSpec((1, D), lambda i: (0, 0)),
                  pl.BlockSpec((C, D), lambda i: (nC - 1 - i, 0)),
                  pl.BlockSpec((C, D), lambda i: (nC - 1 - i, 0)),
                  pl.BlockSpec((1, H, HEAD, HEAD), lambda i: (stride * (nC - 1 - i), 0, 0, 0)),
                  whole(L3), whole(LT3), whole(m), whole(mT)],
        out_specs=[pl.BlockSpec((C, D4), lambda i: (nC - 1 - i, 0)),
                   pl.BlockSpec((1, D), lambda i: (0, 0))],
        scratch=[pltpu.VMEM((H, HEAD, HEAD), F32)], sem=("arbitrary",), comm=comm)


def _softmax_rows(lg_ref):
    n = lg_ref.shape[0]
    rows = [lg_ref[l:l + 1, :] for l in range(n)]
    mx = rows[0]
    for r in rows[1:]:
        mx = jnp.maximum(mx, r)
    es = [jnp.exp(r - mx) for r in rows]
    tot = es[0]
    for e in es[1:]:
        tot = tot + e
    return [e / tot for e in es]


def _lb_fwd(logits):
    n, D = logits.shape

    def body(lg_ref, o_ref):
        soft = _softmax_rows(lg_ref)
        acc = jnp.zeros((1, D), F32)
        o_ref[0:1, :] = acc
        for j in range(1, n):
            acc = acc + soft[j]
            o_ref[j:j + 1, :] = acc

    return pl.pallas_call(body, name="lb_fwd", out_shape=jax.ShapeDtypeStruct((n, D), F32))(logits)


def _small_reduce(parts, logits, lb_row0):
    _, R, D = parts.shape
    n = logits.shape[0]

    def body(p_ref, lg_ref, o_ref):
        acc = p_ref[0]
        for d in range(1, N_DEV):
            acc = acc + p_ref[d]
        o_ref[...] = acc
        soft = _softmax_rows(lg_ref)
        dlb = [o_ref[lb_row0 + j:lb_row0 + j + 1, :] for j in range(n)]
        dsoft = [jnp.zeros((1, D), F32)]
        for l in range(1, n):
            s = dlb[l]
            for j in range(l + 1, n):
                s = s + dlb[j]
            dsoft.append(s)
        dot = soft[0] * dsoft[0]
        for l in range(1, n):
            dot = dot + soft[l] * dsoft[l]
        for l in range(n):
            o_ref[lb_row0 + l:lb_row0 + l + 1, :] = soft[l] * (dsoft[l] - dot)

    return pl.pallas_call(body, name="small_reduce", out_shape=jax.ShapeDtypeStruct((R, D), F32))(parts, logits)


def _adam_math(g, w, m, v):
    m2 = ADAM_B1 * m + (1.0 - ADAM_B1) * g
    v2 = ADAM_B2 * v + (1.0 - ADAM_B2) * (g * g)
    mh = m2 / (1.0 - ADAM_B1 ** ADAM_STEP)
    vh = v2 / (1.0 - ADAM_B2 ** ADAM_STEP)
    delta = -ADAM_LR * (mh / (jnp.sqrt(vh) + ADAM_EPS) + ADAM_WD * w)
    return delta, m2, v2


def _adamw(recv, w, m, v, name):
    nsrc = recv.shape[0]
    shp = w.shape
    cols = shp[-1]
    rows = int(np.prod(shp[:-1]))
    r2 = recv.reshape(nsrc, rows, cols)
    tr = min(rows, max(8, (1 << 20) // (cols * nsrc)))
    while rows % tr:
        tr //= 2

    def body(r_ref, w_ref, m_ref, v_ref, g_ref, d_ref, nm_ref, nv_ref):
        g = r_ref[0].astype(F32)
        for s in range(1, nsrc):
            g = g + r_ref[s].astype(F32)
        delta, m2, v2 = _adam_math(g, w_ref[...], m_ref[...], v_ref[...])
        g_ref[...] = g
        d_ref[...] = delta
        nm_ref[...] = m2
        nv_ref[...] = v2

    blk = pl.BlockSpec((tr, cols), lambda i: (i, 0))
    outs = pl.pallas_call(
        body, name=name, grid=(rows // tr,),
        out_shape=[jax.ShapeDtypeStruct((rows, cols), F32)] * 4,
        in_specs=[pl.BlockSpec((nsrc, tr, cols), lambda i: (0, i, 0)), blk, blk, blk],
        out_specs=[blk] * 4,
        compiler_params=_cp(("parallel",)),
    )(r2, w.reshape(rows, cols), m.reshape(rows, cols), v.reshape(rows, cols))
    return [o.reshape(shp) for o in outs]


def _pack(arrs, lane):
    flat = jnp.concatenate([a.reshape(-1) for a in arrs])
    n = flat.shape[0]
    rows = -(-n // lane)
    rows = -(-rows // 8) * 8
    flat = jnp.pad(flat, (0, rows * lane - n))
    return flat.reshape(rows, lane)


def _unpack(packed, shapes):
    flat = packed.reshape(-1)
    out, off = [], 0
    for s in shapes:
        n = int(np.prod(s))
        out.append(flat[off:off + n].reshape(s))
        off += n
    return out


def kernel(x, ln_mix_g, ln_mix_b, ln_ffn_g, ln_ffn_b, ffn_w1, ffn_w2, a_w_in, a_lb_logits, a_norm_g, a_w_out, b_w_pw1, b_b_pw1, b_w_dw, b_b_dw, b_ln_g, b_ln_b, b_w_pw2, b_b_pw2, loss_target, m_ln_mix_g, m_ln_mix_b, m_ln_ffn_g, m_ln_ffn_b, m_ffn_w1, m_ffn_w2, m_a_w_in, m_a_lb_logits, m_a_norm_g, m_a_w_out, m_b_w_pw1, m_b_b_pw1, m_b_w_dw, m_b_b_dw, m_b_ln_g, m_b_ln_b, m_b_w_pw2, m_b_b_pw2, v_ln_mix_g, v_ln_mix_b, v_ln_ffn_g, v_ln_ffn_b, v_ffn_w1, v_ffn_w2, v_a_w_in, v_a_lb_logits, v_a_norm_g, v_a_w_out, v_b_w_pw1, v_b_b_pw1, v_b_w_dw, v_b_b_dw, v_b_ln_g, v_b_ln_b, v_b_w_pw2, v_b_b_pw2):
    T, D = x.shape[1], x.shape[2]
    nA, nB = a_w_in.shape[0], b_w_pw1.shape[0]
    me = 4 * lax.axis_index("x") + 2 * lax.axis_index("y") + lax.axis_index("c")
    xin = x[0]
    tgt = loss_target[0]

    small_names = [b_b_pw1, b_w_dw, b_b_dw, b_ln_g, b_ln_b, b_b_pw2]
    sp = _pack(small_names, 128)
    wsrc = {"W1": ffn_w1.astype(BF), "W2": ffn_w2.astype(BF), "Win": a_w_in.astype(BF),
            "Wout": a_w_out.astype(BF), "Wp1": b_w_pw1.astype(BF), "Wp2": b_w_pw2.astype(BF), "small": sp[None]}
    W = {k: jax.ShapeDtypeStruct((N_DEV,) + v.shape, v.dtype) for k, v in wsrc.items()}
    GW = {k: jax.ShapeDtypeStruct((N_DEV,) + v.shape, BF) for k, v in wsrc.items() if k != "small"}
    R = {k: jax.ShapeDtypeStruct(v.shape, BF) for k, v in GW.items()}

    def names_of(items):
        out = []
        for n, _ in items:
            if n not in out:
                out.append(n)
        return out

    def hosted(fn, key, sched, mode, src, bufs, *args, **kw):
        items = sched.get(key)
        if items is None:
            return fn(*args, name=key, **kw)
        comm = (mode, [(src[n], l, bufs[n]) for n, l in items])
        outs, new = fn(*args, name=key, comm=comm, **kw)
        for n, b in zip(names_of(items), new):
            bufs[n] = b
        return outs

    first = [("Win", 0), ("small", 0)]
    for n, b in zip(names_of(first), _exchange(("gather", [(wsrc[n], l, W[n]) for n, l in first]), "gather_first")):
        W[n] = b
    fwd_sched = {"a0_proj": [("Wout", 0), ("Wp1", 0)], "a0_rec_fwd": [("W1", 0), ("W2", 0), ("Wp2", 0)],
                 "l0_ffn_up": [("W2", 1)], "l0_ffn_down": [("W1", 1)],
                 "b0_conv": [("Win", 1), ("Wout", 1)], "l1_ffn_up": [("W2", 2)], "l1_ffn_down": [("W1", 2)],
                 "a1_rec_fwd": [("Wp1", 1), ("Wp2", 1), ("W1", 3), ("W2", 3)]}
    bwd_sched = {"a1_rec_bwd": [("W1", 3), ("W2", 3), ("Wp1", 1), ("Wp2", 1), ("W1", 2), ("W2", 2)],
                 "b0_conv_bwd": [("Win", 1), ("Wout", 1)],
                 "a0_rec_bwd": [("W1", 1), ("W2", 1), ("Wp1", 0), ("Wp2", 0), ("W1", 0), ("W2", 0), ("Wout", 0)],
                 "a0_proj_dx": [("Win", 0)]}

    def fwd(fn, key, *args, **kw):
        return hosted(fn, key, fwd_sched, "gather", wsrc, W, *args, **kw)

    def bwd(fn, key, *args, **kw):
        return hosted(fn, key, bwd_sched, "scatter", GW, R, *args, **kw)

    spg = W["small"][:, 0]
    sm = [jnp.stack(p) for p in zip(*[_unpack(spg[d], [a.shape for a in small_names]) for d in range(N_DEV)])]
    bpw1 = jnp.transpose(sm[0], (1, 0, 2)).reshape(nB, 1, 2 * D)
    wdw = jnp.transpose(sm[1], (1, 2, 0, 3)).reshape(nB, CONV_W, D)
    wdw = jnp.pad(wdw, ((0, 0), (0, HALO - CONV_W), (0, 0)))
    bdw, blng, blnb, bpw2 = [jnp.transpose(s, (1, 0, 2)).reshape(nB, 1, D) for s in sm[2:]]
    lb_all = _lb_fwd(a_lb_logits)

    zx = [(D, F32), (D, F32)]
    ln_out = dict(outs=[(D, F32)], sums=[D, D, D])
    ln_mix = [(ln_mix_g[i:i + 1], ln_mix_b[i:i + 1]) for i in range(DEPTH)]
    ln_ffn = [(ln_ffn_g[i:i + 1], ln_ffn_b[i:i + 1]) for i in range(DEPTH)]
    saved = []
    h = xin
    for i in range(DEPTH):
        j = i // 2
        sv = {"xin": h}
        if i % 2 == 0:
            proj = fwd(_mm, f"a{j}_proj", h, W["Win"], j, "nn_col", out_dtype=F32)
            o, yg, ssave = fwd(_rec_fwd, f"a{j}_rec_fwd", proj, lb_all[j:j + 1], a_norm_g[j:j + 1])
            z1, x1 = _mm(yg, W["Wout"], j, "nn_row", out_dtype=F32, name=f"a{j}_out", outs=zx,
                         epi=lambda acc, r, g, b: _z_and_ln(acc + ALPHA * r, g, b), extras=[h, *ln_mix[i]])
            sv.update(proj=proj, o=o, ssave=ssave, yg=yg)
        else:
            u, glu = _mm(h, W["Wp1"], j, "nn_col", out_dtype=F32, name=f"b{j}_pw1",
                         outs=[(2 * D, F32), (D, F32)],
                         epi=lambda acc, b: _u_and_glu(acc + b), extras=[bpw1[j]])
            cv, s = fwd(_conv_fwd, f"b{j}_conv", glu, wdw[j], bdw[j], blng[j], blnb[j])
            z1, x1 = _mm(s, W["Wp2"], j, "nn_row", out_dtype=F32, name=f"b{j}_pw2", outs=zx,
                         epi=lambda acc, bb, r, g, b: _z_and_ln(acc + bb + ALPHA * r, g, b),
                         extras=[bpw2[j], h, *ln_mix[i]])
            sv.update(u=u, glu=glu, cv=cv, s=s)
        hh = fwd(_mm, f"l{i}_ffn_up", x1, W["W1"], i, "nn_col", out_dtype=BF)
        relu2 = lambda t: jnp.square(jnp.maximum(t, 0))
        if i < DEPTH - 1:
            z2, h = fwd(_mm, f"l{i}_ffn_down", hh, W["W2"], i, "nn_row", out_dtype=F32, outs=zx, a_fn=relu2,
                        epi=lambda acc, r, g, b: _z_and_ln(acc + ALPHA * r, g, b), extras=[x1, *ln_ffn[i]])
        else:
            z2 = None
            dz2, dg_top, db_top, loss_row = fwd(
                _mm, f"l{i}_ffn_down", hh, W["W2"], i, "nn_row", out_dtype=F32, a_fn=relu2, **ln_out,
                epi=lambda acc, r, g, b, t: _top_tile(acc + ALPHA * r, g, b, t), extras=[x1, *ln_ffn[i], tgt])
        sv.update(z1=z1, x1=x1, hh=hh, z2=z2)
        saved.append(sv)

    rows = {}
    top = DEPTH - 1
    rows[("ffn_g", top)], rows[("ffn_b", top)] = dg_top, db_top
    for i in reversed(range(DEPTH)):
        j = i // 2
        sv = saved[i]
        if i > 0:
            below = dict(epi=lambda acc, r, z, g: _ln_bwd_tile(acc + ALPHA * r, z, g), **ln_out)
            below_extras = [saved[i - 1]["z2"], ln_ffn[i - 1][0]]
        else:
            below = dict(epi=lambda acc, r: acc + ALPHA * r)
            below_extras = []
        dh = _mm(dz2, W["W2"], i, "nt_row", out_dtype=BF, name=f"l{i}_ffn_down_dx",
                 epi=lambda acc, hv: acc * (2.0 * jnp.maximum(hv.astype(F32), 0.0)), extras=[sv["hh"]])
        GW["W2"] = _mm_tn(GW["W2"], dz2, sv["hh"], i, "rows_t", name=f"l{i}_ffn_down_dw",
                         b_fn=lambda t: jnp.square(jnp.maximum(t, 0)))
        GW["W1"] = _mm_tn(GW["W1"], sv["x1"], dh, i, "cols", name=f"l{i}_ffn_up_dw")
        dz1, dg, db, dz1sum = _mm(dh, W["W1"], i, "nt_col", out_dtype=F32, name=f"l{i}_ffn_up_dx",
                                  epi=lambda acc, r, z, g: _ln_bwd_tile(acc + ALPHA * r, z, g),
                                  extras=[dz2, sv["z1"], ln_mix[i][0]], **ln_out)
        rows[("mix_g", i)], rows[("mix_b", i)] = dg, db
        if i % 2 == 0:
            GW["Wout"] = _mm_tn(GW["Wout"], sv["yg"], dz1, j, "rows", name=f"a{j}_out_dw")
            do, dgr, dng = _mm(dz1, W["Wout"], j, "nt_row", out_dtype=F32, name=f"a{j}_out_dx",
                               outs=[(D, F32), (D, BF)], sums=[D], epi=_gate_bwd_tile,
                               extras=[sv["o"], (sv["proj"], 3), a_norm_g[j:j + 1]])
            rows[("ng", j)] = dng
            dproj, dlb = bwd(_rec_bwd, f"a{j}_rec_bwd", sv["proj"], lb_all[j:j + 1], do, dgr, sv["ssave"])
            rows[("lb", j)] = dlb
            GW["Win"] = _mm_tn(GW["Win"], sv["xin"], dproj, j, "cols", name=f"a{j}_proj_dw")
            res = bwd(_mm, f"a{j}_proj_dx", dproj, W["Win"], j, "nt_col", out_dtype=F32,
                      extras=[dz1, *below_extras], **below)
        else:
            rows[("bpw2", j)] = dz1sum
            GW["Wp2"] = _mm_tn(GW["Wp2"], sv["s"], dz1, j, "rows", name=f"b{j}_pw2_dw")
            dc, dlg, dlb_, dcs = _mm(dz1, W["Wp2"], j, "nt_row", out_dtype=F32, name=f"b{j}_pw2_dx",
                                     outs=[(D, F32)], sums=[D, D, D], epi=_silu_ln_bwd_tile,
                                     extras=[sv["cv"], blng[j], blnb[j]])
            rows[("blng", j)], rows[("blnb", j)], rows[("bdw", j)] = dlg, dlb_, dcs
            du, dwdw, dbu = bwd(_conv_bwd, f"b{j}_conv_bwd", dc, sv["glu"], sv["u"], wdw[j])
            rows[("wdw", j)] = dwdw[:CONV_W]
            rows[("bpw1", j)] = dbu.reshape(2, D)
            GW["Wp1"] = _mm_tn(GW["Wp1"], sv["xin"], du, j, "cols", name=f"b{j}_pw1_dw")
            res = _mm(du, W["Wp1"], j, "nt_col", out_dtype=F32, name=f"b{j}_pw1_dx",
                      extras=[dz1, *below_extras], **below)
        if i > 0:
            dz2, dg, db, _ = res
            rows[("ffn_g", i - 1)], rows[("ffn_b", i - 1)] = dg, db
        else:
            dx = res
    grad_x = dx[None]

    order = ([("mix_g", i) for i in range(DEPTH)] + [("mix_b", i) for i in range(DEPTH)]
             + [("ffn_g", i) for i in range(DEPTH)] + [("ffn_b", i) for i in range(DEPTH)])
    lb_row0 = len(order)
    order += [("lb", j) for j in range(nA)] + [("ng", j) for j in range(nA)]
    for j in range(nB):
        order += [("bpw1", j), ("wdw", j), ("bdw", j), ("blng", j), ("blnb", j), ("bpw2", j)]
    pieces, offs, off = [], {}, 0
    for key in order:
        offs[key] = off
        pieces.append(rows[key])
        off += rows[key].shape[0]
    offs["loss"] = off
    pieces.append(loss_row)
    off += 1
    part = jnp.concatenate(pieces, axis=0)
    part = jnp.pad(part, ((0, -off % 8), (0, 0)))
    parts = _exchange(("gather", [(part[None], 0, jax.ShapeDtypeStruct((N_DEV, 1) + part.shape, F32))]),
                      "gather_small_grads")[0][:, 0]
    G = _small_reduce(parts, a_lb_logits, lb_row0)
    loss = jnp.sum(G[offs["loss"]])

    def rep(kind, n):
        return jnp.concatenate([G[offs[(kind, i)]:offs[(kind, i)] + 1] for i in range(n)], axis=0)

    def shard_cols(full, width):
        return lax.dynamic_slice_in_dim(full, me * width, width, axis=full.ndim - 1)

    g_small = {
        "ln_mix_g": rep("mix_g", DEPTH), "ln_mix_b": rep("mix_b", DEPTH),
        "ln_ffn_g": rep("ffn_g", DEPTH), "ln_ffn_b": rep("ffn_b", DEPTH),
        "a_lb_logits": rep("lb", nA), "a_norm_g": rep("ng", nA),
        "b_b_pw1": shard_cols(jnp.stack([G[offs[("bpw1", j)]:offs[("bpw1", j)] + 2].reshape(2 * D)
                                         for j in range(nB)]), 2 * D // N_DEV),
        "b_w_dw": shard_cols(jnp.stack([G[offs[("wdw", j)]:offs[("wdw", j)] + CONV_W] for j in range(nB)]),
                             D // N_DEV),
        "b_b_dw": shard_cols(rep("bdw", nB), D // N_DEV),
        "b_ln_g": shard_cols(rep("blng", nB), D // N_DEV),
        "b_ln_b": shard_cols(rep("blnb", nB), D // N_DEV),
        "b_b_pw2": shard_cols(rep("bpw2", nB), D // N_DEV),
    }
    small_w = {"ln_mix_g": (ln_mix_g, m_ln_mix_g, v_ln_mix_g), "ln_mix_b": (ln_mix_b, m_ln_mix_b, v_ln_mix_b),
               "ln_ffn_g": (ln_ffn_g, m_ln_ffn_g, v_ln_ffn_g), "ln_ffn_b": (ln_ffn_b, m_ln_ffn_b, v_ln_ffn_b),
               "a_lb_logits": (a_lb_logits, m_a_lb_logits, v_a_lb_logits),
               "a_norm_g": (a_norm_g, m_a_norm_g, v_a_norm_g),
               "b_b_pw1": (b_b_pw1, m_b_b_pw1, v_b_b_pw1), "b_w_dw": (b_w_dw, m_b_w_dw, v_b_w_dw),
               "b_b_dw": (b_b_dw, m_b_b_dw, v_b_b_dw), "b_ln_g": (b_ln_g, m_b_ln_g, v_b_ln_g),
               "b_ln_b": (b_ln_b, m_b_ln_b, v_b_ln_b), "b_b_pw2": (b_b_pw2, m_b_b_pw2, v_b_b_pw2)}
    snames = list(small_w)
    sshapes = [small_w[k][0].shape for k in snames]
    pg = _pack([g_small[k] for k in snames], 1024)
    pw, pm, pv = [_pack([small_w[k][q] for k in snames], 1024) for q in range(3)]
    sres = _adamw(pg[None], pw, pm, pv, "adamw_small")
    sres = [dict(zip(snames, _unpack(r, sshapes))) for r in sres]

    recv = [R[n] for n in ["W1", "W2", "Win", "Wout", "Wp1", "Wp2"]]
    big = {}
    for nm, rv, (w, m, v) in zip(
            ["ffn_w1", "ffn_w2", "a_w_in", "a_w_out", "b_w_pw1", "b_w_pw2"], recv,
            [(ffn_w1, m_ffn_w1, v_ffn_w1), (ffn_w2, m_ffn_w2, v_ffn_w2), (a_w_in, m_a_w_in, v_a_w_in),
             (a_w_out, m_a_w_out, v_a_w_out), (b_w_pw1, m_b_w_pw1, v_b_w_pw1), (b_w_pw2, m_b_w_pw2, v_b_w_pw2)]):
        big[nm] = _adamw(rv, w, m, v, f"adamw_{nm}")

    names = ["ln_mix_g", "ln_mix_b", "ln_ffn_g", "ln_ffn_b", "ffn_w1", "ffn_w2", "a_w_in", "a_lb_logits",
             "a_norm_g", "a_w_out", "b_w_pw1", "b_b_pw1", "b_w_dw", "b_b_dw", "b_ln_g", "b_ln_b", "b_w_pw2",
             "b_b_pw2"]
    out = [loss, grad_x]
    for q in range(4):
        for nm in names:
            out.append(big[nm][q] if nm in big else sres[q][nm])
    return tuple(out)
```

```python
import functools

import numpy as np
import jax
import jax.numpy as jnp
from jax import lax
from jax.experimental import pallas as pl
from jax.experimental.pallas import tpu as pltpu

F32 = jnp.float32
BF = jnp.bfloat16

N_DEV = 8
DEPTH = 4
HEAD = 128
CHUNK = 64
CHUNK_BWD = 64
SUB = 8
CONV_W = 31
HALO = 32
CONV_ROWS = 128
ALPHA = (2.0 * DEPTH) ** 0.25
LN_EPS = 1e-5
RMS_EPS = 1e-6
GATE_EPS = 1e-6
ADAM_LR = 0.001
ADAM_B1 = 0.9
ADAM_B2 = 0.999
ADAM_EPS = 1e-08
ADAM_WD = 0.01
ADAM_STEP = 10
VMEM_LIMIT = 56 * 1024 * 1024
MESH = pl.DeviceIdType.MESH


def _cp(sem=None):
    return pltpu.CompilerParams(vmem_limit_bytes=VMEM_LIMIT, dimension_semantics=sem)


def _dot(a, b):
    return jnp.dot(a, b, preferred_element_type=F32)


def _dot_nt(a, b):
    return lax.dot_general(a, b, (((1,), (1,)), ((), ())), preferred_element_type=F32)


def _dot_tn(a, b):
    return lax.dot_general(a, b, (((0,), (0,)), ((), ())), preferred_element_type=F32)


def _sig(x):
    return jax.nn.sigmoid(x)


def _distinct(items):
    out = []
    for it in items:
        if not any(it is q for q in out):
            out.append(it)
    return out


def _index_of(items, it):
    return next(i for i, q in enumerate(items) if q is it)


def _comm_copies(mode, plan, src_refs, buf_refs, send_sems, recv_sems, loc_sems):
    x, y, c = lax.axis_index("x"), lax.axis_index("y"), lax.axis_index("c")
    me = 4 * x + 2 * y + c
    locs, first, passed = [], [], []
    for k, (si, l, bi) in enumerate(plan):
        src = src_refs[si].at[l] if mode == "gather" else src_refs[si].at[me, l]
        locs.append(pltpu.make_async_copy(src, buf_refs[bi].at[me, l], loc_sems.at[k]))

    def remote(k, r, src, slot, l, bi, to):
        return pltpu.make_async_remote_copy(
            src_ref=src, dst_ref=buf_refs[bi].at[slot, l],
            send_sem=send_sems.at[k, r], recv_sem=recv_sems.at[k, r],
            device_id=to, device_id_type=MESH)

    if mode == "scatter":
        for r in range(1, N_DEV):
            px = (1 - x) if (r >> 2) & 1 else x
            py = (1 - y) if (r >> 1) & 1 else y
            pc = (1 - c) if r & 1 else c
            pid = 4 * px + 2 * py + pc
            for k, (si, l, bi) in enumerate(plan):
                first.append(remote(k, r - 1, src_refs[si].at[pid, l], me, l, bi, (px, py, pc)))
        return locs, first, passed
    sibling = (x, y, 1 - c)
    for k, (si, l, bi) in enumerate(plan):
        first.append(remote(k, 0, src_refs[si].at[l], me, l, bi, sibling))
    for r, (qx, qy) in enumerate([(1 - x, y), (x, 1 - y), (1 - x, 1 - y)]):
        qid = 4 * qx + 2 * qy + c
        for k, (si, l, bi) in enumerate(plan):
            first.append(remote(k, 1 + r, src_refs[si].at[l], me, l, bi, (qx, qy, c)))
            passed.append(remote(k, 4 + r, buf_refs[bi].at[qid, l], qid, l, bi, sibling))
    return locs, first, passed


def _call(body, *, name, grid, operands, in_specs, out_shape, out_specs, scratch=(), sem=None,
          aliases=None, comm=None):
    aliases = dict(aliases or {})
    if comm is None:
        return pl.pallas_call(
            body, name=name, grid=grid, out_shape=list(out_shape), in_specs=list(in_specs),
            out_specs=list(out_specs), scratch_shapes=list(scratch), input_output_aliases=aliases,
            compiler_params=_cp(sem),
        )(*operands)
    mode, pieces = comm
    srcs = _distinct([p[0] for p in pieces])
    bufs = _distinct([p[2] for p in pieces])
    plan = [(_index_of(srcs, s), l, _index_of(bufs, b)) for (s, l, b) in pieces]
    n_in, n_out, n_scr, ns, nb, npc = len(operands), len(out_shape), len(scratch), len(srcs), len(bufs), len(plan)
    nsteps = grid[0]
    old = [b for b in bufs if not isinstance(b, jax.ShapeDtypeStruct)]
    nbi = len(old)

    def wrapped(*refs):
        ins = refs[:n_in]
        src_refs = refs[n_in:n_in + ns]
        o0 = n_in + ns + nbi
        outs = refs[o0:o0 + n_out]
        buf_refs = refs[o0 + n_out:o0 + n_out + nb]
        s0 = o0 + n_out + nb
        scr = refs[s0:s0 + n_scr]
        sems = refs[s0 + n_scr:]
        step = pl.program_id(0)

        @pl.when(step == 0)
        def _():
            locs, first, _ = _comm_copies(mode, plan, src_refs, buf_refs, *sems)
            for d in locs + first:
                d.start()

        body(*ins, *outs, *scr)

        @pl.when(step == (3 * nsteps) // 4)
        def _():
            _, first, passed = _comm_copies(mode, plan, src_refs, buf_refs, *sems)
            for d in first[len(first) - len(passed):] if passed else []:
                d.wait_recv()
            for d in passed:
                d.start()

        @pl.when(step == nsteps - 1)
        def _():
            locs, first, passed = _comm_copies(mode, plan, src_refs, buf_refs, *sems)
            for d in first + passed:
                d.wait_send()
            for d in first[:len(first) - len(passed)] + passed:
                d.wait_recv()
            for d in locs:
                d.wait()

    anyspec = pl.BlockSpec(memory_space=pl.ANY)
    for k, b in enumerate(old):
        aliases[n_in + ns + k] = n_out + _index_of(bufs, b)
    res = pl.pallas_call(
        wrapped, name=name, grid=grid,
        out_shape=list(out_shape) + [jax.ShapeDtypeStruct(b.shape, b.dtype) for b in bufs],
        in_specs=list(in_specs) + [anyspec] * (ns + nbi),
        out_specs=list(out_specs) + [anyspec] * nb,
        scratch_shapes=list(scratch) + [pltpu.SemaphoreType.DMA((npc, N_DEV - 1)),
                                        pltpu.SemaphoreType.DMA((npc, N_DEV - 1)),
                                        pltpu.SemaphoreType.DMA((npc,))],
        input_output_aliases=aliases,
        compiler_params=pltpu.CompilerParams(vmem_limit_bytes=VMEM_LIMIT, has_side_effects=True,
                                             dimension_semantics=("arbitrary",) * len(grid)),
    )(*operands, *srcs, *old)
    return list(res[:n_out]), list(res[n_out:])


def _exchange(comm, name):
    def body():
        pass

    return _call(body, name=name, grid=(1,), operands=[], in_specs=[], out_shape=[], out_specs=[],
                 comm=comm)[1]


def _mm(a, w, l, kind, *, out_dtype, name, tm=512, a_fn=None, epi=None, extras=(), comm=None,
        outs=None, sums=()):
    T, Ka = a.shape
    _, _, d2, d3 = w.shape
    n_out = {"nn_col": N_DEV * d3, "nn_row": d3, "nt_col": d2, "nt_row": N_DEV * d2}[kind]
    tm = min(tm, T)
    ne = len(extras)
    single = outs is None and not sums
    outs = [(n_out, out_dtype)] if outs is None else outs
    no, ns = len(outs), len(sums)

    def body(*refs):
        a_ref, w_ref = refs[0], refs[1]
        e_refs = refs[2:2 + ne]
        o_refs = refs[2 + ne:2 + ne + no]
        s_refs = refs[2 + ne + no:]
        av = a_ref[...]
        if a_fn is not None:
            av = a_fn(av)
        av = av.astype(BF)
        if kind == "nn_col":
            acc = jnp.concatenate([_dot(av, w_ref[j]) for j in range(N_DEV)], axis=1)
        elif kind == "nn_row":
            acc = _dot(av, w_ref[...].reshape(N_DEV * d2, d3))
        elif kind == "nt_col":
            acc = _dot_nt(av[:, 0:d3], w_ref[0])
            for j in range(1, N_DEV):
                acc = acc + _dot_nt(av[:, j * d3:(j + 1) * d3], w_ref[j])
        else:
            acc = jnp.concatenate([_dot_nt(av, w_ref[j]) for j in range(N_DEV)], axis=1)
        res = acc if epi is None else epi(acc, *[e[...] for e in e_refs])
        res = res if isinstance(res, tuple) else (res,)
        for o_ref, val in zip(o_refs, res[:no]):
            o_ref[...] = val.astype(o_ref.dtype)
        if ns:
            first = pl.program_id(0) == 0

            @pl.when(first)
            def _():
                for s_ref, val in zip(s_refs, res[no:]):
                    s_ref[...] = val

            @pl.when(jnp.logical_not(first))
            def _():
                for s_ref, val in zip(s_refs, res[no:]):
                    s_ref[...] += val

    in_specs = [pl.BlockSpec((tm, Ka), lambda i: (i, 0)),
                pl.BlockSpec((N_DEV, None, d2, d3), lambda i: (0, l, 0, 0))]
    extras = [e if isinstance(e, tuple) else (e, 0) for e in extras]
    for e, cb in extras:
        if e.shape[0] == 1:
            in_specs.append(pl.BlockSpec((1, n_out), lambda i: (0, 0)))
        else:
            in_specs.append(pl.BlockSpec((tm, n_out), functools.partial(lambda i, cb: (i, cb), cb=cb)))
    extras = [e for e, _ in extras]
    out_shape = [jax.ShapeDtypeStruct((T, wd), dt) for wd, dt in outs]
    out_shape += [jax.ShapeDtypeStruct((1, wd), F32) for wd in sums]
    out_specs = [pl.BlockSpec((tm, wd), lambda i: (i, 0)) for wd, _ in outs]
    out_specs += [pl.BlockSpec((1, wd), lambda i: (0, 0)) for wd in sums]
    res = _call(body, name=name, grid=(T // tm,), operands=[a, w, *extras], in_specs=in_specs,
                out_shape=out_shape, out_specs=out_specs,
                sem=("arbitrary",) if ns else ("parallel",), comm=comm)
    if comm is None:
        return res[0] if single else res
    return (res[0][0] if single else res[0]), res[1]


def _mm_tn(g, s, b, l, kind, *, name, tt=1024, b_fn=None):
    T, ws = s.shape
    wb = b.shape[1]
    _, _, d2, d3 = g.shape
    tt = min(tt, T)
    nsteps = T // tt
    cw = wb // N_DEV

    fresh = isinstance(g, jax.ShapeDtypeStruct)

    def body(*refs):
        s_ref, b_ref, o_ref, acc, stage, sem = refs if fresh else refs[1:]
        t = pl.program_id(0)

        @pl.when(t == 0)
        def _():
            acc[...] = jnp.zeros((ws, wb), F32)

        st = s_ref[...].astype(F32).T.astype(BF)
        for j in range(N_DEV):
            bv = b_ref[:, j * cw:(j + 1) * cw]
            if b_fn is not None:
                bv = b_fn(bv)
            acc[:, j * cw:(j + 1) * cw] += _dot(st, bv.astype(BF))

        @pl.when(t == nsteps - 1)
        def _():
            for j in range(N_DEV):
                if kind == "cols":
                    blk = acc[:, j * d3:(j + 1) * d3]
                elif kind == "rows":
                    blk = acc[j * d2:(j + 1) * d2, :]
                else:
                    blk = acc[:, j * d2:(j + 1) * d2].T
                stage[...] = blk.astype(BF)
                cp = pltpu.make_async_copy(stage, o_ref.at[j, l], sem)
                cp.start()
                cp.wait()

    anyspec = pl.BlockSpec(memory_space=pl.ANY)
    return pl.pallas_call(
        body, name=name, grid=(nsteps,),
        out_shape=jax.ShapeDtypeStruct(g.shape, BF),
        in_specs=([] if fresh else [anyspec]) + [pl.BlockSpec((tt, ws), lambda t: (t, 0)),
                                                 pl.BlockSpec((tt, wb), lambda t: (t, 0))],
        out_specs=anyspec,
        scratch_shapes=[pltpu.VMEM((ws, wb), F32), pltpu.VMEM((d2, d3), BF), pltpu.SemaphoreType.DMA],
        input_output_aliases={} if fresh else {0: 0},
        compiler_params=_cp(("arbitrary",)),
    )(*([] if fresh else [g]), s, b)


def _ln_stats(z):
    mu = jnp.mean(z, axis=-1, keepdims=True)
    zc = z - mu
    var = jnp.mean(zc * zc, axis=-1, keepdims=True)
    rstd = lax.rsqrt(var + LN_EPS)
    return zc * rstd, rstd


def _ln_bwd_core(dy, zh, rstd, g):
    dzh = dy * g
    m1 = jnp.mean(dzh, axis=-1, keepdims=True)
    m2 = jnp.mean(dzh * zh, axis=-1, keepdims=True)
    return rstd * (dzh - m1 - zh * m2)


def _colsum(v):
    return jnp.sum(v, axis=0, keepdims=True)


def _z_and_ln(z, g, b):
    zh, _ = _ln_stats(z)
    return z, zh * g + b


def _ln_bwd_tile(dy, z, g):
    zh, rstd = _ln_stats(z)
    dz = _ln_bwd_core(dy, zh, rstd, g)
    return dz, _colsum(dy * zh), _colsum(dy), _colsum(dz)


def _heads(D):
    return D // HEAD


def _rms_parts(o, D):
    xs, rs = [], []
    for h in range(_heads(D)):
        oh = o[:, h * HEAD:(h + 1) * HEAD]
        r = lax.rsqrt(jnp.mean(oh * oh, axis=-1, keepdims=True) + RMS_EPS)
        xs.append(oh * r)
        rs.append(r)
    return xs, rs


def _gate_bwd_tile(dyt, ot, grt, ngr):
    D = ot.shape[1]
    xs, rs = _rms_parts(ot, D)
    xh = jnp.concatenate(xs, axis=1)
    sg = _sig(grt)
    on = xh * ngr
    dgr = dyt * on * (sg * (1.0 + grt * (1.0 - sg)))
    don = dyt * (grt * sg)
    dxh = don * ngr
    dos = []
    for h in range(_heads(D)):
        sl = slice(h * HEAD, (h + 1) * HEAD)
        m = jnp.mean(dxh[:, sl] * xs[h], axis=-1, keepdims=True)
        dos.append(rs[h] * (dxh[:, sl] - xs[h] * m))
    return jnp.concatenate(dos, axis=1), dgr, _colsum(don * xh)


def _u_and_glu(u):
    D = u.shape[1] // 2
    return u, u[:, :D] * _sig(u[:, D:])


def _silu_ln_bwd_tile(dst, ct, gr, br):
    zh, rstd = _ln_stats(ct)
    ln = zh * gr + br
    sg = _sig(ln)
    dln = dst * (sg * (1.0 + ln * (1.0 - sg)))
    dc = _ln_bwd_core(dln, zh, rstd, gr)
    return dc, _colsum(dln * zh), _colsum(dln), _colsum(dc)


def _top_tile(z, g, b, tgt):
    D = z.shape[1]
    zh, rstd = _ln_stats(z)
    e = zh * g + b - tgt
    dy = e * (1.0 / D)
    dz = _ln_bwd_core(dy, zh, rstd, g)
    return dz, _colsum(dy * zh), _colsum(dy), _colsum(e * e) * (0.5 / D)


def _shifted(ext, n, tr):
    for b in range(8):
        rb = ext if b == 0 else pltpu.roll(ext, n - b, 0)
        for a in range(HALO // 8 + 1):
            o = 8 * a + b
            if o <= HALO:
                yield o, rb[8 * a:8 * a + tr, :]


def _lane_strips(D):
    return [slice(c, c + 128) for c in range(0, D, 128)]


def _conv_fwd(glu, w, bdw, g, b, name, tr=256, comm=None):
    T, D = glu.shape
    tr = min(tr, T // 2)
    rt = min(CONV_ROWS, tr)
    hb = tr // HALO

    def body(cur_ref, halo_ref, w_ref, bdw_ref, g_ref, b_ref, c_ref, s_ref):
        i = pl.program_id(0)
        for cs in _lane_strips(D):
            halo = jnp.where(i > 0, halo_ref[:, cs], 0.0)
            for r0 in range(0, tr, rt):
                if r0 == 0:
                    ext = jnp.concatenate([halo, cur_ref[0:rt, cs]], axis=0)
                else:
                    ext = cur_ref[r0 - HALO:r0 + rt, cs]
                acc = None
                for o, sh in _shifted(ext, rt + HALO, rt):
                    k = o - (HALO - CONV_W + 1)
                    if 0 <= k < CONV_W:
                        term = w_ref[k:k + 1, cs] * sh
                        acc = term if acc is None else acc + term
                c_ref[r0:r0 + rt, cs] = acc + bdw_ref[:, cs]
        cv = c_ref[...]
        zh, _ = _ln_stats(cv)
        ln = zh * g_ref[...] + b_ref[...]
        s_ref[...] = (ln * _sig(ln)).astype(BF)

    row = pl.BlockSpec((1, D), lambda i: (0, 0))
    return _call(
        body, name=name, grid=(T // tr,), operands=[glu, glu, w, bdw, g, b],
        out_shape=[jax.ShapeDtypeStruct((T, D), F32), jax.ShapeDtypeStruct((T, D), BF)],
        in_specs=[pl.BlockSpec((tr, D), lambda i: (i, 0)),
                  pl.BlockSpec((HALO, D), lambda i: (jnp.maximum(i * hb - 1, 0), 0)),
                  pl.BlockSpec((HALO, D), lambda i: (0, 0)), row, row, row],
        out_specs=[pl.BlockSpec((tr, D), lambda i: (i, 0))] * 2,
        sem=("parallel",), comm=comm)


def _conv_bwd(dc, glu, u, w, name, tr=256, comm=None):
    T, D = glu.shape
    tr = min(tr, T // 2)
    rt = min(CONV_ROWS, tr)
    hb = tr // HALO
    nsteps = T // tr
    last_hb = T // HALO - 1

    def body(dc_ref, nxt_ref, glu_ref, prv_ref, ua_ref, ug_ref, w_ref, du_ref, dw_ref, dbu_ref):
        i = pl.program_id(0)

        @pl.when(i == 0)
        def _():
            dw_ref[...] = jnp.zeros((HALO, D), F32)
            dbu_ref[...] = jnp.zeros((1, 2 * D), F32)

        for cs in _lane_strips(D):
            nxt = jnp.where(i < nsteps - 1, nxt_ref[:, cs], 0.0)
            prv = jnp.where(i > 0, prv_ref[:, cs], 0.0)
            dws = [None] * CONV_W
            dba = dbg = None
            gs = slice(D + cs.start, D + cs.stop)
            for r0 in range(0, tr, rt):
                if r0 + rt == tr:
                    ext2 = jnp.concatenate([dc_ref[r0:tr, cs], nxt], axis=0)
                else:
                    ext2 = dc_ref[r0:r0 + rt + HALO, cs]
                acc = None
                for o, sh in _shifted(ext2, rt + HALO, rt):
                    k = CONV_W - 1 - o
                    if 0 <= k < CONV_W:
                        term = w_ref[k:k + 1, cs] * sh
                        acc = term if acc is None else acc + term
                sg = _sig(ug_ref[r0:r0 + rt, cs])
                da = acc * sg
                dgt = acc * ua_ref[r0:r0 + rt, cs] * sg * (1.0 - sg)
                du_ref[r0:r0 + rt, cs] = da.astype(BF)
                du_ref[r0:r0 + rt, gs] = dgt.astype(BF)
                dba = _colsum(da) if dba is None else dba + _colsum(da)
                dbg = _colsum(dgt) if dbg is None else dbg + _colsum(dgt)
                if r0 == 0:
                    ext = jnp.concatenate([prv, glu_ref[0:rt, cs]], axis=0)
                else:
                    ext = glu_ref[r0 - HALO:r0 + rt, cs]
                dcs = dc_ref[r0:r0 + rt, cs]
                for o, sh in _shifted(ext, rt + HALO, rt):
                    k = o - (HALO - CONV_W + 1)
                    if 0 <= k < CONV_W:
                        part = _colsum(dcs * sh)
                        dws[k] = part if dws[k] is None else dws[k] + part
            for k in range(CONV_W):
                dw_ref[k:k + 1, cs] += dws[k]
            dbu_ref[:, cs] += dba
            dbu_ref[:, gs] += dbg

    return _call(
        body, name=name, grid=(nsteps,), operands=[dc, dc, glu, glu, u, u, w],
        out_shape=[jax.ShapeDtypeStruct((T, 2 * D), BF), jax.ShapeDtypeStruct((HALO, D), F32),
                   jax.ShapeDtypeStruct((1, 2 * D), F32)],
        in_specs=[pl.BlockSpec((tr, D), lambda i: (i, 0)),
                  pl.BlockSpec((HALO, D), lambda i: (jnp.minimum((i + 1) * hb, last_hb), 0)),
                  pl.BlockSpec((tr, D), lambda i: (i, 0)),
                  pl.BlockSpec((HALO, D), lambda i: (jnp.maximum(i * hb - 1, 0), 0)),
                  pl.BlockSpec((tr, D), lambda i: (i, 0)),
                  pl.BlockSpec((tr, D), lambda i: (i, 1)),
                  pl.BlockSpec((HALO, D), lambda i: (0, 0))],
        out_specs=[pl.BlockSpec((tr, 2 * D), lambda i: (i, 0)),
                   pl.BlockSpec((HALO, D), lambda i: (0, 0)),
                   pl.BlockSpec((1, 2 * D), lambda i: (0, 0))],
        sem=("arbitrary",), comm=comm)


def _rec_consts(C):
    t = np.arange(C)
    lb = (t[None, :] <= t[:, None]).astype(np.float32)

    def cum_at(idx):
        return (t[None, :] <= idx[:, None]).astype(np.float32)

    blocks, masks = [], []
    mid = SUB * (t // SUB) + SUB // 2 - 1
    eq = lb - cum_at(mid)
    blocks += [eq, -eq]
    masks.append(((t[:, None] // SUB) == (t[None, :] // SUB)) & (t[None, :] <= t[:, None]))
    nb = SUB
    while nb < C:
        odd = (t // nb) % 2 == 1
        e_t = nb * (t // nb) - 1
        e_s = nb * (t // nb) + nb - 1
        blocks.append(np.where(odd[:, None], lb - cum_at(e_t), 0.0))
        blocks.append(np.where(~odd[:, None], cum_at(e_s) - lb, 0.0))
        masks.append(((t[:, None] // (2 * nb)) == (t[None, :] // (2 * nb))) & odd[:, None] & ~odd[None, :])
        nb *= 2
    blocks += [lb, 1.0 - lb, np.ones((HEAD, C), np.float32)]
    L = np.concatenate(blocks, axis=0).astype(np.float32)
    L3 = np.concatenate([L, L, L], axis=1)
    LT3 = np.concatenate([L.T, L.T], axis=1)
    m = np.stack(masks).astype(np.float32)
    mT = np.transpose(m, (0, 2, 1)).copy()
    return (jnp.asarray(L3, BF), jnp.asarray(LT3, BF), jnp.asarray(m), jnp.asarray(mT), len(masks))


def _split3(x):
    h = x.astype(BF)
    r = x - h.astype(F32)
    m = r.astype(BF)
    lo = (r - m.astype(F32)).astype(BF)
    return h, m, lo


def _split2(x):
    h = x.astype(BF)
    return h, (x - h.astype(F32)).astype(BF)


def _gates(qr, fz, lbr):
    sq = _sig(qr)
    q = qr * sq
    sg = _sig(fz)
    f = lbr + (1.0 - lbr) * sg
    fc = jnp.maximum(f, GATE_EPS)
    return q, 1.0 - f, jnp.log(fc), sq, sg, f, fc


def _rec_fwd(proj, lbr, ng, name, comm=None):
    T, D4 = proj.shape
    D = D4 // 4
    H = _heads(D)
    C = CHUNK
    nC = T // C
    L3, _, m, _, nl = _rec_consts(C)
    R = L3.shape[0]

    def body(q_ref, f_ref, v_ref, gr_ref, lb_ref, ng_ref, l_ref, m_ref, o_ref, y_ref, s_ref, st):
        @pl.when(pl.program_id(0) == 0)
        def _():
            st[...] = jnp.zeros((H, HEAD, HEAD), F32)

        q, k, logf = _gates(q_ref[...], f_ref[...], lb_ref[...])[:3]
        ex = jnp.exp(_dot(l_ref[...], jnp.concatenate(_split3(logf), axis=0)))
        vb = v_ref[...].astype(BF)
        s_ref[0] = st[...]
        outs = []
        for h in range(H):
            sl = slice(h * HEAD, (h + 1) * HEAD)
            qh, kh = q[:, sl], k[:, sl]
            p = jnp.zeros((C, C), F32)
            for lv in range(nl):
                qt = (qh * ex[2 * lv * C:(2 * lv + 1) * C, sl]).astype(BF)
                kt = (kh * ex[(2 * lv + 1) * C:(2 * lv + 2) * C, sl]).astype(BF)
                p = p + jnp.where(m_ref[lv] > 0.0, _dot_nt(qt, kt), 0.0)
            base = 2 * nl * C
            qhat = (qh * ex[base:base + C, sl]).astype(BF)
            khat = (kh * ex[base + C:base + 2 * C, sl]).astype(BF)
            elast = ex[base + 2 * C:base + 2 * C + HEAD, sl]
            sth = st[h]
            outs.append(_dot(p.astype(BF), vb[:, sl]) + _dot_nt(qhat, sth.astype(BF)))
            st[h] = elast * sth + _dot_tn(vb[:, sl], khat)
        o_ref[...] = jnp.concatenate(outs, axis=1)
        xh = jnp.concatenate([oh * lax.rsqrt(jnp.mean(oh * oh, axis=-1, keepdims=True) + RMS_EPS) for oh in outs],
                             axis=1)
        gr = gr_ref[...]
        y_ref[...] = (xh * ng_ref[...] * (gr * _sig(gr))).astype(BF)

    def cblk(cb):
        return pl.BlockSpec((C, D), lambda i: (i, cb))

    return _call(
        body, name=name, grid=(nC,), operands=[proj, proj, proj, proj, lbr, ng, L3, m],
        out_shape=[jax.ShapeDtypeStruct((T, D), F32), jax.ShapeDtypeStruct((T, D), BF),
                   jax.ShapeDtypeStruct((nC, H, HEAD, HEAD), F32)],
        in_specs=[cblk(0), cblk(1), cblk(2), cblk(3), pl.BlockSpec((1, D), lambda i: (0, 0)),
                  pl.BlockSpec((1, D), lambda i: (0, 0)),
                  pl.BlockSpec(L3.shape, lambda i: (0, 0)), pl.BlockSpec(m.shape, lambda i: (0, 0, 0))],
        out_specs=[pl.BlockSpec((C, D), lambda i: (i, 0)), pl.BlockSpec((C, D), lambda i: (i, 0)),
                   pl.BlockSpec((1, H, HEAD, HEAD), lambda i: (i, 0, 0, 0))],
        scratch=[pltpu.VMEM((H, HEAD, HEAD), F32)], sem=("arbitrary",), comm=comm)


def _rec_bwd(proj, lbr, do, dgr, ssave, name, comm=None):
    T, D4 = proj.shape
    D = D4 // 4
    H = _heads(D)
    C = min(CHUNK_BWD, T // 2)
    nC = T // C
    stride = C // CHUNK
    L3, LT3, m, mT, nl = _rec_consts(C)

    def body(q_ref, f_ref, v_ref, lb_ref, do_ref, dgr_ref, s_ref, l_ref, lt_ref, m_ref, mt_ref,
             dp_ref, dlb_ref, dst):
        @pl.when(pl.program_id(0) == 0)
        def _():
            dst[...] = jnp.zeros((H, HEAD, HEAD), F32)
            dlb_ref[...] = jnp.zeros((1, D), F32)

        qr = q_ref[...]
        lbv = lb_ref[...]
        q, k, logf, sq, sg, f, fc = _gates(qr, f_ref[...], lbv)
        ex = jnp.exp(_dot(l_ref[...], jnp.concatenate(_split3(logf), axis=0)))
        vb = v_ref[...].astype(BF)
        dob = do_ref[...].astype(BF)
        base = 2 * nl * C
        de = [[] for _ in range(2 * nl + 3)]
        dqs, dks, dvs = [], [], []
        for h in range(H):
            sl = slice(h * HEAD, (h + 1) * HEAD)
            qh, kh, vh, doh = q[:, sl], k[:, sl], vb[:, sl], dob[:, sl]
            dp = _dot_nt(doh, vh)
            dpt = _dot_nt(vh, doh)
            sth = s_ref[0, h]
            dsth = dst[h]
            dsb = dsth.astype(BF)
            pt = jnp.zeros((C, C), F32)
            dq = jnp.zeros((C, HEAD), F32)
            dk = jnp.zeros((C, HEAD), F32)
            for lv in range(nl):
                exq = ex[2 * lv * C:(2 * lv + 1) * C, sl]
                exk = ex[(2 * lv + 1) * C:(2 * lv + 2) * C, sl]
                qt = qh * exq
                kt = kh * exk
                qtb, ktb = qt.astype(BF), kt.astype(BF)
                pt = pt + jnp.where(mt_ref[lv] > 0.0, _dot_nt(ktb, qtb), 0.0)
                dqt = _dot(jnp.where(m_ref[lv] > 0.0, dp, 0.0).astype(BF), ktb)
                dkt = _dot(jnp.where(mt_ref[lv] > 0.0, dpt, 0.0).astype(BF), qtb)
                dq = dq + dqt * exq
                dk = dk + dkt * exk
                de[2 * lv].append(dqt * qt)
                de[2 * lv + 1].append(dkt * kt)
            exb = ex[base:base + C, sl]
            exkh = ex[base + C:base + 2 * C, sl]
            elast = ex[base + 2 * C:base + 2 * C + HEAD, sl]
            qhat = qh * exb
            khat = kh * exkh
            dqh = _dot(doh, sth.astype(BF))
            dkh = _dot(vh, dsb)
            dq = dq + dqh * exb
            dk = dk + dkh * exkh
            de[2 * nl].append(dqh * qhat)
            de[2 * nl + 1].append(dkh * khat)
            de[2 * nl + 2].append(dsth * sth * elast)
            dvs.append(_dot(pt.astype(BF), doh) + _dot_nt(khat.astype(BF), dsb))
            dst[h] = elast * dsth + _dot_tn(doh, qhat.astype(BF))
            dqs.append(dq)
            dks.append(dk)
        de_all = jnp.concatenate([jnp.concatenate(b, axis=1) for b in de], axis=0)
        dlogf = _dot(lt_ref[...], jnp.concatenate(_split2(de_all), axis=0))
        dq = jnp.concatenate(dqs, axis=1)
        dk = jnp.concatenate(dks, axis=1)
        dv = jnp.concatenate(dvs, axis=1)
        ind = jnp.where(f > GATE_EPS, 1.0, jnp.where(f == GATE_EPS, 0.5, 0.0))
        df = dlogf * ind / fc - dk
        dfz = df * (1.0 - lbv) * sg * (1.0 - sg)
        dlb_ref[...] += _colsum(df * (1.0 - sg))
        dqr = dq * (sq * (1.0 + qr * (1.0 - sq)))
        dp_ref[...] = jnp.concatenate([dqr.astype(BF), dfz.astype(BF), dv.astype(BF), dgr_ref[...]], axis=1)

    def cblk(cb):
        return pl.BlockSpec((C, D), lambda i: (nC - 1 - i, cb))

    def whole(a):
        nd = a.ndim
        return pl.BlockSpec(a.shape, lambda i: (0,) * nd)

    return _call(
        body, name=name, grid=(nC,), operands=[proj, proj, proj, lbr, do, dgr, ssave, L3, LT3, m, mT],
        out_shape=[jax.ShapeDtypeStruct((T, D4), BF), jax.ShapeDtypeStruct((1, D), F32)],
        in_specs=[cblk(0), cblk(1), cblk(2), pl.BlockSpec((1, D), lambda i: (0, 0)),
                  pl.BlockSpec((C, D), lambda i: (nC - 1 - i, 0)),
                  pl.BlockSpec((C, D), lambda i: (nC - 1 - i, 0)),
                  pl.BlockSpec((1, H, HEAD, HEAD), lambda i: (stride * (nC - 1 - i), 0, 0, 0)),
                  whole(L3), whole(LT3), whole(m), whole(mT)],
        out_specs=[pl.BlockSpec((C, D4), lambda i: (nC - 1 - i, 0)),
                   pl.BlockSpec((1, D), lambda i: (0, 0))],
        scratch=[pltpu.VMEM((H, HEAD, HEAD), F32)], sem=("arbitrary",), comm=comm)


def _softmax_rows(lg_ref):
    n = lg_ref.shape[0]
    rows = [lg_ref[l:l + 1, :] for l in range(n)]
    mx = rows[0]
    for r in rows[1:]:
        mx = jnp.maximum(mx, r)
    es = [jnp.exp(r - mx) for r in rows]
    tot = es[0]
    for e in es[1:]:
        tot = tot + e
    return [e / tot for e in es]


def _lb_fwd(logits):
    n, D = logits.shape

    def body(lg_ref, o_ref):
        soft = _softmax_rows(lg_ref)
        acc = jnp.zeros((1, D), F32)
        o_ref[0:1, :] = acc
        for j in range(1, n):
            acc = acc + soft[j]
            o_ref[j:j + 1, :] = acc

    return pl.pallas_call(body, name="lb_fwd", out_shape=jax.ShapeDtypeStruct((n, D), F32))(logits)


def _small_reduce(parts, logits, lb_row0):
    _, R, D = parts.shape
    n = logits.shape[0]

    def body(p_ref, lg_ref, o_ref):
        acc = p_ref[0]
        for d in range(1, N_DEV):
            acc = acc + p_ref[d]
        o_ref[...] = acc
        soft = _softmax_rows(lg_ref)
        dlb = [o_ref[lb_row0 + j:lb_row0 + j + 1, :] for j in range(n)]
        dsoft = [jnp.zeros((1, D), F32)]
        for l in range(1, n):
            s = dlb[l]
            for j in range(l + 1, n):
                s = s + dlb[j]
            dsoft.append(s)
        dot = soft[0] * dsoft[0]
        for l in range(1, n):
            dot = dot + soft[l] * dsoft[l]
        for l in range(n):
            o_ref[lb_row0 + l:lb_row0 + l + 1, :] = soft[l] * (dsoft[l] - dot)

    return pl.pallas_call(body, name="small_reduce", out_shape=jax.ShapeDtypeStruct((R, D), F32))(parts, logits)


def _adam_math(g, w, m, v):
    m2 = ADAM_B1 * m + (1.0 - ADAM_B1) * g
    v2 = ADAM_B2 * v + (1.0 - ADAM_B2) * (g * g)
    mh = m2 / (1.0 - ADAM_B1 ** ADAM_STEP)
    vh = v2 / (1.0 - ADAM_B2 ** ADAM_STEP)
    delta = -ADAM_LR * (mh / (jnp.sqrt(vh) + ADAM_EPS) + ADAM_WD * w)
    return delta, m2, v2


def _adamw(recv, w, m, v, name):
    nsrc = recv.shape[0]
    shp = w.shape
    cols = shp[-1]
    rows = int(np.prod(shp[:-1]))
    r2 = recv.reshape(nsrc, rows, cols)
    tr = min(rows, max(8, (1 << 20) // (cols * nsrc)))
    while rows % tr:
        tr //= 2

    def body(r_ref, w_ref, m_ref, v_ref, g_ref, d_ref, nm_ref, nv_ref):
        g = r_ref[0].astype(F32)
        for s in range(1, nsrc):
            g = g + r_ref[s].astype(F32)
        delta, m2, v2 = _adam_math(g, w_ref[...], m_ref[...], v_ref[...])
        g_ref[...] = g
        d_ref[...] = delta
        nm_ref[...] = m2
        nv_ref[...] = v2

    blk = pl.BlockSpec((tr, cols), lambda i: (i, 0))
    outs = pl.pallas_call(
        body, name=name, grid=(rows // tr,),
        out_shape=[jax.ShapeDtypeStruct((rows, cols), F32)] * 4,
        in_specs=[pl.BlockSpec((nsrc, tr, cols), lambda i: (0, i, 0)), blk, blk, blk],
        out_specs=[blk] * 4,
        compiler_params=_cp(("parallel",)),
    )(r2, w.reshape(rows, cols), m.reshape(rows, cols), v.reshape(rows, cols))
    return [o.reshape(shp) for o in outs]


def _pack(arrs, lane):
    flat = jnp.concatenate([a.reshape(-1) for a in arrs])
    n = flat.shape[0]
    rows = -(-n // lane)
    rows = -(-rows // 8) * 8
    flat = jnp.pad(flat, (0, rows * lane - n))
    return flat.reshape(rows, lane)


def _unpack(packed, shapes):
    flat = packed.reshape(-1)
    out, off = [], 0
    for s in shapes:
        n = int(np.prod(s))
        out.append(flat[off:off + n].reshape(s))
        off += n
    return out


def kernel(x, ln_mix_g, ln_mix_b, ln_ffn_g, ln_ffn_b, ffn_w1, ffn_w2, a_w_in, a_lb_logits, a_norm_g, a_w_out, b_w_pw1, b_b_pw1, b_w_dw, b_b_dw, b_ln_g, b_ln_b, b_w_pw2, b_b_pw2, loss_target, m_ln_mix_g, m_ln_mix_b, m_ln_ffn_g, m_ln_ffn_b, m_ffn_w1, m_ffn_w2, m_a_w_in, m_a_lb_logits, m_a_norm_g, m_a_w_out, m_b_w_pw1, m_b_b_pw1, m_b_w_dw, m_b_b_dw, m_b_ln_g, m_b_ln_b, m_b_w_pw2, m_b_b_pw2, v_ln_mix_g, v_ln_mix_b, v_ln_ffn_g, v_ln_ffn_b, v_ffn_w1, v_ffn_w2, v_a_w_in, v_a_lb_logits, v_a_norm_g, v_a_w_out, v_b_w_pw1, v_b_b_pw1, v_b_w_dw, v_b_b_dw, v_b_ln_g, v_b_ln_b, v_b_w_pw2, v_b_b_pw2):
    T, D = x.shape[1], x.shape[2]
    nA, nB = a_w_in.shape[0], b_w_pw1.shape[0]
    me = 4 * lax.axis_index("x") + 2 * lax.axis_index("y") + lax.axis_index("c")
    xin = x[0]
    tgt = loss_target[0]

    small_names = [b_b_pw1, b_w_dw, b_b_dw, b_ln_g, b_ln_b, b_b_pw2]
    sp = _pack(small_names, 128)
    wsrc = {"W1": ffn_w1.astype(BF), "W2": ffn_w2.astype(BF), "Win": a_w_in.astype(BF),
            "Wout": a_w_out.astype(BF), "Wp1": b_w_pw1.astype(BF), "Wp2": b_w_pw2.astype(BF), "small": sp[None]}
    W = {k: jax.ShapeDtypeStruct((N_DEV,) + v.shape, v.dtype) for k, v in wsrc.items()}
    GW = {k: jax.ShapeDtypeStruct((N_DEV,) + v.shape, BF) for k, v in wsrc.items() if k != "small"}
    R = {k: jax.ShapeDtypeStruct(v.shape, BF) for k, v in GW.items()}

    def names_of(items):
        out = []
        for n, _ in items:
            if n not in out:
                out.append(n)
        return out

    def hosted(fn, key, sched, mode, src, bufs, *args, **kw):
        items = sched.get(key)
        if items is None:
            return fn(*args, name=key, **kw)
        comm = (mode, [(src[n], l, bufs[n]) for n, l in items])
        outs, new = fn(*args, name=key, comm=comm, **kw)
        for n, b in zip(names_of(items), new):
            bufs[n] = b
        return outs

    first = [("Win", 0), ("small", 0)]
    for n, b in zip(names_of(first), _exchange(("gather", [(wsrc[n], l, W[n]) for n, l in first]), "gather_first")):
        W[n] = b
    fwd_sched = {"a0_proj": [("Wout", 0), ("Wp1", 0)], "a0_rec_fwd": [("W1", 0), ("W2", 0), ("Wp2", 0)],
                 "l0_ffn_up": [("W2", 1)], "l0_ffn_down": [("W1", 1)],
                 "b0_conv": [("Win", 1), ("Wout", 1)], "l1_ffn_up": [("W2", 2)], "l1_ffn_down": [("W1", 2)],
                 "a1_rec_fwd": [("Wp1", 1), ("Wp2", 1), ("W1", 3), ("W2", 3)]}
    bwd_sched = {"a1_rec_bwd": [("W1", 3), ("W2", 3), ("Wp1", 1), ("Wp2", 1), ("W1", 2), ("W2", 2)],
                 "b0_conv_bwd": [("Win", 1), ("Wout", 1)],
                 "a0_rec_bwd": [("W1", 1), ("W2", 1), ("Wp1", 0), ("Wp2", 0), ("W1", 0), ("W2", 0), ("Wout", 0)],
                 "a0_proj_dx": [("Win", 0)]}

    def fwd(fn, key, *args, **kw):
        return hosted(fn, key, fwd_sched, "gather", wsrc, W, *args, **kw)

    def bwd(fn, key, *args, **kw):
        return hosted(fn, key, bwd_sched, "scatter", GW, R, *args, **kw)

    spg = W["small"][:, 0]
    sm = [jnp.stack(p) for p in zip(*[_unpack(spg[d], [a.shape for a in small_names]) for d in range(N_DEV)])]
    bpw1 = jnp.transpose(sm[0], (1, 0, 2)).reshape(nB, 1, 2 * D)
    wdw = jnp.transpose(sm[1], (1, 2, 0, 3)).reshape(nB, CONV_W, D)
    wdw = jnp.pad(wdw, ((0, 0), (0, HALO - CONV_W), (0, 0)))
    bdw, blng, blnb, bpw2 = [jnp.transpose(s, (1, 0, 2)).reshape(nB, 1, D) for s in sm[2:]]
    lb_all = _lb_fwd(a_lb_logits)

    zx = [(D, F32), (D, F32)]
    ln_out = dict(outs=[(D, F32)], sums=[D, D, D])
    ln_mix = [(ln_mix_g[i:i + 1], ln_mix_b[i:i + 1]) for i in range(DEPTH)]
    ln_ffn = [(ln_ffn_g[i:i + 1], ln_ffn_b[i:i + 1]) for i in range(DEPTH)]
    saved = []
    h = xin
    for i in range(DEPTH):
        j = i // 2
        sv = {"xin": h}
        if i % 2 == 0:
            proj = fwd(_mm, f"a{j}_proj", h, W["Win"], j, "nn_col", out_dtype=F32)
            o, yg, ssave = fwd(_rec_fwd, f"a{j}_rec_fwd", proj, lb_all[j:j + 1], a_norm_g[j:j + 1])
            z1, x1 = _mm(yg, W["Wout"], j, "nn_row", out_dtype=F32, name=f"a{j}_out", outs=zx,
                         epi=lambda acc, r, g, b: _z_and_ln(acc + ALPHA * r, g, b), extras=[h, *ln_mix[i]])
            sv.update(proj=proj, o=o, ssave=ssave, yg=yg)
        else:
            u, glu = _mm(h, W["Wp1"], j, "nn_col", out_dtype=F32, name=f"b{j}_pw1",
                         outs=[(2 * D, F32), (D, F32)],
                         epi=lambda acc, b: _u_and_glu(acc + b), extras=[bpw1[j]])
            cv, s = fwd(_conv_fwd, f"b{j}_conv", glu, wdw[j], bdw[j], blng[j], blnb[j])
            z1, x1 = _mm(s, W["Wp2"], j, "nn_row", out_dtype=F32, name=f"b{j}_pw2", outs=zx,
                         epi=lambda acc, bb, r, g, b: _z_and_ln(acc + bb + ALPHA * r, g, b),
                         extras=[bpw2[j], h, *ln_mix[i]])
            sv.update(u=u, glu=glu, cv=cv, s=s)
        hh = fwd(_mm, f"l{i}_ffn_up", x1, W["W1"], i, "nn_col", out_dtype=BF)
        relu2 = lambda t: jnp.square(jnp.maximum(t, 0))
        if i < DEPTH - 1:
            z2, h = fwd(_mm, f"l{i}_ffn_down", hh, W["W2"], i, "nn_row", out_dtype=F32, outs=zx, a_fn=relu2,
                        epi=lambda acc, r, g, b: _z_and_ln(acc + ALPHA * r, g, b), extras=[x1, *ln_ffn[i]])
        else:
            z2 = None
            dz2, dg_top, db_top, loss_row = fwd(
                _mm, f"l{i}_ffn_down", hh, W["W2"], i, "nn_row", out_dtype=F32, a_fn=relu2, **ln_out,
                epi=lambda acc, r, g, b, t: _top_tile(acc + ALPHA * r, g, b, t), extras=[x1, *ln_ffn[i], tgt])
        sv.update(z1=z1, x1=x1, hh=hh, z2=z2)
        saved.append(sv)

    rows = {}
    top = DEPTH - 1
    rows[("ffn_g", top)], rows[("ffn_b", top)] = dg_top, db_top
    for i in reversed(range(DEPTH)):
        j = i // 2
        sv = saved[i]
        if i > 0:
            below = dict(epi=lambda acc, r, z, g: _ln_bwd_tile(acc + ALPHA * r, z, g), **ln_out)
            below_extras = [saved[i - 1]["z2"], ln_ffn[i - 1][0]]
        else:
            below = dict(epi=lambda acc, r: acc + ALPHA * r)
            below_extras = []
        dh = _mm(dz2, W["W2"], i, "nt_row", out_dtype=BF, name=f"l{i}_ffn_down_dx",
                 epi=lambda acc, hv: acc * (2.0 * jnp.maximum(hv.astype(F32), 0.0)), extras=[sv["hh"]])
        GW["W2"] = _mm_tn(GW["W2"], dz2, sv["hh"], i, "rows_t", name=f"l{i}_ffn_down_dw",
                         b_fn=lambda t: jnp.square(jnp.maximum(t, 0)))
        GW["W1"] = _mm_tn(GW["W1"], sv["x1"], dh, i, "cols", name=f"l{i}_ffn_up_dw")
        dz1, dg, db, dz1sum = _mm(dh, W["W1"], i, "nt_col", out_dtype=F32, name=f"l{i}_ffn_up_dx",
                                  epi=lambda acc, r, z, g: _ln_bwd_tile(acc + ALPHA * r, z, g),
                                  extras=[dz2, sv["z1"], ln_mix[i][0]], **ln_out)
        rows[("mix_g", i)], rows[("mix_b", i)] = dg, db
        if i % 2 == 0:
            GW["Wout"] = _mm_tn(GW["Wout"], sv["yg"], dz1, j, "rows", name=f"a{j}_out_dw")
            do, dgr, dng = _mm(dz1, W["Wout"], j, "nt_row", out_dtype=F32, name=f"a{j}_out_dx",
                               outs=[(D, F32), (D, BF)], sums=[D], epi=_gate_bwd_tile,
                               extras=[sv["o"], (sv["proj"], 3), a_norm_g[j:j + 1]])
            rows[("ng", j)] = dng
            dproj, dlb = bwd(_rec_bwd, f"a{j}_rec_bwd", sv["proj"], lb_all[j:j + 1], do, dgr, sv["ssave"])
            rows[("lb", j)] = dlb
            GW["Win"] = _mm_tn(GW["Win"], sv["xin"], dproj, j, "cols", name=f"a{j}_proj_dw")
            res = bwd(_mm, f"a{j}_proj_dx", dproj, W["Win"], j, "nt_col", out_dtype=F32,
                      extras=[dz1, *below_extras], **below)
        else:
            rows[("bpw2", j)] = dz1sum
            GW["Wp2"] = _mm_tn(GW["Wp2"], sv["s"], dz1, j, "rows", name=f"b{j}_pw2_dw")
            dc, dlg, dlb_, dcs = _mm(dz1, W["Wp2"], j, "nt_row", out_dtype=F32, name=f"b{j}_pw2_dx",
                                     outs=[(D, F32)], sums=[D, D, D], epi=_silu_ln_bwd_tile,
                                     extras=[sv["cv"], blng[j], blnb[j]])
            rows[("blng", j)], rows[("blnb", j)], rows[("bdw", j)] = dlg, dlb_, dcs
            du, dwdw, dbu = bwd(_conv_bwd, f"b{j}_conv_bwd", dc, sv["glu"], sv["u"], wdw[j])
            rows[("wdw", j)] = dwdw[:CONV_W]
            rows[("bpw1", j)] = dbu.reshape(2, D)
            GW["Wp1"] = _mm_tn(GW["Wp1"], sv["xin"], du, j, "cols", name=f"b{j}_pw1_dw")
            res = _mm(du, W["Wp1"], j, "nt_col", out_dtype=F32, name=f"b{j}_pw1_dx",
                      extras=[dz1, *below_extras], **below)
        if i > 0:
            dz2, dg, db, _ = res
            rows[("ffn_g", i - 1)], rows[("ffn_b", i - 1)] = dg, db
        else:
            dx = res
    grad_x = dx[None]

    order = ([("mix_g", i) for i in range(DEPTH)] + [("mix_b", i) for i in range(DEPTH)]
             + [("ffn_g", i) for i in range(DEPTH)] + [("ffn_b", i) for i in range(DEPTH)])
    lb_row0 = len(order)
    order += [("lb", j) for j in range(nA)] + [("ng", j) for j in range(nA)]
    for j in range(nB):
        order += [("bpw1", j), ("wdw", j), ("bdw", j), ("blng", j), ("blnb", j), ("bpw2", j)]
    pieces, offs, off = [], {}, 0
    for key in order:
        offs[key] = off
        pieces.append(rows[key])
        off += rows[key].shape[0]
    offs["loss"] = off
    pieces.append(loss_row)
    off += 1
    part = jnp.concatenate(pieces, axis=0)
    part = jnp.pad(part, ((0, -off % 8), (0, 0)))
    parts = _exchange(("gather", [(part[None], 0, jax.ShapeDtypeStruct((N_DEV, 1) + part.shape, F32))]),
                      "gather_small_grads")[0][:, 0]
    G = _small_reduce(parts, a_lb_logits, lb_row0)
    loss = jnp.sum(G[offs["loss"]])

    def rep(kind, n):
        return jnp.concatenate([G[offs[(kind, i)]:offs[(kind, i)] + 1] for i in range(n)], axis=0)

    def shard_cols(full, width):
        return lax.dynamic_slice_in_dim(full, me * width, width, axis=full.ndim - 1)

    g_small = {
        "ln_mix_g": rep("mix_g", DEPTH), "ln_mix_b": rep("mix_b", DEPTH),
        "ln_ffn_g": rep("ffn_g", DEPTH), "ln_ffn_b": rep("ffn_b", DEPTH),
        "a_lb_logits": rep("lb", nA), "a_norm_g": rep("ng", nA),
        "b_b_pw1": shard_cols(jnp.stack([G[offs[("bpw1", j)]:offs[("bpw1", j)] + 2].reshape(2 * D)
                                         for j in range(nB)]), 2 * D // N_DEV),
        "b_w_dw": shard_cols(jnp.stack([G[offs[("wdw", j)]:offs[("wdw", j)] + CONV_W] for j in range(nB)]),
                             D // N_DEV),
        "b_b_dw": shard_cols(rep("bdw", nB), D // N_DEV),
        "b_ln_g": shard_cols(rep("blng", nB), D // N_DEV),
        "b_ln_b": shard_cols(rep("blnb", nB), D // N_DEV),
        "b_b_pw2": shard_cols(rep("bpw2", nB), D // N_DEV),
    }
    small_w = {"ln_mix_g": (ln_mix_g, m_ln_mix_g, v_ln_mix_g), "ln_mix_b": (ln_mix_b, m_ln_mix_b, v_ln_mix_b),
               "ln_ffn_g": (ln_ffn_g, m_ln_ffn_g, v_ln_ffn_g), "ln_ffn_b": (ln_ffn_b, m_ln_ffn_b, v_ln_ffn_b),
               "a_lb_logits": (a_lb_logits, m_a_lb_logits, v_a_lb_logits),
               "a_norm_g": (a_norm_g, m_a_norm_g, v_a_norm_g),
               "b_b_pw1": (b_b_pw1, m_b_b_pw1, v_b_b_pw1), "b_w_dw": (b_w_dw, m_b_w_dw, v_b_w_dw),
               "b_b_dw": (b_b_dw, m_b_b_dw, v_b_b_dw), "b_ln_g": (b_ln_g, m_b_ln_g, v_b_ln_g),
               "b_ln_b": (b_ln_b, m_b_ln_b, v_b_ln_b), "b_b_pw2": (b_b_pw2, m_b_b_pw2, v_b_b_pw2)}
    snames = list(small_w)
    sshapes = [small_w[k][0].shape for k in snames]
    pg = _pack([g_small[k] for k in snames], 1024)
    pw, pm, pv = [_pack([small_w[k][q] for k in snames], 1024) for q in range(3)]
    sres = _adamw(pg[None], pw, pm, pv, "adamw_small")
    sres = [dict(zip(snames, _unpack(r, sshapes))) for r in sres]

    recv = [R[n] for n in ["W1", "W2", "Win", "Wout", "Wp1", "Wp2"]]
    big = {}
    for nm, rv, (w, m, v) in zip(
            ["ffn_w1", "ffn_w2", "a_w_in", "a_w_out", "b_w_pw1", "b_w_pw2"], recv,
            [(ffn_w1, m_ffn_w1, v_ffn_w1), (ffn_w2, m_ffn_w2, v_ffn_w2), (a_w_in, m_a_w_in, v_a_w_in),
             (a_w_out, m_a_w_out, v_a_w_out), (b_w_pw1, m_b_w_pw1, v_b_w_pw1), (b_w_pw2, m_b_w_pw2, v_b_w_pw2)]):
        big[nm] = _adamw(rv, w, m, v, f"adamw_{nm}")

    names = ["ln_mix_g", "ln_mix_b", "ln_ffn_g", "ln_ffn_b", "ffn_w1", "ffn_w2", "a_w_in", "a_lb_logits",
             "a_norm_g", "a_w_out", "b_w_pw1", "b_b_pw1", "b_w_dw", "b_b_dw", "b_ln_g", "b_ln_b", "b_w_pw2",
             "b_b_pw2"]
    out = [loss, grad_x]
    for q in range(4):
        for nm in names:
            out.append(big[nm][q] if nm in big else sres[q][nm])
    return tuple(out)
```

```python
import functools

import numpy as np
import jax
import jax.numpy as jnp
from jax import lax
from jax.experimental import pallas as pl
from jax.experimental.pallas import tpu as pltpu

F32 = jnp.float32
BF = jnp.bfloat16

N_DEV = 8
DEPTH = 4
HEAD = 128
CHUNK = 64
CHUNK_BWD = 64
SUB = 8
CONV_W = 31
HALO = 32
CONV_ROWS = 128
ALPHA = (2.0 * DEPTH) ** 0.25
LN_EPS = 1e-5
RMS_EPS = 1e-6
GATE_EPS = 1e-6
ADAM_LR = 0.001
ADAM_B1 = 0.9
ADAM_B2 = 0.999
ADAM_EPS = 1e-08
ADAM_WD = 0.01
ADAM_STEP = 10
VMEM_LIMIT = 56 * 1024 * 1024
MESH = pl.DeviceIdType.MESH


def _cp(sem=None):
    return pltpu.CompilerParams(vmem_limit_bytes=VMEM_LIMIT, dimension_semantics=sem)


def _dot(a, b):
    return jnp.dot(a, b, preferred_element_type=F32)


def _dot_nt(a, b):
    return lax.dot_general(a, b, (((1,), (1,)), ((), ())), preferred_element_type=F32)


def _dot_tn(a, b):
    return lax.dot_general(a, b, (((0,), (0,)), ((), ())), preferred_element_type=F32)


def _sig(x):
    return jax.nn.sigmoid(x)


def _distinct(items):
    out = []
    for it in items:
        if not any(it is q for q in out):
            out.append(it)
    return out


def _index_of(items, it):
    return next(i for i, q in enumerate(items) if q is it)


def _comm_copies(mode, plan, src_refs, buf_refs, send_sems, recv_sems, loc_sems):
    x, y, c = lax.axis_index("x"), lax.axis_index("y"), lax.axis_index("c")
    me = 4 * x + 2 * y + c
    locs, first, passed = [], [], []
    for k, (si, l, bi) in enumerate(plan):
        src = src_refs[si].at[l] if mode == "gather" else src_refs[si].at[me, l]
        locs.append(pltpu.make_async_copy(src, buf_refs[bi].at[me, l], loc_sems.at[k]))

    def remote(k, r, src, slot, l, bi, to):
        return pltpu.make_async_remote_copy(
            src_ref=src, dst_ref=buf_refs[bi].at[slot, l],
            send_sem=send_sems.at[k, r], recv_sem=recv_sems.at[k, r],
            device_id=to, device_id_type=MESH)

    if mode == "scatter":
        for r in range(1, N_DEV):
            px = (1 - x) if (r >> 2) & 1 else x
            py = (1 - y) if (r >> 1) & 1 else y
            pc = (1 - c) if r & 1 else c
            pid = 4 * px + 2 * py + pc
            for k, (si, l, bi) in enumerate(plan):
                first.append(remote(k, r - 1, src_refs[si].at[pid, l], me, l, bi, (px, py, pc)))
        return locs, first, passed
    sibling = (x, y, 1 - c)
    for k, (si, l, bi) in enumerate(plan):
        first.append(remote(k, 0, src_refs[si].at[l], me, l, bi, sibling))
    for r, (qx, qy) in enumerate([(1 - x, y), (x, 1 - y), (1 - x, 1 - y)]):
        qid = 4 * qx + 2 * qy + c
        for k, (si, l, bi) in enumerate(plan):
            first.append(remote(k, 1 + r, src_refs[si].at[l], me, l, bi, (qx, qy, c)))
            passed.append(remote(k, 4 + r, buf_refs[bi].at[qid, l], qid, l, bi, sibling))
    return locs, first, passed


def _call(body, *, name, grid, operands, in_specs, out_shape, out_specs, scratch=(), sem=None,
          aliases=None, comm=None):
    aliases = dict(aliases or {})
    if comm is None:
        return pl.pallas_call(
            body, name=name, grid=grid, out_shape=list(out_shape), in_specs=list(in_specs),
            out_specs=list(out_specs), scratch_shapes=list(scratch), input_output_aliases=aliases,
            compiler_params=_cp(sem),
        )(*operands)
    mode, pieces = comm
    srcs = _distinct([p[0] for p in pieces])
    bufs = _distinct([p[2] for p in pieces])
    plan = [(_index_of(srcs, s), l, _index_of(bufs, b)) for (s, l, b) in pieces]
    n_in, n_out, n_scr, ns, nb, npc = len(operands), len(out_shape), len(scratch), len(srcs), len(bufs), len(plan)
    nsteps = grid[0]
    old = [b for b in bufs if not isinstance(b, jax.ShapeDtypeStruct)]
    nbi = len(old)

    def wrapped(*refs):
        ins = refs[:n_in]
        src_refs = refs[n_in:n_in + ns]
        o0 = n_in + ns + nbi
        outs = refs[o0:o0 + n_out]
        buf_refs = refs[o0 + n_out:o0 + n_out + nb]
        s0 = o0 + n_out + nb
        scr = refs[s0:s0 + n_scr]
        sems = refs[s0 + n_scr:]
        step = pl.program_id(0)

        @pl.when(step == 0)
        def _():
            locs, first, _ = _comm_copies(mode, plan, src_refs, buf_refs, *sems)
            for d in locs + first:
                d.start()

        body(*ins, *outs, *scr)

        @pl.when(step == (3 * nsteps) // 4)
        def _():
            _, first, passed = _comm_copies(mode, plan, src_refs, buf_refs, *sems)
            for d in first[len(first) - len(passed):] if passed else []:
                d.wait_recv()
            for d in passed:
                d.start()

        @pl.when(step == nsteps - 1)
        def _():
            locs, first, passed = _comm_copies(mode, plan, src_refs, buf_refs, *sems)
            for d in first + passed:
                d.wait_send()
            for d in first[:len(first) - len(passed)] + passed:
                d.wait_recv()
            for d in locs:
                d.wait()

    anyspec = pl.BlockSpec(memory_space=pl.ANY)
    for k, b in enumerate(old):
        aliases[n_in + ns + k] = n_out + _index_of(bufs, b)
    res = pl.pallas_call(
        wrapped, name=name, grid=grid,
        out_shape=list(out_shape) + [jax.ShapeDtypeStruct(b.shape, b.dtype) for b in bufs],
        in_specs=list(in_specs) + [anyspec] * (ns + nbi),
        out_specs=list(out_specs) + [anyspec] * nb,
        scratch_shapes=list(scratch) + [pltpu.SemaphoreType.DMA((npc, N_DEV - 1)),
                                        pltpu.SemaphoreType.DMA((npc, N_DEV - 1)),
                                        pltpu.SemaphoreType.DMA((npc,))],
        input_output_aliases=aliases,
        compiler_params=pltpu.CompilerParams(vmem_limit_bytes=VMEM_LIMIT, has_side_effects=True,
                                             dimension_semantics=("arbitrary",) * len(grid)),
    )(*operands, *srcs, *old)
    return list(res[:n_out]), list(res[n_out:])


def _exchange(comm, name):
    def body():
        pass

    return _call(body, name=name, grid=(1,), operands=[], in_specs=[], out_shape=[], out_specs=[],
                 comm=comm)[1]


def _mm(a, w, l, kind, *, out_dtype, name, tm=512, a_fn=None, epi=None, extras=(), comm=None,
        outs=None, sums=()):
    T, Ka = a.shape
    _, _, d2, d3 = w.shape
    n_out = {"nn_col": N_DEV * d3, "nn_row": d3, "nt_col": d2, "nt_row": N_DEV * d2}[kind]
    tm = min(tm, T)
    ne = len(extras)
    single = outs is None and not sums
    outs = [(n_out, out_dtype)] if outs is None else outs
    no, ns = len(outs), len(sums)

    def body(*refs):
        a_ref, w_ref = refs[0], refs[1]
        e_refs = refs[2:2 + ne]
        o_refs = refs[2 + ne:2 + ne + no]
        s_refs = refs[2 + ne + no:]
        av = a_ref[...]
        if a_fn is not None:
            av = a_fn(av)
        av = av.astype(BF)
        if kind == "nn_col":
            acc = jnp.concatenate([_dot(av, w_ref[j]) for j in range(N_DEV)], axis=1)
        elif kind == "nn_row":
            acc = _dot(av, w_ref[...].reshape(N_DEV * d2, d3))
        elif kind == "nt_col":
            acc = _dot_nt(av[:, 0:d3], w_ref[0])
            for j in range(1, N_DEV):
                acc = acc + _dot_nt(av[:, j * d3:(j + 1) * d3], w_ref[j])
        else:
            acc = jnp.concatenate([_dot_nt(av, w_ref[j]) for j in range(N_DEV)], axis=1)
        res = acc if epi is None else epi(acc, *[e[...] for e in e_refs])
        res = res if isinstance(res, tuple) else (res,)
        for o_ref, val in zip(o_refs, res[:no]):
            o_ref[...] = val.astype(o_ref.dtype)
        if ns:
            first = pl.program_id(0) == 0

            @pl.when(first)
            def _():
                for s_ref, val in zip(s_refs, res[no:]):
                    s_ref[...] = val

            @pl.when(jnp.logical_not(first))
            def _():
                for s_ref, val in zip(s_refs, res[no:]):
                    s_ref[...] += val

    in_specs = [pl.BlockSpec((tm, Ka), lambda i: (i, 0)),
                pl.BlockSpec((N_DEV, None, d2, d3), lambda i: (0, l, 0, 0))]
    extras = [e if isinstance(e, tuple) else (e, 0) for e in extras]
    for e, cb in extras:
        if e.shape[0] == 1:
            in_specs.append(pl.BlockSpec((1, n_out), lambda i: (0, 0)))
        else:
            in_specs.append(pl.BlockSpec((tm, n_out), functools.partial(lambda i, cb: (i, cb), cb=cb)))
    extras = [e for e, _ in extras]
    out_shape = [jax.ShapeDtypeStruct((T, wd), dt) for wd, dt in outs]
    out_shape += [jax.ShapeDtypeStruct((1, wd), F32) for wd in sums]
    out_specs = [pl.BlockSpec((tm, wd), lambda i: (i, 0)) for wd, _ in outs]
    out_specs += [pl.BlockSpec((1, wd), lambda i: (0, 0)) for wd in sums]
    res = _call(body, name=name, grid=(T // tm,), operands=[a, w, *extras], in_specs=in_specs,
                out_shape=out_shape, out_specs=out_specs,
                sem=("arbitrary",) if ns else ("parallel",), comm=comm)
    if comm is None:
        return res[0] if single else res
    return (res[0][0] if single else res[0]), res[1]


def _mm_tn(g, s, b, l, kind, *, name, tt=1024, b_fn=None):
    T, ws = s.shape
    wb = b.shape[1]
    _, _, d2, d3 = g.shape
    tt = min(tt, T)
    nsteps = T // tt
    cw = wb // N_DEV

    fresh = isinstance(g, jax.ShapeDtypeStruct)

    def body(*refs):
        s_ref, b_ref, o_ref, acc, stage, sem = refs if fresh else refs[1:]
        t = pl.program_id(0)

        @pl.when(t == 0)
        def _():
            acc[...] = jnp.zeros((ws, wb), F32)

        st = s_ref[...].astype(F32).T.astype(BF)
        for j in range(N_DEV):
            bv = b_ref[:, j * cw:(j + 1) * cw]
            if b_fn is not None:
                bv = b_fn(bv)
            acc[:, j * cw:(j + 1) * cw] += _dot(st, bv.astype(BF))

        @pl.when(t == nsteps - 1)
        def _():
            for j in range(N_DEV):
                if kind == "cols":
                    blk = acc[:, j * d3:(j + 1) * d3]
                elif kind == "rows":
                    blk = acc[j * d2:(j + 1) * d2, :]
                else:
                    blk = acc[:, j * d2:(j + 1) * d2].T
                stage[...] = blk.astype(BF)
                cp = pltpu.make_async_copy(stage, o_ref.at[j, l], sem)
                cp.start()
                cp.wait()

    anyspec = pl.BlockSpec(memory_space=pl.ANY)
    return pl.pallas_call(
        body, name=name, grid=(nsteps,),
        out_shape=jax.ShapeDtypeStruct(g.shape, BF),
        in_specs=([] if fresh else [anyspec]) + [pl.BlockSpec((tt, ws), lambda t: (t, 0)),
                                                 pl.BlockSpec((tt, wb), lambda t: (t, 0))],
        out_specs=anyspec,
        scratch_shapes=[pltpu.VMEM((ws, wb), F32), pltpu.VMEM((d2, d3), BF), pltpu.SemaphoreType.DMA],
        input_output_aliases={} if fresh else {0: 0},
        compiler_params=_cp(("arbitrary",)),
    )(*([] if fresh else [g]), s, b)


def _ln_stats(z):
    mu = jnp.mean(z, axis=-1, keepdims=True)
    zc = z - mu
    var = jnp.mean(zc * zc, axis=-1, keepdims=True)
    rstd = lax.rsqrt(var + LN_EPS)
    return zc * rstd, rstd


def _ln_bwd_core(dy, zh, rstd, g):
    dzh = dy * g
    m1 = jnp.mean(dzh, axis=-1, keepdims=True)
    m2 = jnp.mean(dzh * zh, axis=-1, keepdims=True)
    return rstd * (dzh - m1 - zh * m2)


def _colsum(v):
    return jnp.sum(v, axis=0, keepdims=True)


def _z_and_ln(z, g, b):
    zh, _ = _ln_stats(z)
    return z, zh * g + b


def _ln_bwd_tile(dy, z, g):
    zh, rstd = _ln_stats(z)
    dz = _ln_bwd_core(dy, zh, rstd, g)
    return dz, _colsum(dy * zh), _colsum(dy), _colsum(dz)


def _heads(D):
    return D // HEAD


def _rms_parts(o, D):
    xs, rs = [], []
    for h in range(_heads(D)):
        oh = o[:, h * HEAD:(h + 1) * HEAD]
        r = lax.rsqrt(jnp.mean(oh * oh, axis=-1, keepdims=True) + RMS_EPS)
        xs.append(oh * r)
        rs.append(r)
    return xs, rs


def _gate_bwd_tile(dyt, ot, grt, ngr):
    D = ot.shape[1]
    xs, rs = _rms_parts(ot, D)
    xh = jnp.concatenate(xs, axis=1)
    sg = _sig(grt)
    on = xh * ngr
    dgr = dyt * on * (sg * (1.0 + grt * (1.0 - sg)))
    don = dyt * (grt * sg)
    dxh = don * ngr
    dos = []
    for h in range(_heads(D)):
        sl = slice(h * HEAD, (h + 1) * HEAD)
        m = jnp.mean(dxh[:, sl] * xs[h], axis=-1, keepdims=True)
        dos.append(rs[h] * (dxh[:, sl] - xs[h] * m))
    return jnp.concatenate(dos, axis=1), dgr, _colsum(don * xh)


def _u_and_glu(u):
    D = u.shape[1] // 2
    return u, u[:, :D] * _sig(u[:, D:])


def _silu_ln_bwd_tile(dst, ct, gr, br):
    zh, rstd = _ln_stats(ct)
    ln = zh * gr + br
    sg = _sig(ln)
    dln = dst * (sg * (1.0 + ln * (1.0 - sg)))
    dc = _ln_bwd_core(dln, zh, rstd, gr)
    return dc, _colsum(dln * zh), _colsum(dln), _colsum(dc)


def _top_tile(z, g, b, tgt):
    D = z.shape[1]
    zh, rstd = _ln_stats(z)
    e = zh * g + b - tgt
    dy = e * (1.0 / D)
    dz = _ln_bwd_core(dy, zh, rstd, g)
    return dz, _colsum(dy * zh), _colsum(dy), _colsum(e * e) * (0.5 / D)


def _shifted(ext, n, tr):
    for b in range(8):
        rb = ext if b == 0 else pltpu.roll(ext, n - b, 0)
        for a in range(HALO // 8 + 1):
            o = 8 * a + b
            if o <= HALO:
                yield o, rb[8 * a:8 * a + tr, :]


def _lane_strips(D):
    return [slice(c, c + 128) for c in range(0, D, 128)]


def _conv_fwd(glu, w, bdw, g, b, name, tr=256, comm=None):
    T, D = glu.shape
    tr = min(tr, T // 2)
    rt = min(CONV_ROWS, tr)
    hb = tr // HALO

    def body(cur_ref, halo_ref, w_ref, bdw_ref, g_ref, b_ref, c_ref, s_ref):
        i = pl.program_id(0)
        for cs in _lane_strips(D):
            halo = jnp.where(i > 0, halo_ref[:, cs], 0.0)
            for r0 in range(0, tr, rt):
                if r0 == 0:
                    ext = jnp.concatenate([halo, cur_ref[0:rt, cs]], axis=0)
                else:
                    ext = cur_ref[r0 - HALO:r0 + rt, cs]
                acc = None
                for o, sh in _shifted(ext, rt + HALO, rt):
                    k = o - (HALO - CONV_W + 1)
                    if 0 <= k < CONV_W:
                        term = w_ref[k:k + 1, cs] * sh
                        acc = term if acc is None else acc + term
                c_ref[r0:r0 + rt, cs] = acc + bdw_ref[:, cs]
        cv = c_ref[...]
        zh, _ = _ln_stats(cv)
        ln = zh * g_ref[...] + b_ref[...]
        s_ref[...] = (ln * _sig(ln)).astype(BF)

    row = pl.BlockSpec((1, D), lambda i: (0, 0))
    return _call(
        body, name=name, grid=(T // tr,), operands=[glu, glu, w, bdw, g, b],
        out_shape=[jax.ShapeDtypeStruct((T, D), F32), jax.ShapeDtypeStruct((T, D), BF)],
        in_specs=[pl.BlockSpec((tr, D), lambda i: (i, 0)),
                  pl.BlockSpec((HALO, D), lambda i: (jnp.maximum(i * hb - 1, 0), 0)),
                  pl.BlockSpec((HALO, D), lambda i: (0, 0)), row, row, row],
        out_specs=[pl.BlockSpec((tr, D), lambda i: (i, 0))] * 2,
        sem=("parallel",), comm=comm)


def _conv_bwd(dc, glu, u, w, name, tr=256, comm=None):
    T, D = glu.shape
    tr = min(tr, T // 2)
    rt = min(CONV_ROWS, tr)
    hb = tr // HALO
    nsteps = T // tr
    last_hb = T // HALO - 1

    def body(dc_ref, nxt_ref, glu_ref, prv_ref, ua_ref, ug_ref, w_ref, du_ref, dw_ref, dbu_ref):
        i = pl.program_id(0)

        @pl.when(i == 0)
        def _():
            dw_ref[...] = jnp.zeros((HALO, D), F32)
            dbu_ref[...] = jnp.zeros((1, 2 * D), F32)

        for cs in _lane_strips(D):
            nxt = jnp.where(i < nsteps - 1, nxt_ref[:, cs], 0.0)
            prv = jnp.where(i > 0, prv_ref[:, cs], 0.0)
            dws = [None] * CONV_W
            dba = dbg = None
            gs = slice(D + cs.start, D + cs.stop)
            for r0 in range(0, tr, rt):
                if r0 + rt == tr:
                    ext2 = jnp.concatenate([dc_ref[r0:tr, cs], nxt], axis=0)
                else:
                    ext2 = dc_ref[r0:r0 + rt + HALO, cs]
                acc = None
                for o, sh in _shifted(ext2, rt + HALO, rt):
                    k = CONV_W - 1 - o
                    if 0 <= k < CONV_W:
                        term = w_ref[k:k + 1, cs] * sh
                        acc = term if acc is None else acc + term
                sg = _sig(ug_ref[r0:r0 + rt, cs])
                da = acc * sg
                dgt = acc * ua_ref[r0:r0 + rt, cs] * sg * (1.0 - sg)
                du_ref[r0:r0 + rt, cs] = da.astype(BF)
                du_ref[r0:r0 + rt, gs] = dgt.astype(BF)
                dba = _colsum(da) if dba is None else dba + _colsum(da)
                dbg = _colsum(dgt) if dbg is None else dbg + _colsum(dgt)
                if r0 == 0:
                    ext = jnp.concatenate([prv, glu_ref[0:rt, cs]], axis=0)
                else:
                    ext = glu_ref[r0 - HALO:r0 + rt, cs]
                dcs = dc_ref[r0:r0 + rt, cs]
                for o, sh in _shifted(ext, rt + HALO, rt):
                    k = o - (HALO - CONV_W + 1)
                    if 0 <= k < CONV_W:
                        part = _colsum(dcs * sh)
                        dws[k] = part if dws[k] is None else dws[k] + part
            for k in range(CONV_W):
                dw_ref[k:k + 1, cs] += dws[k]
            dbu_ref[:, cs] += dba
            dbu_ref[:, gs] += dbg

    return _call(
        body, name=name, grid=(nsteps,), operands=[dc, dc, glu, glu, u, u, w],
        out_shape=[jax.ShapeDtypeStruct((T, 2 * D), BF), jax.ShapeDtypeStruct((HALO, D), F32),
                   jax.ShapeDtypeStruct((1, 2 * D), F32)],
        in_specs=[pl.BlockSpec((tr, D), lambda i: (i, 0)),
                  pl.BlockSpec((HALO, D), lambda i: (jnp.minimum((i + 1) * hb, last_hb), 0)),
                  pl.BlockSpec((tr, D), lambda i: (i, 0)),
                  pl.BlockSpec((HALO, D), lambda i: (jnp.maximum(i * hb - 1, 0), 0)),
                  pl.BlockSpec((tr, D), lambda i: (i, 0)),
                  pl.BlockSpec((tr, D), lambda i: (i, 1)),
                  pl.BlockSpec((HALO, D), lambda i: (0, 0))],
        out_specs=[pl.BlockSpec((tr, 2 * D), lambda i: (i, 0)),
                   pl.BlockSpec((HALO, D), lambda i: (0, 0)),
                   pl.BlockSpec((1, 2 * D), lambda i: (0, 0))],
        sem=("arbitrary",), comm=comm)


def _rec_consts(C):
    t = np.arange(C)
    lb = (t[None, :] <= t[:, None]).astype(np.float32)

    def cum_at(idx):
        return (t[None, :] <= idx[:, None]).astype(np.float32)

    blocks, masks = [], []
    mid = SUB * (t // SUB) + SUB // 2 - 1
    eq = lb - cum_at(mid)
    blocks += [eq, -eq]
    masks.append(((t[:, None] // SUB) == (t[None, :] // SUB)) & (t[None, :] <= t[:, None]))
    nb = SUB
    while nb < C:
        odd = (t // nb) % 2 == 1
        e_t = nb * (t // nb) - 1
        e_s = nb * (t // nb) + nb - 1
        blocks.append(np.where(odd[:, None], lb - cum_at(e_t), 0.0))
        blocks.append(np.where(~odd[:, None], cum_at(e_s) - lb, 0.0))
        masks.append(((t[:, None] // (2 * nb)) == (t[None, :] // (2 * nb))) & odd[:, None] & ~odd[None, :])
        nb *= 2
    blocks += [lb, 1.0 - lb, np.ones((HEAD, C), np.float32)]
    L = np.concatenate(blocks, axis=0).astype(np.float32)
    L3 = np.concatenate([L, L, L], axis=1)
    LT3 = np.concatenate([L.T, L.T], axis=1)
    m = np.stack(masks).astype(np.float32)
    mT = np.transpose(m, (0, 2, 1)).copy()
    return (jnp.asarray(L3, BF), jnp.asarray(LT3, BF), jnp.asarray(m), jnp.asarray(mT), len(masks))


def _split3(x):
    h = x.astype(BF)
    r = x - h.astype(F32)
    m = r.astype(BF)
    lo = (r - m.astype(F32)).astype(BF)
    return h, m, lo


def _split2(x):
    h = x.astype(BF)
    return h, (x - h.astype(F32)).astype(BF)


def _gates(qr, fz, lbr):
    sq = _sig(qr)
    q = qr * sq
    sg = _sig(fz)
    f = lbr + (1.0 - lbr) * sg
    fc = jnp.maximum(f, GATE_EPS)
    return q, 1.0 - f, jnp.log(fc), sq, sg, f, fc


def _rec_fwd(proj, lbr, ng, name, comm=None):
    T, D4 = proj.shape
    D = D4 // 4
    H = _heads(D)
    C = CHUNK
    nC = T // C
    L3, _, m, _, nl = _rec_consts(C)
    R = L3.shape[0]

    def body(q_ref, f_ref, v_ref, gr_ref, lb_ref, ng_ref, l_ref, m_ref, o_ref, y_ref, s_ref, st):
        @pl.when(pl.program_id(0) == 0)
        def _():
            st[...] = jnp.zeros((H, HEAD, HEAD), F32)

        q, k, logf = _gates(q_ref[...], f_ref[...], lb_ref[...])[:3]
        ex = jnp.exp(_dot(l_ref[...], jnp.concatenate(_split3(logf), axis=0)))
        vb = v_ref[...].astype(BF)
        s_ref[0] = st[...]
        outs = []
        for h in range(H):
            sl = slice(h * HEAD, (h + 1) * HEAD)
            qh, kh = q[:, sl], k[:, sl]
            p = jnp.zeros((C, C), F32)
            for lv in range(nl):
                qt = (qh * ex[2 * lv * C:(2 * lv + 1) * C, sl]).astype(BF)
                kt = (kh * ex[(2 * lv + 1) * C:(2 * lv + 2) * C, sl]).astype(BF)
                p = p + jnp.where(m_ref[lv] > 0.0, _dot_nt(qt, kt), 0.0)
            base = 2 * nl * C
            qhat = (qh * ex[base:base + C, sl]).astype(BF)
            khat = (kh * ex[base + C:base + 2 * C, sl]).astype(BF)
            elast = ex[base + 2 * C:base + 2 * C + HEAD, sl]
            sth = st[h]
            outs.append(_dot(p.astype(BF), vb[:, sl]) + _dot_nt(qhat, sth.astype(BF)))
            st[h] = elast * sth + _dot_tn(vb[:, sl], khat)
        o_ref[...] = jnp.concatenate(outs, axis=1)
        xh = jnp.concatenate([oh * lax.rsqrt(jnp.mean(oh * oh, axis=-1, keepdims=True) + RMS_EPS) for oh in outs],
                             axis=1)
        gr = gr_ref[...]
        y_ref[...] = (xh * ng_ref[...] * (gr * _sig(gr))).astype(BF)

    def cblk(cb):
        return pl.BlockSpec((C, D), lambda i: (i, cb))

    return _call(
        body, name=name, grid=(nC,), operands=[proj, proj, proj, proj, lbr, ng, L3, m],
        out_shape=[jax.ShapeDtypeStruct((T, D), F32), jax.ShapeDtypeStruct((T, D), BF),
                   jax.ShapeDtypeStruct((nC, H, HEAD, HEAD), F32)],
        in_specs=[cblk(0), cblk(1), cblk(2), cblk(3), pl.BlockSpec((1, D), lambda i: (0, 0)),
                  pl.BlockSpec((1, D), lambda i: (0, 0)),
                  pl.BlockSpec(L3.shape, lambda i: (0, 0)), pl.BlockSpec(m.shape, lambda i: (0, 0, 0))],
        out_specs=[pl.BlockSpec((C, D), lambda i: (i, 0)), pl.BlockSpec((C, D), lambda i: (i, 0)),
                   pl.BlockSpec((1, H, HEAD, HEAD), lambda i: (i, 0, 0, 0))],
        scratch=[pltpu.VMEM((H, HEAD, HEAD), F32)], sem=("arbitrary",), comm=comm)


def _rec_bwd(proj, lbr, do, dgr, ssave, name, comm=None):
    T, D4 = proj.shape
    D = D4 // 4
    H = _heads(D)
    C = min(CHUNK_BWD, T // 2)
    nC = T // C
    stride = C // CHUNK
    L3, LT3, m, mT, nl = _rec_consts(C)

    def body(q_ref, f_ref, v_ref, lb_ref, do_ref, dgr_ref, s_ref, l_ref, lt_ref, m_ref, mt_ref,
             dp_ref, dlb_ref, dst, des):
        @pl.when(pl.program_id(0) == 0)
        def _():
            dst[...] = jnp.zeros((H, HEAD, HEAD), F32)
            dlb_ref[...] = jnp.zeros((1, D), F32)

        qr = q_ref[...]
        lbv = lb_ref[...]
        q, k, logf, sq, sg, f, fc = _gates(qr, f_ref[...], lbv)
        ex = jnp.exp(_dot(l_ref[...], jnp.concatenate(_split3(logf), axis=0)))
        vb = v_ref[...].astype(BF)
        dob = do_ref[...].astype(BF)
        base = 2 * nl * C
        n_rows = (2 * nl + 2) * C + HEAD

        def put(block, sl, val):
            hi, lo = _split2(val)
            r0 = block * C
            des[r0:r0 + val.shape[0], sl] = hi
            des[n_rows + r0:n_rows + r0 + val.shape[0], sl] = lo

        dqs, dks, dvs = [], [], []
        for h in range(H):
            sl = slice(h * HEAD, (h + 1) * HEAD)
            qh, kh, vh, doh = q[:, sl], k[:, sl], vb[:, sl], dob[:, sl]
            dp = _dot_nt(doh, vh)
            dpt = _dot_nt(vh, doh)
            sth = s_ref[0, h]
            dsth = dst[h]
            dsb = dsth.astype(BF)
            pt = jnp.zeros((C, C), F32)
            dq = jnp.zeros((C, HEAD), F32)
            dk = jnp.zeros((C, HEAD), F32)
            for lv in range(nl):
                exq = ex[2 * lv * C:(2 * lv + 1) * C, sl]
                exk = ex[(2 * lv + 1) * C:(2 * lv + 2) * C, sl]
                qt = qh * exq
                kt = kh * exk
                qtb, ktb = qt.astype(BF), kt.astype(BF)
                pt = pt + jnp.where(mt_ref[lv] > 0.0, _dot_nt(ktb, qtb), 0.0)
                dqt = _dot(jnp.where(m_ref[lv] > 0.0, dp, 0.0).astype(BF), ktb)
                dkt = _dot(jnp.where(mt_ref[lv] > 0.0, dpt, 0.0).astype(BF), qtb)
                dq = dq + dqt * exq
                dk = dk + dkt * exk
                put(2 * lv, sl, dqt * qt)
                put(2 * lv + 1, sl, dkt * kt)
            exb = ex[base:base + C, sl]
            exkh = ex[base + C:base + 2 * C, sl]
            elast = ex[base + 2 * C:base + 2 * C + HEAD, sl]
            qhat = qh * exb
            khat = kh * exkh
            dqh = _dot(doh, sth.astype(BF))
            dkh = _dot(vh, dsb)
            dq = dq + dqh * exb
            dk = dk + dkh * exkh
            put(2 * nl, sl, dqh * qhat)
            put(2 * nl + 1, sl, dkh * khat)
            put(2 * nl + 2, sl, dsth * sth * elast)
            dvs.append(_dot(pt.astype(BF), doh) + _dot_nt(khat.astype(BF), dsb))
            dst[h] = elast * dsth + _dot_tn(doh, qhat.astype(BF))
            dqs.append(dq)
            dks.append(dk)
        dlogf = _dot(lt_ref[...], des[...])
        dq = jnp.concatenate(dqs, axis=1)
        dk = jnp.concatenate(dks, axis=1)
        dv = jnp.concatenate(dvs, axis=1)
        ind = jnp.where(f > GATE_EPS, 1.0, jnp.where(f == GATE_EPS, 0.5, 0.0))
        df = dlogf * ind / fc - dk
        dfz = df * (1.0 - lbv) * sg * (1.0 - sg)
        dlb_ref[...] += _colsum(df * (1.0 - sg))
        dqr = dq * (sq * (1.0 + qr * (1.0 - sq)))
        dp_ref[...] = jnp.concatenate([dqr.astype(BF), dfz.astype(BF), dv.astype(BF), dgr_ref[...]], axis=1)

    def cblk(cb):
        return pl.BlockSpec((C, D), lambda i: (nC - 1 - i, cb))

    def whole(a):
        nd = a.ndim
        return pl.BlockSpec(a.shape, lambda i: (0,) * nd)

    return _call(
        body, name=name, grid=(nC,), operands=[proj, proj, proj, lbr, do, dgr, ssave, L3, LT3, m, mT],
        out_shape=[jax.ShapeDtypeStruct((T, D4), BF), jax.ShapeDtypeStruct((1, D), F32)],
        in_specs=[cblk(0), cblk(1), cblk(2), pl.BlockSpec((1, D), lambda i: (0, 0)),
                  pl.BlockSpec((C, D), lambda i: (nC - 1 - i, 0)),
                  pl.BlockSpec((C, D), lambda i: (nC - 1 - i, 0)),
                  pl.BlockSpec((1, H, HEAD, HEAD), lambda i: (stride * (nC - 1 - i), 0, 0, 0)),
                  whole(L3), whole(LT3), whole(m), whole(mT)],
        out_specs=[pl.BlockSpec((C, D4), lambda i: (nC - 1 - i, 0)),
                   pl.BlockSpec((1, D), lambda i: (0, 0))],
        scratch=[pltpu.VMEM((H, HEAD, HEAD), F32), pltpu.VMEM((LT3.shape[1], D), BF)],
        sem=("arbitrary",), comm=comm)


def _softmax_rows(lg_ref):
    n = lg_ref.shape[0]
    rows = [lg_ref[l:l + 1, :] for l in range(n)]
    mx = rows[0]
    for r in rows[1:]:
        mx = jnp.maximum(mx, r)
    es = [jnp.exp(r - mx) for r in rows]
    tot = es[0]
    for e in es[1:]:
        tot = tot + e
    return [e / tot for e in es]


def _lb_fwd(logits):
    n, D = logits.shape

    def body(lg_ref, o_ref):
        soft = _softmax_rows(lg_ref)
        acc = jnp.zeros((1, D), F32)
        o_ref[0:1, :] = acc
        for j in range(1, n):
            acc = acc + soft[j]
            o_ref[j:j + 1, :] = acc

    return pl.pallas_call(body, name="lb_fwd", out_shape=jax.ShapeDtypeStruct((n, D), F32))(logits)


def _small_reduce(parts, logits, lb_row0):
    _, R, D = parts.shape
    n = logits.shape[0]

    def body(p_ref, lg_ref, o_ref):
        acc = p_ref[0]
        for d in range(1, N_DEV):
            acc = acc + p_ref[d]
        o_ref[...] = acc
        soft = _softmax_rows(lg_ref)
        dlb = [o_ref[lb_row0 + j:lb_row0 + j + 1, :] for j in range(n)]
        dsoft = [jnp.zeros((1, D), F32)]
        for l in range(1, n):
            s = dlb[l]
            for j in range(l + 1, n):
                s = s + dlb[j]
            dsoft.append(s)
        dot = soft[0] * dsoft[0]
        for l in range(1, n):
            dot = dot + soft[l] * dsoft[l]
        for l in range(n):
            o_ref[lb_row0 + l:lb_row0 + l + 1, :] = soft[l] * (dsoft[l] - dot)

    return pl.pallas_call(body, name="small_reduce", out_shape=jax.ShapeDtypeStruct((R, D), F32))(parts, logits)


def _adam_math(g, w, m, v):
    m2 = ADAM_B1 * m + (1.0 - ADAM_B1) * g
    v2 = ADAM_B2 * v + (1.0 - ADAM_B2) * (g * g)
    mh = m2 / (1.0 - ADAM_B1 ** ADAM_STEP)
    vh = v2 / (1.0 - ADAM_B2 ** ADAM_STEP)
    delta = -ADAM_LR * (mh / (jnp.sqrt(vh) + ADAM_EPS) + ADAM_WD * w)
    return delta, m2, v2


def _adamw(recv, w, m, v, name):
    nsrc = recv.shape[0]
    shp = w.shape
    cols = shp[-1]
    rows = int(np.prod(shp[:-1]))
    r2 = recv.reshape(nsrc, rows, cols)
    tr = min(rows, max(8, (1 << 20) // (cols * nsrc)))
    while rows % tr:
        tr //= 2

    def body(r_ref, w_ref, m_ref, v_ref, g_ref, d_ref, nm_ref, nv_ref):
        g = r_ref[0].astype(F32)
        for s in range(1, nsrc):
            g = g + r_ref[s].astype(F32)
        delta, m2, v2 = _adam_math(g, w_ref[...], m_ref[...], v_ref[...])
        g_ref[...] = g
        d_ref[...] = delta
        nm_ref[...] = m2
        nv_ref[...] = v2

    blk = pl.BlockSpec((tr, cols), lambda i: (i, 0))
    outs = pl.pallas_call(
        body, name=name, grid=(rows // tr,),
        out_shape=[jax.ShapeDtypeStruct((rows, cols), F32)] * 4,
        in_specs=[pl.BlockSpec((nsrc, tr, cols), lambda i: (0, i, 0)), blk, blk, blk],
        out_specs=[blk] * 4,
        compiler_params=_cp(("parallel",)),
    )(r2, w.reshape(rows, cols), m.reshape(rows, cols), v.reshape(rows, cols))
    return [o.reshape(shp) for o in outs]


def _pack(arrs, lane):
    flat = jnp.concatenate([a.reshape(-1) for a in arrs])
    n = flat.shape[0]
    rows = -(-n // lane)
    rows = -(-rows // 8) * 8
    flat = jnp.pad(flat, (0, rows * lane - n))
    return flat.reshape(rows, lane)


def _unpack(packed, shapes):
    flat = packed.reshape(-1)
    out, off = [], 0
    for s in shapes:
        n = int(np.prod(s))
        out.append(flat[off:off + n].reshape(s))
        off += n
    return out


def kernel(x, ln_mix_g, ln_mix_b, ln_ffn_g, ln_ffn_b, ffn_w1, ffn_w2, a_w_in, a_lb_logits, a_norm_g, a_w_out, b_w_pw1, b_b_pw1, b_w_dw, b_b_dw, b_ln_g, b_ln_b, b_w_pw2, b_b_pw2, loss_target, m_ln_mix_g, m_ln_mix_b, m_ln_ffn_g, m_ln_ffn_b, m_ffn_w1, m_ffn_w2, m_a_w_in, m_a_lb_logits, m_a_norm_g, m_a_w_out, m_b_w_pw1, m_b_b_pw1, m_b_w_dw, m_b_b_dw, m_b_ln_g, m_b_ln_b, m_b_w_pw2, m_b_b_pw2, v_ln_mix_g, v_ln_mix_b, v_ln_ffn_g, v_ln_ffn_b, v_ffn_w1, v_ffn_w2, v_a_w_in, v_a_lb_logits, v_a_norm_g, v_a_w_out, v_b_w_pw1, v_b_b_pw1, v_b_w_dw, v_b_b_dw, v_b_ln_g, v_b_ln_b, v_b_w_pw2, v_b_b_pw2):
    T, D = x.shape[1], x.shape[2]
    nA, nB = a_w_in.shape[0], b_w_pw1.shape[0]
    me = 4 * lax.axis_index("x") + 2 * lax.axis_index("y") + lax.axis_index("c")
    xin = x[0]
    tgt = loss_target[0]

    small_names = [b_b_pw1, b_w_dw, b_b_dw, b_ln_g, b_ln_b, b_b_pw2]
    sp = _pack(small_names, 128)
    wsrc = {"W1": ffn_w1.astype(BF), "W2": ffn_w2.astype(BF), "Win": a_w_in.astype(BF),
            "Wout": a_w_out.astype(BF), "Wp1": b_w_pw1.astype(BF), "Wp2": b_w_pw2.astype(BF), "small": sp[None]}
    W = {k: jax.ShapeDtypeStruct((N_DEV,) + v.shape, v.dtype) for k, v in wsrc.items()}
    GW = {k: jax.ShapeDtypeStruct((N_DEV,) + v.shape, BF) for k, v in wsrc.items() if k != "small"}
    R = {k: jax.ShapeDtypeStruct(v.shape, BF) for k, v in GW.items()}

    def names_of(items):
        out = []
        for n, _ in items:
            if n not in out:
                out.append(n)
        return out

    def hosted(fn, key, sched, mode, src, bufs, *args, **kw):
        items = sched.get(key)
        if items is None:
            return fn(*args, name=key, **kw)
        comm = (mode, [(src[n], l, bufs[n]) for n, l in items])
        outs, new = fn(*args, name=key, comm=comm, **kw)
        for n, b in zip(names_of(items), new):
            bufs[n] = b
        return outs

    first = [("Win", 0), ("small", 0)]
    for n, b in zip(names_of(first), _exchange(("gather", [(wsrc[n], l, W[n]) for n, l in first]), "gather_first")):
        W[n] = b
    fwd_sched = {"a0_proj": [("Wout", 0), ("Wp1", 0)], "a0_rec_fwd": [("W1", 0), ("W2", 0), ("Wp2", 0)],
                 "l0_ffn_up": [("W2", 1)], "l0_ffn_down": [("W1", 1)],
                 "b0_conv": [("Win", 1), ("Wout", 1)], "l1_ffn_up": [("W2", 2)], "l1_ffn_down": [("W1", 2)],
                 "a1_rec_fwd": [("Wp1", 1), ("Wp2", 1), ("W1", 3), ("W2", 3)]}
    bwd_sched = {"a1_rec_bwd": [("W1", 3), ("W2", 3), ("Wp1", 1), ("Wp2", 1), ("W1", 2), ("W2", 2)],
                 "b0_conv_bwd": [("Win", 1), ("Wout", 1)],
                 "a0_rec_bwd": [("W1", 1), ("W2", 1), ("Wp1", 0), ("Wp2", 0), ("W1", 0), ("W2", 0), ("Wout", 0)],
                 "a0_proj_dx": [("Win", 0)]}

    def fwd(fn, key, *args, **kw):
        return hosted(fn, key, fwd_sched, "gather", wsrc, W, *args, **kw)

    def bwd(fn, key, *args, **kw):
        return hosted(fn, key, bwd_sched, "scatter", GW, R, *args, **kw)

    spg = W["small"][:, 0]
    sm = [jnp.stack(p) for p in zip(*[_unpack(spg[d], [a.shape for a in small_names]) for d in range(N_DEV)])]
    bpw1 = jnp.transpose(sm[0], (1, 0, 2)).reshape(nB, 1, 2 * D)
    wdw = jnp.transpose(sm[1], (1, 2, 0, 3)).reshape(nB, CONV_W, D)
    wdw = jnp.pad(wdw, ((0, 0), (0, HALO - CONV_W), (0, 0)))
    bdw, blng, blnb, bpw2 = [jnp.transpose(s, (1, 0, 2)).reshape(nB, 1, D) for s in sm[2:]]
    lb_all = _lb_fwd(a_lb_logits)

    zx = [(D, F32), (D, F32)]
    ln_out = dict(outs=[(D, F32)], sums=[D, D, D])
    ln_mix = [(ln_mix_g[i:i + 1], ln_mix_b[i:i + 1]) for i in range(DEPTH)]
    ln_ffn = [(ln_ffn_g[i:i + 1], ln_ffn_b[i:i + 1]) for i in range(DEPTH)]
    saved = []
    h = xin
    for i in range(DEPTH):
        j = i // 2
        sv = {"xin": h}
        if i % 2 == 0:
            proj = fwd(_mm, f"a{j}_proj", h, W["Win"], j, "nn_col", out_dtype=F32)
            o, yg, ssave = fwd(_rec_fwd, f"a{j}_rec_fwd", proj, lb_all[j:j + 1], a_norm_g[j:j + 1])
            z1, x1 = _mm(yg, W["Wout"], j, "nn_row", out_dtype=F32, name=f"a{j}_out", outs=zx,
                         epi=lambda acc, r, g, b: _z_and_ln(acc + ALPHA * r, g, b), extras=[h, *ln_mix[i]])
            sv.update(proj=proj, o=o, ssave=ssave, yg=yg)
        else:
            u, glu = _mm(h, W["Wp1"], j, "nn_col", out_dtype=F32, name=f"b{j}_pw1",
                         outs=[(2 * D, F32), (D, F32)],
                         epi=lambda acc, b: _u_and_glu(acc + b), extras=[bpw1[j]])
            cv, s = fwd(_conv_fwd, f"b{j}_conv", glu, wdw[j], bdw[j], blng[j], blnb[j])
            z1, x1 = _mm(s, W["Wp2"], j, "nn_row", out_dtype=F32, name=f"b{j}_pw2", outs=zx,
                         epi=lambda acc, bb, r, g, b: _z_and_ln(acc + bb + ALPHA * r, g, b),
                         extras=[bpw2[j], h, *ln_mix[i]])
            sv.update(u=u, glu=glu, cv=cv, s=s)
        hh = fwd(_mm, f"l{i}_ffn_up", x1, W["W1"], i, "nn_col", out_dtype=BF)
        relu2 = lambda t: jnp.square(jnp.maximum(t, 0))
        if i < DEPTH - 1:
            z2, h = fwd(_mm, f"l{i}_ffn_down", hh, W["W2"], i, "nn_row", out_dtype=F32, outs=zx, a_fn=relu2,
                        epi=lambda acc, r, g, b: _z_and_ln(acc + ALPHA * r, g, b), extras=[x1, *ln_ffn[i]])
        else:
            z2 = None
            dz2, dg_top, db_top, loss_row = fwd(
                _mm, f"l{i}_ffn_down", hh, W["W2"], i, "nn_row", out_dtype=F32, a_fn=relu2, **ln_out,
                epi=lambda acc, r, g, b, t: _top_tile(acc + ALPHA * r, g, b, t), extras=[x1, *ln_ffn[i], tgt])
        sv.update(z1=z1, x1=x1, hh=hh, z2=z2)
        saved.append(sv)

    rows = {}
    top = DEPTH - 1
    rows[("ffn_g", top)], rows[("ffn_b", top)] = dg_top, db_top
    for i in reversed(range(DEPTH)):
        j = i // 2
        sv = saved[i]
        if i > 0:
            below = dict(epi=lambda acc, r, z, g: _ln_bwd_tile(acc + ALPHA * r, z, g), **ln_out)
            below_extras = [saved[i - 1]["z2"], ln_ffn[i - 1][0]]
        else:
            below = dict(epi=lambda acc, r: acc + ALPHA * r)
            below_extras = []
        dh = _mm(dz2, W["W2"], i, "nt_row", out_dtype=BF, name=f"l{i}_ffn_down_dx",
                 epi=lambda acc, hv: acc * (2.0 * jnp.maximum(hv.astype(F32), 0.0)), extras=[sv["hh"]])
        GW["W2"] = _mm_tn(GW["W2"], dz2, sv["hh"], i, "rows_t", name=f"l{i}_ffn_down_dw",
                         b_fn=lambda t: jnp.square(jnp.maximum(t, 0)))
        GW["W1"] = _mm_tn(GW["W1"], sv["x1"], dh, i, "cols", name=f"l{i}_ffn_up_dw")
        dz1, dg, db, dz1sum = _mm(dh, W["W1"], i, "nt_col", out_dtype=F32, name=f"l{i}_ffn_up_dx",
                                  epi=lambda acc, r, z, g: _ln_bwd_tile(acc + ALPHA * r, z, g),
                                  extras=[dz2, sv["z1"], ln_mix[i][0]], **ln_out)
        rows[("mix_g", i)], rows[("mix_b", i)] = dg, db
        if i % 2 == 0:
            GW["Wout"] = _mm_tn(GW["Wout"], sv["yg"], dz1, j, "rows", name=f"a{j}_out_dw")
            do, dgr, dng = _mm(dz1, W["Wout"], j, "nt_row", out_dtype=F32, name=f"a{j}_out_dx",
                               outs=[(D, F32), (D, BF)], sums=[D], epi=_gate_bwd_tile,
                               extras=[sv["o"], (sv["proj"], 3), a_norm_g[j:j + 1]])
            rows[("ng", j)] = dng
            dproj, dlb = bwd(_rec_bwd, f"a{j}_rec_bwd", sv["proj"], lb_all[j:j + 1], do, dgr, sv["ssave"])
            rows[("lb", j)] = dlb
            GW["Win"] = _mm_tn(GW["Win"], sv["xin"], dproj, j, "cols", name=f"a{j}_proj_dw")
            res = bwd(_mm, f"a{j}_proj_dx", dproj, W["Win"], j, "nt_col", out_dtype=F32,
                      extras=[dz1, *below_extras], **below)
        else:
            rows[("bpw2", j)] = dz1sum
            GW["Wp2"] = _mm_tn(GW["Wp2"], sv["s"], dz1, j, "rows", name=f"b{j}_pw2_dw")
            dc, dlg, dlb_, dcs = _mm(dz1, W["Wp2"], j, "nt_row", out_dtype=F32, name=f"b{j}_pw2_dx",
                                     outs=[(D, F32)], sums=[D, D, D], epi=_silu_ln_bwd_tile,
                                     extras=[sv["cv"], blng[j], blnb[j]])
            rows[("blng", j)], rows[("blnb", j)], rows[("bdw", j)] = dlg, dlb_, dcs
            du, dwdw, dbu = bwd(_conv_bwd, f"b{j}_conv_bwd", dc, sv["glu"], sv["u"], wdw[j])
            rows[("wdw", j)] = dwdw[:CONV_W]
            rows[("bpw1", j)] = dbu.reshape(2, D)
            GW["Wp1"] = _mm_tn(GW["Wp1"], sv["xin"], du, j, "cols", name=f"b{j}_pw1_dw")
            res = _mm(du, W["Wp1"], j, "nt_col", out_dtype=F32, name=f"b{j}_pw1_dx",
                      extras=[dz1, *below_extras], **below)
        if i > 0:
            dz2, dg, db, _ = res
            rows[("ffn_g", i - 1)], rows[("ffn_b", i - 1)] = dg, db
        else:
            dx = res
    grad_x = dx[None]

    order = ([("mix_g", i) for i in range(DEPTH)] + [("mix_b", i) for i in range(DEPTH)]
             + [("ffn_g", i) for i in range(DEPTH)] + [("ffn_b", i) for i in range(DEPTH)])
    lb_row0 = len(order)
    order += [("lb", j) for j in range(nA)] + [("ng", j) for j in range(nA)]
    for j in range(nB):
        order += [("bpw1", j), ("wdw", j), ("bdw", j), ("blng", j), ("blnb", j), ("bpw2", j)]
    pieces, offs, off = [], {}, 0
    for key in order:
        offs[key] = off
        pieces.append(rows[key])
        off += rows[key].shape[0]
    offs["loss"] = off
    pieces.append(loss_row)
    off += 1
    part = jnp.concatenate(pieces, axis=0)
    part = jnp.pad(part, ((0, -off % 8), (0, 0)))
    parts = _exchange(("gather", [(part[None], 0, jax.ShapeDtypeStruct((N_DEV, 1) + part.shape, F32))]),
                      "gather_small_grads")[0][:, 0]
    G = _small_reduce(parts, a_lb_logits, lb_row0)
    loss = jnp.sum(G[offs["loss"]])

    def rep(kind, n):
        return jnp.concatenate([G[offs[(kind, i)]:offs[(kind, i)] + 1] for i in range(n)], axis=0)

    def shard_cols(full, width):
        return lax.dynamic_slice_in_dim(full, me * width, width, axis=full.ndim - 1)

    g_small = {
        "ln_mix_g": rep("mix_g", DEPTH), "ln_mix_b": rep("mix_b", DEPTH),
        "ln_ffn_g": rep("ffn_g", DEPTH), "ln_ffn_b": rep("ffn_b", DEPTH),
        "a_lb_logits": rep("lb", nA), "a_norm_g": rep("ng", nA),
        "b_b_pw1": shard_cols(jnp.stack([G[offs[("bpw1", j)]:offs[("bpw1", j)] + 2].reshape(2 * D)
                                         for j in range(nB)]), 2 * D // N_DEV),
        "b_w_dw": shard_cols(jnp.stack([G[offs[("wdw", j)]:offs[("wdw", j)] + CONV_W] for j in range(nB)]),
                             D // N_DEV),
        "b_b_dw": shard_cols(rep("bdw", nB), D // N_DEV),
        "b_ln_g": shard_cols(rep("blng", nB), D // N_DEV),
        "b_ln_b": shard_cols(rep("blnb", nB), D // N_DEV),
        "b_b_pw2": shard_cols(rep("bpw2", nB), D // N_DEV),
    }
    small_w = {"ln_mix_g": (ln_mix_g, m_ln_mix_g, v_ln_mix_g), "ln_mix_b": (ln_mix_b, m_ln_mix_b, v_ln_mix_b),
               "ln_ffn_g": (ln_ffn_g, m_ln_ffn_g, v_ln_ffn_g), "ln_ffn_b": (ln_ffn_b, m_ln_ffn_b, v_ln_ffn_b),
               "a_lb_logits": (a_lb_logits, m_a_lb_logits, v_a_lb_logits),
               "a_norm_g": (a_norm_g, m_a_norm_g, v_a_norm_g),
               "b_b_pw1": (b_b_pw1, m_b_b_pw1, v_b_b_pw1), "b_w_dw": (b_w_dw, m_b_w_dw, v_b_w_dw),
               "b_b_dw": (b_b_dw, m_b_b_dw, v_b_b_dw), "b_ln_g": (b_ln_g, m_b_ln_g, v_b_ln_g),
               "b_ln_b": (b_ln_b, m_b_ln_b, v_b_ln_b), "b_b_pw2": (b_b_pw2, m_b_b_pw2, v_b_b_pw2)}
    snames = list(small_w)
    sshapes = [small_w[k][0].shape for k in snames]
    pg = _pack([g_small[k] for k in snames], 1024)
    pw, pm, pv = [_pack([small_w[k][q] for k in snames], 1024) for q in range(3)]
    sres = _adamw(pg[None], pw, pm, pv, "adamw_small")
    sres = [dict(zip(snames, _unpack(r, sshapes))) for r in sres]

    recv = [R[n] for n in ["W1", "W2", "Win", "Wout", "Wp1", "Wp2"]]
    big = {}
    for nm, rv, (w, m, v) in zip(
            ["ffn_w1", "ffn_w2", "a_w_in", "a_w_out", "b_w_pw1", "b_w_pw2"], recv,
            [(ffn_w1, m_ffn_w1, v_ffn_w1), (ffn_w2, m_ffn_w2, v_ffn_w2), (a_w_in, m_a_w_in, v_a_w_in),
             (a_w_out, m_a_w_out, v_a_w_out), (b_w_pw1, m_b_w_pw1, v_b_w_pw1), (b_w_pw2, m_b_w_pw2, v_b_w_pw2)]):
        big[nm] = _adamw(rv, w, m, v, f"adamw_{nm}")

    names = ["ln_mix_g", "ln_mix_b", "ln_ffn_g", "ln_ffn_b", "ffn_w1", "ffn_w2", "a_w_in", "a_lb_logits",
             "a_norm_g", "a_w_out", "b_w_pw1", "b_b_pw1", "b_w_dw", "b_b_dw", "b_ln_g", "b_ln_b", "b_w_pw2",
             "b_b_pw2"]
    out = [loss, grad_x]
    for q in range(4):
        for nm in names:
            out.append(big[nm][q] if nm in big else sres[q][nm])
    return tuple(out)
```

```python
import functools

import numpy as np
import jax
import jax.numpy as jnp
from jax import lax
from jax.experimental import pallas as pl
from jax.experimental.pallas import tpu as pltpu

F32 = jnp.float32
BF = jnp.bfloat16

N_DEV = 8
DEPTH = 4
HEAD = 128
CHUNK = 64
CHUNK_BWD = 128
SUB = 8
CONV_W = 31
HALO = 32
CONV_ROWS = 128
ALPHA = (2.0 * DEPTH) ** 0.25
LN_EPS = 1e-5
RMS_EPS = 1e-6
GATE_EPS = 1e-6
ADAM_LR = 0.001
ADAM_B1 = 0.9
ADAM_B2 = 0.999
ADAM_EPS = 1e-08
ADAM_WD = 0.01
ADAM_STEP = 10
VMEM_LIMIT = 56 * 1024 * 1024
MESH = pl.DeviceIdType.MESH


def _cp(sem=None):
    return pltpu.CompilerParams(vmem_limit_bytes=VMEM_LIMIT, dimension_semantics=sem)


def _dot(a, b):
    return jnp.dot(a, b, preferred_element_type=F32)


def _dot_nt(a, b):
    return lax.dot_general(a, b, (((1,), (1,)), ((), ())), preferred_element_type=F32)


def _dot_tn(a, b):
    return lax.dot_general(a, b, (((0,), (0,)), ((), ())), preferred_element_type=F32)


def _sig(x):
    return jax.nn.sigmoid(x)


def _distinct(items):
    out = []
    for it in items:
        if not any(it is q for q in out):
            out.append(it)
    return out


def _index_of(items, it):
    return next(i for i, q in enumerate(items) if q is it)


def _comm_copies(mode, plan, src_refs, buf_refs, send_sems, recv_sems, loc_sems):
    x, y, c = lax.axis_index("x"), lax.axis_index("y"), lax.axis_index("c")
    me = 4 * x + 2 * y + c
    locs, first, passed = [], [], []
    for k, (si, l, bi) in enumerate(plan):
        src = src_refs[si].at[l] if mode == "gather" else src_refs[si].at[me, l]
        locs.append(pltpu.make_async_copy(src, buf_refs[bi].at[me, l], loc_sems.at[k]))

    def remote(k, r, src, slot, l, bi, to):
        return pltpu.make_async_remote_copy(
            src_ref=src, dst_ref=buf_refs[bi].at[slot, l],
            send_sem=send_sems.at[k, r], recv_sem=recv_sems.at[k, r],
            device_id=to, device_id_type=MESH)

    if mode == "scatter":
        for r in range(1, N_DEV):
            px = (1 - x) if (r >> 2) & 1 else x
            py = (1 - y) if (r >> 1) & 1 else y
            pc = (1 - c) if r & 1 else c
            pid = 4 * px + 2 * py + pc
            for k, (si, l, bi) in enumerate(plan):
                first.append(remote(k, r - 1, src_refs[si].at[pid, l], me, l, bi, (px, py, pc)))
        return locs, first, passed
    sibling = (x, y, 1 - c)
    for k, (si, l, bi) in enumerate(plan):
        first.append(remote(k, 0, src_refs[si].at[l], me, l, bi, sibling))
    for r, (qx, qy) in enumerate([(1 - x, y), (x, 1 - y), (1 - x, 1 - y)]):
        qid = 4 * qx + 2 * qy + c
        for k, (si, l, bi) in enumerate(plan):
            first.append(remote(k, 1 + r, src_refs[si].at[l], me, l, bi, (qx, qy, c)))
            passed.append(remote(k, 4 + r, buf_refs[bi].at[qid, l], qid, l, bi, sibling))
    return locs, first, passed


def _call(body, *, name, grid, operands, in_specs, out_shape, out_specs, scratch=(), sem=None,
          aliases=None, comm=None):
    aliases = dict(aliases or {})
    if comm is None:
        return pl.pallas_call(
            body, name=name, grid=grid, out_shape=list(out_shape), in_specs=list(in_specs),
            out_specs=list(out_specs), scratch_shapes=list(scratch), input_output_aliases=aliases,
            compiler_params=_cp(sem),
        )(*operands)
    mode, pieces = comm
    srcs = _distinct([p[0] for p in pieces])
    bufs = _distinct([p[2] for p in pieces])
    plan = [(_index_of(srcs, s), l, _index_of(bufs, b)) for (s, l, b) in pieces]
    n_in, n_out, n_scr, ns, nb, npc = len(operands), len(out_shape), len(scratch), len(srcs), len(bufs), len(plan)
    nsteps = grid[0]
    old = [b for b in bufs if not isinstance(b, jax.ShapeDtypeStruct)]
    nbi = len(old)

    def wrapped(*refs):
        ins = refs[:n_in]
        src_refs = refs[n_in:n_in + ns]
        o0 = n_in + ns + nbi
        outs = refs[o0:o0 + n_out]
        buf_refs = refs[o0 + n_out:o0 + n_out + nb]
        s0 = o0 + n_out + nb
        scr = refs[s0:s0 + n_scr]
        sems = refs[s0 + n_scr:]
        step = pl.program_id(0)

        @pl.when(step == 0)
        def _():
            locs, first, _ = _comm_copies(mode, plan, src_refs, buf_refs, *sems)
            for d in locs + first:
                d.start()

        body(*ins, *outs, *scr)

        @pl.when(step == (3 * nsteps) // 4)
        def _():
            _, first, passed = _comm_copies(mode, plan, src_refs, buf_refs, *sems)
            for d in first[len(first) - len(passed):] if passed else []:
                d.wait_recv()
            for d in passed:
                d.start()

        @pl.when(step == nsteps - 1)
        def _():
            locs, first, passed = _comm_copies(mode, plan, src_refs, buf_refs, *sems)
            for d in first + passed:
                d.wait_send()
            for d in first[:len(first) - len(passed)] + passed:
                d.wait_recv()
            for d in locs:
                d.wait()

    anyspec = pl.BlockSpec(memory_space=pl.ANY)
    for k, b in enumerate(old):
        aliases[n_in + ns + k] = n_out + _index_of(bufs, b)
    res = pl.pallas_call(
        wrapped, name=name, grid=grid,
        out_shape=list(out_shape) + [jax.ShapeDtypeStruct(b.shape, b.dtype) for b in bufs],
        in_specs=list(in_specs) + [anyspec] * (ns + nbi),
        out_specs=list(out_specs) + [anyspec] * nb,
        scratch_shapes=list(scratch) + [pltpu.SemaphoreType.DMA((npc, N_DEV - 1)),
                                        pltpu.SemaphoreType.DMA((npc, N_DEV - 1)),
                                        pltpu.SemaphoreType.DMA((npc,))],
        input_output_aliases=aliases,
        compiler_params=pltpu.CompilerParams(vmem_limit_bytes=VMEM_LIMIT, has_side_effects=True,
                                             dimension_semantics=("arbitrary",) * len(grid)),
    )(*operands, *srcs, *old)
    return list(res[:n_out]), list(res[n_out:])


def _exchange(comm, name):
    def body():
        pass

    return _call(body, name=name, grid=(1,), operands=[], in_specs=[], out_shape=[], out_specs=[],
                 comm=comm)[1]


def _mm(a, w, l, kind, *, out_dtype, name, tm=512, a_fn=None, epi=None, extras=(), comm=None,
        outs=None, sums=()):
    T, Ka = a.shape
    _, _, d2, d3 = w.shape
    n_out = {"nn_col": N_DEV * d3, "nn_row": d3, "nt_col": d2, "nt_row": N_DEV * d2}[kind]
    tm = min(tm, T)
    ne = len(extras)
    single = outs is None and not sums
    outs = [(n_out, out_dtype)] if outs is None else outs
    no, ns = len(outs), len(sums)

    def body(*refs):
        a_ref, w_ref = refs[0], refs[1]
        e_refs = refs[2:2 + ne]
        o_refs = refs[2 + ne:2 + ne + no]
        s_refs = refs[2 + ne + no:]
        av = a_ref[...]
        if a_fn is not None:
            av = a_fn(av)
        av = av.astype(BF)
        if kind == "nn_col":
            acc = jnp.concatenate([_dot(av, w_ref[j]) for j in range(N_DEV)], axis=1)
        elif kind == "nn_row":
            acc = _dot(av, w_ref[...].reshape(N_DEV * d2, d3))
        elif kind == "nt_col":
            acc = _dot_nt(av[:, 0:d3], w_ref[0])
            for j in range(1, N_DEV):
                acc = acc + _dot_nt(av[:, j * d3:(j + 1) * d3], w_ref[j])
        else:
            acc = jnp.concatenate([_dot_nt(av, w_ref[j]) for j in range(N_DEV)], axis=1)
        res = acc if epi is None else epi(acc, *[e[...] for e in e_refs])
        res = res if isinstance(res, tuple) else (res,)
        for o_ref, val in zip(o_refs, res[:no]):
            o_ref[...] = val.astype(o_ref.dtype)
        if ns:
            first = pl.program_id(0) == 0

            @pl.when(first)
            def _():
                for s_ref, val in zip(s_refs, res[no:]):
                    s_ref[...] = val

            @pl.when(jnp.logical_not(first))
            def _():
                for s_ref, val in zip(s_refs, res[no:]):
                    s_ref[...] += val

    in_specs = [pl.BlockSpec((tm, Ka), lambda i: (i, 0)),
                pl.BlockSpec((N_DEV, None, d2, d3), lambda i: (0, l, 0, 0))]
    extras = [e if isinstance(e, tuple) else (e, 0) for e in extras]
    for e, cb in extras:
        if e.shape[0] == 1:
            in_specs.append(pl.BlockSpec((1, n_out), lambda i: (0, 0)))
        else:
            in_specs.append(pl.BlockSpec((tm, n_out), functools.partial(lambda i, cb: (i, cb), cb=cb)))
    extras = [e for e, _ in extras]
    out_shape = [jax.ShapeDtypeStruct((T, wd), dt) for wd, dt in outs]
    out_shape += [jax.ShapeDtypeStruct((1, wd), F32) for wd in sums]
    out_specs = [pl.BlockSpec((tm, wd), lambda i: (i, 0)) for wd, _ in outs]
    out_specs += [pl.BlockSpec((1, wd), lambda i: (0, 0)) for wd in sums]
    res = _call(body, name=name, grid=(T // tm,), operands=[a, w, *extras], in_specs=in_specs,
                out_shape=out_shape, out_specs=out_specs,
                sem=("arbitrary",) if ns else ("parallel",), comm=comm)
    if comm is None:
        return res[0] if single else res
    return (res[0][0] if single else res[0]), res[1]


def _mm_tn(g, s, b, l, kind, *, name, tt=1024, b_fn=None):
    T, ws = s.shape
    wb = b.shape[1]
    _, _, d2, d3 = g.shape
    tt = min(tt, T)
    nsteps = T // tt
    cw = wb // N_DEV

    fresh = isinstance(g, jax.ShapeDtypeStruct)

    def body(*refs):
        s_ref, b_ref, o_ref, acc, stage, sem = refs if fresh else refs[1:]
        t = pl.program_id(0)

        @pl.when(t == 0)
        def _():
            acc[...] = jnp.zeros((ws, wb), F32)

        st = s_ref[...].astype(F32).T.astype(BF)
        for j in range(N_DEV):
            bv = b_ref[:, j * cw:(j + 1) * cw]
            if b_fn is not None:
                bv = b_fn(bv)
            acc[:, j * cw:(j + 1) * cw] += _dot(st, bv.astype(BF))

        @pl.when(t == nsteps - 1)
        def _():
            for j in range(N_DEV):
                if kind == "cols":
                    blk = acc[:, j * d3:(j + 1) * d3]
                elif kind == "rows":
                    blk = acc[j * d2:(j + 1) * d2, :]
                else:
                    blk = acc[:, j * d2:(j + 1) * d2].T
                stage[...] = blk.astype(BF)
                cp = pltpu.make_async_copy(stage, o_ref.at[j, l], sem)
                cp.start()
                cp.wait()

    anyspec = pl.BlockSpec(memory_space=pl.ANY)
    return pl.pallas_call(
        body, name=name, grid=(nsteps,),
        out_shape=jax.ShapeDtypeStruct(g.shape, BF),
        in_specs=([] if fresh else [anyspec]) + [pl.BlockSpec((tt, ws), lambda t: (t, 0)),
                                                 pl.BlockSpec((tt, wb), lambda t: (t, 0))],
        out_specs=anyspec,
        scratch_shapes=[pltpu.VMEM((ws, wb), F32), pltpu.VMEM((d2, d3), BF), pltpu.SemaphoreType.DMA],
        input_output_aliases={} if fresh else {0: 0},
        compiler_params=_cp(("arbitrary",)),
    )(*([] if fresh else [g]), s, b)


def _ln_stats(z):
    mu = jnp.mean(z, axis=-1, keepdims=True)
    zc = z - mu
    var = jnp.mean(zc * zc, axis=-1, keepdims=True)
    rstd = lax.rsqrt(var + LN_EPS)
    return zc * rstd, rstd


def _ln_bwd_core(dy, zh, rstd, g):
    dzh = dy * g
    m1 = jnp.mean(dzh, axis=-1, keepdims=True)
    m2 = jnp.mean(dzh * zh, axis=-1, keepdims=True)
    return rstd * (dzh - m1 - zh * m2)


def _colsum(v):
    return jnp.sum(v, axis=0, keepdims=True)


def _z_and_ln(z, g, b):
    zh, _ = _ln_stats(z)
    return z, zh * g + b


def _ln_bwd_tile(dy, z, g):
    zh, rstd = _ln_stats(z)
    dz = _ln_bwd_core(dy, zh, rstd, g)
    return dz, _colsum(dy * zh), _colsum(dy), _colsum(dz)


def _heads(D):
    return D // HEAD


def _rms_parts(o, D):
    xs, rs = [], []
    for h in range(_heads(D)):
        oh = o[:, h * HEAD:(h + 1) * HEAD]
        r = lax.rsqrt(jnp.mean(oh * oh, axis=-1, keepdims=True) + RMS_EPS)
        xs.append(oh * r)
        rs.append(r)
    return xs, rs


def _gate_bwd_tile(dyt, ot, grt, ngr):
    D = ot.shape[1]
    xs, rs = _rms_parts(ot, D)
    xh = jnp.concatenate(xs, axis=1)
    sg = _sig(grt)
    on = xh * ngr
    dgr = dyt * on * (sg * (1.0 + grt * (1.0 - sg)))
    don = dyt * (grt * sg)
    dxh = don * ngr
    dos = []
    for h in range(_heads(D)):
        sl = slice(h * HEAD, (h + 1) * HEAD)
        m = jnp.mean(dxh[:, sl] * xs[h], axis=-1, keepdims=True)
        dos.append(rs[h] * (dxh[:, sl] - xs[h] * m))
    return jnp.concatenate(dos, axis=1), dgr, _colsum(don * xh)


def _u_and_glu(u):
    D = u.shape[1] // 2
    return u, u[:, :D] * _sig(u[:, D:])


def _silu_ln_bwd_tile(dst, ct, gr, br):
    zh, rstd = _ln_stats(ct)
    ln = zh * gr + br
    sg = _sig(ln)
    dln = dst * (sg * (1.0 + ln * (1.0 - sg)))
    dc = _ln_bwd_core(dln, zh, rstd, gr)
    return dc, _colsum(dln * zh), _colsum(dln), _colsum(dc)


def _top_tile(z, g, b, tgt):
    D = z.shape[1]
    zh, rstd = _ln_stats(z)
    e = zh * g + b - tgt
    dy = e * (1.0 / D)
    dz = _ln_bwd_core(dy, zh, rstd, g)
    return dz, _colsum(dy * zh), _colsum(dy), _colsum(e * e) * (0.5 / D)


def _shifted(ext, n, tr):
    for b in range(8):
        rb = ext if b == 0 else pltpu.roll(ext, n - b, 0)
        for a in range(HALO // 8 + 1):
            o = 8 * a + b
            if o <= HALO:
                yield o, rb[8 * a:8 * a + tr, :]


def _lane_strips(D):
    return [slice(c, c + 128) for c in range(0, D, 128)]


def _conv_fwd(glu, w, bdw, g, b, name, tr=256, comm=None):
    T, D = glu.shape
    tr = min(tr, T // 2)
    rt = min(CONV_ROWS, tr)
    hb = tr // HALO

    def body(cur_ref, halo_ref, w_ref, bdw_ref, g_ref, b_ref, c_ref, s_ref):
        i = pl.program_id(0)
        for cs in _lane_strips(D):
            halo = jnp.where(i > 0, halo_ref[:, cs], 0.0)
            for r0 in range(0, tr, rt):
                if r0 == 0:
                    ext = jnp.concatenate([halo, cur_ref[0:rt, cs]], axis=0)
                else:
                    ext = cur_ref[r0 - HALO:r0 + rt, cs]
                acc = None
                for o, sh in _shifted(ext, rt + HALO, rt):
                    k = o - (HALO - CONV_W + 1)
                    if 0 <= k < CONV_W:
                        term = w_ref[k:k + 1, cs] * sh
                        acc = term if acc is None else acc + term
                c_ref[r0:r0 + rt, cs] = acc + bdw_ref[:, cs]
        cv = c_ref[...]
        zh, _ = _ln_stats(cv)
        ln = zh * g_ref[...] + b_ref[...]
        s_ref[...] = (ln * _sig(ln)).astype(BF)

    row = pl.BlockSpec((1, D), lambda i: (0, 0))
    return _call(
        body, name=name, grid=(T // tr,), operands=[glu, glu, w, bdw, g, b],
        out_shape=[jax.ShapeDtypeStruct((T, D), F32), jax.ShapeDtypeStruct((T, D), BF)],
        in_specs=[pl.BlockSpec((tr, D), lambda i: (i, 0)),
                  pl.BlockSpec((HALO, D), lambda i: (jnp.maximum(i * hb - 1, 0), 0)),
                  pl.BlockSpec((HALO, D), lambda i: (0, 0)), row, row, row],
        out_specs=[pl.BlockSpec((tr, D), lambda i: (i, 0))] * 2,
        sem=("parallel",), comm=comm)


def _conv_bwd(dc, glu, u, w, name, tr=256, comm=None):
    T, D = glu.shape
    tr = min(tr, T // 2)
    rt = min(CONV_ROWS, tr)
    hb = tr // HALO
    nsteps = T // tr
    last_hb = T // HALO - 1

    def body(dc_ref, nxt_ref, glu_ref, prv_ref, ua_ref, ug_ref, w_ref, du_ref, dw_ref, dbu_ref):
        i = pl.program_id(0)

        @pl.when(i == 0)
        def _():
            dw_ref[...] = jnp.zeros((HALO, D), F32)
            dbu_ref[...] = jnp.zeros((1, 2 * D), F32)

        for cs in _lane_strips(D):
            nxt = jnp.where(i < nsteps - 1, nxt_ref[:, cs], 0.0)
            prv = jnp.where(i > 0, prv_ref[:, cs], 0.0)
            dws = [None] * CONV_W
            dba = dbg = None
            gs = slice(D + cs.start, D + cs.stop)
            for r0 in range(0, tr, rt):
                if r0 + rt == tr:
                    ext2 = jnp.concatenate([dc_ref[r0:tr, cs], nxt], axis=0)
                else:
                    ext2 = dc_ref[r0:r0 + rt + HALO, cs]
                acc = None
                for o, sh in _shifted(ext2, rt + HALO, rt):
                    k = CONV_W - 1 - o
                    if 0 <= k < CONV_W:
                        term = w_ref[k:k + 1, cs] * sh
                        acc = term if acc is None else acc + term
                sg = _sig(ug_ref[r0:r0 + rt, cs])
                da = acc * sg
                dgt = acc * ua_ref[r0:r0 + rt, cs] * sg * (1.0 - sg)
                du_ref[r0:r0 + rt, cs] = da.astype(BF)
                du_ref[r0:r0 + rt, gs] = dgt.astype(BF)
                dba = _colsum(da) if dba is None else dba + _colsum(da)
                dbg = _colsum(dgt) if dbg is None else dbg + _colsum(dgt)
                if r0 == 0:
                    ext = jnp.concatenate([prv, glu_ref[0:rt, cs]], axis=0)
                else:
                    ext = glu_ref[r0 - HALO:r0 + rt, cs]
                dcs = dc_ref[r0:r0 + rt, cs]
                for o, sh in _shifted(ext, rt + HALO, rt):
                    k = o - (HALO - CONV_W + 1)
                    if 0 <= k < CONV_W:
                        part = _colsum(dcs * sh)
                        dws[k] = part if dws[k] is None else dws[k] + part
            for k in range(CONV_W):
                dw_ref[k:k + 1, cs] += dws[k]
            dbu_ref[:, cs] += dba
            dbu_ref[:, gs] += dbg

    return _call(
        body, name=name, grid=(nsteps,), operands=[dc, dc, glu, glu, u, u, w],
        out_shape=[jax.ShapeDtypeStruct((T, 2 * D), BF), jax.ShapeDtypeStruct((HALO, D), F32),
                   jax.ShapeDtypeStruct((1, 2 * D), F32)],
        in_specs=[pl.BlockSpec((tr, D), lambda i: (i, 0)),
                  pl.BlockSpec((HALO, D), lambda i: (jnp.minimum((i + 1) * hb, last_hb), 0)),
                  pl.BlockSpec((tr, D), lambda i: (i, 0)),
                  pl.BlockSpec((HALO, D), lambda i: (jnp.maximum(i * hb - 1, 0), 0)),
                  pl.BlockSpec((tr, D), lambda i: (i, 0)),
                  pl.BlockSpec((tr, D), lambda i: (i, 1)),
                  pl.BlockSpec((HALO, D), lambda i: (0, 0))],
        out_specs=[pl.BlockSpec((tr, 2 * D), lambda i: (i, 0)),
                   pl.BlockSpec((HALO, D), lambda i: (0, 0)),
                   pl.BlockSpec((1, 2 * D), lambda i: (0, 0))],
        sem=("arbitrary",), comm=comm)


def _rec_consts(C):
    t = np.arange(C)
    lb = (t[None, :] <= t[:, None]).astype(np.float32)

    def cum_at(idx):
        return (t[None, :] <= idx[:, None]).astype(np.float32)

    blocks, masks = [], []
    mid = SUB * (t // SUB) + SUB // 2 - 1
    eq = lb - cum_at(mid)
    blocks += [eq, -eq]
    masks.append(((t[:, None] // SUB) == (t[None, :] // SUB)) & (t[None, :] <= t[:, None]))
    nb = SUB
    while nb < C:
        odd = (t // nb) % 2 == 1
        e_t = nb * (t // nb) - 1
        e_s = nb * (t // nb) + nb - 1
        blocks.append(np.where(odd[:, None], lb - cum_at(e_t), 0.0))
        blocks.append(np.where(~odd[:, None], cum_at(e_s) - lb, 0.0))
        masks.append(((t[:, None] // (2 * nb)) == (t[None, :] // (2 * nb))) & odd[:, None] & ~odd[None, :])
        nb *= 2
    blocks += [lb, 1.0 - lb, np.ones((HEAD, C), np.float32)]
    L = np.concatenate(blocks, axis=0).astype(np.float32)
    L3 = np.concatenate([L, L, L], axis=1)
    LT3 = np.concatenate([L.T, L.T], axis=1)
    m = np.stack(masks).astype(np.float32)
    mT = np.transpose(m, (0, 2, 1)).copy()
    return (jnp.asarray(L3, BF), jnp.asarray(LT3, BF), jnp.asarray(m), jnp.asarray(mT), len(masks))


def _split3(x):
    h = x.astype(BF)
    r = x - h.astype(F32)
    m = r.astype(BF)
    lo = (r - m.astype(F32)).astype(BF)
    return h, m, lo


def _split2(x):
    h = x.astype(BF)
    return h, (x - h.astype(F32)).astype(BF)


def _gates(qr, fz, lbr):
    sq = _sig(qr)
    q = qr * sq
    sg = _sig(fz)
    f = lbr + (1.0 - lbr) * sg
    fc = jnp.maximum(f, GATE_EPS)
    return q, 1.0 - f, jnp.log(fc), sq, sg, f, fc


def _rec_fwd(proj, lbr, ng, name, comm=None):
    T, D4 = proj.shape
    D = D4 // 4
    H = _heads(D)
    C = CHUNK
    nC = T // C
    L3, _, m, _, nl = _rec_consts(C)
    R = L3.shape[0]

    def body(q_ref, f_ref, v_ref, gr_ref, lb_ref, ng_ref, l_ref, m_ref, o_ref, y_ref, s_ref, st):
        @pl.when(pl.program_id(0) == 0)
        def _():
            st[...] = jnp.zeros((H, HEAD, HEAD), F32)

        q, k, logf = _gates(q_ref[...], f_ref[...], lb_ref[...])[:3]
        ex = jnp.exp(_dot(l_ref[...], jnp.concatenate(_split3(logf), axis=0)))
        vb = v_ref[...].astype(BF)
        s_ref[0] = st[...]
        outs = []
        for h in range(H):
            sl = slice(h * HEAD, (h + 1) * HEAD)
            qh, kh = q[:, sl], k[:, sl]
            p = jnp.zeros((C, C), F32)
            for lv in range(nl):
                qt = (qh * ex[2 * lv * C:(2 * lv + 1) * C, sl]).astype(BF)
                kt = (kh * ex[(2 * lv + 1) * C:(2 * lv + 2) * C, sl]).astype(BF)
                p = p + jnp.where(m_ref[lv] > 0.0, _dot_nt(qt, kt), 0.0)
            base = 2 * nl * C
            qhat = (qh * ex[base:base + C, sl]).astype(BF)
            khat = (kh * ex[base + C:base + 2 * C, sl]).astype(BF)
            elast = ex[base + 2 * C:base + 2 * C + HEAD, sl]
            sth = st[h]
            outs.append(_dot(p.astype(BF), vb[:, sl]) + _dot_nt(qhat, sth.astype(BF)))
            st[h] = elast * sth + _dot_tn(vb[:, sl], khat)
        o_ref[...] = jnp.concatenate(outs, axis=1)
        xh = jnp.concatenate([oh * lax.rsqrt(jnp.mean(oh * oh, axis=-1, keepdims=True) + RMS_EPS) for oh in outs],
                             axis=1)
        gr = gr_ref[...]
        y_ref[...] = (xh * ng_ref[...] * (gr * _sig(gr))).astype(BF)

    def cblk(cb):
        return pl.BlockSpec((C, D), lambda i: (i, cb))

    return _call(
        body, name=name, grid=(nC,), operands=[proj, proj, proj, proj, lbr, ng, L3, m],
        out_shape=[jax.ShapeDtypeStruct((T, D), F32), jax.ShapeDtypeStruct((T, D), BF),
                   jax.ShapeDtypeStruct((nC, H, HEAD, HEAD), F32)],
        in_specs=[cblk(0), cblk(1), cblk(2), cblk(3), pl.BlockSpec((1, D), lambda i: (0, 0)),
                  pl.BlockSpec((1, D), lambda i: (0, 0)),
                  pl.BlockSpec(L3.shape, lambda i: (0, 0)), pl.BlockSpec(m.shape, lambda i: (0, 0, 0))],
        out_specs=[pl.BlockSpec((C, D), lambda i: (i, 0)), pl.BlockSpec((C, D), lambda i: (i, 0)),
                   pl.BlockSpec((1, H, HEAD, HEAD), lambda i: (i, 0, 0, 0))],
        scratch=[pltpu.VMEM((H, HEAD, HEAD), F32)], sem=("arbitrary",), comm=comm)


def _rec_bwd(proj, lbr, do, dgr, ssave, name, comm=None):
    T, D4 = proj.shape
    D = D4 // 4
    H = _heads(D)
    C = min(CHUNK_BWD, T // 2)
    nC = T // C
    stride = C // CHUNK
    L3, LT3, m, mT, nl = _rec_consts(C)

    def body(q_ref, f_ref, v_ref, lb_ref, do_ref, dgr_ref, s_ref, l_ref, lt_ref, m_ref, mt_ref,
             dp_ref, dlb_ref, dst):
        @pl.when(pl.program_id(0) == 0)
        def _():
            dst[...] = jnp.zeros((H, HEAD, HEAD), F32)
            dlb_ref[...] = jnp.zeros((1, D), F32)

        qr = q_ref[...]
        lbv = lb_ref[...]
        q, k, logf, sq, sg, f, fc = _gates(qr, f_ref[...], lbv)
        ex = jnp.exp(_dot(l_ref[...], jnp.concatenate(_split3(logf), axis=0)))
        vb = v_ref[...].astype(BF)
        dob = do_ref[...].astype(BF)
        base = 2 * nl * C
        de = [[] for _ in range(2 * nl + 3)]
        dqs, dks, dvs = [], [], []
        for h in range(H):
            sl = slice(h * HEAD, (h + 1) * HEAD)
            qh, kh, vh, doh = q[:, sl], k[:, sl], vb[:, sl], dob[:, sl]
            dp = _dot_nt(doh, vh)
            dpt = _dot_nt(vh, doh)
            sth = s_ref[0, h]
            dsth = dst[h]
            dsb = dsth.astype(BF)
            pt = jnp.zeros((C, C), F32)
            dq = jnp.zeros((C, HEAD), F32)
            dk = jnp.zeros((C, HEAD), F32)
            for lv in range(nl):
                exq = ex[2 * lv * C:(2 * lv + 1) * C, sl]
                exk = ex[(2 * lv + 1) * C:(2 * lv + 2) * C, sl]
                qt = qh * exq
                kt = kh * exk
                qtb, ktb = qt.astype(BF), kt.astype(BF)
                pt = pt + jnp.where(mt_ref[lv] > 0.0, _dot_nt(ktb, qtb), 0.0)
                dqt = _dot(jnp.where(m_ref[lv] > 0.0, dp, 0.0).astype(BF), ktb)
                dkt = _dot(jnp.where(mt_ref[lv] > 0.0, dpt, 0.0).astype(BF), qtb)
                dq = dq + dqt * exq
                dk = dk + dkt * exk
                de[2 * lv].append(dqt * qt)
                de[2 * lv + 1].append(dkt * kt)
            exb = ex[base:base + C, sl]
            exkh = ex[base + C:base + 2 * C, sl]
            elast = ex[base + 2 * C:base + 2 * C + HEAD, sl]
            qhat = qh * exb
            khat = kh * exkh
            dqh = _dot(doh, sth.astype(BF))
            dkh = _dot(vh, dsb)
            dq = dq + dqh * exb
            dk = dk + dkh * exkh
            de[2 * nl].append(dqh * qhat)
            de[2 * nl + 1].append(dkh * khat)
            de[2 * nl + 2].append(dsth * sth * elast)
            dvs.append(_dot(pt.astype(BF), doh) + _dot_nt(khat.astype(BF), dsb))
            dst[h] = elast * dsth + _dot_tn(doh, qhat.astype(BF))
            dqs.append(dq)
            dks.append(dk)
        de_all = jnp.concatenate([jnp.concatenate(b, axis=1) for b in de], axis=0)
        dlogf = _dot(lt_ref[...], jnp.concatenate(_split2(de_all), axis=0))
        dq = jnp.concatenate(dqs, axis=1)
        dk = jnp.concatenate(dks, axis=1)
        dv = jnp.concatenate(dvs, axis=1)
        ind = jnp.where(f > GATE_EPS, 1.0, jnp.where(f == GATE_EPS, 0.5, 0.0))
        df = dlogf * ind / fc - dk
        dfz = df * (1.0 - lbv) * sg * (1.0 - sg)
        dlb_ref[...] += _colsum(df * (1.0 - sg))
        dqr = dq * (sq * (1.0 + qr * (1.0 - sq)))
        dp_ref[...] = jnp.concatenate([dqr.astype(BF), dfz.astype(BF), dv.astype(BF), dgr_ref[...]], axis=1)

    def cblk(cb):
        return pl.BlockSpec((C, D), lambda i: (nC - 1 - i, cb))

    def whole(a):
        nd = a.ndim
        return pl.BlockSpec(a.shape, lambda i: (0,) * nd)

    return _call(
        body, name=name, grid=(nC,), operands=[proj, proj, proj, lbr, do, dgr, ssave, L3, LT3, m, mT],
        out_shape=[jax.ShapeDtypeStruct((T, D4), BF), jax.ShapeDtypeStruct((1, D), F32)],
        in_specs=[cblk(0), cblk(1), cblk(2), pl.BlockSpec((1, D), lambda i: (0, 0)),
                  pl.BlockSpec((C, D), lambda i: (nC - 1 - i, 0)),
                  pl.BlockSpec((C, D), lambda i: (nC - 1 - i, 0)),
                  pl.BlockSpec((1, H, HEAD, HEAD), lambda i: (stride * (nC - 1 - i), 0, 0, 0)),
                  whole(L3), whole(LT3), whole(m), whole(mT)],
        out_specs=[pl.BlockSpec((C, D4), lambda i: (nC - 1 - i, 0)),
                   pl.BlockSpec((1, D), lambda i: (0, 0))],
        scratch=[pltpu.VMEM((H, HEAD, HEAD), F32)], sem=("arbitrary",), comm=comm)


def _softmax_rows(lg_ref):
    n = lg_ref.shape[0]
    rows = [lg_ref[l:l + 1, :] for l in range(n)]
    mx = rows[0]
    for r in rows[1:]:
        mx = jnp.maximum(mx, r)
    es = [jnp.exp(r - mx) for r in rows]
    tot = es[0]
    for e in es[1:]:
        tot = tot + e
    return [e / tot for e in es]


def _lb_fwd(logits):
    n, D = logits.shape

    def body(lg_ref, o_ref):
        soft = _softmax_rows(lg_ref)
        acc = jnp.zeros((1, D), F32)
        o_ref[0:1, :] = acc
        for j in range(1, n):
            acc = acc + soft[j]
            o_ref[j:j + 1, :] = acc

    return pl.pallas_call(body, name="lb_fwd", out_shape=jax.ShapeDtypeStruct((n, D), F32))(logits)


def _small_reduce(parts, logits, lb_row0):
    _, R, D = parts.shape
    n = logits.shape[0]

    def body(p_ref, lg_ref, o_ref):
        acc = p_ref[0]
        for d in range(1, N_DEV):
            acc = acc + p_ref[d]
        o_ref[...] = acc
        soft = _softmax_rows(lg_ref)
        dlb = [o_ref[lb_row0 + j:lb_row0 + j + 1, :] for j in range(n)]
        dsoft = [jnp.zeros((1, D), F32)]
        for l in range(1, n):
            s = dlb[l]
            for j in range(l + 1, n):
                s = s + dlb[j]
            dsoft.append(s)
        dot = soft[0] * dsoft[0]
        for l in range(1, n):
            dot = dot + soft[l] * dsoft[l]
        for l in range(n):
            o_ref[lb_row0 + l:lb_row0 + l + 1, :] = soft[l] * (dsoft[l] - dot)

    return pl.pallas_call(body, name="small_reduce", out_shape=jax.ShapeDtypeStruct((R, D), F32))(parts, logits)


def _adam_math(g, w, m, v):
    m2 = ADAM_B1 * m + (1.0 - ADAM_B1) * g
    v2 = ADAM_B2 * v + (1.0 - ADAM_B2) * (g * g)
    mh = m2 / (1.0 - ADAM_B1 ** ADAM_STEP)
    vh = v2 / (1.0 - ADAM_B2 ** ADAM_STEP)
    delta = -ADAM_LR * (mh / (jnp.sqrt(vh) + ADAM_EPS) + ADAM_WD * w)
    return delta, m2, v2


def _adamw(recv, w, m, v, name):
    nsrc = recv.shape[0]
    shp = w.shape
    cols = shp[-1]
    rows = int(np.prod(shp[:-1]))
    r2 = recv.reshape(nsrc, rows, cols)
    tr = min(rows, max(8, (1 << 20) // (cols * nsrc)))
    while rows % tr:
        tr //= 2

    def body(r_ref, w_ref, m_ref, v_ref, g_ref, d_ref, nm_ref, nv_ref):
        g = r_ref[0].astype(F32)
        for s in range(1, nsrc):
            g = g + r_ref[s].astype(F32)
        delta, m2, v2 = _adam_math(g, w_ref[...], m_ref[...], v_ref[...])
        g_ref[...] = g
        d_ref[...] = delta
        nm_ref[...] = m2
        nv_ref[...] = v2

    blk = pl.BlockSpec((tr, cols), lambda i: (i, 0))
    outs = pl.pallas_call(
        body, name=name, grid=(rows // tr,),
        out_shape=[jax.ShapeDtypeStruct((rows, cols), F32)] * 4,
        in_specs=[pl.BlockSpec((nsrc, tr, cols), lambda i: (0, i, 0)), blk, blk, blk],
        out_specs=[blk] * 4,
        compiler_params=_cp(("parallel",)),
    )(r2, w.reshape(rows, cols), m.reshape(rows, cols), v.reshape(rows, cols))
    return [o.reshape(shp) for o in outs]


def _pack(arrs, lane):
    flat = jnp.concatenate([a.reshape(-1) for a in arrs])
    n = flat.shape[0]
    rows = -(-n // lane)
    rows = -(-rows // 8) * 8
    flat = jnp.pad(flat, (0, rows * lane - n))
    return flat.reshape(rows, lane)


def _unpack(packed, shapes):
    flat = packed.reshape(-1)
    out, off = [], 0
    for s in shapes:
        n = int(np.prod(s))
        out.append(flat[off:off + n].reshape(s))
        off += n
    return out


def kernel(x, ln_mix_g, ln_mix_b, ln_ffn_g, ln_ffn_b, ffn_w1, ffn_w2, a_w_in, a_lb_logits, a_norm_g, a_w_out, b_w_pw1, b_b_pw1, b_w_dw, b_b_dw, b_ln_g, b_ln_b, b_w_pw2, b_b_pw2, loss_target, m_ln_mix_g, m_ln_mix_b, m_ln_ffn_g, m_ln_ffn_b, m_ffn_w1, m_ffn_w2, m_a_w_in, m_a_lb_logits, m_a_norm_g, m_a_w_out, m_b_w_pw1, m_b_b_pw1, m_b_w_dw, m_b_b_dw, m_b_ln_g, m_b_ln_b, m_b_w_pw2, m_b_b_pw2, v_ln_mix_g, v_ln_mix_b, v_ln_ffn_g, v_ln_ffn_b, v_ffn_w1, v_ffn_w2, v_a_w_in, v_a_lb_logits, v_a_norm_g, v_a_w_out, v_b_w_pw1, v_b_b_pw1, v_b_w_dw, v_b_b_dw, v_b_ln_g, v_b_ln_b, v_b_w_pw2, v_b_b_pw2):
    T, D = x.shape[1], x.shape[2]
    nA, nB = a_w_in.shape[0], b_w_pw1.shape[0]
    me = 4 * lax.axis_index("x") + 2 * lax.axis_index("y") + lax.axis_index("c")
    xin = x[0]
    tgt = loss_target[0]

    small_names = [b_b_pw1, b_w_dw, b_b_dw, b_ln_g, b_ln_b, b_b_pw2]
    sp = _pack(small_names, 128)
    wsrc = {"W1": ffn_w1.astype(BF), "W2": ffn_w2.astype(BF), "Win": a_w_in.astype(BF),
            "Wout": a_w_out.astype(BF), "Wp1": b_w_pw1.astype(BF), "Wp2": b_w_pw2.astype(BF), "small": sp[None]}
    W = {k: jax.ShapeDtypeStruct((N_DEV,) + v.shape, v.dtype) for k, v in wsrc.items()}
    GW = {k: jax.ShapeDtypeStruct((N_DEV,) + v.shape, BF) for k, v in wsrc.items() if k != "small"}
    R = {k: jax.ShapeDtypeStruct(v.shape, BF) for k, v in GW.items()}

    def names_of(items):
        out = []
        for n, _ in items:
            if n not in out:
                out.append(n)
        return out

    def hosted(fn, key, sched, mode, src, bufs, *args, **kw):
        items = sched.get(key)
        if items is None:
            return fn(*args, name=key, **kw)
        comm = (mode, [(src[n], l, bufs[n]) for n, l in items])
        outs, new = fn(*args, name=key, comm=comm, **kw)
        for n, b in zip(names_of(items), new):
            bufs[n] = b
        return outs

    first = [("Win", 0), ("small", 0)]
    for n, b in zip(names_of(first), _exchange(("gather", [(wsrc[n], l, W[n]) for n, l in first]), "gather_first")):
        W[n] = b
    fwd_sched = {"a0_proj": [("Wout", 0), ("Wp1", 0)], "a0_rec_fwd": [("W1", 0), ("W2", 0), ("Wp2", 0)],
                 "l0_ffn_up": [("W2", 1)], "l0_ffn_down": [("W1", 1)],
                 "b0_conv": [("Win", 1), ("Wout", 1)], "l1_ffn_up": [("W2", 2)], "l1_ffn_down": [("W1", 2)],
                 "a1_rec_fwd": [("Wp1", 1), ("Wp2", 1), ("W1", 3), ("W2", 3)]}
    bwd_sched = {"a1_rec_bwd": [("W1", 3), ("W2", 3), ("Wp1", 1), ("Wp2", 1), ("W1", 2), ("W2", 2)],
                 "b0_conv_bwd": [("Win", 1), ("Wout", 1)],
                 "a0_rec_bwd": [("W1", 1), ("W2", 1), ("Wp1", 0), ("Wp2", 0), ("W1", 0), ("W2", 0), ("Wout", 0)],
                 "a0_proj_dx": [("Win", 0)]}

    def fwd(fn, key, *args, **kw):
        return hosted(fn, key, fwd_sched, "gather", wsrc, W, *args, **kw)

    def bwd(fn, key, *args, **kw):
        return hosted(fn, key, bwd_sched, "scatter", GW, R, *args, **kw)

    spg = W["small"][:, 0]
    sm = [jnp.stack(p) for p in zip(*[_unpack(spg[d], [a.shape for a in small_names]) for d in range(N_DEV)])]
    bpw1 = jnp.transpose(sm[0], (1, 0, 2)).reshape(nB, 1, 2 * D)
    wdw = jnp.transpose(sm[1], (1, 2, 0, 3)).reshape(nB, CONV_W, D)
    wdw = jnp.pad(wdw, ((0, 0), (0, HALO - CONV_W), (0, 0)))
    bdw, blng, blnb, bpw2 = [jnp.transpose(s, (1, 0, 2)).reshape(nB, 1, D) for s in sm[2:]]
    lb_all = _lb_fwd(a_lb_logits)

    zx = [(D, F32), (D, F32)]
    ln_out = dict(outs=[(D, F32)], sums=[D, D, D])
    ln_mix = [(ln_mix_g[i:i + 1], ln_mix_b[i:i + 1]) for i in range(DEPTH)]
    ln_ffn = [(ln_ffn_g[i:i + 1], ln_ffn_b[i:i + 1]) for i in range(DEPTH)]
    saved = []
    h = xin
    for i in range(DEPTH):
        j = i // 2
        sv = {"xin": h}
        if i % 2 == 0:
            proj = fwd(_mm, f"a{j}_proj", h, W["Win"], j, "nn_col", out_dtype=F32)
            o, yg, ssave = fwd(_rec_fwd, f"a{j}_rec_fwd", proj, lb_all[j:j + 1], a_norm_g[j:j + 1])
            z1, x1 = _mm(yg, W["Wout"], j, "nn_row", out_dtype=F32, name=f"a{j}_out", outs=zx,
                         epi=lambda acc, r, g, b: _z_and_ln(acc + ALPHA * r, g, b), extras=[h, *ln_mix[i]])
            sv.update(proj=proj, o=o, ssave=ssave, yg=yg)
        else:
            u, glu = _mm(h, W["Wp1"], j, "nn_col", out_dtype=F32, name=f"b{j}_pw1",
                         outs=[(2 * D, F32), (D, F32)],
                         epi=lambda acc, b: _u_and_glu(acc + b), extras=[bpw1[j]])
            cv, s = fwd(_conv_fwd, f"b{j}_conv", glu, wdw[j], bdw[j], blng[j], blnb[j])
            z1, x1 = _mm(s, W["Wp2"], j, "nn_row", out_dtype=F32, name=f"b{j}_pw2", outs=zx,
                         epi=lambda acc, bb, r, g, b: _z_and_ln(acc + bb + ALPHA * r, g, b),
                         extras=[bpw2[j], h, *ln_mix[i]])
            sv.update(u=u, glu=glu, cv=cv, s=s)
        hh = fwd(_mm, f"l{i}_ffn_up", x1, W["W1"], i, "nn_col", out_dtype=BF)
        relu2 = lambda t: jnp.square(jnp.maximum(t, 0))
        if i < DEPTH - 1:
            z2, h = fwd(_mm, f"l{i}_ffn_down", hh, W["W2"], i, "nn_row", out_dtype=F32, outs=zx, a_fn=relu2,
                        epi=lambda acc, r, g, b: _z_and_ln(acc + ALPHA * r, g, b), extras=[x1, *ln_ffn[i]])
        else:
            z2 = None
            dz2, dg_top, db_top, loss_row = fwd(
                _mm, f"l{i}_ffn_down", hh, W["W2"], i, "nn_row", out_dtype=F32, a_fn=relu2, **ln_out,
                epi=lambda acc, r, g, b, t: _top_tile(acc + ALPHA * r, g, b, t), extras=[x1, *ln_ffn[i], tgt])
        sv.update(z1=z1, x1=x1, hh=hh, z2=z2)
        saved.append(sv)

    rows = {}
    top = DEPTH - 1
    rows[("ffn_g", top)], rows[("ffn_b", top)] = dg_top, db_top
    for i in reversed(range(DEPTH)):
        j = i // 2
        sv = saved[i]
        if i > 0:
            below = dict(epi=lambda acc, r, z, g: _ln_bwd_tile(acc + ALPHA * r, z, g), **ln_out)
            below_extras = [saved[i - 1]["z2"], ln_ffn[i - 1][0]]
        else:
            below = dict(epi=lambda acc, r: acc + ALPHA * r)
            below_extras = []
        dh = _mm(dz2, W["W2"], i, "nt_row", out_dtype=BF, name=f"l{i}_ffn_down_dx",
                 epi=lambda acc, hv: acc * (2.0 * jnp.maximum(hv.astype(F32), 0.0)), extras=[sv["hh"]])
        GW["W2"] = _mm_tn(GW["W2"], dz2, sv["hh"], i, "rows_t", name=f"l{i}_ffn_down_dw",
                         b_fn=lambda t: jnp.square(jnp.maximum(t, 0)))
        GW["W1"] = _mm_tn(GW["W1"], sv["x1"], dh, i, "cols", name=f"l{i}_ffn_up_dw")
        dz1, dg, db, dz1sum = _mm(dh, W["W1"], i, "nt_col", out_dtype=F32, name=f"l{i}_ffn_up_dx",
                                  epi=lambda acc, r, z, g: _ln_bwd_tile(acc + ALPHA * r, z, g),
                                  extras=[dz2, sv["z1"], ln_mix[i][0]], **ln_out)
        rows[("mix_g", i)], rows[("mix_b", i)] = dg, db
        if i % 2 == 0:
            GW["Wout"] = _mm_tn(GW["Wout"], sv["yg"], dz1, j, "rows", name=f"a{j}_out_dw")
            do, dgr, dng = _mm(dz1, W["Wout"], j, "nt_row", out_dtype=F32, name=f"a{j}_out_dx",
                               outs=[(D, F32), (D, BF)], sums=[D], epi=_gate_bwd_tile,
                               extras=[sv["o"], (sv["proj"], 3), a_norm_g[j:j + 1]])
            rows[("ng", j)] = dng
            dproj, dlb = bwd(_rec_bwd, f"a{j}_rec_bwd", sv["proj"], lb_all[j:j + 1], do, dgr, sv["ssave"])
            rows[("lb", j)] = dlb
            GW["Win"] = _mm_tn(GW["Win"], sv["xin"], dproj, j, "cols", name=f"a{j}_proj_dw")
            res = bwd(_mm, f"a{j}_proj_dx", dproj, W["Win"], j, "nt_col", out_dtype=F32,
                      extras=[dz1, *below_extras], **below)
        else:
            rows[("bpw2", j)] = dz1sum
            GW["Wp2"] = _mm_tn(GW["Wp2"], sv["s"], dz1, j, "rows", name=f"b{j}_pw2_dw")
            dc, dlg, dlb_, dcs = _mm(dz1, W["Wp2"], j, "nt_row", out_dtype=F32, name=f"b{j}_pw2_dx",
                                     outs=[(D, F32)], sums=[D, D, D], epi=_silu_ln_bwd_tile,
                                     extras=[sv["cv"], blng[j], blnb[j]])
            rows[("blng", j)], rows[("blnb", j)], rows[("bdw", j)] = dlg, dlb_, dcs
            du, dwdw, dbu = bwd(_conv_bwd, f"b{j}_conv_bwd", dc, sv["glu"], sv["u"], wdw[j])
            rows[("wdw", j)] = dwdw[:CONV_W]
            rows[("bpw1", j)] = dbu.reshape(2, D)
            GW["Wp1"] = _mm_tn(GW["Wp1"], sv["xin"], du, j, "cols", name=f"b{j}_pw1_dw")
            res = _mm(du, W["Wp1"], j, "nt_col", out_dtype=F32, name=f"b{j}_pw1_dx",
                      extras=[dz1, *below_extras], **below)
        if i > 0:
            dz2, dg, db, _ = res
            rows[("ffn_g", i - 1)], rows[("ffn_b", i - 1)] = dg, db
        else:
            dx = res
    grad_x = dx[None]

    order = ([("mix_g", i) for i in range(DEPTH)] + [("mix_b", i) for i in range(DEPTH)]
             + [("ffn_g", i) for i in range(DEPTH)] + [("ffn_b", i) for i in range(DEPTH)])
    lb_row0 = len(order)
    order += [("lb", j) for j in range(nA)] + [("ng", j) for j in range(nA)]
    for j in range(nB):
        order += [("bpw1", j), ("wdw", j), ("bdw", j), ("blng", j), ("blnb", j), ("bpw2", j)]
    pieces, offs, off = [], {}, 0
    for key in order:
        offs[key] = off
        pieces.append(rows[key])
        off += rows[key].shape[0]
    offs["loss"] = off
    pieces.append(loss_row)
    off += 1
    part = jnp.concatenate(pieces, axis=0)
    part = jnp.pad(part, ((0, -off % 8), (0, 0)))
    parts = _exchange(("gather", [(part[None], 0, jax.ShapeDtypeStruct((N_DEV, 1) + part.shape, F32))]),
                      "gather_small_grads")[0][:, 0]
    G = _small_reduce(parts, a_lb_logits, lb_row0)
    loss = jnp.sum(G[offs["loss"]])

    def rep(kind, n):
        return jnp.concatenate([G[offs[(kind, i)]:offs[(kind, i)] + 1] for i in range(n)], axis=0)

    def shard_cols(full, width):
        return lax.dynamic_slice_in_dim(full, me * width, width, axis=full.ndim - 1)

    g_small = {
        "ln_mix_g": rep("mix_g", DEPTH), "ln_mix_b": rep("mix_b", DEPTH),
        "ln_ffn_g": rep("ffn_g", DEPTH), "ln_ffn_b": rep("ffn_b", DEPTH),
        "a_lb_logits": rep("lb", nA), "a_norm_g": rep("ng", nA),
        "b_b_pw1": shard_cols(jnp.stack([G[offs[("bpw1", j)]:offs[("bpw1", j)] + 2].reshape(2 * D)
                                         for j in range(nB)]), 2 * D // N_DEV),
        "b_w_dw": shard_cols(jnp.stack([G[offs[("wdw", j)]:offs[("wdw", j)] + CONV_W] for j in range(nB)]),
                             D // N_DEV),
        "b_b_dw": shard_cols(rep("bdw", nB), D // N_DEV),
        "b_ln_g": shard_cols(rep("blng", nB), D // N_DEV),
        "b_ln_b": shard_cols(rep("blnb", nB), D // N_DEV),
        "b_b_pw2": shard_cols(rep("bpw2", nB), D // N_DEV),
    }
    small_w = {"ln_mix_g": (ln_mix_g, m_ln_mix_g, v_ln_mix_g), "ln_mix_b": (ln_mix_b, m_ln_mix_b, v_ln_mix_b),
               "ln_ffn_g": (ln_ffn_g, m_ln_ffn_g, v_ln_ffn_g), "ln_ffn_b": (ln_ffn_b, m_ln_ffn_b, v_ln_ffn_b),
               "a_lb_logits": (a_lb_logits, m_a_lb_logits, v_a_lb_logits),
               "a_norm_g": (a_norm_g, m_a_norm_g, v_a_norm_g),
               "b_b_pw1": (b_b_pw1, m_b_b_pw1, v_b_b_pw1), "b_w_dw": (b_w_dw, m_b_w_dw, v_b_w_dw),
               "b_b_dw": (b_b_dw, m_b_b_dw, v_b_b_dw), "b_ln_g": (b_ln_g, m_b_ln_g, v_b_ln_g),
               "b_ln_b": (b_ln_b, m_b_ln_b, v_b_ln_b), "b_b_pw2": (b_b_pw2, m_b_b_pw2, v_b_b_pw2)}
    snames = list(small_w)
    sshapes = [small_w[k][0].shape for k in snames]
    pg = _pack([g_small[k] for k in snames], 1024)
    pw, pm, pv = [_pack([small_w[k][q] for k in snames], 1024) for q in range(3)]
    sres = _adamw(pg[None], pw, pm, pv, "adamw_small")
    sres = [dict(zip(snames, _unpack(r, sshapes))) for r in sres]

    recv = [R[n] for n in ["W1", "W2", "Win", "Wout", "Wp1", "Wp2"]]
    big = {}
    for nm, rv, (w, m, v) in zip(
            ["ffn_w1", "ffn_w2", "a_w_in", "a_w_out", "b_w_pw1", "b_w_pw2"], recv,
            [(ffn_w1, m_ffn_w1, v_ffn_w1), (ffn_w2, m_ffn_w2, v_ffn_w2), (a_w_in, m_a_w_in, v_a_w_in),
             (a_w_out, m_a_w_out, v_a_w_out), (b_w_pw1, m_b_w_pw1, v_b_w_pw1), (b_w_pw2, m_b_w_pw2, v_b_w_pw2)]):
        big[nm] = _adamw(rv, w, m, v, f"adamw_{nm}")

    names = ["ln_mix_g", "ln_mix_b", "ln_ffn_g", "ln_ffn_b", "ffn_w1", "ffn_w2", "a_w_in", "a_lb_logits",
             "a_norm_g", "a_w_out", "b_w_pw1", "b_b_pw1", "b_w_dw", "b_b_dw", "b_ln_g", "b_ln_b", "b_w_pw2",
             "b_b_pw2"]
    out = [loss, grad_x]
    for q in range(4):
        for nm in names:
            out.append(big[nm][q] if nm in big else sres[q][nm])
    return tuple(out)
```

```python
import functools

import numpy as np
import jax
import jax.numpy as jnp
from jax import lax
from jax.experimental import pallas as pl
from jax.experimental.pallas import tpu as pltpu

F32 = jnp.float32
BF = jnp.bfloat16

N_DEV = 8
DEPTH = 4
HEAD = 128
CHUNK = 64
CHUNK_BWD = 64
SUB = 8
CONV_W = 31
HALO = 32
CONV_ROWS = 128
ALPHA = (2.0 * DEPTH) ** 0.25
LN_EPS = 1e-5
RMS_EPS = 1e-6
GATE_EPS = 1e-6
ADAM_LR = 0.001
ADAM_B1 = 0.9
ADAM_B2 = 0.999
ADAM_EPS = 1e-08
ADAM_WD = 0.01
ADAM_STEP = 10
VMEM_LIMIT = 56 * 1024 * 1024
MESH = pl.DeviceIdType.MESH


def _cp(sem=None):
    return pltpu.CompilerParams(vmem_limit_bytes=VMEM_LIMIT, dimension_semantics=sem)


def _dot(a, b):
    return jnp.dot(a, b, preferred_element_type=F32)


def _dot_nt(a, b):
    return lax.dot_general(a, b, (((1,), (1,)), ((), ())), preferred_element_type=F32)


def _dot_tn(a, b):
    return lax.dot_general(a, b, (((0,), (0,)), ((), ())), preferred_element_type=F32)


def _sig(x):
    return jax.nn.sigmoid(x)


def _distinct(items):
    out = []
    for it in items:
        if not any(it is q for q in out):
            out.append(it)
    return out


def _index_of(items, it):
    return next(i for i, q in enumerate(items) if q is it)


def _comm_copies(mode, plan, src_refs, buf_refs, send_sems, recv_sems, loc_sems):
    x, y, c = lax.axis_index("x"), lax.axis_index("y"), lax.axis_index("c")
    me = 4 * x + 2 * y + c
    locs, first, passed = [], [], []
    for k, (si, l, bi) in enumerate(plan):
        src = src_refs[si].at[l] if mode == "gather" else src_refs[si].at[me, l]
        locs.append(pltpu.make_async_copy(src, buf_refs[bi].at[me, l], loc_sems.at[k]))

    def remote(k, r, src, slot, l, bi, to):
        return pltpu.make_async_remote_copy(
            src_ref=src, dst_ref=buf_refs[bi].at[slot, l],
            send_sem=send_sems.at[k, r], recv_sem=recv_sems.at[k, r],
            device_id=to, device_id_type=MESH)

    if mode == "scatter":
        for r in range(1, N_DEV):
            px = (1 - x) if (r >> 2) & 1 else x
            py = (1 - y) if (r >> 1) & 1 else y
            pc = (1 - c) if r & 1 else c
            pid = 4 * px + 2 * py + pc
            for k, (si, l, bi) in enumerate(plan):
                first.append(remote(k, r - 1, src_refs[si].at[pid, l], me, l, bi, (px, py, pc)))
        return locs, first, passed
    sibling = (x, y, 1 - c)
    for k, (si, l, bi) in enumerate(plan):
        first.append(remote(k, 0, src_refs[si].at[l], me, l, bi, sibling))
    for r, (qx, qy) in enumerate([(1 - x, y), (x, 1 - y), (1 - x, 1 - y)]):
        qid = 4 * qx + 2 * qy + c
        for k, (si, l, bi) in enumerate(plan):
            first.append(remote(k, 1 + r, src_refs[si].at[l], me, l, bi, (qx, qy, c)))
            passed.append(remote(k, 4 + r, buf_refs[bi].at[qid, l], qid, l, bi, sibling))
    return locs, first, passed


def _call(body, *, name, grid, operands, in_specs, out_shape, out_specs, scratch=(), sem=None,
          aliases=None, comm=None):
    aliases = dict(aliases or {})
    if comm is None:
        return pl.pallas_call(
            body, name=name, grid=grid, out_shape=list(out_shape), in_specs=list(in_specs),
            out_specs=list(out_specs), scratch_shapes=list(scratch), input_output_aliases=aliases,
            compiler_params=_cp(sem),
        )(*operands)
    mode, pieces = comm
    srcs = _distinct([p[0] for p in pieces])
    bufs = _distinct([p[2] for p in pieces])
    plan = [(_index_of(srcs, s), l, _index_of(bufs, b)) for (s, l, b) in pieces]
    n_in, n_out, n_scr, ns, nb, npc = len(operands), len(out_shape), len(scratch), len(srcs), len(bufs), len(plan)
    nsteps = grid[0]
    old = [b for b in bufs if not isinstance(b, jax.ShapeDtypeStruct)]
    nbi = len(old)

    def wrapped(*refs):
        ins = refs[:n_in]
        src_refs = refs[n_in:n_in + ns]
        o0 = n_in + ns + nbi
        outs = refs[o0:o0 + n_out]
        buf_refs = refs[o0 + n_out:o0 + n_out + nb]
        s0 = o0 + n_out + nb
        scr = refs[s0:s0 + n_scr]
        sems = refs[s0 + n_scr:]
        step = pl.program_id(0)

        @pl.when(step == 0)
        def _():
            locs, first, _ = _comm_copies(mode, plan, src_refs, buf_refs, *sems)
            for d in locs + first:
                d.start()

        body(*ins, *outs, *scr)

        @pl.when(step == (3 * nsteps) // 4)
        def _():
            _, first, passed = _comm_copies(mode, plan, src_refs, buf_refs, *sems)
            for d in first[len(first) - len(passed):] if passed else []:
                d.wait_recv()
            for d in passed:
                d.start()

        @pl.when(step == nsteps - 1)
        def _():
            locs, first, passed = _comm_copies(mode, plan, src_refs, buf_refs, *sems)
            for d in first + passed:
                d.wait_send()
            for d in first[:len(first) - len(passed)] + passed:
                d.wait_recv()
            for d in locs:
                d.wait()

    anyspec = pl.BlockSpec(memory_space=pl.ANY)
    for k, b in enumerate(old):
        aliases[n_in + ns + k] = n_out + _index_of(bufs, b)
    res = pl.pallas_call(
        wrapped, name=name, grid=grid,
        out_shape=list(out_shape) + [jax.ShapeDtypeStruct(b.shape, b.dtype) for b in bufs],
        in_specs=list(in_specs) + [anyspec] * (ns + nbi),
        out_specs=list(out_specs) + [anyspec] * nb,
        scratch_shapes=list(scratch) + [pltpu.SemaphoreType.DMA((npc, N_DEV - 1)),
                                        pltpu.SemaphoreType.DMA((npc, N_DEV - 1)),
                                        pltpu.SemaphoreType.DMA((npc,))],
        input_output_aliases=aliases,
        compiler_params=pltpu.CompilerParams(vmem_limit_bytes=VMEM_LIMIT, has_side_effects=True,
                                             dimension_semantics=("arbitrary",) * len(grid)),
    )(*operands, *srcs, *old)
    return list(res[:n_out]), list(res[n_out:])


def _exchange(comm, name):
    def body():
        pass

    return _call(body, name=name, grid=(1,), operands=[], in_specs=[], out_shape=[], out_specs=[],
                 comm=comm)[1]


def _mm(a, w, l, kind, *, out_dtype, name, tm=512, a_fn=None, epi=None, extras=(), comm=None,
        outs=None, sums=()):
    T, Ka = a.shape
    _, _, d2, d3 = w.shape
    n_out = {"nn_col": N_DEV * d3, "nn_row": d3, "nt_col": d2, "nt_row": N_DEV * d2}[kind]
    tm = min(tm, T)
    ne = len(extras)
    single = outs is None and not sums
    outs = [(n_out, out_dtype)] if outs is None else outs
    no, ns = len(outs), len(sums)

    def body(*refs):
        a_ref, w_ref = refs[0], refs[1]
        e_refs = refs[2:2 + ne]
        o_refs = refs[2 + ne:2 + ne + no]
        s_refs = refs[2 + ne + no:]
        av = a_ref[...]
        if a_fn is not None:
            av = a_fn(av)
        av = av.astype(BF)
        if kind == "nn_col":
            acc = jnp.concatenate([_dot(av, w_ref[j]) for j in range(N_DEV)], axis=1)
        elif kind == "nn_row":
            acc = _dot(av, w_ref[...].reshape(N_DEV * d2, d3))
        elif kind == "nt_col":
            acc = _dot_nt(av[:, 0:d3], w_ref[0])
            for j in range(1, N_DEV):
                acc = acc + _dot_nt(av[:, j * d3:(j + 1) * d3], w_ref[j])
        else:
            acc = jnp.concatenate([_dot_nt(av, w_ref[j]) for j in range(N_DEV)], axis=1)
        res = acc if epi is None else epi(acc, *[e[...] for e in e_refs])
        res = res if isinstance(res, tuple) else (res,)
        for o_ref, val in zip(o_refs, res[:no]):
            o_ref[...] = val.astype(o_ref.dtype)
        if ns:
            first = pl.program_id(0) == 0

            @pl.when(first)
            def _():
                for s_ref, val in zip(s_refs, res[no:]):
                    s_ref[...] = val

            @pl.when(jnp.logical_not(first))
            def _():
                for s_ref, val in zip(s_refs, res[no:]):
                    s_ref[...] += val

    in_specs = [pl.BlockSpec((tm, Ka), lambda i: (i, 0)),
                pl.BlockSpec((N_DEV, None, d2, d3), lambda i: (0, l, 0, 0))]
    extras = [e if isinstance(e, tuple) else (e, 0) for e in extras]
    for e, cb in extras:
        if e.shape[0] == 1:
            in_specs.append(pl.BlockSpec((1, n_out), lambda i: (0, 0)))
        else:
            in_specs.append(pl.BlockSpec((tm, n_out), functools.partial(lambda i, cb: (i, cb), cb=cb)))
    extras = [e for e, _ in extras]
    out_shape = [jax.ShapeDtypeStruct((T, wd), dt) for wd, dt in outs]
    out_shape += [jax.ShapeDtypeStruct((1, wd), F32) for wd in sums]
    out_specs = [pl.BlockSpec((tm, wd), lambda i: (i, 0)) for wd, _ in outs]
    out_specs += [pl.BlockSpec((1, wd), lambda i: (0, 0)) for wd in sums]
    res = _call(body, name=name, grid=(T // tm,), operands=[a, w, *extras], in_specs=in_specs,
                out_shape=out_shape, out_specs=out_specs,
                sem=("arbitrary",) if ns else ("parallel",), comm=comm)
    if comm is None:
        return res[0] if single else res
    return (res[0][0] if single else res[0]), res[1]


def _mm_tn(g, s, b, l, kind, *, name, tt=1024, b_fn=None):
    T, ws = s.shape
    wb = b.shape[1]
    _, _, d2, d3 = g.shape
    tt = min(tt, T)
    nsteps = T // tt
    cw = wb // N_DEV

    fresh = isinstance(g, jax.ShapeDtypeStruct)

    def body(*refs):
        s_ref, b_ref, o_ref, acc, stage, sem = refs if fresh else refs[1:]
        t = pl.program_id(0)

        @pl.when(t == 0)
        def _():
            acc[...] = jnp.zeros((ws, wb), F32)

        st = s_ref[...].astype(F32).T.astype(BF)
        for j in range(N_DEV):
            bv = b_ref[:, j * cw:(j + 1) * cw]
            if b_fn is not None:
                bv = b_fn(bv)
            acc[:, j * cw:(j + 1) * cw] += _dot(st, bv.astype(BF))

        @pl.when(t == nsteps - 1)
        def _():
            for j in range(N_DEV):
                if kind == "cols":
                    blk = acc[:, j * d3:(j + 1) * d3]
                elif kind == "rows":
                    blk = acc[j * d2:(j + 1) * d2, :]
                else:
                    blk = acc[:, j * d2:(j + 1) * d2].T
                stage[...] = blk.astype(BF)
                cp = pltpu.make_async_copy(stage, o_ref.at[j, l], sem)
                cp.start()
                cp.wait()

    anyspec = pl.BlockSpec(memory_space=pl.ANY)
    return pl.pallas_call(
        body, name=name, grid=(nsteps,),
        out_shape=jax.ShapeDtypeStruct(g.shape, BF),
        in_specs=([] if fresh else [anyspec]) + [pl.BlockSpec((tt, ws), lambda t: (t, 0)),
                                                 pl.BlockSpec((tt, wb), lambda t: (t, 0))],
        out_specs=anyspec,
        scratch_shapes=[pltpu.VMEM((ws, wb), F32), pltpu.VMEM((d2, d3), BF), pltpu.SemaphoreType.DMA],
        input_output_aliases={} if fresh else {0: 0},
        compiler_params=_cp(("arbitrary",)),
    )(*([] if fresh else [g]), s, b)


def _ln_stats(z):
    mu = jnp.mean(z, axis=-1, keepdims=True)
    zc = z - mu
    var = jnp.mean(zc * zc, axis=-1, keepdims=True)
    rstd = lax.rsqrt(var + LN_EPS)
    return zc * rstd, rstd


def _ln_bwd_core(dy, zh, rstd, g):
    dzh = dy * g
    m1 = jnp.mean(dzh, axis=-1, keepdims=True)
    m2 = jnp.mean(dzh * zh, axis=-1, keepdims=True)
    return rstd * (dzh - m1 - zh * m2)


def _colsum(v):
    return jnp.sum(v, axis=0, keepdims=True)


def _z_and_ln(z, g, b):
    zh, _ = _ln_stats(z)
    return z, zh * g + b


def _ln_bwd_tile(dy, z, g):
    zh, rstd = _ln_stats(z)
    dz = _ln_bwd_core(dy, zh, rstd, g)
    return dz, _colsum(dy * zh), _colsum(dy), _colsum(dz)


def _heads(D):
    return D // HEAD


def _rms_parts(o, D):
    xs, rs = [], []
    for h in range(_heads(D)):
        oh = o[:, h * HEAD:(h + 1) * HEAD]
        r = lax.rsqrt(jnp.mean(oh * oh, axis=-1, keepdims=True) + RMS_EPS)
        xs.append(oh * r)
        rs.append(r)
    return xs, rs


def _gate_bwd_tile(dyt, ot, grt, ngr):
    D = ot.shape[1]
    xs, rs = _rms_parts(ot, D)
    xh = jnp.concatenate(xs, axis=1)
    sg = _sig(grt)
    on = xh * ngr
    dgr = dyt * on * (sg * (1.0 + grt * (1.0 - sg)))
    don = dyt * (grt * sg)
    dxh = don * ngr
    dos = []
    for h in range(_heads(D)):
        sl = slice(h * HEAD, (h + 1) * HEAD)
        m = jnp.mean(dxh[:, sl] * xs[h], axis=-1, keepdims=True)
        dos.append(rs[h] * (dxh[:, sl] - xs[h] * m))
    return jnp.concatenate(dos, axis=1), dgr, _colsum(don * xh)


def _u_and_glu(u):
    D = u.shape[1] // 2
    return u, u[:, :D] * _sig(u[:, D:])


def _silu_ln_bwd_tile(dst, ct, gr, br):
    zh, rstd = _ln_stats(ct)
    ln = zh * gr + br
    sg = _sig(ln)
    dln = dst * (sg * (1.0 + ln * (1.0 - sg)))
    dc = _ln_bwd_core(dln, zh, rstd, gr)
    return dc, _colsum(dln * zh), _colsum(dln), _colsum(dc)


def _top_tile(z, g, b, tgt):
    D = z.shape[1]
    zh, rstd = _ln_stats(z)
    e = zh * g + b - tgt
    dy = e * (1.0 / D)
    dz = _ln_bwd_core(dy, zh, rstd, g)
    return dz, _colsum(dy * zh), _colsum(dy), _colsum(e * e) * (0.5 / D)


def _shifted(ext, n, tr):
    for b in range(8):
        rb = ext if b == 0 else pltpu.roll(ext, n - b, 0)
        for a in range(HALO // 8 + 1):
            o = 8 * a + b
            if o <= HALO:
                yield o, rb[8 * a:8 * a + tr, :]


def _lane_strips(D):
    return [slice(c, c + 128) for c in range(0, D, 128)]


def _conv_fwd(glu, w, bdw, g, b, name, tr=256, comm=None):
    T, D = glu.shape
    tr = min(tr, T // 2)
    rt = min(CONV_ROWS, tr)
    hb = tr // HALO

    def body(cur_ref, halo_ref, w_ref, bdw_ref, g_ref, b_ref, c_ref, s_ref):
        i = pl.program_id(0)
        for cs in _lane_strips(D):
            halo = jnp.where(i > 0, halo_ref[:, cs], 0.0)
            for r0 in range(0, tr, rt):
                if r0 == 0:
                    ext = jnp.concatenate([halo, cur_ref[0:rt, cs]], axis=0)
                else:
                    ext = cur_ref[r0 - HALO:r0 + rt, cs]
                acc = None
                for o, sh in _shifted(ext, rt + HALO, rt):
                    k = o - (HALO - CONV_W + 1)
                    if 0 <= k < CONV_W:
                        term = w_ref[k:k + 1, cs] * sh
                        acc = term if acc is None else acc + term
                c_ref[r0:r0 + rt, cs] = acc + bdw_ref[:, cs]
        cv = c_ref[...]
        zh, _ = _ln_stats(cv)
        ln = zh * g_ref[...] + b_ref[...]
        s_ref[...] = (ln * _sig(ln)).astype(BF)

    row = pl.BlockSpec((1, D), lambda i: (0, 0))
    return _call(
        body, name=name, grid=(T // tr,), operands=[glu, glu, w, bdw, g, b],
        out_shape=[jax.ShapeDtypeStruct((T, D), F32), jax.ShapeDtypeStruct((T, D), BF)],
        in_specs=[pl.BlockSpec((tr, D), lambda i: (i, 0)),
                  pl.BlockSpec((HALO, D), lambda i: (jnp.maximum(i * hb - 1, 0), 0)),
                  pl.BlockSpec((HALO, D), lambda i: (0, 0)), row, row, row],
        out_specs=[pl.BlockSpec((tr, D), lambda i: (i, 0))] * 2,
        sem=("parallel",), comm=comm)


def _conv_bwd(dc, glu, u, w, name, tr=256, comm=None):
    T, D = glu.shape
    tr = min(tr, T // 2)
    rt = min(CONV_ROWS, tr)
    hb = tr // HALO
    nsteps = T // tr
    last_hb = T // HALO - 1

    def body(dc_ref, nxt_ref, glu_ref, prv_ref, ua_ref, ug_ref, w_ref, du_ref, dw_ref, dbu_ref):
        i = pl.program_id(0)

        @pl.when(i == 0)
        def _():
            dw_ref[...] = jnp.zeros((HALO, D), F32)
            dbu_ref[...] = jnp.zeros((1, 2 * D), F32)

        for cs in _lane_strips(D):
            nxt = jnp.where(i < nsteps - 1, nxt_ref[:, cs], 0.0)
            prv = jnp.where(i > 0, prv_ref[:, cs], 0.0)
            dws = [None] * CONV_W
            dba = dbg = None
            gs = slice(D + cs.start, D + cs.stop)
            for r0 in range(0, tr, rt):
                if r0 + rt == tr:
                    ext2 = jnp.concatenate([dc_ref[r0:tr, cs], nxt], axis=0)
                else:
                    ext2 = dc_ref[r0:r0 + rt + HALO, cs]
                acc = None
                for o, sh in _shifted(ext2, rt + HALO, rt):
                    k = CONV_W - 1 - o
                    if 0 <= k < CONV_W:
                        term = w_ref[k:k + 1, cs] * sh
                        acc = term if acc is None else acc + term
                sg = _sig(ug_ref[r0:r0 + rt, cs])
                da = acc * sg
                dgt = acc * ua_ref[r0:r0 + rt, cs] * sg * (1.0 - sg)
                du_ref[r0:r0 + rt, cs] = da.astype(BF)
                du_ref[r0:r0 + rt, gs] = dgt.astype(BF)
                dba = _colsum(da) if dba is None else dba + _colsum(da)
                dbg = _colsum(dgt) if dbg is None else dbg + _colsum(dgt)
                if r0 == 0:
                    ext = jnp.concatenate([prv, glu_ref[0:rt, cs]], axis=0)
                else:
                    ext = glu_ref[r0 - HALO:r0 + rt, cs]
                dcs = dc_ref[r0:r0 + rt, cs]
                for o, sh in _shifted(ext, rt + HALO, rt):
                    k = o - (HALO - CONV_W + 1)
                    if 0 <= k < CONV_W:
                        part = _colsum(dcs * sh)
                        dws[k] = part if dws[k] is None else dws[k] + part
            for k in range(CONV_W):
                dw_ref[k:k + 1, cs] += dws[k]
            dbu_ref[:, cs] += dba
            dbu_ref[:, gs] += dbg

    return _call(
        body, name=name, grid=(nsteps,), operands=[dc, dc, glu, glu, u, u, w],
        out_shape=[jax.ShapeDtypeStruct((T, 2 * D), BF), jax.ShapeDtypeStruct((HALO, D), F32),
                   jax.ShapeDtypeStruct((1, 2 * D), F32)],
        in_specs=[pl.BlockSpec((tr, D), lambda i: (i, 0)),
                  pl.BlockSpec((HALO, D), lambda i: (jnp.minimum((i + 1) * hb, last_hb), 0)),
                  pl.BlockSpec((tr, D), lambda i: (i, 0)),
                  pl.BlockSpec((HALO, D), lambda i: (jnp.maximum(i * hb - 1, 0), 0)),
                  pl.BlockSpec((tr, D), lambda i: (i, 0)),
                  pl.BlockSpec((tr, D), lambda i: (i, 1)),
                  pl.BlockSpec((HALO, D), lambda i: (0, 0))],
        out_specs=[pl.BlockSpec((tr, 2 * D), lambda i: (i, 0)),
                   pl.BlockSpec((HALO, D), lambda i: (0, 0)),
                   pl.BlockSpec((1, 2 * D), lambda i: (0, 0))],
        sem=("arbitrary",), comm=comm)


def _rec_consts(C):
    t = np.arange(C)
    lb = (t[None, :] <= t[:, None]).astype(np.float32)

    def cum_at(idx):
        return (t[None, :] <= idx[:, None]).astype(np.float32)

    blocks, masks = [], []
    mid = SUB * (t // SUB) + SUB // 2 - 1
    eq = lb - cum_at(mid)
    blocks += [eq, -eq]
    masks.append(((t[:, None] // SUB) == (t[None, :] // SUB)) & (t[None, :] <= t[:, None]))
    nb = SUB
    while nb < C:
        odd = (t // nb) % 2 == 1
        e_t = nb * (t // nb) - 1
        e_s = nb * (t // nb) + nb - 1
        blocks.append(np.where(odd[:, None], lb - cum_at(e_t), 0.0))
        blocks.append(np.where(~odd[:, None], cum_at(e_s) - lb, 0.0))
        masks.append(((t[:, None] // (2 * nb)) == (t[None, :] // (2 * nb))) & odd[:, None] & ~odd[None, :])
        nb *= 2
    blocks += [lb, 1.0 - lb, np.ones((HEAD, C), np.float32)]
    L = np.concatenate(blocks, axis=0).astype(np.float32)
    L3 = np.concatenate([L, L, L], axis=1)
    LT3 = np.concatenate([L.T, L.T], axis=1)
    m = np.stack(masks).astype(np.float32)
    mT = np.transpose(m, (0, 2, 1)).copy()
    return (jnp.asarray(L3, BF), jnp.asarray(LT3, BF), jnp.asarray(m), jnp.asarray(mT), len(masks))


def _split3(x):
    h = x.astype(BF)
    r = x - h.astype(F32)
    m = r.astype(BF)
    lo = (r - m.astype(F32)).astype(BF)
    return h, m, lo


def _split2(x):
    h = x.astype(BF)
    return h, (x - h.astype(F32)).astype(BF)


def _gates(qr, fz, lbr):
    sq = _sig(qr)
    q = qr * sq
    sg = _sig(fz)
    f = lbr + (1.0 - lbr) * sg
    fc = jnp.maximum(f, GATE_EPS)
    return q, 1.0 - f, jnp.log(fc), sq, sg, f, fc


def _rec_fwd(proj, lbr, ng, name, comm=None):
    T, D4 = proj.shape
    D = D4 // 4
    H = _heads(D)
    C = CHUNK
    nC = T // C
    L3, _, m, _, nl = _rec_consts(C)
    R = L3.shape[0]

    def body(q_ref, f_ref, v_ref, gr_ref, lb_ref, ng_ref, l_ref, m_ref, o_ref, y_ref, s_ref, st):
        @pl.when(pl.program_id(0) == 0)
        def _():
            st[...] = jnp.zeros((H, HEAD, HEAD), F32)

        q, k, logf = _gates(q_ref[...], f_ref[...], lb_ref[...])[:3]
        ex = jnp.exp(_dot(l_ref[...], jnp.concatenate(_split3(logf), axis=0)))
        vb = v_ref[...].astype(BF)
        s_ref[0] = st[...]
        outs = []
        for h in range(H):
            sl = slice(h * HEAD, (h + 1) * HEAD)
            qh, kh = q[:, sl], k[:, sl]
            p = jnp.zeros((C, C), F32)
            for lv in range(nl):
                qt = (qh * ex[2 * lv * C:(2 * lv + 1) * C, sl]).astype(BF)
                kt = (kh * ex[(2 * lv + 1) * C:(2 * lv + 2) * C, sl]).astype(BF)
                p = p + jnp.where(m_ref[lv] > 0.0, _dot_nt(qt, kt), 0.0)
            base = 2 * nl * C
            qhat = (qh * ex[base:base + C, sl]).astype(BF)
            khat = (kh * ex[base + C:base + 2 * C, sl]).astype(BF)
            elast = ex[base + 2 * C:base + 2 * C + HEAD, sl]
            sth = st[h]
            outs.append(_dot(p.astype(BF), vb[:, sl]) + _dot_nt(qhat, sth.astype(BF)))
            st[h] = elast * sth + _dot_tn(vb[:, sl], khat)
        o_ref[...] = jnp.concatenate(outs, axis=1)
        xh = jnp.concatenate([oh * lax.rsqrt(jnp.mean(oh * oh, axis=-1, keepdims=True) + RMS_EPS) for oh in outs],
                             axis=1)
        gr = gr_ref[...]
        y_ref[...] = (xh * ng_ref[...] * (gr * _sig(gr))).astype(BF)

    def cblk(cb):
        return pl.BlockSpec((C, D), lambda i: (i, cb))

    return _call(
        body, name=name, grid=(nC,), operands=[proj, proj, proj, proj, lbr, ng, L3, m],
        out_shape=[jax.ShapeDtypeStruct((T, D), F32), jax.ShapeDtypeStruct((T, D), BF),
                   jax.ShapeDtypeStruct((nC, H, HEAD, HEAD), F32)],
        in_specs=[cblk(0), cblk(1), cblk(2), cblk(3), pl.BlockSpec((1, D), lambda i: (0, 0)),
                  pl.BlockSpec((1, D), lambda i: (0, 0)),
                  pl.BlockSpec(L3.shape, lambda i: (0, 0)), pl.BlockSpec(m.shape, lambda i: (0, 0, 0))],
        out_specs=[pl.BlockSpec((C, D), lambda i: (i, 0)), pl.BlockSpec((C, D), lambda i: (i, 0)),
                   pl.BlockSpec((1, H, HEAD, HEAD), lambda i: (i, 0, 0, 0))],
        scratch=[pltpu.VMEM((H, HEAD, HEAD), F32)], sem=("arbitrary",), comm=comm)


def _rec_bwd(proj, lbr, do, dgr, ssave, name, comm=None):
    T, D4 = proj.shape
    D = D4 // 4
    H = _heads(D)
    C = min(CHUNK_BWD, T // 2)
    nC = T // C
    stride = C // CHUNK
    L3, LT3, m, mT, nl = _rec_consts(C)

    def body(q_ref, f_ref, v_ref, lb_ref, do_ref, dgr_ref, s_ref, l_ref, lt_ref, m_ref, mt_ref,
             dp_ref, dlb_ref, dst):
        @pl.when(pl.program_id(0) == 0)
        def _():
            dst[...] = jnp.zeros((H, HEAD, HEAD), F32)
            dlb_ref[...] = jnp.zeros((1, D), F32)

        qr = q_ref[...]
        lbv = lb_ref[...]
        q, k, logf, sq, sg, f, fc = _gates(qr, f_ref[...], lbv)
        ex = jnp.exp(_dot(l_ref[...], jnp.concatenate(_split3(logf), axis=0)))
        vb = v_ref[...].astype(BF)
        dob = do_ref[...].astype(BF)
        base = 2 * nl * C
        de = [[] for _ in range(2 * nl + 3)]
        dqs, dks, dvs = [], [], []
        for h in range(H):
            sl = slice(h * HEAD, (h + 1) * HEAD)
            qh, kh, vh, doh = q[:, sl], k[:, sl], vb[:, sl], dob[:, sl]
            dp = _dot_nt(doh, vh)
            dpt = _dot_nt(vh, doh)
            sth = s_ref[0, h]
            dsth = dst[h]
            dsb = dsth.astype(BF)
            pt = jnp.zeros((C, C), F32)
            dq = jnp.zeros((C, HEAD), F32)
            dk = jnp.zeros((C, HEAD), F32)
            for lv in range(nl):
                exq = ex[2 * lv * C:(2 * lv + 1) * C, sl]
                exk = ex[(2 * lv + 1) * C:(2 * lv + 2) * C, sl]
                qt = qh * exq
                kt = kh * exk
                qtb, ktb = qt.astype(BF), kt.astype(BF)
                pt = pt + jnp.where(mt_ref[lv] > 0.0, _dot_nt(ktb, qtb), 0.0)
                dqt = _dot(jnp.where(m_ref[lv] > 0.0, dp, 0.0).astype(BF), ktb)
                dkt = _dot(jnp.where(mt_ref[lv] > 0.0, dpt, 0.0).astype(BF), qtb)
                dq = dq + dqt * exq
                dk = dk + dkt * exk
                de[2 * lv].append(dqt * qt)
                de[2 * lv + 1].append(dkt * kt)
            exb = ex[base:base + C, sl]
            exkh = ex[base + C:base + 2 * C, sl]
            elast = ex[base + 2 * C:base + 2 * C + HEAD, sl]
            qhat = qh * exb
            khat = kh * exkh
            dqh = _dot(doh, sth.astype(BF))
            dkh = _dot(vh, dsb)
            dq = dq + dqh * exb
            dk = dk + dkh * exkh
            de[2 * nl].append(dqh * qhat)
            de[2 * nl + 1].append(dkh * khat)
            de[2 * nl + 2].append(dsth * sth * elast)
            dvs.append(_dot(pt.astype(BF), doh) + _dot_nt(khat.astype(BF), dsb))
            dst[h] = elast * dsth + _dot_tn(doh, qhat.astype(BF))
            dqs.append(dq)
            dks.append(dk)
        de_all = jnp.concatenate([jnp.concatenate(b, axis=1) for b in de], axis=0)
        dlogf = _dot(lt_ref[...], jnp.concatenate(_split2(de_all), axis=0))
        dq = jnp.concatenate(dqs, axis=1)
        dk = jnp.concatenate(dks, axis=1)
        dv = jnp.concatenate(dvs, axis=1)
        ind = jnp.where(f > GATE_EPS, 1.0, jnp.where(f == GATE_EPS, 0.5, 0.0))
        df = dlogf * ind / fc - dk
        dfz = df * (1.0 - lbv) * sg * (1.0 - sg)
        dlb_ref[...] += _colsum(df * (1.0 - sg))
        dqr = dq * (sq * (1.0 + qr * (1.0 - sq)))
        dp_ref[...] = jnp.concatenate([dqr.astype(BF), dfz.astype(BF), dv.astype(BF), dgr_ref[...]], axis=1)

    def cblk(cb):
        return pl.BlockSpec((C, D), lambda i: (nC - 1 - i, cb))

    def whole(a):
        nd = a.ndim
        return pl.BlockSpec(a.shape, lambda i: (0,) * nd)

    return _call(
        body, name=name, grid=(nC,), operands=[proj, proj, proj, lbr, do, dgr, ssave, L3, LT3, m, mT],
        out_shape=[jax.ShapeDtypeStruct((T, D4), BF), jax.ShapeDtypeStruct((1, D), F32)],
        in_specs=[cblk(0), cblk(1), cblk(2), pl.BlockSpec((1, D), lambda i: (0, 0)),
                  pl.BlockSpec((C, D), lambda i: (nC - 1 - i, 0)),
                  pl.BlockSpec((C, D), lambda i: (nC - 1 - i, 0)),
                  pl.BlockSpec((1, H, HEAD, HEAD), lambda i: (stride * (nC - 1 - i), 0, 0, 0)),
                  whole(L3), whole(LT3), whole(m), whole(mT)],
        out_specs=[pl.BlockSpec((C, D4), lambda i: (nC - 1 - i, 0)),
                   pl.BlockSpec((1, D), lambda i: (0, 0))],
        scratch=[pltpu.VMEM((H, HEAD, HEAD), F32)], sem=("arbitrary",), comm=comm)


def _softmax_rows(lg_ref):
    n = lg_ref.shape[0]
    rows = [lg_ref[l:l + 1, :] for l in range(n)]
    mx = rows[0]
    for r in rows[1:]:
        mx = jnp.maximum(mx, r)
    es = [jnp.exp(r - mx) for r in rows]
    tot = es[0]
    for e in es[1:]:
        tot = tot + e
    return [e / tot for e in es]


def _lb_fwd(logits):
    n, D = logits.shape

    def body(lg_ref, o_ref):
        soft = _softmax_rows(lg_ref)
        acc = jnp.zeros((1, D), F32)
        o_ref[0:1, :] = acc
        for j in range(1, n):
            acc = acc + soft[j]
            o_ref[j:j + 1, :] = acc

    return pl.pallas_call(body, name="lb_fwd", out_shape=jax.ShapeDtypeStruct((n, D), F32))(logits)


def _small_reduce(parts, logits, lb_row0):
    _, R, D = parts.shape
    n = logits.shape[0]

    def body(p_ref, lg_ref, o_ref):
        acc = p_ref[0]
        for d in range(1, N_DEV):
            acc = acc + p_ref[d]
        o_ref[...] = acc
        soft = _softmax_rows(lg_ref)
        dlb = [o_ref[lb_row0 + j:lb_row0 + j + 1, :] for j in range(n)]
        dsoft = [jnp.zeros((1, D), F32)]
        for l in range(1, n):
            s = dlb[l]
            for j in range(l + 1, n):
                s = s + dlb[j]
            dsoft.append(s)
        dot = soft[0] * dsoft[0]
        for l in range(1, n):
            dot = dot + soft[l] * dsoft[l]
        for l in range(n):
            o_ref[lb_row0 + l:lb_row0 + l + 1, :] = soft[l] * (dsoft[l] - dot)

    return pl.pallas_call(body, name="small_reduce", out_shape=jax.ShapeDtypeStruct((R, D), F32))(parts, logits)


def _adam_math(g, w, m, v):
    m2 = ADAM_B1 * m + (1.0 - ADAM_B1) * g
    v2 = ADAM_B2 * v + (1.0 - ADAM_B2) * (g * g)
    mh = m2 / (1.0 - ADAM_B1 ** ADAM_STEP)
    vh = v2 / (1.0 - ADAM_B2 ** ADAM_STEP)
    delta = -ADAM_LR * (mh / (jnp.sqrt(vh) + ADAM_EPS) + ADAM_WD * w)
    return delta, m2, v2


def _adamw(recv, w, m, v, name):
    nsrc = recv.shape[0]
    shp = w.shape
    cols = shp[-1]
    rows = int(np.prod(shp[:-1]))
    r2 = recv.reshape(nsrc, rows, cols)
    tr = min(rows, max(8, (1 << 20) // (cols * nsrc)))
    while rows % tr:
        tr //= 2

    def body(r_ref, w_ref, m_ref, v_ref, g_ref, d_ref, nm_ref, nv_ref):
        g = r_ref[0].astype(F32)
        for s in range(1, nsrc):
            g = g + r_ref[s].astype(F32)
        delta, m2, v2 = _adam_math(g, w_ref[...], m_ref[...], v_ref[...])
        g_ref[...] = g
        d_ref[...] = delta
        nm_ref[...] = m2
        nv_ref[...] = v2

    blk = pl.BlockSpec((tr, cols), lambda i: (i, 0))
    outs = pl.pallas_call(
        body, name=name, grid=(rows // tr,),
        out_shape=[jax.ShapeDtypeStruct((rows, cols), F32)] * 4,
        in_specs=[pl.BlockSpec((nsrc, tr, cols), lambda i: (0, i, 0)), blk, blk, blk],
        out_specs=[blk] * 4,
        compiler_params=_cp(("parallel",)),
    )(r2, w.reshape(rows, cols), m.reshape(rows, cols), v.reshape(rows, cols))
    return [o.reshape(shp) for o in outs]


def _pack(arrs, lane):
    flat = jnp.concatenate([a.reshape(-1) for a in arrs])
    n = flat.shape[0]
    rows = -(-n // lane)
    rows = -(-rows // 8) * 8
    flat = jnp.pad(flat, (0, rows * lane - n))
    return flat.reshape(rows, lane)


def _unpack(packed, shapes):
    flat = packed.reshape(-1)
    out, off = [], 0
    for s in shapes:
        n = int(np.prod(s))
        out.append(flat[off:off + n].reshape(s))
        off += n
    return out


def kernel(x, ln_mix_g, ln_mix_b, ln_ffn_g, ln_ffn_b, ffn_w1, ffn_w2, a_w_in, a_lb_logits, a_norm_g, a_w_out, b_w_pw1, b_b_pw1, b_w_dw, b_b_dw, b_ln_g, b_ln_b, b_w_pw2, b_b_pw2, loss_target, m_ln_mix_g, m_ln_mix_b, m_ln_ffn_g, m_ln_ffn_b, m_ffn_w1, m_ffn_w2, m_a_w_in, m_a_lb_logits, m_a_norm_g, m_a_w_out, m_b_w_pw1, m_b_b_pw1, m_b_w_dw, m_b_b_dw, m_b_ln_g, m_b_ln_b, m_b_w_pw2, m_b_b_pw2, v_ln_mix_g, v_ln_mix_b, v_ln_ffn_g, v_ln_ffn_b, v_ffn_w1, v_ffn_w2, v_a_w_in, v_a_lb_logits, v_a_norm_g, v_a_w_out, v_b_w_pw1, v_b_b_pw1, v_b_w_dw, v_b_b_dw, v_b_ln_g, v_b_ln_b, v_b_w_pw2, v_b_b_pw2):
    T, D = x.shape[1], x.shape[2]
    nA, nB = a_w_in.shape[0], b_w_pw1.shape[0]
    me = 4 * lax.axis_index("x") + 2 * lax.axis_index("y") + lax.axis_index("c")
    xin = x[0]
    tgt = loss_target[0]

    small_names = [b_b_pw1, b_w_dw, b_b_dw, b_ln_g, b_ln_b, b_b_pw2]
    sp = _pack(small_names, 128)
    wsrc = {"W1": ffn_w1.astype(BF), "W2": ffn_w2.astype(BF), "Win": a_w_in.astype(BF),
            "Wout": a_w_out.astype(BF), "Wp1": b_w_pw1.astype(BF), "Wp2": b_w_pw2.astype(BF), "small": sp[None]}
    W = {k: jax.ShapeDtypeStruct((N_DEV,) + v.shape, v.dtype) for k, v in wsrc.items()}
    GW = {k: jax.ShapeDtypeStruct((N_DEV,) + v.shape, BF) for k, v in wsrc.items() if k != "small"}
    R = {k: jax.ShapeDtypeStruct(v.shape, BF) for k, v in GW.items()}

    def names_of(items):
        out = []
        for n, _ in items:
            if n not in out:
                out.append(n)
        return out

    def hosted(fn, key, sched, mode, src, bufs, *args, **kw):
        items = sched.get(key)
        if items is None:
            return fn(*args, name=key, **kw)
        comm = (mode, [(src[n], l, bufs[n]) for n, l in items])
        outs, new = fn(*args, name=key, comm=comm, **kw)
        for n, b in zip(names_of(items), new):
            bufs[n] = b
        return outs

    first = [("Win", 0), ("small", 0)]
    for n, b in zip(names_of(first), _exchange(("gather", [(wsrc[n], l, W[n]) for n, l in first]), "gather_first")):
        W[n] = b
    fwd_sched = {"a0_proj": [("Wout", 0), ("Wp1", 0)],
                 "a0_rec_fwd": [("W1", 0), ("W2", 0), ("Wp2", 0), ("W1", 1), ("W2", 1)],
                 "b0_conv": [("Win", 1), ("Wout", 1), ("W2", 2)], "l1_ffn_down": [("W1", 2)],
                 "a1_rec_fwd": [("Wp1", 1), ("Wp2", 1), ("W1", 3), ("W2", 3)]}
    bwd_sched = {"a1_rec_bwd": [("W1", 3), ("W2", 3), ("Wp1", 1), ("Wp2", 1), ("W1", 2), ("W2", 2)],
                 "b0_conv_bwd": [("Win", 1), ("Wout", 1)],
                 "a0_rec_bwd": [("W1", 1), ("W2", 1), ("Wp1", 0), ("Wp2", 0), ("W1", 0), ("W2", 0), ("Wout", 0)],
                 "a0_proj_dx": [("Win", 0)]}

    def fwd(fn, key, *args, **kw):
        return hosted(fn, key, fwd_sched, "gather", wsrc, W, *args, **kw)

    def bwd(fn, key, *args, **kw):
        return hosted(fn, key, bwd_sched, "scatter", GW, R, *args, **kw)

    spg = W["small"][:, 0]
    sm = [jnp.stack(p) for p in zip(*[_unpack(spg[d], [a.shape for a in small_names]) for d in range(N_DEV)])]
    bpw1 = jnp.transpose(sm[0], (1, 0, 2)).reshape(nB, 1, 2 * D)
    wdw = jnp.transpose(sm[1], (1, 2, 0, 3)).reshape(nB, CONV_W, D)
    wdw = jnp.pad(wdw, ((0, 0), (0, HALO - CONV_W), (0, 0)))
    bdw, blng, blnb, bpw2 = [jnp.transpose(s, (1, 0, 2)).reshape(nB, 1, D) for s in sm[2:]]
    lb_all = _lb_fwd(a_lb_logits)

    zx = [(D, F32), (D, F32)]
    ln_out = dict(outs=[(D, F32)], sums=[D, D, D])
    ln_mix = [(ln_mix_g[i:i + 1], ln_mix_b[i:i + 1]) for i in range(DEPTH)]
    ln_ffn = [(ln_ffn_g[i:i + 1], ln_ffn_b[i:i + 1]) for i in range(DEPTH)]
    saved = []
    h = xin
    for i in range(DEPTH):
        j = i // 2
        sv = {"xin": h}
        if i % 2 == 0:
            proj = fwd(_mm, f"a{j}_proj", h, W["Win"], j, "nn_col", out_dtype=F32)
            o, yg, ssave = fwd(_rec_fwd, f"a{j}_rec_fwd", proj, lb_all[j:j + 1], a_norm_g[j:j + 1])
            z1, x1 = _mm(yg, W["Wout"], j, "nn_row", out_dtype=F32, name=f"a{j}_out", outs=zx,
                         epi=lambda acc, r, g, b: _z_and_ln(acc + ALPHA * r, g, b), extras=[h, *ln_mix[i]])
            sv.update(proj=proj, o=o, ssave=ssave, yg=yg)
        else:
            u, glu = _mm(h, W["Wp1"], j, "nn_col", out_dtype=F32, name=f"b{j}_pw1",
                         outs=[(2 * D, F32), (D, F32)],
                         epi=lambda acc, b: _u_and_glu(acc + b), extras=[bpw1[j]])
            cv, s = fwd(_conv_fwd, f"b{j}_conv", glu, wdw[j], bdw[j], blng[j], blnb[j])
            z1, x1 = _mm(s, W["Wp2"], j, "nn_row", out_dtype=F32, name=f"b{j}_pw2", outs=zx,
                         epi=lambda acc, bb, r, g, b: _z_and_ln(acc + bb + ALPHA * r, g, b),
                         extras=[bpw2[j], h, *ln_mix[i]])
            sv.update(u=u, glu=glu, cv=cv, s=s)
        hh = fwd(_mm, f"l{i}_ffn_up", x1, W["W1"], i, "nn_col", out_dtype=BF)
        relu2 = lambda t: jnp.square(jnp.maximum(t, 0))
        if i < DEPTH - 1:
            z2, h = fwd(_mm, f"l{i}_ffn_down", hh, W["W2"], i, "nn_row", out_dtype=F32, outs=zx, a_fn=relu2,
                        epi=lambda acc, r, g, b: _z_and_ln(acc + ALPHA * r, g, b), extras=[x1, *ln_ffn[i]])
        else:
            z2 = None
            dz2, dg_top, db_top, loss_row = fwd(
                _mm, f"l{i}_ffn_down", hh, W["W2"], i, "nn_row", out_dtype=F32, a_fn=relu2, **ln_out,
                epi=lambda acc, r, g, b, t: _top_tile(acc + ALPHA * r, g, b, t), extras=[x1, *ln_ffn[i], tgt])
        sv.update(z1=z1, x1=x1, hh=hh, z2=z2)
        saved.append(sv)

    rows = {}
    top = DEPTH - 1
    rows[("ffn_g", top)], rows[("ffn_b", top)] = dg_top, db_top
    for i in reversed(range(DEPTH)):
        j = i // 2
        sv = saved[i]
        if i > 0:
            below = dict(epi=lambda acc, r, z, g: _ln_bwd_tile(acc + ALPHA * r, z, g), **ln_out)
            below_extras = [saved[i - 1]["z2"], ln_ffn[i - 1][0]]
        else:
            below = dict(epi=lambda acc, r: acc + ALPHA * r)
            below_extras = []
        dh = _mm(dz2, W["W2"], i, "nt_row", out_dtype=BF, name=f"l{i}_ffn_down_dx",
                 epi=lambda acc, hv: acc * (2.0 * jnp.maximum(hv.astype(F32), 0.0)), extras=[sv["hh"]])
        GW["W2"] = _mm_tn(GW["W2"], dz2, sv["hh"], i, "rows_t", name=f"l{i}_ffn_down_dw",
                         b_fn=lambda t: jnp.square(jnp.maximum(t, 0)))
        GW["W1"] = _mm_tn(GW["W1"], sv["x1"], dh, i, "cols", name=f"l{i}_ffn_up_dw")
        dz1, dg, db, dz1sum = _mm(dh, W["W1"], i, "nt_col", out_dtype=F32, name=f"l{i}_ffn_up_dx",
                                  epi=lambda acc, r, z, g: _ln_bwd_tile(acc + ALPHA * r, z, g),
                                  extras=[dz2, sv["z1"], ln_mix[i][0]], **ln_out)
        rows[("mix_g", i)], rows[("mix_b", i)] = dg, db
        if i % 2 == 0:
            GW["Wout"] = _mm_tn(GW["Wout"], sv["yg"], dz1, j, "rows", name=f"a{j}_out_dw")
            do, dgr, dng = _mm(dz1, W["Wout"], j, "nt_row", out_dtype=F32, name=f"a{j}_out_dx",
                               outs=[(D, F32), (D, BF)], sums=[D], epi=_gate_bwd_tile,
                               extras=[sv["o"], (sv["proj"], 3), a_norm_g[j:j + 1]])
            rows[("ng", j)] = dng
            dproj, dlb = bwd(_rec_bwd, f"a{j}_rec_bwd", sv["proj"], lb_all[j:j + 1], do, dgr, sv["ssave"])
            rows[("lb", j)] = dlb
            GW["Win"] = _mm_tn(GW["Win"], sv["xin"], dproj, j, "cols", name=f"a{j}_proj_dw")
            res = bwd(_mm, f"a{j}_proj_dx", dproj, W["Win"], j, "nt_col", out_dtype=F32,
                      extras=[dz1, *below_extras], **below)
        else:
            rows[("bpw2", j)] = dz1sum
            GW["Wp2"] = _mm_tn(GW["Wp2"], sv["s"], dz1, j, "rows", name=f"b{j}_pw2_dw")
            dc, dlg, dlb_, dcs = _mm(dz1, W["Wp2"], j, "nt_row", out_dtype=F32, name=f"b{j}_pw2_dx",
                                     outs=[(D, F32)], sums=[D, D, D], epi=_silu_ln_bwd_tile,
                                     extras=[sv["cv"], blng[j], blnb[j]])
            rows[("blng", j)], rows[("blnb", j)], rows[("bdw", j)] = dlg, dlb_, dcs
            du, dwdw, dbu = bwd(_conv_bwd, f"b{j}_conv_bwd", dc, sv["glu"], sv["u"], wdw[j])
            rows[("wdw", j)] = dwdw[:CONV_W]
            rows[("bpw1", j)] = dbu.reshape(2, D)
            GW["Wp1"] = _mm_tn(GW["Wp1"], sv["xin"], du, j, "cols", name=f"b{j}_pw1_dw")
            res = _mm(du, W["Wp1"], j, "nt_col", out_dtype=F32, name=f"b{j}_pw1_dx",
                      extras=[dz1, *below_extras], **below)
        if i > 0:
            dz2, dg, db, _ = res
            rows[("ffn_g", i - 1)], rows[("ffn_b", i - 1)] = dg, db
        else:
            dx = res
    grad_x = dx[None]

    order = ([("mix_g", i) for i in range(DEPTH)] + [("mix_b", i) for i in range(DEPTH)]
             + [("ffn_g", i) for i in range(DEPTH)] + [("ffn_b", i) for i in range(DEPTH)])
    lb_row0 = len(order)
    order += [("lb", j) for j in range(nA)] + [("ng", j) for j in range(nA)]
    for j in range(nB):
        order += [("bpw1", j), ("wdw", j), ("bdw", j), ("blng", j), ("blnb", j), ("bpw2", j)]
    pieces, offs, off = [], {}, 0
    for key in order:
        offs[key] = off
        pieces.append(rows[key])
        off += rows[key].shape[0]
    offs["loss"] = off
    pieces.append(loss_row)
    off += 1
    part = jnp.concatenate(pieces, axis=0)
    part = jnp.pad(part, ((0, -off % 8), (0, 0)))
    parts = _exchange(("gather", [(part[None], 0, jax.ShapeDtypeStruct((N_DEV, 1) + part.shape, F32))]),
                      "gather_small_grads")[0][:, 0]
    G = _small_reduce(parts, a_lb_logits, lb_row0)
    loss = jnp.sum(G[offs["loss"]])

    def rep(kind, n):
        return jnp.concatenate([G[offs[(kind, i)]:offs[(kind, i)] + 1] for i in range(n)], axis=0)

    def shard_cols(full, width):
        return lax.dynamic_slice_in_dim(full, me * width, width, axis=full.ndim - 1)

    g_small = {
        "ln_mix_g": rep("mix_g", DEPTH), "ln_mix_b": rep("mix_b", DEPTH),
        "ln_ffn_g": rep("ffn_g", DEPTH), "ln_ffn_b": rep("ffn_b", DEPTH),
        "a_lb_logits": rep("lb", nA), "a_norm_g": rep("ng", nA),
        "b_b_pw1": shard_cols(jnp.stack([G[offs[("bpw1", j)]:offs[("bpw1", j)] + 2].reshape(2 * D)
                                         for j in range(nB)]), 2 * D // N_DEV),
        "b_w_dw": shard_cols(jnp.stack([G[offs[("wdw", j)]:offs[("wdw", j)] + CONV_W] for j in range(nB)]),
                             D // N_DEV),
        "b_b_dw": shard_cols(rep("bdw", nB), D // N_DEV),
        "b_ln_g": shard_cols(rep("blng", nB), D // N_DEV),
        "b_ln_b": shard_cols(rep("blnb", nB), D // N_DEV),
        "b_b_pw2": shard_cols(rep("bpw2", nB), D // N_DEV),
    }
    small_w = {"ln_mix_g": (ln_mix_g, m_ln_mix_g, v_ln_mix_g), "ln_mix_b": (ln_mix_b, m_ln_mix_b, v_ln_mix_b),
               "ln_ffn_g": (ln_ffn_g, m_ln_ffn_g, v_ln_ffn_g), "ln_ffn_b": (ln_ffn_b, m_ln_ffn_b, v_ln_ffn_b),
               "a_lb_logits": (a_lb_logits, m_a_lb_logits, v_a_lb_logits),
               "a_norm_g": (a_norm_g, m_a_norm_g, v_a_norm_g),
               "b_b_pw1": (b_b_pw1, m_b_b_pw1, v_b_b_pw1), "b_w_dw": (b_w_dw, m_b_w_dw, v_b_w_dw),
               "b_b_dw": (b_b_dw, m_b_b_dw, v_b_b_dw), "b_ln_g": (b_ln_g, m_b_ln_g, v_b_ln_g),
               "b_ln_b": (b_ln_b, m_b_ln_b, v_b_ln_b), "b_b_pw2": (b_b_pw2, m_b_b_pw2, v_b_b_pw2)}
    snames = list(small_w)
    sshapes = [small_w[k][0].shape for k in snames]
    pg = _pack([g_small[k] for k in snames], 1024)
    pw, pm, pv = [_pack([small_w[k][q] for k in snames], 1024) for q in range(3)]
    sres = _adamw(pg[None], pw, pm, pv, "adamw_small")
    sres = [dict(zip(snames, _unpack(r, sshapes))) for r in sres]

    recv = [R[n] for n in ["W1", "W2", "Win", "Wout", "Wp1", "Wp2"]]
    big = {}
    for nm, rv, (w, m, v) in zip(
            ["ffn_w1", "ffn_w2", "a_w_in", "a_w_out", "b_w_pw1", "b_w_pw2"], recv,
            [(ffn_w1, m_ffn_w1, v_ffn_w1), (ffn_w2, m_ffn_w2, v_ffn_w2), (a_w_in, m_a_w_in, v_a_w_in),
             (a_w_out, m_a_w_out, v_a_w_out), (b_w_pw1, m_b_w_pw1, v_b_w_pw1), (b_w_pw2, m_b_w_pw2, v_b_w_pw2)]):
        big[nm] = _adamw(rv, w, m, v, f"adamw_{nm}")

    names = ["ln_mix_g", "ln_mix_b", "ln_ffn_g", "ln_ffn_b", "ffn_w1", "ffn_w2", "a_w_in", "a_lb_logits",
             "a_norm_g", "a_w_out", "b_w_pw1", "b_b_pw1", "b_w_dw", "b_b_dw", "b_ln_g", "b_ln_b", "b_w_pw2",
             "b_b_pw2"]
    out = [loss, grad_x]
    for q in range(4):
        for nm in names:
            out.append(big[nm][q] if nm in big else sres[q][nm])
    return tuple(out)
```
